```python
import math
import jax
import jax.numpy as jnp
from jax import lax
import numpy as np


D_MODEL = 2048
BATCH = 4
SEQ = 4096
DEPTH = 1

HYENA_WIDTH = 1024
HYENA_ORDER = 2
HYENA_EMB = 33
HYENA_FILTER_HIDDEN = 64
HYENA_FAST_DECAY = 0.3
HYENA_SLOW_DECAY = 1.5
HYENA_TARGET = 1e-2
MLSTM_WIDTH = 1024
MLSTM_HEADS = 4
MLSTM_HEAD_DIM = MLSTM_WIDTH // MLSTM_HEADS
MLSTM_CHUNK = 128
N_MLSTM_GATES = 4 * MLSTM_HEADS
SHORT_CONV = 3
SHORT_CONV_CH = 3 * HYENA_WIDTH + 2 * MLSTM_WIDTH
IN_COLS = SHORT_CONV_CH + 2 * MLSTM_WIDTH + N_MLSTM_GATES + 2 * D_MODEL
FFN_HIDDEN = -(-8 * D_MODEL // (3 * 256)) * 256
RMS_EPS = 1e-6

kernel_name = 'hybrid_hyena_mlstm_encoder_block'


def rmsnorm(x, w):
    xf = x.astype(jnp.float32)
    y = xf * lax.rsqrt(jnp.mean(xf * xf, axis=-1, keepdims=True) + RMS_EPS)
    return (y * w.astype(jnp.float32)).astype(x.dtype)


def centred_depthwise_conv(u, w, b):
    K = w.shape[0]
    pad = K // 2
    L = u.shape[1]
    up = jnp.pad(u, ((0, 0), (pad, pad), (0, 0)))
    out = up[:, 0:L] * w[0] + b
    for j in range(1, K):
        out = out + up[:, j:j + L] * w[j]
    return out


def hyena_filters(L, w1, b1, f1, w2, b2, f2, w3):
    f32 = jnp.float32
    t = jnp.linspace(0.0, 1.0, L, dtype=f32)[:, None]
    bands = (HYENA_EMB - 1) // 2
    omega = 2.0 * math.pi * jnp.arange(L, dtype=f32) / L
    freqs = jnp.linspace(1e-4, bands - 1, bands, dtype=f32)
    ang = omega[:, None] * freqs[None, :]
    z = jnp.concatenate([t, jnp.cos(ang), -jnp.sin(ang)], axis=-1)
    h = jnp.sin(f1.astype(f32) * (z @ w1.astype(f32) + b1.astype(f32)))
    h = jnp.sin(f2.astype(f32) * (h @ w2.astype(f32) + b2.astype(f32)))
    h = h @ w3.astype(f32)
    max_decay = math.log(HYENA_TARGET) / HYENA_FAST_DECAY
    min_decay = math.log(HYENA_TARGET) / HYENA_SLOW_DECAY
    deltas = jnp.linspace(min_decay, max_decay, HYENA_WIDTH, dtype=f32)
    window = jnp.exp(-t * jnp.abs(deltas)[None, :])
    return h.reshape(L, HYENA_ORDER, 2, HYENA_WIDTH) * window[:, None, None, :]


def two_sided_spectrum(h):
    h_fwd = h[:, :, 0]
    h_bwd = h[:, :, 1]
    kern = jnp.concatenate([h_fwd, jnp.zeros_like(h_fwd[:1]), h_bwd[:0:-1]], axis=0)
    return jnp.fft.rfft(kern, axis=0)


def long_conv(u, kf, bias):
    L = u.shape[1]
    uf = jnp.fft.rfft(u, n=2 * L, axis=1)
    y = jnp.fft.irfft(uf * kf[None], n=2 * L, axis=1)[:, :L]
    return y + u * bias


def mlstm_direction(q, k, v, i_pre, f_pre):
    B, L, H, dh = q.shape
    nc = L // MLSTM_CHUNK

    def to_chunks(t):
        return t.reshape(B, nc, MLSTM_CHUNK, H, -1).transpose(0, 3, 1, 2, 4)

    qc = to_chunks(q)
    kc = to_chunks(k) * (dh ** -0.5)
    vc = to_chunks(v)
    ig = to_chunks(i_pre[..., None])[..., 0]
    logf = to_chunks(jax.nn.log_sigmoid(f_pre)[..., None])[..., 0]
    b = jnp.cumsum(logf, axis=-1)
    b_last = b[..., -1]
    a = b_last[..., None] - b + ig
    m_loc = jnp.max(a, axis=-1)

    def step(carry, xs):
        C, n, m = carry
        bl, a_c, ml, k_c, v_c = xs
        m_new = jnp.maximum(bl + m, ml)
        keep = jnp.exp(bl + m - m_new)
        w = jnp.exp(a_c - m_new[..., None])
        C_new = keep[..., None, None] * C + jnp.einsum('bhsv,bhsk->bhvk', v_c * w[..., None], k_c)
        n_new = keep[..., None] * n + jnp.einsum('bhs,bhsk->bhk', w, k_c)
        return (C_new, n_new, m_new), (C, n, m)

    xs = tuple(jnp.moveaxis(t, 2, 0) for t in (b_last, a, m_loc, kc, vc))
    init = (jnp.zeros((B, H, dh, dh), jnp.float32),
            jnp.zeros((B, H, dh), jnp.float32),
            jnp.zeros((B, H), jnp.float32))
    _, (C_prev, n_prev, m_prev) = lax.scan(step, init, xs)
    C_prev = jnp.moveaxis(C_prev, 0, 2)
    n_prev = jnp.moveaxis(n_prev, 0, 2)
    m_prev = jnp.moveaxis(m_prev, 0, 2)

    pos = jnp.arange(MLSTM_CHUNK)
    lower = pos[:, None] >= pos[None, :]
    dlog = b[..., :, None] - b[..., None, :] + ig[..., None, :]
    dlog = jnp.where(lower, dlog, -jnp.inf)
    inter = b + m_prev[..., None]
    m_j = jnp.maximum(inter, jnp.max(dlog, axis=-1))
    s = jnp.einsum('bhcjd,bhcsd->bhcjs', qc, kc) * jnp.exp(dlog - m_j[..., None])
    inter_w = jnp.exp(inter - m_j)
    num = (jnp.einsum('bhcjs,bhcsv->bhcjv', s, vc)
           + inter_w[..., None] * jnp.einsum('bhcjk,bhcvk->bhcjv', qc, C_prev))
    den = jnp.sum(s, axis=-1) + inter_w * jnp.einsum('bhcjk,bhck->bhcj', qc, n_prev)
    h = num / jnp.maximum(jnp.abs(den), jnp.exp(-m_j))[..., None]
    return h.transpose(0, 2, 3, 1, 4).reshape(B, L, H, dh)


def setup_inputs(seed: int = 0) -> dict:
    key = jax.random.key(seed)
    ks = jax.random.split(key, 24)
    f32 = jnp.float32

    def nrm(k, shape, scale):
        return jax.random.normal(k, shape, f32) * scale

    Ld = DEPTH
    H = MLSTM_HEADS
    FH = HYENA_FILTER_HIDDEN
    x = nrm(ks[0], (BATCH, SEQ, D_MODEL), 1.0)
    norm1_w = 1.0 + nrm(ks[1], (Ld, D_MODEL), 0.05)
    w_in = nrm(ks[2], (Ld, D_MODEL, IN_COLS), D_MODEL ** -0.5)
    conv_w = nrm(ks[3], (Ld, SHORT_CONV, SHORT_CONV_CH), SHORT_CONV ** -0.5)
    conv_b = nrm(ks[4], (Ld, SHORT_CONV_CH), 0.02)
    filt_w1 = nrm(ks[5], (Ld, HYENA_EMB, FH), HYENA_EMB ** -0.5)
    filt_b1 = nrm(ks[6], (Ld, FH), 0.1)
    filt_freq1 = 1.0 + nrm(ks[7], (Ld, FH), 0.05)
    filt_w2 = nrm(ks[8], (Ld, FH, FH), FH ** -0.5)
    filt_b2 = nrm(ks[9], (Ld, FH), 0.1)
    filt_freq2 = 1.0 + nrm(ks[10], (Ld, FH), 0.05)
    filt_w3 = nrm(ks[11], (Ld, FH, HYENA_ORDER * 2 * HYENA_WIDTH), 0.5 * SEQ ** -0.5)
    hyena_bias = nrm(ks[12], (Ld, HYENA_ORDER, HYENA_WIDTH), 0.5)
    f_base = jnp.linspace(3.0, 6.0, H, dtype=f32)
    zero_h = jnp.zeros((H,), f32)
    gate_base = jnp.stack([zero_h, f_base, zero_h, f_base])
    mlstm_gate_bias = gate_base[None] + nrm(ks[13], (Ld, 4, H), 0.1)
    w_branch_a = nrm(ks[14], (Ld, HYENA_WIDTH, D_MODEL), HYENA_WIDTH ** -0.5)
    w_branch_b = nrm(ks[15], (Ld, MLSTM_WIDTH, D_MODEL), MLSTM_WIDTH ** -0.5)
    w_out = nrm(ks[16], (Ld, D_MODEL, D_MODEL), D_MODEL ** -0.5)
    norm2_w = 1.0 + nrm(ks[17], (Ld, D_MODEL), 0.05)
    w_gate_up = nrm(ks[18], (Ld, D_MODEL, 2 * FFN_HIDDEN), D_MODEL ** -0.5)
    w_down = nrm(ks[19], (Ld, FFN_HIDDEN, D_MODEL), FFN_HIDDEN ** -0.5)
    norm_f_w = 1.0 + nrm(ks[20], (D_MODEL,), 0.05)
    return {'x': x, 'norm1_w': norm1_w, 'w_in': w_in, 'conv_w': conv_w, 'conv_b': conv_b,
            'filt_w1': filt_w1, 'filt_b1': filt_b1, 'filt_freq1': filt_freq1,
            'filt_w2': filt_w2, 'filt_b2': filt_b2, 'filt_freq2': filt_freq2, 'filt_w3': filt_w3,
            'hyena_bias': hyena_bias, 'mlstm_gate_bias': mlstm_gate_bias,
            'w_branch_a': w_branch_a, 'w_branch_b': w_branch_b, 'w_out': w_out,
            'norm2_w': norm2_w, 'w_gate_up': w_gate_up, 'w_down': w_down, 'norm_f_w': norm_f_w}


def reference(x, norm1_w, w_in, conv_w, conv_b, filt_w1, filt_b1, filt_freq1, filt_w2, filt_b2,
              filt_freq2, filt_w3, hyena_bias, mlstm_gate_bias, w_branch_a, w_branch_b, w_out,
              norm2_w, w_gate_up, w_down, norm_f_w):
    B, L, _ = x.shape
    f32 = jnp.float32
    HW, MW, H, dh = HYENA_WIDTH, MLSTM_WIDTH, MLSTM_HEADS, MLSTM_HEAD_DIM
    for l in range(DEPTH):
        hn = rmsnorm(x, norm1_w[l])
        proj = hn @ w_in[l]
        conv_out = centred_depthwise_conv(proj[..., :SHORT_CONV_CH], conv_w[l], conv_b[l])
        rest = proj[..., SHORT_CONV_CH:]

        hy = conv_out[..., :3 * HW].astype(f32)
        hv, hx1, hx2 = hy[..., :HW], hy[..., HW:2 * HW], hy[..., 2 * HW:]
        filt = hyena_filters(L, filt_w1[l], filt_b1[l], filt_freq1[l], filt_w2[l], filt_b2[l],
                             filt_freq2[l], filt_w3[l])
        kf = two_sided_spectrum(filt)
        hb = hyena_bias[l].astype(f32)
        z = hx1 * long_conv(hv, kf[:, 0], hb[0])
        y_a = (hx2 * long_conv(z, kf[:, 1], hb[1])).astype(x.dtype)

        qk = jax.nn.silu(conv_out[..., 3 * HW:].astype(f32))
        q = qk[..., :MW].reshape(B, L, H, dh)
        k = qk[..., MW:].reshape(B, L, H, dh)
        v = rest[..., :MW].astype(f32).reshape(B, L, H, dh)
        o = jax.nn.sigmoid(rest[..., MW:2 * MW].astype(f32))
        g = (rest[..., 2 * MW:2 * MW + N_MLSTM_GATES].astype(f32).reshape(B, L, 4, H)
             + mlstm_gate_bias[l].astype(f32))
        h_fwd = mlstm_direction(q, k, v, g[:, :, 0], g[:, :, 1])
        flip = lambda t: jnp.flip(t, axis=1)
        h_bwd = flip(mlstm_direction(flip(q), flip(k), flip(v), flip(g[:, :, 2]), flip(g[:, :, 3])))
        y_b = (o * (h_fwd + h_bwd).reshape(B, L, MW)).astype(x.dtype)

        merge = rest[..., 2 * MW + N_MLSTM_GATES:]
        gate_a = jax.nn.sigmoid(merge[..., :D_MODEL])
        gate_b = jax.nn.sigmoid(merge[..., D_MODEL:])
        mixed = gate_a * (y_a @ w_branch_a[l]) + gate_b * (y_b @ w_branch_b[l])
        x = x + mixed @ w_out[l]

        hn2 = rmsnorm(x, norm2_w[l])
        gu = hn2 @ w_gate_up[l]
        x = x + (jax.nn.silu(gu[..., :FFN_HIDDEN]) * gu[..., FFN_HIDDEN:]) @ w_down[l]
    return rmsnorm(x, norm_f_w)
```

```python
import functools
import math

import numpy as np
import jax
import jax.numpy as jnp
from jax import lax
from jax.experimental import pallas as pl
from jax.experimental.pallas import tpu as pltpu

_F32 = jnp.float32
_BF16 = jnp.bfloat16
_HIGHEST = lax.Precision.HIGHEST

D_MODEL = 2048
HYENA_WIDTH = 1024
HYENA_EMB = 33
HYENA_FILTER_HIDDEN = 64
HYENA_FAST_DECAY = 0.3
HYENA_SLOW_DECAY = 1.5
HYENA_TARGET = 1e-2
MLSTM_WIDTH = 1024
MLSTM_HEADS = 4
MLSTM_HEAD_DIM = 256
MLSTM_CHUNK = 128
FFN_HIDDEN = 5632
RMS_EPS = 1e-6

_COL_MA, _COL_MB, _COL_HV, _COL_Q, _COL_V, _COL_O = 0, 2, 4, 7, 9, 10
_MAIN_COLS = 11 * 1024

_N2 = 32
_T2_PER_PHASE = 8
_F1_PER_PHASE = 64
_FFT_CT = 256

_VMEM_LIMIT = 56 * 1024 * 1024


def _cparams(sem, vmem=_VMEM_LIMIT):
    return pltpu.CompilerParams(dimension_semantics=sem, vmem_limit_bytes=vmem)


def _inproj_body(x_ref, nw_ref, w_ref, wgi_ref, wgf_ref, o_ref, gi_ref, gf_ref, hn_ref):
    @pl.when(pl.program_id(1) == 0)
    def _():
        x = x_ref[...]
        hn = x * lax.rsqrt(jnp.mean(x * x, axis=-1, keepdims=True) + RMS_EPS) * nw_ref[...]
        hn_ref[...] = hn.astype(_BF16)
        gi_ref[...] = jnp.dot(hn, wgi_ref[...], precision=_HIGHEST, preferred_element_type=_F32)
        gf_ref[...] = jnp.dot(hn, wgf_ref[...], precision=_HIGHEST, preferred_element_type=_F32)

    o_ref[...] = jnp.dot(hn_ref[...], w_ref[...], preferred_element_type=_F32).astype(o_ref.dtype)


def _inproj(x2, norm_w, w_main, wgi, wgf):
    m = x2.shape[0]
    tm, tn = 1024, 1024
    return pl.pallas_call(
        _inproj_body,
        grid=(m // tm, _MAIN_COLS // tn),
        in_specs=[
            pl.BlockSpec((tm, D_MODEL), lambda i, j: (i, 0)),
            pl.BlockSpec((1, D_MODEL), lambda i, j: (0, 0)),
            pl.BlockSpec((D_MODEL, tn), lambda i, j: (0, j)),
            pl.BlockSpec((D_MODEL, 128), lambda i, j: (0, 0)),
            pl.BlockSpec((D_MODEL, 128), lambda i, j: (0, 0)),
        ],
        out_specs=[
            pl.BlockSpec((tm, tn), lambda i, j: (i, j)),
            pl.BlockSpec((tm, 128), lambda i, j: (i, 0)),
            pl.BlockSpec((tm, 128), lambda i, j: (i, 0)),
        ],
        out_shape=[
            jax.ShapeDtypeStruct((m, _MAIN_COLS), _BF16),
            jax.ShapeDtypeStruct((m, 128), _F32),
            jax.ShapeDtypeStruct((m, 128), _F32),
        ],
        scratch_shapes=[pltpu.VMEM((tm, D_MODEL), _BF16)],
        compiler_params=_cparams(("arbitrary", "arbitrary")),
        name="inproj",
    )(x2, norm_w, w_main, wgi, wgf)


def _shortconv_body(u_ref, w_ref, b_ref, o_ref, *, silu_cols, k_scale):
    u = u_ref[...].astype(_F32)
    n = u.shape[0]
    row = lax.broadcasted_iota(jnp.int32, u.shape, 0)
    up = jnp.where(row == 0, 0.0, pltpu.roll(u, 1, axis=0))
    un = jnp.where(row == n - 1, 0.0, pltpu.roll(u, n - 1, axis=0))
    w = w_ref[...]
    y = up * w[0:1, :] + b_ref[...] + u * w[1:2, :] + un * w[2:3, :]
    if silu_cols:
        y = y * (1.0 / (1.0 + jnp.exp(-y)))
        is_k = pl.program_id(1) >= (MLSTM_WIDTH // u.shape[1])
        y = y * jnp.where(is_k, k_scale, 1.0)
    o_ref[...] = y.astype(o_ref.dtype)


def _shortconv(proj3, conv_w, conv_b, proj_col0, conv_col0, ncols, out_dtype, silu_cols):
    b, l, _ = proj3.shape
    ct = 256
    p0, c0 = proj_col0 // ct, conv_col0 // ct
    body = functools.partial(_shortconv_body, silu_cols=silu_cols, k_scale=MLSTM_HEAD_DIM ** -0.5)
    return pl.pallas_call(
        body,
        grid=(b, ncols // ct),
        in_specs=[
            pl.BlockSpec((None, l, ct), lambda i, j: (i, 0, p0 + j)),
            pl.BlockSpec((3, ct), lambda i, j: (0, c0 + j)),
            pl.BlockSpec((1, ct), lambda i, j: (0, c0 + j)),
        ],
        out_specs=pl.BlockSpec((None, l, ct), lambda i, j: (i, 0, j)),
        out_shape=jax.ShapeDtypeStruct((b, l, ncols), out_dtype),
        compiler_params=_cparams(("arbitrary", "arbitrary")),
        name="shortconv_silu" if silu_cols else "shortconv",
    )(proj3, conv_w, conv_b)


def _dft_tables(n):
    n2 = _N2
    n1 = n // n2
    h1 = n1 // 2
    f1 = np.arange(n1)[:, None]
    t1 = np.arange(n1)[None, :]
    ang = 2.0 * np.pi * ((f1 * t1) % n1) / n1
    c, s = np.cos(ang), np.sin(ang)
    ch, sh = c[:, :h1], s[:, :h1]
    fa = np.block([[ch, sh], [-sh, ch]])
    fa_full = np.concatenate([c, -s], axis=0)
    fai = np.block([[ch.T, -sh.T], [sh.T, ch.T]]) / n
    f1v = np.arange(n1)[:, None, None]
    f2 = np.arange(n2)[None, :, None]
    t2 = np.arange(n2)[None, None, :]
    ph = 2.0 * np.pi * ((f1v * t2 + f2 * t2 * n1) % n) / n
    tc, ts = np.cos(ph), np.sin(ph)
    fb = np.concatenate([np.concatenate([tc, ts], axis=2),
                         np.concatenate([-ts, tc], axis=2)], axis=1)
    tct, tst = np.swapaxes(tc, 1, 2), np.swapaxes(ts, 1, 2)
    fbi = np.concatenate([np.concatenate([tct, -tst], axis=2),
                          np.concatenate([tst, tct], axis=2)], axis=1)
    to = lambda a: jnp.asarray(a.astype(np.float32)).astype(_BF16)
    return to(fa), to(fa_full), to(fai), to(fb), to(fbi)


def _filter_positions(l):
    n = 2 * l
    n1 = n // _N2
    r = (np.arange(_N2)[:, None] + _N2 * np.arange(n1)[None, :]).reshape(-1)
    pos = np.where(r < l, r, n - r)
    pos = np.where(r == l, 0, pos)
    bands = (HYENA_EMB - 1) // 2
    tt = np.linspace(0.0, 1.0, l, dtype=np.float32).astype(np.float64)[pos]
    omega = (2.0 * math.pi * np.arange(l, dtype=np.float32) / l).astype(np.float32)
    freqs = np.linspace(1e-4, bands - 1, bands, dtype=np.float32)
    ang = (omega[:, None] * freqs[None, :]).astype(np.float64)[pos]
    z = np.zeros((n, 128), np.float64)
    z[:, 0] = tt
    z[:, 1:1 + bands] = np.cos(ang)
    z[:, 1 + bands:1 + 2 * bands] = -np.sin(ang)
    z[:, 33] = (r != l)
    z[:, 34] = (r < l)
    return jnp.asarray(z.astype(np.float32))


def _filt_mlp_body(z_ref, w1_ref, b1_ref, q1_ref, w2_ref, b2_ref, q2_ref, w3f_ref, w3b_ref, dabs_ref,
                   o_ref, hf_ref, hb_ref):
    @pl.when(pl.program_id(1) == 0)
    def _():
        z = z_ref[...]
        h = jnp.sin(q1_ref[...] * (jnp.dot(z, w1_ref[...], precision=_HIGHEST,
                                           preferred_element_type=_F32) + b1_ref[...]))
        h = jnp.sin(q2_ref[...] * (jnp.dot(h, w2_ref[...], precision=_HIGHEST,
                                           preferred_element_type=_F32) + b2_ref[...]))
        valid = z[:, 33:34]
        fwd = z[:, 34:35]
        hf_ref[...] = h * (valid * fwd)
        hb_ref[...] = h * (valid * (1.0 - fwd))

    raw = (jnp.dot(hf_ref[...], w3f_ref[...], precision=_HIGHEST, preferred_element_type=_F32)
           + jnp.dot(hb_ref[...], w3b_ref[...], precision=_HIGHEST, preferred_element_type=_F32))
    window = jnp.exp(-z_ref[:, 0:1] * dabs_ref[...])
    o_ref[...] = raw * window


def _filt_mlp(zpos, w1p, b1, q1, w2, b2, q2, w3, dabs):
    n = zpos.shape[0]
    ct = _FFT_CT
    tr = 1024
    per_order = HYENA_WIDTH // ct
    fh = HYENA_FILTER_HIDDEN
    full = lambda shape: pl.BlockSpec(shape, lambda r, i: (0,) * len(shape))
    return pl.pallas_call(
        _filt_mlp_body,
        grid=(n // tr, 2 * per_order),
        in_specs=[
            pl.BlockSpec((tr, 128), lambda r, i: (r, 0)),
            full((128, fh)), full((1, fh)), full((1, fh)),
            full((fh, fh)), full((1, fh)), full((1, fh)),
            pl.BlockSpec((fh, ct), lambda r, i: (0, (i // per_order) * 2 * per_order + i % per_order)),
            pl.BlockSpec((fh, ct), lambda r, i: (0, (i // per_order) * 2 * per_order + per_order + i % per_order)),
            pl.BlockSpec((1, ct), lambda r, i: (0, i % per_order)),
        ],
        out_specs=pl.BlockSpec((tr, ct), lambda r, i: (r, i)),
        out_shape=jax.ShapeDtypeStruct((n, 2 * HYENA_WIDTH), _F32),
        scratch_shapes=[pltpu.VMEM((tr, fh), _F32), pltpu.VMEM((tr, fh), _F32)],
        compiler_params=_cparams(("arbitrary", "arbitrary")),
        name="filt_mlp",
    )(zpos, w1p, b1, q1, w2, b2, q2, w3, w3, dabs)


def _stage_b_operand(s_ref, f1, n1):
    are = s_ref[:, f1, :]
    aim = s_ref[:, n1 + f1, :]
    return jnp.concatenate([are, aim], axis=0).astype(_BF16)


def _filt_fft_body(kern_ref, fa_ref, fb_ref, o_ref, s_ref, *, n_in, n1, f1pp):
    s = pl.program_id(1)

    @pl.when(s < n_in)
    def _():
        for i in range(_T2_PER_PHASE):
            u = kern_ref[i].astype(_BF16)
            s_ref[s * _T2_PER_PHASE + i] = jnp.dot(fa_ref[...], u, preferred_element_type=_F32)

    @pl.when(s >= n_in)
    def _():
        def body(fl, carry):
            f1 = (s - n_in) * f1pp + fl
            v = _stage_b_operand(s_ref, f1, n1)
            row = pl.multiple_of(fl * 2 * _N2, 2 * _N2)
            o_ref[pl.ds(row, 2 * _N2), :] = jnp.dot(fb_ref[fl], v, preferred_element_type=_F32)
            return carry

        lax.fori_loop(0, f1pp, body, 0)


def _filt_fft(kern3, fa_full, fb):
    n2, n1, cols = kern3.shape
    ct = _FFT_CT
    n_in = n2 // _T2_PER_PHASE
    f1pp = min(_F1_PER_PHASE, n1)
    n_mid = n1 // f1pp
    body = functools.partial(_filt_fft_body, n_in=n_in, n1=n1, f1pp=f1pp)
    return pl.pallas_call(
        body,
        grid=(cols // ct, n_in + n_mid),
        in_specs=[
            pl.BlockSpec((_T2_PER_PHASE, n1, ct), lambda j, s: (jnp.minimum(s, n_in - 1), 0, j)),
            pl.BlockSpec((2 * n1, n1), lambda j, s: (0, 0)),
            pl.BlockSpec((f1pp, 2 * n2, 2 * n2), lambda j, s: (jnp.maximum(s - n_in, 0), 0, 0)),
        ],
        out_specs=pl.BlockSpec((f1pp * 2 * n2, ct), lambda j, s: (jnp.maximum(s - n_in, 0), j)),
        out_shape=jax.ShapeDtypeStruct((n1 * 2 * n2, cols), _F32),
        scratch_shapes=[pltpu.VMEM((n2, 2 * n1, ct), _F32)],
        compiler_params=_cparams(("arbitrary", "arbitrary")),
        name="filt_fft",
    )(kern3, fa_full, fb)


def _hyena_conv_body(u_ref, u2_ref, g_ref, k_ref, bias_ref, fa_ref, fai_ref, fb_ref, fbi_ref,
                     o_ref, s_ref, *, n_in, n_mid, n1, f1pp):
    s = pl.program_id(2)
    h1 = n1 // 2

    @pl.when(s < n_in)
    def _():
        for i in range(_T2_PER_PHASE):
            u = jnp.concatenate([u_ref[0, :, i, :], u_ref[1, :, i, :]], axis=0).astype(_BF16)
            s_ref[s * _T2_PER_PHASE + i] = jnp.dot(fa_ref[...], u, preferred_element_type=_F32)

    @pl.when(jnp.logical_and(s >= n_in, s < n_in + n_mid))
    def _():
        def body(fl, carry):
            f1 = (s - n_in) * f1pp + fl
            v = _stage_b_operand(s_ref, f1, n1)
            x = jnp.dot(fb_ref[fl], v, preferred_element_type=_F32)
            xre, xim = x[:_N2], x[_N2:]
            row = pl.multiple_of(fl * 2 * _N2, 2 * _N2)
            kre = k_ref[pl.ds(row, _N2), :]
            kim = k_ref[pl.ds(row + _N2, _N2), :]
            y = jnp.concatenate([xre * kre - xim * kim, xre * kim + xim * kre], axis=0).astype(_BF16)
            w = jnp.dot(fbi_ref[fl], y, preferred_element_type=_F32)
            s_ref[:, f1, :] = w[:_N2]
            s_ref[:, n1 + f1, :] = w[_N2:]
            return carry

        lax.fori_loop(0, f1pp, body, 0)

    @pl.when(s >= n_in + n_mid)
    def _():
        bias = bias_ref[...]
        for i in range(_T2_PER_PHASE):
            bv = s_ref[(s - n_in - n_mid) * _T2_PER_PHASE + i].astype(_BF16)
            y = jnp.dot(fai_ref[...], bv, preferred_element_type=_F32)
            for m in range(2):
                ym = y[m * h1:(m + 1) * h1]
                um = u2_ref[m, :, i, :]
                gm = g_ref[m, :, i, :]
                o_ref[m, :, i, :] = (gm * (ym + um * bias)).astype(o_ref.dtype)


def _hyena_conv(u4, ucol, g4, gcol, kspec, kcol, bias, tabs, out_dtype):
    fa, fai, fb, fbi = tabs
    b, h1, n2, _ = u4.shape
    n1 = 2 * h1
    ct = _FFT_CT
    n_in = n2 // _T2_PER_PHASE
    f1pp = min(_F1_PER_PHASE, n1)
    n_mid = n1 // f1pp
    n_out = n_in
    tiles = HYENA_WIDTH // ct
    body = functools.partial(_hyena_conv_body, n_in=n_in, n_mid=n_mid, n1=n1, f1pp=f1pp)
    mid = lambda s: jnp.clip(s - n_in, 0, n_mid - 1)
    last = lambda s: jnp.clip(s - n_in - n_mid, 0, n_out - 1)
    blk = (2, h1, _T2_PER_PHASE, ct)
    return pl.pallas_call(
        body,
        grid=(tiles, b // 2, n_in + n_mid + n_out),
        in_specs=[
            pl.BlockSpec(blk, lambda j, p, s: (p, 0, jnp.minimum(s, n_in - 1), ucol + j)),
            pl.BlockSpec(blk, lambda j, p, s: (p, 0, last(s), ucol + j)),
            pl.BlockSpec(blk, lambda j, p, s: (p, 0, last(s), gcol + j)),
            pl.BlockSpec((f1pp * 2 * n2, ct), lambda j, p, s: (mid(s), kcol + j)),
            pl.BlockSpec((1, ct), lambda j, p, s: (0, j)),
            pl.BlockSpec((2 * n1, n1), lambda j, p, s: (0, 0)),
            pl.BlockSpec((n1, 2 * n1), lambda j, p, s: (0, 0)),
            pl.BlockSpec((f1pp, 2 * n2, 2 * n2), lambda j, p, s: (mid(s), 0, 0)),
            pl.BlockSpec((f1pp, 2 * n2, 2 * n2), lambda j, p, s: (mid(s), 0, 0)),
        ],
        out_specs=pl.BlockSpec(blk, lambda j, p, s: (p, 0, last(s), j)),
        out_shape=jax.ShapeDtypeStruct((b, h1, n2, HYENA_WIDTH), out_dtype),
        scratch_shapes=[pltpu.VMEM((n2, 2 * n1, ct), _F32)],
        compiler_params=_cparams(("arbitrary", "arbitrary", "arbitrary")),
        name="hyena_conv",
    )(u4, u4, g4, kspec, bias, fa, fai, fb, fbi)


def _gate_prep_body(gi_ref, gf_ref, bi_ref, bf_ref, col_ref, row_ref,
                    b_scr, a_scr, tot_scr, mloc_scr, mprev_scr, *, nc):
    ch = MLSTM_CHUNK
    lane = lax.broadcasted_iota(jnp.int32, (1, 128), 1)
    is_fwd = lane < MLSTM_HEADS
    live = lane < 2 * MLSTM_HEADS
    jj = lax.broadcasted_iota(jnp.int32, (ch, ch), 0)
    ss = lax.broadcasted_iota(jnp.int32, (ch, ch), 1)
    t_lo = (ss <= jj).astype(_F32)
    t_up = (ss >= jj).astype(_F32)

    def chunk_stats(c, carry):
        r0 = pl.multiple_of(c * ch, ch)
        gf = gf_ref[pl.ds(r0, ch), :] + bf_ref[...]
        gi = gi_ref[pl.ds(r0, ch), :] + bi_ref[...]
        logf = jnp.minimum(gf, 0.0) - jnp.log1p(jnp.exp(-jnp.abs(gf)))
        cs_lo = jnp.dot(t_lo, logf, precision=_HIGHEST, preferred_element_type=_F32)
        cs_up = jnp.dot(t_up, logf, precision=_HIGHEST, preferred_element_type=_F32)
        bcs = jnp.where(is_fwd, cs_lo, cs_up)
        tot = cs_lo[ch - 1:ch, :]
        a = tot - bcs + gi
        b_scr[pl.ds(r0, ch), :] = bcs
        a_scr[pl.ds(r0, ch), :] = a
        tot_scr[pl.ds(c, 1), :] = tot
        mloc_scr[pl.ds(c, 1), :] = jnp.max(a, axis=0, keepdims=True)
        row_ref[:, pl.ds(r0, ch)] = (gi - bcs).T[0:8, :]
        return carry

    lax.fori_loop(0, nc, chunk_stats, 0)

    def scan_f(c, m):
        mprev_scr[pl.ds(c, 1), :] = jnp.where(is_fwd, m, mprev_scr[pl.ds(c, 1), :])
        return jnp.maximum(tot_scr[pl.ds(c, 1), :] + m, mloc_scr[pl.ds(c, 1), :])

    def scan_b(i, m):
        c = nc - 1 - i
        mprev_scr[pl.ds(c, 1), :] = jnp.where(is_fwd, mprev_scr[pl.ds(c, 1), :], m)
        return jnp.maximum(tot_scr[pl.ds(c, 1), :] + m, mloc_scr[pl.ds(c, 1), :])

    mprev_scr[...] = jnp.zeros_like(mprev_scr)
    lax.fori_loop(0, nc, scan_f, jnp.zeros((1, 128), _F32))
    lax.fori_loop(0, nc, scan_b, jnp.zeros((1, 128), _F32))

    def emit(c, carry):
        r0 = pl.multiple_of(c * ch, ch)
        mprev = mprev_scr[pl.ds(c, 1), :]
        tot = tot_scr[pl.ds(c, 1), :]
        mnew = jnp.maximum(tot + mprev, mloc_scr[pl.ds(c, 1), :])
        w = jnp.exp(a_scr[pl.ds(r0, ch), :] - mnew)
        keep = jnp.exp(tot + mprev - mnew)
        zero = jnp.zeros((ch, 128), _F32)
        pack = (jnp.where(live, b_scr[pl.ds(r0, ch), :], 0.0)
                + pltpu.roll(jnp.where(live, w, 0.0), 8, axis=1)
                + pltpu.roll(jnp.where(live, mprev + zero, 0.0), 16, axis=1)
                + pltpu.roll(jnp.where(live, keep + zero, 0.0), 24, axis=1))
        col_ref[pl.ds(r0, ch), :] = pack
        return carry

    lax.fori_loop(0, nc, emit, 0)


def _gate_prep(gi3, gf3, bias_i, bias_f):
    b, l, _ = gi3.shape
    nc = l // MLSTM_CHUNK
    body = functools.partial(_gate_prep_body, nc=nc)
    return pl.pallas_call(
        body,
        grid=(b,),
        in_specs=[
            pl.BlockSpec((None, l, 128), lambda i: (i, 0, 0)),
            pl.BlockSpec((None, l, 128), lambda i: (i, 0, 0)),
            pl.BlockSpec((1, 128), lambda i: (0, 0)),
            pl.BlockSpec((1, 128), lambda i: (0, 0)),
        ],
        out_specs=[
            pl.BlockSpec((None, l, 128), lambda i: (i, 0, 0)),
            pl.BlockSpec((None, 8, l), lambda i: (i, 0, 0)),
        ],
        out_shape=[
            jax.ShapeDtypeStruct((b, l, 128), _F32),
            jax.ShapeDtypeStruct((b, 8, l), _F32),
        ],
        scratch_shapes=[
            pltpu.VMEM((l, 128), _F32), pltpu.VMEM((l, 128), _F32),
            pltpu.VMEM((nc, 128), _F32), pltpu.VMEM((nc, 128), _F32), pltpu.VMEM((nc, 128), _F32),
        ],
        compiler_params=_cparams(("arbitrary",)),
        name="gate_prep",
    )(gi3, gf3, bias_i, bias_f)


def _mlstm_body(qf_ref, kf_ref, vf_ref, colf_ref, rowf_ref, qb_ref, kb_ref, vb_ref, colb_ref, rowb_ref,
                hf_ref, hb_ref, ct_ref, n_ref):
    @pl.when(pl.program_id(1) == 0)
    def _():
        ct_ref[...] = jnp.zeros_like(ct_ref)
        n_ref[...] = jnp.zeros_like(n_ref)

    ch, dh = MLSTM_CHUNK, MLSTM_HEAD_DIM
    jj = lax.broadcasted_iota(jnp.int32, (ch, ch), 0)
    ss = lax.broadcasted_iota(jnp.int32, (ch, ch), 1)
    dirs = ((qf_ref, kf_ref, vf_ref, colf_ref, rowf_ref, hf_ref, ss <= jj),
            (qb_ref, kb_ref, vb_ref, colb_ref, rowb_ref, hb_ref, ss >= jj))
    for d, (q_ref, k_ref, v_ref, col_ref, row_ref, o_ref, mask) in enumerate(dirs):
        col = col_ref[...]
        for h in range(MLSTM_HEADS):
            hd = d * MLSTM_HEADS + h
            bj = col[:, hd:hd + 1]
            wj = col[:, 8 + hd:9 + hd]
            mp = col[:, 16 + hd:17 + hd]
            keep = col[0:1, 24 + hd:25 + hd]
            r = row_ref[hd:hd + 1, :]
            lo, hi = h * dh, (h + 1) * dh
            qh, kh, vh = q_ref[:, lo:hi], k_ref[:, lo:hi], v_ref[:, lo:hi]
            sc = lax.dot_general(qh, kh, (((1,), (1,)), ((), ())), preferred_element_type=_F32)
            dl = jnp.where(mask, bj + r, -jnp.inf)
            inter = bj + mp
            mj = jnp.maximum(inter, jnp.max(dl, axis=1, keepdims=True))
            p = sc * jnp.exp(dl - mj)
            iw = jnp.exp(inter - mj)
            ct = ct_ref[hd]
            nv = n_ref[hd]
            num = (jnp.dot(p.astype(_BF16), vh, preferred_element_type=_F32)
                   + iw * jnp.dot(qh, ct.astype(_BF16), preferred_element_type=_F32))
            den = (jnp.sum(p, axis=1, keepdims=True)
                   + iw * jnp.sum(qh.astype(_F32) * nv, axis=1, keepdims=True))
            o_ref[:, lo:hi] = num / jnp.maximum(jnp.abs(den), jnp.exp(-mj))
            vw = (vh.astype(_F32) * wj).astype(_BF16)
            ct_ref[hd] = keep * ct + lax.dot_general(kh, vw, (((0,), (0,)), ((), ())),
                                                     preferred_element_type=_F32)
            n_ref[hd] = keep * nv + jnp.sum(kh.astype(_F32) * wj, axis=0, keepdims=True)


def _mlstm(qk, proj3, col, row):
    b, l, _ = qk.shape
    ch, mw = MLSTM_CHUNK, MLSTM_WIDTH
    nc = l // ch
    fw = lambda c: c
    bw = lambda c: nc - 1 - c

    def specs(cm):
        return [
            pl.BlockSpec((None, ch, mw), lambda i, c: (i, cm(c), 0)),
            pl.BlockSpec((None, ch, mw), lambda i, c: (i, cm(c), 1)),
            pl.BlockSpec((None, ch, mw), lambda i, c: (i, cm(c), _COL_V)),
            pl.BlockSpec((None, ch, 128), lambda i, c: (i, cm(c), 0)),
            pl.BlockSpec((None, 8, ch), lambda i, c: (i, 0, cm(c))),
        ]

    return pl.pallas_call(
        _mlstm_body,
        grid=(b, nc),
        in_specs=specs(fw) + specs(bw),
        out_specs=[
            pl.BlockSpec((None, ch, mw), lambda i, c: (i, c, 0)),
            pl.BlockSpec((None, ch, mw), lambda i, c: (i, nc - 1 - c, 0)),
        ],
        out_shape=[jax.ShapeDtypeStruct((b, l, mw), _F32), jax.ShapeDtypeStruct((b, l, mw), _F32)],
        scratch_shapes=[
            pltpu.VMEM((2 * MLSTM_HEADS, MLSTM_HEAD_DIM, MLSTM_HEAD_DIM), _F32),
            pltpu.VMEM((2 * MLSTM_HEADS, 1, MLSTM_HEAD_DIM), _F32),
        ],
        compiler_params=_cparams(("arbitrary", "arbitrary")),
        name="mlstm",
    )(qk, qk, proj3, col, row, qk, qk, proj3, col, row)


def _sigmoid(x):
    return 1.0 / (1.0 + jnp.exp(-x))


def _merge_body(x_ref, ya_ref, hf_ref, hb_ref, o_ref, ma_ref, mb_ref, wa_ref, wb_ref, wo_ref, out_ref):
    yb = (_sigmoid(o_ref[...].astype(_F32)) * (hf_ref[...] + hb_ref[...])).astype(_BF16)
    pa = jnp.dot(ya_ref[...], wa_ref[...], preferred_element_type=_F32)
    pb = jnp.dot(yb, wb_ref[...], preferred_element_type=_F32)
    mixed = _sigmoid(ma_ref[...].astype(_F32)) * pa + _sigmoid(mb_ref[...].astype(_F32)) * pb
    out_ref[...] = x_ref[...] + jnp.dot(mixed.astype(_BF16), wo_ref[...], preferred_element_type=_F32)


def _merge(x2, ya2, hf2, hb2, proj, wa, wb, wo):
    m = x2.shape[0]
    tm = 256
    d, hw, mw = D_MODEL, HYENA_WIDTH, MLSTM_WIDTH
    const = lambda shape: pl.BlockSpec(shape, lambda i: (0, 0), pipeline_mode=pl.Buffered(1))
    return pl.pallas_call(
        _merge_body,
        grid=(m // tm,),
        in_specs=[
            pl.BlockSpec((tm, d), lambda i: (i, 0)),
            pl.BlockSpec((tm, hw), lambda i: (i, 0)),
            pl.BlockSpec((tm, mw), lambda i: (i, 0)),
            pl.BlockSpec((tm, mw), lambda i: (i, 0)),
            pl.BlockSpec((tm, mw), lambda i: (i, _COL_O)),
            pl.BlockSpec((tm, d), lambda i: (i, _COL_MA // 2)),
            pl.BlockSpec((tm, d), lambda i: (i, _COL_MB // 2)),
            const((hw, d)), const((mw, d)), const((d, d)),
        ],
        out_specs=pl.BlockSpec((tm, d), lambda i: (i, 0)),
        out_shape=jax.ShapeDtypeStruct((m, d), _F32),
        compiler_params=_cparams(("arbitrary",)),
        name="merge",
    )(x2, ya2, hf2, hb2, proj, proj, proj, wa, wb, wo)


def _ffn_body(x_ref, n2_ref, wg_ref, wu_ref, wd_ref, nf_ref, o_ref, hn_ref, acc_ref):
    f = pl.program_id(1)

    @pl.when(f == 0)
    def _():
        x = x_ref[...]
        hn = x * lax.rsqrt(jnp.mean(x * x, axis=-1, keepdims=True) + RMS_EPS) * n2_ref[...]
        hn_ref[...] = hn.astype(_BF16)
        acc_ref[...] = x

    hn = hn_ref[...]
    g = jnp.dot(hn, wg_ref[...], preferred_element_type=_F32)
    u = jnp.dot(hn, wu_ref[...], preferred_element_type=_F32)
    a = (g * _sigmoid(g) * u).astype(_BF16)
    acc_ref[...] += jnp.dot(a, wd_ref[...], preferred_element_type=_F32)

    @pl.when(f == pl.num_programs(1) - 1)
    def _():
        y = acc_ref[...]
        o_ref[...] = y * lax.rsqrt(jnp.mean(y * y, axis=-1, keepdims=True) + RMS_EPS) * nf_ref[...]


def _ffn(x2, norm2_w, w_gate_up, w_down, norm_f_w):
    m = x2.shape[0]
    tm, tf = 512, 512
    d = D_MODEL
    nf = FFN_HIDDEN // tf
    return pl.pallas_call(
        _ffn_body,
        grid=(m // tm, nf),
        in_specs=[
            pl.BlockSpec((tm, d), lambda i, f: (i, 0)),
            pl.BlockSpec((1, d), lambda i, f: (0, 0)),
            pl.BlockSpec((d, tf), lambda i, f: (0, f)),
            pl.BlockSpec((d, tf), lambda i, f: (0, nf + f)),
            pl.BlockSpec((tf, d), lambda i, f: (f, 0)),
            pl.BlockSpec((1, d), lambda i, f: (0, 0)),
        ],
        out_specs=pl.BlockSpec((tm, d), lambda i, f: (i, 0)),
        out_shape=jax.ShapeDtypeStruct((m, d), _F32),
        scratch_shapes=[pltpu.VMEM((tm, d), _BF16), pltpu.VMEM((tm, d), _F32)],
        compiler_params=_cparams(("arbitrary", "arbitrary")),
        name="ffn",
    )(x2, norm2_w, w_gate_up, w_gate_up, w_down, norm_f_w)


def kernel(x, norm1_w, w_in, conv_w, conv_b, filt_w1, filt_b1, filt_freq1, filt_w2, filt_b2, filt_freq2,
           filt_w3, hyena_bias, mlstm_gate_bias, w_branch_a, w_branch_b, w_out, norm2_w, w_gate_up, w_down,
           norm_f_w):
    b, l, d = x.shape
    assert d == D_MODEL and b % 2 == 0 and l % (_N2 * _T2_PER_PHASE) == 0
    assert norm1_w.shape[0] == 1, "single-layer block"
    hw, mw, nh = HYENA_WIDTH, MLSTM_WIDTH, MLSTM_HEADS
    m = b * l
    n = 2 * l
    n1 = n // _N2
    sc_cols = 3 * hw + 2 * mw
    g0 = sc_cols + 2 * mw

    w_in0 = w_in[0]
    w_main = jnp.concatenate([w_in0[:, g0 + 4 * nh:], w_in0[:, :g0]], axis=1).astype(_BF16)
    wg = w_in0[:, g0:g0 + 4 * nh]
    pad = lambda a: jnp.pad(a, ((0, 0), (0, 128 - a.shape[1])))
    wgi = pad(jnp.concatenate([wg[:, 0:nh], wg[:, 2 * nh:3 * nh]], axis=1))
    wgf = pad(jnp.concatenate([wg[:, nh:2 * nh], wg[:, 3 * nh:4 * nh]], axis=1))
    gb = mlstm_gate_bias[0].astype(_F32)
    bias_i = pad(jnp.concatenate([gb[0], gb[2]])[None, :])
    bias_f = pad(jnp.concatenate([gb[1], gb[3]])[None, :])

    x2 = x.reshape(m, d)
    proj, gi, gf = _inproj(x2, norm1_w[0][None, :], w_main, wgi, wgf)
    proj3 = proj.reshape(b, l, _MAIN_COLS)

    cw, cb = conv_w[0], conv_b[0][None, :]
    hy = _shortconv(proj3, cw, cb, _COL_HV * 1024, 0, 3 * hw, _F32, False)
    qk = _shortconv(proj3, cw, cb, _COL_Q * 1024, 3 * hw, 2 * mw, _BF16, True)

    fa, fa_full, fai, fb, fbi = _dft_tables(n)
    zpos = _filter_positions(l)
    w1p = jnp.pad(filt_w1[0].astype(_F32), ((0, 128 - HYENA_EMB), (0, 0)))
    max_decay = math.log(HYENA_TARGET) / HYENA_FAST_DECAY
    min_decay = math.log(HYENA_TARGET) / HYENA_SLOW_DECAY
    dabs = jnp.asarray(np.abs(np.linspace(min_decay, max_decay, hw, dtype=np.float32))[None, :])
    kern = _filt_mlp(zpos, w1p, filt_b1[0][None, :], filt_freq1[0][None, :], filt_w2[0],
                     filt_b2[0][None, :], filt_freq2[0][None, :], filt_w3[0], dabs)
    kspec = _filt_fft(kern.reshape(_N2, n1, 2 * hw), fa_full, fb)
    tabs = (fa, fai, fb, fbi)
    tiles = hw // _FFT_CT
    hy4 = hy.reshape(b, n1 // 2, _N2, 3 * hw)
    hbias = hyena_bias[0].astype(_F32)
    z4 = _hyena_conv(hy4, 0, hy4, tiles, kspec, 0, hbias[0][None, :], tabs, _F32)
    ya4 = _hyena_conv(z4, 0, hy4, 2 * tiles, kspec, tiles, hbias[1][None, :], tabs, _BF16)
    ya2 = ya4.reshape(m, hw)

    col, row = _gate_prep(gi.reshape(b, l, 128), gf.reshape(b, l, 128), bias_i, bias_f)
    hf, hb = _mlstm(qk, proj3, col, row)

    x_mid = _merge(x2, ya2, hf.reshape(m, mw), hb.reshape(m, mw), proj,
                   w_branch_a[0].astype(_BF16), w_branch_b[0].astype(_BF16), w_out[0].astype(_BF16))
    out = _ffn(x_mid, norm2_w[0][None, :], w_gate_up[0].astype(_BF16), w_down[0].astype(_BF16),
               norm_f_w[None, :])
    return out.reshape(b, l, d)
```

```python
import functools
import math

import numpy as np
import jax
import jax.numpy as jnp
from jax import lax
from jax.experimental import pallas as pl
from jax.experimental.pallas import tpu as pltpu

_F32 = jnp.float32
_BF16 = jnp.bfloat16
_HIGHEST = lax.Precision.HIGHEST

D_MODEL = 2048
HYENA_WIDTH = 1024
HYENA_EMB = 33
HYENA_FILTER_HIDDEN = 64
HYENA_FAST_DECAY = 0.3
HYENA_SLOW_DECAY = 1.5
HYENA_TARGET = 1e-2
MLSTM_WIDTH = 1024
MLSTM_HEADS = 4
MLSTM_HEAD_DIM = 256
MLSTM_CHUNK = 128
FFN_HIDDEN = 5632
RMS_EPS = 1e-6

_COL_MA, _COL_MB, _COL_HV, _COL_Q, _COL_V, _COL_O = 0, 2, 4, 7, 9, 10
_MAIN_COLS = 11 * 1024

_N2 = 32
_T2_PER_PHASE = 8
_F1_PER_PHASE = 64
_FFT_CT = 256

_VMEM_LIMIT = 56 * 1024 * 1024


def _cparams(sem, vmem=_VMEM_LIMIT):
    return pltpu.CompilerParams(dimension_semantics=sem, vmem_limit_bytes=vmem)


def _inproj_body(x_ref, nw_ref, w_ref, wgi_ref, wgf_ref, o_ref, gi_ref, gf_ref, hn_ref):
    @pl.when(pl.program_id(1) == 0)
    def _():
        x = x_ref[...]
        hn = x * lax.rsqrt(jnp.mean(x * x, axis=-1, keepdims=True) + RMS_EPS) * nw_ref[...]
        hn_ref[...] = hn.astype(_BF16)
        gi_ref[...] = jnp.dot(hn, wgi_ref[...], precision=_HIGHEST, preferred_element_type=_F32)
        gf_ref[...] = jnp.dot(hn, wgf_ref[...], precision=_HIGHEST, preferred_element_type=_F32)

    o_ref[...] = jnp.dot(hn_ref[...], w_ref[...], preferred_element_type=_F32).astype(o_ref.dtype)


def _inproj(x2, norm_w, w_main, wgi, wgf):
    m = x2.shape[0]
    tm, tn = 1024, 1024
    return pl.pallas_call(
        _inproj_body,
        grid=(m // tm, _MAIN_COLS // tn),
        in_specs=[
            pl.BlockSpec((tm, D_MODEL), lambda i, j: (i, 0)),
            pl.BlockSpec((1, D_MODEL), lambda i, j: (0, 0)),
            pl.BlockSpec((D_MODEL, tn), lambda i, j: (0, j)),
            pl.BlockSpec((D_MODEL, 128), lambda i, j: (0, 0)),
            pl.BlockSpec((D_MODEL, 128), lambda i, j: (0, 0)),
        ],
        out_specs=[
            pl.BlockSpec((tm, tn), lambda i, j: (i, j)),
            pl.BlockSpec((tm, 128), lambda i, j: (i, 0)),
            pl.BlockSpec((tm, 128), lambda i, j: (i, 0)),
        ],
        out_shape=[
            jax.ShapeDtypeStruct((m, _MAIN_COLS), _BF16),
            jax.ShapeDtypeStruct((m, 128), _F32),
            jax.ShapeDtypeStruct((m, 128), _F32),
        ],
        scratch_shapes=[pltpu.VMEM((tm, D_MODEL), _BF16)],
        compiler_params=_cparams(("arbitrary", "arbitrary")),
        name="inproj",
    )(x2, norm_w, w_main, wgi, wgf)


def _shortconv_body(u_ref, w_ref, b_ref, o_ref, *, silu_cols, k_scale, t2_major):
    u = u_ref[...].astype(_F32)
    n = u.shape[0]
    row = lax.broadcasted_iota(jnp.int32, u.shape, 0)
    up = jnp.where(row == 0, 0.0, pltpu.roll(u, 1, axis=0))
    un = jnp.where(row == n - 1, 0.0, pltpu.roll(u, n - 1, axis=0))
    w = w_ref[...]
    y = up * w[0:1, :] + b_ref[...] + u * w[1:2, :] + un * w[2:3, :]
    if silu_cols:
        y = y * (1.0 / (1.0 + jnp.exp(-y)))
        is_k = pl.program_id(1) >= (MLSTM_WIDTH // u.shape[1])
        y = y * jnp.where(is_k, k_scale, 1.0)
    if t2_major:
        rows = 8 * _N2
        for k in range(n // rows):
            blk = y[k * rows:(k + 1) * rows].reshape(8, _N2, y.shape[1])
            o_ref[:, k * 8:(k + 1) * 8, :] = jnp.swapaxes(blk, 0, 1).astype(o_ref.dtype)
    else:
        o_ref[...] = y.astype(o_ref.dtype)


def _shortconv(proj3, conv_w, conv_b, proj_col0, conv_col0, ncols, out_dtype, silu_cols, t2_major):
    b, l, _ = proj3.shape
    ct = 256
    p0, c0 = proj_col0 // ct, conv_col0 // ct
    body = functools.partial(_shortconv_body, silu_cols=silu_cols, k_scale=MLSTM_HEAD_DIM ** -0.5,
                             t2_major=t2_major)
    if t2_major:
        out_spec = pl.BlockSpec((None, _N2, l // _N2, ct), lambda i, j: (i, 0, 0, j))
        out_shape = jax.ShapeDtypeStruct((b, _N2, l // _N2, ncols), out_dtype)
    else:
        out_spec = pl.BlockSpec((None, l, ct), lambda i, j: (i, 0, j))
        out_shape = jax.ShapeDtypeStruct((b, l, ncols), out_dtype)
    return pl.pallas_call(
        body,
        grid=(b, ncols // ct),
        in_specs=[
            pl.BlockSpec((None, l, ct), lambda i, j: (i, 0, p0 + j)),
            pl.BlockSpec((3, ct), lambda i, j: (0, c0 + j)),
            pl.BlockSpec((1, ct), lambda i, j: (0, c0 + j)),
        ],
        out_specs=out_spec,
        out_shape=out_shape,
        compiler_params=_cparams(("arbitrary", "arbitrary")),
        name="shortconv_silu" if silu_cols else "shortconv",
    )(proj3, conv_w, conv_b)


def _dft_tables(n):
    n2 = _N2
    n1 = n // n2
    h1 = n1 // 2
    f1 = np.arange(n1)[:, None]
    t1 = np.arange(n1)[None, :]
    ang = 2.0 * np.pi * ((f1 * t1) % n1) / n1
    c, s = np.cos(ang), np.sin(ang)
    ch, sh = c[:, :h1], s[:, :h1]
    fa = np.block([[ch, sh], [-sh, ch]])
    fa_full = np.concatenate([c, -s], axis=0)
    fai = np.block([[ch.T, -sh.T], [sh.T, ch.T]]) / n
    f1v = np.arange(n1)[:, None, None]
    f2 = np.arange(n2)[None, :, None]
    t2 = np.arange(n2)[None, None, :]
    ph = 2.0 * np.pi * ((f1v * t2 + f2 * t2 * n1) % n) / n
    tc, ts = np.cos(ph), np.sin(ph)
    fb = np.concatenate([np.concatenate([tc, ts], axis=2),
                         np.concatenate([-ts, tc], axis=2)], axis=1)
    tct, tst = np.swapaxes(tc, 1, 2), np.swapaxes(ts, 1, 2)
    fbi = np.concatenate([np.concatenate([tct, -tst], axis=2),
                          np.concatenate([tst, tct], axis=2)], axis=1)
    to = lambda a: jnp.asarray(a.astype(np.float32)).astype(_BF16)
    return to(fa), to(fa_full), to(fai), to(fb), to(fbi)


def _filter_positions(l):
    n = 2 * l
    n1 = n // _N2
    r = (np.arange(_N2)[:, None] + _N2 * np.arange(n1)[None, :]).reshape(-1)
    pos = np.where(r < l, r, n - r)
    pos = np.where(r == l, 0, pos)
    bands = (HYENA_EMB - 1) // 2
    tt = np.linspace(0.0, 1.0, l, dtype=np.float32).astype(np.float64)[pos]
    omega = (2.0 * math.pi * np.arange(l, dtype=np.float32) / l).astype(np.float32)
    freqs = np.linspace(1e-4, bands - 1, bands, dtype=np.float32)
    ang = (omega[:, None] * freqs[None, :]).astype(np.float64)[pos]
    z = np.zeros((n, 128), np.float64)
    z[:, 0] = tt
    z[:, 1:1 + bands] = np.cos(ang)
    z[:, 1 + bands:1 + 2 * bands] = -np.sin(ang)
    z[:, 33] = (r != l)
    z[:, 34] = (r < l)
    return jnp.asarray(z.astype(np.float32))


def _filt_mlp_body(z_ref, w1_ref, b1_ref, q1_ref, w2_ref, b2_ref, q2_ref, w3f_ref, w3b_ref, dabs_ref,
                   o_ref, hf_ref, hb_ref):
    @pl.when(pl.program_id(1) == 0)
    def _():
        z = z_ref[...]
        h = jnp.sin(q1_ref[...] * (jnp.dot(z, w1_ref[...], precision=_HIGHEST,
                                           preferred_element_type=_F32) + b1_ref[...]))
        h = jnp.sin(q2_ref[...] * (jnp.dot(h, w2_ref[...], precision=_HIGHEST,
                                           preferred_element_type=_F32) + b2_ref[...]))
        valid = z[:, 33:34]
        fwd = z[:, 34:35]
        hf_ref[...] = h * (valid * fwd)
        hb_ref[...] = h * (valid * (1.0 - fwd))

    raw = (jnp.dot(hf_ref[...], w3f_ref[...], precision=_HIGHEST, preferred_element_type=_F32)
           + jnp.dot(hb_ref[...], w3b_ref[...], precision=_HIGHEST, preferred_element_type=_F32))
    window = jnp.exp(-z_ref[:, 0:1] * dabs_ref[...])
    o_ref[...] = raw * window


def _filt_mlp(zpos, w1p, b1, q1, w2, b2, q2, w3, dabs):
    n = zpos.shape[0]
    ct = _FFT_CT
    tr = 1024
    per_order = HYENA_WIDTH // ct
    fh = HYENA_FILTER_HIDDEN
    full = lambda shape: pl.BlockSpec(shape, lambda r, i: (0,) * len(shape))
    return pl.pallas_call(
        _filt_mlp_body,
        grid=(n // tr, 2 * per_order),
        in_specs=[
            pl.BlockSpec((tr, 128), lambda r, i: (r, 0)),
            full((128, fh)), full((1, fh)), full((1, fh)),
            full((fh, fh)), full((1, fh)), full((1, fh)),
            pl.BlockSpec((fh, ct), lambda r, i: (0, (i // per_order) * 2 * per_order + i % per_order)),
            pl.BlockSpec((fh, ct), lambda r, i: (0, (i // per_order) * 2 * per_order + per_order + i % per_order)),
            pl.BlockSpec((1, ct), lambda r, i: (0, i % per_order)),
        ],
        out_specs=pl.BlockSpec((tr, ct), lambda r, i: (r, i)),
        out_shape=jax.ShapeDtypeStruct((n, 2 * HYENA_WIDTH), _F32),
        scratch_shapes=[pltpu.VMEM((tr, fh), _F32), pltpu.VMEM((tr, fh), _F32)],
        compiler_params=_cparams(("arbitrary", "arbitrary")),
        name="filt_mlp",
    )(zpos, w1p, b1, q1, w2, b2, q2, w3, w3, dabs)


_F1_GROUP = 8


def _stage_b_operands(s_ref, f0, n1):
    tre = jnp.swapaxes(s_ref[:, pl.ds(f0, _F1_GROUP), :], 0, 1)
    tim = jnp.swapaxes(s_ref[:, pl.ds(n1 + f0, _F1_GROUP), :], 0, 1)
    return [jnp.concatenate([tre[k], tim[k]], axis=0).astype(_BF16) for k in range(_F1_GROUP)]


def _filt_fft_body(kern_ref, fa_ref, fb_ref, o_ref, s_ref, *, n_in, n1, f1pp):
    s = pl.program_id(1)

    @pl.when(s < n_in)
    def _():
        for i in range(_T2_PER_PHASE):
            u = kern_ref[i].astype(_BF16)
            s_ref[s * _T2_PER_PHASE + i] = jnp.dot(fa_ref[...], u, preferred_element_type=_F32)

    @pl.when(s >= n_in)
    def _():
        def body(g, carry):
            fl0 = g * _F1_GROUP
            f0 = pl.multiple_of((s - n_in) * f1pp + fl0, _F1_GROUP)
            for k, v in enumerate(_stage_b_operands(s_ref, f0, n1)):
                row = pl.multiple_of((fl0 + k) * 2 * _N2, 2 * _N2)
                o_ref[pl.ds(row, 2 * _N2), :] = jnp.dot(fb_ref[fl0 + k], v, preferred_element_type=_F32)
            return carry

        lax.fori_loop(0, f1pp // _F1_GROUP, body, 0)


def _filt_fft(kern3, fa_full, fb):
    n2, n1, cols = kern3.shape
    ct = _FFT_CT
    n_in = n2 // _T2_PER_PHASE
    f1pp = min(_F1_PER_PHASE, n1)
    n_mid = n1 // f1pp
    body = functools.partial(_filt_fft_body, n_in=n_in, n1=n1, f1pp=f1pp)
    return pl.pallas_call(
        body,
        grid=(cols // ct, n_in + n_mid),
        in_specs=[
            pl.BlockSpec((_T2_PER_PHASE, n1, ct), lambda j, s: (jnp.minimum(s, n_in - 1), 0, j)),
            pl.BlockSpec((2 * n1, n1), lambda j, s: (0, 0)),
            pl.BlockSpec((f1pp, 2 * n2, 2 * n2), lambda j, s: (jnp.maximum(s - n_in, 0), 0, 0)),
        ],
        out_specs=pl.BlockSpec((f1pp * 2 * n2, ct), lambda j, s: (jnp.maximum(s - n_in, 0), j)),
        out_shape=jax.ShapeDtypeStruct((n1 * 2 * n2, cols), _F32),
        scratch_shapes=[pltpu.VMEM((n2, 2 * n1, ct), _F32)],
        compiler_params=_cparams(("arbitrary", "arbitrary")),
        name="filt_fft",
    )(kern3, fa_full, fb)


def _hyena_conv_body(u_ref, u2_ref, g_ref, k_ref, bias_ref, fa_ref, fai_ref, fb_ref, fbi_ref,
                     o_ref, s_ref, *, n_in, n_mid, n1, f1pp, natural_out):
    s = pl.program_id(2)
    h1 = n1 // 2

    @pl.when(s < n_in)
    def _():
        for i in range(_T2_PER_PHASE):
            u = jnp.concatenate([u_ref[0, i], u_ref[1, i]], axis=0).astype(_BF16)
            s_ref[s * _T2_PER_PHASE + i] = jnp.dot(fa_ref[...], u, preferred_element_type=_F32)

    @pl.when(jnp.logical_and(s >= n_in, s < n_in + n_mid))
    def _():
        def body(g, carry):
            fl0 = g * _F1_GROUP
            f0 = pl.multiple_of((s - n_in) * f1pp + fl0, _F1_GROUP)
            wre, wim = [], []
            for k, v in enumerate(_stage_b_operands(s_ref, f0, n1)):
                x = jnp.dot(fb_ref[fl0 + k], v, preferred_element_type=_F32)
                xre, xim = x[:_N2], x[_N2:]
                row = pl.multiple_of((fl0 + k) * 2 * _N2, 2 * _N2)
                kre = k_ref[pl.ds(row, _N2), :]
                kim = k_ref[pl.ds(row + _N2, _N2), :]
                y = jnp.concatenate([xre * kre - xim * kim, xre * kim + xim * kre], axis=0).astype(_BF16)
                w = jnp.dot(fbi_ref[fl0 + k], y, preferred_element_type=_F32)
                wre.append(w[:_N2])
                wim.append(w[_N2:])
            s_ref[:, pl.ds(f0, _F1_GROUP), :] = jnp.swapaxes(jnp.stack(wre, axis=0), 0, 1)
            s_ref[:, pl.ds(n1 + f0, _F1_GROUP), :] = jnp.swapaxes(jnp.stack(wim, axis=0), 0, 1)
            return carry

        lax.fori_loop(0, f1pp // _F1_GROUP, body, 0)

    @pl.when(s >= n_in + n_mid)
    def _():
        bias = bias_ref[...]
        outs = [[], []]
        for i in range(_T2_PER_PHASE):
            bv = s_ref[(s - n_in - n_mid) * _T2_PER_PHASE + i].astype(_BF16)
            y = jnp.dot(fai_ref[...], bv, preferred_element_type=_F32)
            for m in range(2):
                res = g_ref[m, i] * (y[m * h1:(m + 1) * h1] + u2_ref[m, i] * bias)
                if natural_out:
                    outs[m].append(res)
                else:
                    o_ref[m, i] = res.astype(o_ref.dtype)
        if natural_out:
            for m in range(2):
                o_ref[m] = jnp.swapaxes(jnp.stack(outs[m], axis=0), 0, 1).astype(o_ref.dtype)


def _hyena_conv(u4, ucol, g4, gcol, kspec, kcol, bias, tabs, natural_out):
    fa, fai, fb, fbi = tabs
    b, n2, h1, _ = u4.shape
    n1 = 2 * h1
    ct = _FFT_CT
    n_in = n2 // _T2_PER_PHASE
    f1pp = min(_F1_PER_PHASE, n1)
    n_mid = n1 // f1pp
    n_out = n_in
    tiles = HYENA_WIDTH // ct
    body = functools.partial(_hyena_conv_body, n_in=n_in, n_mid=n_mid, n1=n1, f1pp=f1pp,
                             natural_out=natural_out)
    mid = lambda s: jnp.clip(s - n_in, 0, n_mid - 1)
    last = lambda s: jnp.clip(s - n_in - n_mid, 0, n_out - 1)
    blk = (2, _T2_PER_PHASE, h1, ct)
    if natural_out:
        out_spec = pl.BlockSpec((2, h1, _T2_PER_PHASE, ct), lambda j, p, s: (p, 0, last(s), j))
        out_shape = jax.ShapeDtypeStruct((b, h1, n2, HYENA_WIDTH), _F32)
    else:
        out_spec = pl.BlockSpec(blk, lambda j, p, s: (p, last(s), 0, j))
        out_shape = jax.ShapeDtypeStruct((b, n2, h1, HYENA_WIDTH), _F32)
    return pl.pallas_call(
        body,
        grid=(tiles, b // 2, n_in + n_mid + n_out),
        in_specs=[
            pl.BlockSpec(blk, lambda j, p, s: (p, jnp.minimum(s, n_in - 1), 0, ucol + j)),
            pl.BlockSpec(blk, lambda j, p, s: (p, last(s), 0, ucol + j)),
            pl.BlockSpec(blk, lambda j, p, s: (p, last(s), 0, gcol + j)),
            pl.BlockSpec((f1pp * 2 * n2, ct), lambda j, p, s: (mid(s), kcol + j)),
            pl.BlockSpec((1, ct), lambda j, p, s: (0, j)),
            pl.BlockSpec((2 * n1, n1), lambda j, p, s: (0, 0)),
            pl.BlockSpec((n1, 2 * n1), lambda j, p, s: (0, 0)),
            pl.BlockSpec((f1pp, 2 * n2, 2 * n2), lambda j, p, s: (mid(s), 0, 0)),
            pl.BlockSpec((f1pp, 2 * n2, 2 * n2), lambda j, p, s: (mid(s), 0, 0)),
        ],
        out_specs=out_spec,
        out_shape=out_shape,
        scratch_shapes=[pltpu.VMEM((n2, 2 * n1, ct), _F32)],
        compiler_params=_cparams(("arbitrary", "arbitrary", "arbitrary")),
        name="hyena_conv",
    )(u4, u4, g4, kspec, bias, fa, fai, fb, fbi)


def _gate_prep_body(gi_ref, gf_ref, bi_ref, bf_ref, col_ref, row_ref,
                    b_scr, a_scr, tot_scr, mloc_scr, mprev_scr, *, nc):
    ch = MLSTM_CHUNK
    lane = lax.broadcasted_iota(jnp.int32, (1, 128), 1)
    is_fwd = lane < MLSTM_HEADS
    live = lane < 2 * MLSTM_HEADS
    jj = lax.broadcasted_iota(jnp.int32, (ch, ch), 0)
    ss = lax.broadcasted_iota(jnp.int32, (ch, ch), 1)
    t_lo = (ss <= jj).astype(_F32)
    t_up = (ss >= jj).astype(_F32)

    def chunk_stats(c, carry):
        r0 = pl.multiple_of(c * ch, ch)
        gf = gf_ref[pl.ds(r0, ch), :] + bf_ref[...]
        gi = gi_ref[pl.ds(r0, ch), :] + bi_ref[...]
        logf = jnp.minimum(gf, 0.0) - jnp.log1p(jnp.exp(-jnp.abs(gf)))
        cs_lo = jnp.dot(t_lo, logf, precision=_HIGHEST, preferred_element_type=_F32)
        cs_up = jnp.dot(t_up, logf, precision=_HIGHEST, preferred_element_type=_F32)
        bcs = jnp.where(is_fwd, cs_lo, cs_up)
        tot = cs_lo[ch - 1:ch, :]
        a = tot - bcs + gi
        b_scr[pl.ds(r0, ch), :] = bcs
        a_scr[pl.ds(r0, ch), :] = a
        tot_scr[pl.ds(c, 1), :] = tot
        mloc_scr[pl.ds(c, 1), :] = jnp.max(a, axis=0, keepdims=True)
        row_ref[:, pl.ds(r0, ch)] = (gi - bcs).T[0:8, :]
        return carry

    lax.fori_loop(0, nc, chunk_stats, 0)

    def scan_f(c, m):
        mprev_scr[pl.ds(c, 1), :] = jnp.where(is_fwd, m, mprev_scr[pl.ds(c, 1), :])
        return jnp.maximum(tot_scr[pl.ds(c, 1), :] + m, mloc_scr[pl.ds(c, 1), :])

    def scan_b(i, m):
        c = nc - 1 - i
        mprev_scr[pl.ds(c, 1), :] = jnp.where(is_fwd, mprev_scr[pl.ds(c, 1), :], m)
        return jnp.maximum(tot_scr[pl.ds(c, 1), :] + m, mloc_scr[pl.ds(c, 1), :])

    mprev_scr[...] = jnp.zeros_like(mprev_scr)
    lax.fori_loop(0, nc, scan_f, jnp.zeros((1, 128), _F32))
    lax.fori_loop(0, nc, scan_b, jnp.zeros((1, 128), _F32))

    def emit(c, carry):
        r0 = pl.multiple_of(c * ch, ch)
        mprev = mprev_scr[pl.ds(c, 1), :]
        tot = tot_scr[pl.ds(c, 1), :]
        mnew = jnp.maximum(tot + mprev, mloc_scr[pl.ds(c, 1), :])
        w = jnp.exp(a_scr[pl.ds(r0, ch), :] - mnew)
        keep = jnp.exp(tot + mprev - mnew)
        zero = jnp.zeros((ch, 128), _F32)
        pack = (jnp.where(live, b_scr[pl.ds(r0, ch), :], 0.0)
                + pltpu.roll(jnp.where(live, w, 0.0), 8, axis=1)
                + pltpu.roll(jnp.where(live, mprev + zero, 0.0), 16, axis=1)
                + pltpu.roll(jnp.where(live, keep + zero, 0.0), 24, axis=1))
        col_ref[pl.ds(r0, ch), :] = pack
        return carry

    lax.fori_loop(0, nc, emit, 0)


def _gate_prep(gi3, gf3, bias_i, bias_f):
    b, l, _ = gi3.shape
    nc = l // MLSTM_CHUNK
    body = functools.partial(_gate_prep_body, nc=nc)
    return pl.pallas_call(
        body,
        grid=(b,),
        in_specs=[
            pl.BlockSpec((None, l, 128), lambda i: (i, 0, 0)),
            pl.BlockSpec((None, l, 128), lambda i: (i, 0, 0)),
            pl.BlockSpec((1, 128), lambda i: (0, 0)),
            pl.BlockSpec((1, 128), lambda i: (0, 0)),
        ],
        out_specs=[
            pl.BlockSpec((None, l, 128), lambda i: (i, 0, 0)),
            pl.BlockSpec((None, 8, l), lambda i: (i, 0, 0)),
        ],
        out_shape=[
            jax.ShapeDtypeStruct((b, l, 128), _F32),
            jax.ShapeDtypeStruct((b, 8, l), _F32),
        ],
        scratch_shapes=[
            pltpu.VMEM((l, 128), _F32), pltpu.VMEM((l, 128), _F32),
            pltpu.VMEM((nc, 128), _F32), pltpu.VMEM((nc, 128), _F32), pltpu.VMEM((nc, 128), _F32),
        ],
        compiler_params=_cparams(("arbitrary",)),
        name="gate_prep",
    )(gi3, gf3, bias_i, bias_f)


def _mlstm_body(qf_ref, kf_ref, vf_ref, colf_ref, rowf_ref, qb_ref, kb_ref, vb_ref, colb_ref, rowb_ref,
                hf_ref, hb_ref, ct_ref, n_ref):
    @pl.when(pl.program_id(1) == 0)
    def _():
        ct_ref[...] = jnp.zeros_like(ct_ref)
        n_ref[...] = jnp.zeros_like(n_ref)

    ch, dh = MLSTM_CHUNK, MLSTM_HEAD_DIM
    jj = lax.broadcasted_iota(jnp.int32, (ch, ch), 0)
    ss = lax.broadcasted_iota(jnp.int32, (ch, ch), 1)
    dirs = ((qf_ref, kf_ref, vf_ref, colf_ref, rowf_ref, hf_ref, ss <= jj),
            (qb_ref, kb_ref, vb_ref, colb_ref, rowb_ref, hb_ref, ss >= jj))
    for d, (q_ref, k_ref, v_ref, col_ref, row_ref, o_ref, mask) in enumerate(dirs):
        col = col_ref[...]
        for h in range(MLSTM_HEADS):
            hd = d * MLSTM_HEADS + h
            bj = col[:, hd:hd + 1]
            wj = col[:, 8 + hd:9 + hd]
            mp = col[:, 16 + hd:17 + hd]
            keep = col[0:1, 24 + hd:25 + hd]
            r = row_ref[hd:hd + 1, :]
            lo, hi = h * dh, (h + 1) * dh
            qh, kh, vh = q_ref[:, lo:hi], k_ref[:, lo:hi], v_ref[:, lo:hi]
            sc = lax.dot_general(qh, kh, (((1,), (1,)), ((), ())), preferred_element_type=_F32)
            dl = jnp.where(mask, bj + r, -jnp.inf)
            inter = bj + mp
            mj = jnp.maximum(inter, jnp.max(dl, axis=1, keepdims=True))
            p = sc * jnp.exp(dl - mj)
            iw = jnp.exp(inter - mj)
            ct = ct_ref[hd]
            nv = n_ref[hd]
            num = (jnp.dot(p.astype(_BF16), vh, preferred_element_type=_F32)
                   + iw * jnp.dot(qh, ct.astype(_BF16), preferred_element_type=_F32))
            den = (jnp.sum(p, axis=1, keepdims=True)
                   + iw * jnp.sum(qh.astype(_F32) * nv, axis=1, keepdims=True))
            o_ref[:, lo:hi] = num / jnp.maximum(jnp.abs(den), jnp.exp(-mj))
            vw = (vh.astype(_F32) * wj).astype(_BF16)
            ct_ref[hd] = keep * ct + lax.dot_general(kh, vw, (((0,), (0,)), ((), ())),
                                                     preferred_element_type=_F32)
            n_ref[hd] = keep * nv + jnp.sum(kh.astype(_F32) * wj, axis=0, keepdims=True)


def _mlstm(qk, proj3, col, row):
    b, l, _ = qk.shape
    ch, mw = MLSTM_CHUNK, MLSTM_WIDTH
    nc = l // ch
    fw = lambda c: c
    bw = lambda c: nc - 1 - c

    def specs(cm):
        return [
            pl.BlockSpec((None, ch, mw), lambda i, c: (i, cm(c), 0)),
            pl.BlockSpec((None, ch, mw), lambda i, c: (i, cm(c), 1)),
            pl.BlockSpec((None, ch, mw), lambda i, c: (i, cm(c), _COL_V)),
            pl.BlockSpec((None, ch, 128), lambda i, c: (i, cm(c), 0)),
            pl.BlockSpec((None, 8, ch), lambda i, c: (i, 0, cm(c))),
        ]

    return pl.pallas_call(
        _mlstm_body,
        grid=(b, nc),
        in_specs=specs(fw) + specs(bw),
        out_specs=[
            pl.BlockSpec((None, ch, mw), lambda i, c: (i, c, 0)),
            pl.BlockSpec((None, ch, mw), lambda i, c: (i, nc - 1 - c, 0)),
        ],
        out_shape=[jax.ShapeDtypeStruct((b, l, mw), _F32), jax.ShapeDtypeStruct((b, l, mw), _F32)],
        scratch_shapes=[
            pltpu.VMEM((2 * MLSTM_HEADS, MLSTM_HEAD_DIM, MLSTM_HEAD_DIM), _F32),
            pltpu.VMEM((2 * MLSTM_HEADS, 1, MLSTM_HEAD_DIM), _F32),
        ],
        compiler_params=_cparams(("arbitrary", "arbitrary")),
        name="mlstm",
    )(qk, qk, proj3, col, row, qk, qk, proj3, col, row)


def _sigmoid(x):
    return 1.0 / (1.0 + jnp.exp(-x))


def _merge_body(x_ref, ya_ref, hf_ref, hb_ref, o_ref, ma_ref, mb_ref, wa_ref, wb_ref, wo_ref, out_ref):
    yb = (_sigmoid(o_ref[...].astype(_F32)) * (hf_ref[...] + hb_ref[...])).astype(_BF16)
    pa = jnp.dot(ya_ref[...].astype(_BF16), wa_ref[...], preferred_element_type=_F32)
    pb = jnp.dot(yb, wb_ref[...], preferred_element_type=_F32)
    mixed = _sigmoid(ma_ref[...].astype(_F32)) * pa + _sigmoid(mb_ref[...].astype(_F32)) * pb
    out_ref[...] = x_ref[...] + jnp.dot(mixed.astype(_BF16), wo_ref[...], preferred_element_type=_F32)


def _merge(x2, ya2, hf2, hb2, proj, wa, wb, wo):
    m = x2.shape[0]
    tm = 256
    d, hw, mw = D_MODEL, HYENA_WIDTH, MLSTM_WIDTH
    const = lambda shape: pl.BlockSpec(shape, lambda i: (0, 0), pipeline_mode=pl.Buffered(1))
    return pl.pallas_call(
        _merge_body,
        grid=(m // tm,),
        in_specs=[
            pl.BlockSpec((tm, d), lambda i: (i, 0)),
            pl.BlockSpec((tm, hw), lambda i: (i, 0)),
            pl.BlockSpec((tm, mw), lambda i: (i, 0)),
            pl.BlockSpec((tm, mw), lambda i: (i, 0)),
            pl.BlockSpec((tm, mw), lambda i: (i, _COL_O)),
            pl.BlockSpec((tm, d), lambda i: (i, _COL_MA // 2)),
            pl.BlockSpec((tm, d), lambda i: (i, _COL_MB // 2)),
            const((hw, d)), const((mw, d)), const((d, d)),
        ],
        out_specs=pl.BlockSpec((tm, d), lambda i: (i, 0)),
        out_shape=jax.ShapeDtypeStruct((m, d), _F32),
        compiler_params=_cparams(("arbitrary",)),
        name="merge",
    )(x2, ya2, hf2, hb2, proj, proj, proj, wa, wb, wo)


def _ffn_body(x_ref, n2_ref, wg_ref, wu_ref, wd_ref, nf_ref, o_ref, hn_ref, acc_ref):
    f = pl.program_id(1)

    @pl.when(f == 0)
    def _():
        x = x_ref[...]
        hn = x * lax.rsqrt(jnp.mean(x * x, axis=-1, keepdims=True) + RMS_EPS) * n2_ref[...]
        hn_ref[...] = hn.astype(_BF16)
        acc_ref[...] = x

    hn = hn_ref[...]
    g = jnp.dot(hn, wg_ref[...], preferred_element_type=_F32)
    u = jnp.dot(hn, wu_ref[...], preferred_element_type=_F32)
    a = (g * _sigmoid(g) * u).astype(_BF16)
    acc_ref[...] += jnp.dot(a, wd_ref[...], preferred_element_type=_F32)

    @pl.when(f == pl.num_programs(1) - 1)
    def _():
        y = acc_ref[...]
        o_ref[...] = y * lax.rsqrt(jnp.mean(y * y, axis=-1, keepdims=True) + RMS_EPS) * nf_ref[...]


def _ffn(x2, norm2_w, w_gate_up, w_down, norm_f_w):
    m = x2.shape[0]
    tm, tf = 512, 512
    d = D_MODEL
    nf = FFN_HIDDEN // tf
    return pl.pallas_call(
        _ffn_body,
        grid=(m // tm, nf),
        in_specs=[
            pl.BlockSpec((tm, d), lambda i, f: (i, 0)),
            pl.BlockSpec((1, d), lambda i, f: (0, 0)),
            pl.BlockSpec((d, tf), lambda i, f: (0, f)),
            pl.BlockSpec((d, tf), lambda i, f: (0, nf + f)),
            pl.BlockSpec((tf, d), lambda i, f: (f, 0)),
            pl.BlockSpec((1, d), lambda i, f: (0, 0)),
        ],
        out_specs=pl.BlockSpec((tm, d), lambda i, f: (i, 0)),
        out_shape=jax.ShapeDtypeStruct((m, d), _F32),
        scratch_shapes=[pltpu.VMEM((tm, d), _BF16), pltpu.VMEM((tm, d), _F32)],
        compiler_params=_cparams(("arbitrary", "arbitrary")),
        name="ffn",
    )(x2, norm2_w, w_gate_up, w_gate_up, w_down, norm_f_w)


def kernel(x, norm1_w, w_in, conv_w, conv_b, filt_w1, filt_b1, filt_freq1, filt_w2, filt_b2, filt_freq2,
           filt_w3, hyena_bias, mlstm_gate_bias, w_branch_a, w_branch_b, w_out, norm2_w, w_gate_up, w_down,
           norm_f_w):
    b, l, d = x.shape
    assert d == D_MODEL and b % 2 == 0 and l % (_N2 * _T2_PER_PHASE) == 0
    assert norm1_w.shape[0] == 1, "single-layer block"
    hw, mw, nh = HYENA_WIDTH, MLSTM_WIDTH, MLSTM_HEADS
    m = b * l
    n = 2 * l
    n1 = n // _N2
    sc_cols = 3 * hw + 2 * mw
    g0 = sc_cols + 2 * mw

    w_in0 = w_in[0]
    w_main = jnp.concatenate([w_in0[:, g0 + 4 * nh:], w_in0[:, :g0]], axis=1).astype(_BF16)
    wg = w_in0[:, g0:g0 + 4 * nh]
    pad = lambda a: jnp.pad(a, ((0, 0), (0, 128 - a.shape[1])))
    wgi = pad(jnp.concatenate([wg[:, 0:nh], wg[:, 2 * nh:3 * nh]], axis=1))
    wgf = pad(jnp.concatenate([wg[:, nh:2 * nh], wg[:, 3 * nh:4 * nh]], axis=1))
    gb = mlstm_gate_bias[0].astype(_F32)
    bias_i = pad(jnp.concatenate([gb[0], gb[2]])[None, :])
    bias_f = pad(jnp.concatenate([gb[1], gb[3]])[None, :])

    x2 = x.reshape(m, d)
    proj, gi, gf = _inproj(x2, norm1_w[0][None, :], w_main, wgi, wgf)
    proj3 = proj.reshape(b, l, _MAIN_COLS)

    cw, cb = conv_w[0], conv_b[0][None, :]
    hy4 = _shortconv(proj3, cw, cb, _COL_HV * 1024, 0, 3 * hw, _F32, False, True)
    qk = _shortconv(proj3, cw, cb, _COL_Q * 1024, 3 * hw, 2 * mw, _BF16, True, False)

    fa, fa_full, fai, fb, fbi = _dft_tables(n)
    zpos = _filter_positions(l)
    w1p = jnp.pad(filt_w1[0].astype(_F32), ((0, 128 - HYENA_EMB), (0, 0)))
    max_decay = math.log(HYENA_TARGET) / HYENA_FAST_DECAY
    min_decay = math.log(HYENA_TARGET) / HYENA_SLOW_DECAY
    dabs = jnp.asarray(np.abs(np.linspace(min_decay, max_decay, hw, dtype=np.float32))[None, :])
    kern = _filt_mlp(zpos, w1p, filt_b1[0][None, :], filt_freq1[0][None, :], filt_w2[0],
                     filt_b2[0][None, :], filt_freq2[0][None, :], filt_w3[0], dabs)
    kspec = _filt_fft(kern.reshape(_N2, n1, 2 * hw), fa_full, fb)
    tabs = (fa, fai, fb, fbi)
    tiles = hw // _FFT_CT
    hbias = hyena_bias[0].astype(_F32)
    z4 = _hyena_conv(hy4, 0, hy4, tiles, kspec, 0, hbias[0][None, :], tabs, False)
    ya4 = _hyena_conv(z4, 0, hy4, 2 * tiles, kspec, tiles, hbias[1][None, :], tabs, True)
    ya2 = ya4.reshape(m, hw)

    col, row = _gate_prep(gi.reshape(b, l, 128), gf.reshape(b, l, 128), bias_i, bias_f)
    hf, hb = _mlstm(qk, proj3, col, row)

    x_mid = _merge(x2, ya2, hf.reshape(m, mw), hb.reshape(m, mw), proj,
                   w_branch_a[0].astype(_BF16), w_branch_b[0].astype(_BF16), w_out[0].astype(_BF16))
    out = _ffn(x_mid, norm2_w[0][None, :], w_gate_up[0].astype(_BF16), w_down[0].astype(_BF16),
               norm_f_w[None, :])
    return out.reshape(b, l, d)
```

```python
import functools
import math

import numpy as np
import jax
import jax.numpy as jnp
from jax import lax
from jax.experimental import pallas as pl
from jax.experimental.pallas import tpu as pltpu

_F32 = jnp.float32
_BF16 = jnp.bfloat16
_HIGHEST = lax.Precision.HIGHEST

D_MODEL = 2048
HYENA_WIDTH = 1024
HYENA_EMB = 33
HYENA_FILTER_HIDDEN = 64
HYENA_FAST_DECAY = 0.3
HYENA_SLOW_DECAY = 1.5
HYENA_TARGET = 1e-2
MLSTM_WIDTH = 1024
MLSTM_HEADS = 4
MLSTM_HEAD_DIM = 256
MLSTM_CHUNK = 128
FFN_HIDDEN = 5632
RMS_EPS = 1e-6

_COL_MA, _COL_MB, _COL_HV, _COL_Q, _COL_V, _COL_O = 0, 2, 4, 7, 9, 10
_MAIN_COLS = 11 * 1024

_N2 = 32
_T2_PER_PHASE = 8
_F1_PER_PHASE = 64
_FFT_CT = 256

_VMEM_LIMIT = 56 * 1024 * 1024


def _cparams(sem, vmem=_VMEM_LIMIT):
    return pltpu.CompilerParams(dimension_semantics=sem, vmem_limit_bytes=vmem)


def _split_bf16(a):
    hi = a.astype(_BF16)
    lo = (a - hi.astype(_F32)).astype(_BF16)
    return hi, lo


def _dot3(a_hi, a_lo, b_hi, b_lo):
    d = functools.partial(jnp.dot, preferred_element_type=_F32)
    return d(a_hi, b_hi) + (d(a_hi, b_lo) + d(a_lo, b_hi))


def _inproj_body(x_ref, nw_ref, wm_ref, w_ref, wgh_ref, wgl_ref, o_ref, g_ref, hn_ref, *, n_merge):
    j = pl.program_id(1)

    @pl.when(j == 0)
    def _():
        rows = 256
        for r in range(0, x_ref.shape[0], rows):
            x = x_ref[r:r + rows, :]
            hn = x * lax.rsqrt(jnp.mean(x * x, axis=-1, keepdims=True) + RMS_EPS) * nw_ref[...]
            hn_hi, hn_lo = _split_bf16(hn)
            hn_ref[r:r + rows, :] = hn_hi
            g_ref[r:r + rows, :] = _dot3(hn_hi, hn_lo, wgh_ref[...], wgl_ref[...])

    @pl.when(j < n_merge)
    def _():
        o_ref[...] = jnp.dot(hn_ref[...], wm_ref[...], preferred_element_type=_F32).astype(o_ref.dtype)

    @pl.when(j >= n_merge)
    def _():
        o_ref[...] = jnp.dot(hn_ref[...], w_ref[...], preferred_element_type=_F32).astype(o_ref.dtype)


def _inproj(x2, norm_w, w_merge, w_first, wg_hi, wg_lo):
    m = x2.shape[0]
    tm, tn = 1024, 1024
    n_merge = w_merge.shape[1] // tn
    body = functools.partial(_inproj_body, n_merge=n_merge)
    return pl.pallas_call(
        body,
        grid=(m // tm, _MAIN_COLS // tn),
        in_specs=[
            pl.BlockSpec((tm, D_MODEL), lambda i, j: (i, 0)),
            pl.BlockSpec((1, D_MODEL), lambda i, j: (0, 0)),
            pl.BlockSpec((D_MODEL, tn), lambda i, j: (0, jnp.minimum(j, n_merge - 1))),
            pl.BlockSpec((D_MODEL, tn), lambda i, j: (0, jnp.maximum(j - n_merge, 0))),
            pl.BlockSpec((D_MODEL, 128), lambda i, j: (0, 0)),
            pl.BlockSpec((D_MODEL, 128), lambda i, j: (0, 0)),
        ],
        out_specs=[
            pl.BlockSpec((tm, tn), lambda i, j: (i, j)),
            pl.BlockSpec((tm, 128), lambda i, j: (i, 0)),
        ],
        out_shape=[
            jax.ShapeDtypeStruct((m, _MAIN_COLS), _BF16),
            jax.ShapeDtypeStruct((m, 128), _F32),
        ],
        scratch_shapes=[pltpu.VMEM((tm, D_MODEL), _BF16)],
        compiler_params=_cparams(("arbitrary", "arbitrary")),
        name="inproj",
    )(x2, norm_w, w_merge, w_first, wg_hi, wg_lo)


def _shortconv_body(u_ref, w_ref, b_ref, o_ref, *, silu_cols, k_scale, t2_major):
    u = u_ref[...].astype(_F32)
    n = u.shape[0]
    row = lax.broadcasted_iota(jnp.int32, u.shape, 0)
    up = jnp.where(row == 0, 0.0, pltpu.roll(u, 1, axis=0))
    un = jnp.where(row == n - 1, 0.0, pltpu.roll(u, n - 1, axis=0))
    w = w_ref[...]
    y = up * w[0:1, :] + b_ref[...] + u * w[1:2, :] + un * w[2:3, :]
    if silu_cols:
        y = y * (1.0 / (1.0 + jnp.exp(-y)))
        is_k = pl.program_id(1) >= (MLSTM_WIDTH // u.shape[1])
        y = y * jnp.where(is_k, k_scale, 1.0)
    if t2_major:
        rows = 8 * _N2
        for k in range(n // rows):
            blk = y[k * rows:(k + 1) * rows].reshape(8, _N2, y.shape[1])
            o_ref[:, k * 8:(k + 1) * 8, :] = jnp.swapaxes(blk, 0, 1).astype(o_ref.dtype)
    else:
        o_ref[...] = y.astype(o_ref.dtype)


def _shortconv(proj3, conv_w, conv_b, proj_col0, conv_col0, ncols, out_dtype, silu_cols, t2_major):
    b, l, _ = proj3.shape
    ct = 256
    p0, c0 = proj_col0 // ct, conv_col0 // ct
    body = functools.partial(_shortconv_body, silu_cols=silu_cols, k_scale=MLSTM_HEAD_DIM ** -0.5,
                             t2_major=t2_major)
    if t2_major:
        out_spec = pl.BlockSpec((None, _N2, l // _N2, ct), lambda i, j: (i, 0, 0, j))
        out_shape = jax.ShapeDtypeStruct((b, _N2, l // _N2, ncols), out_dtype)
    else:
        out_spec = pl.BlockSpec((None, l, ct), lambda i, j: (i, 0, j))
        out_shape = jax.ShapeDtypeStruct((b, l, ncols), out_dtype)
    return pl.pallas_call(
        body,
        grid=(b, ncols // ct),
        in_specs=[
            pl.BlockSpec((None, l, ct), lambda i, j: (i, 0, p0 + j)),
            pl.BlockSpec((3, ct), lambda i, j: (0, c0 + j)),
            pl.BlockSpec((1, ct), lambda i, j: (0, c0 + j)),
        ],
        out_specs=out_spec,
        out_shape=out_shape,
        compiler_params=_cparams(("arbitrary", "arbitrary")),
        name="shortconv_silu" if silu_cols else "shortconv",
    )(proj3, conv_w, conv_b)


def _dft_tables(n):
    n2 = _N2
    n1 = n // n2
    h1 = n1 // 2
    f1 = np.arange(n1)[:, None]
    t1 = np.arange(n1)[None, :]
    ang = 2.0 * np.pi * ((f1 * t1) % n1) / n1
    c, s = np.cos(ang), np.sin(ang)
    ch, sh = c[:, :h1], s[:, :h1]
    fa = np.block([[ch, sh], [-sh, ch]])
    fa_full = np.concatenate([c, -s], axis=0)
    fai = np.block([[ch.T, -sh.T], [sh.T, ch.T]]) / n
    f1v = np.arange(n1)[:, None, None]
    f2 = np.arange(n2)[None, :, None]
    t2 = np.arange(n2)[None, None, :]
    ph = 2.0 * np.pi * ((f1v * t2 + f2 * t2 * n1) % n) / n
    tc, ts = np.cos(ph), np.sin(ph)
    fb = np.concatenate([np.concatenate([tc, ts], axis=2),
                         np.concatenate([-ts, tc], axis=2)], axis=1)
    tct, tst = np.swapaxes(tc, 1, 2), np.swapaxes(ts, 1, 2)
    fbi = np.concatenate([np.concatenate([tct, -tst], axis=2),
                          np.concatenate([tst, tct], axis=2)], axis=1)
    to = lambda a: jnp.asarray(a.astype(np.float32)).astype(_BF16)
    return to(fa), to(fa_full), to(fai), to(fb), to(fbi)


def _filter_positions(l):
    n = 2 * l
    n1 = n // _N2
    r = (np.arange(_N2)[:, None] + _N2 * np.arange(n1)[None, :]).reshape(-1)
    pos = np.where(r < l, r, n - r)
    pos = np.where(r == l, 0, pos)
    bands = (HYENA_EMB - 1) // 2
    tt = np.linspace(0.0, 1.0, l, dtype=np.float32).astype(np.float64)[pos]
    omega = (2.0 * math.pi * np.arange(l, dtype=np.float32) / l).astype(np.float32)
    freqs = np.linspace(1e-4, bands - 1, bands, dtype=np.float32)
    ang = (omega[:, None] * freqs[None, :]).astype(np.float64)[pos]
    z = np.zeros((n, 128), np.float64)
    z[:, 0] = tt
    z[:, 1:1 + bands] = np.cos(ang)
    z[:, 1 + bands:1 + 2 * bands] = -np.sin(ang)
    z[:, 33] = (r != l)
    z[:, 34] = (r < l)
    return jnp.asarray(z.astype(np.float32))


def _filt_mlp_body(z_ref, w1_ref, b1_ref, q1_ref, w2_ref, b2_ref, q2_ref, w3h_ref, w3l_ref, dabs_ref,
                   o_ref, hh_ref, hl_ref):
    fh = HYENA_FILTER_HIDDEN

    @pl.when(pl.program_id(1) == 0)
    def _():
        z = z_ref[...]
        h = jnp.sin(q1_ref[...] * (jnp.dot(z, w1_ref[...], precision=_HIGHEST,
                                           preferred_element_type=_F32) + b1_ref[...]))
        h = jnp.sin(q2_ref[...] * (jnp.dot(h, w2_ref[...], precision=_HIGHEST,
                                           preferred_element_type=_F32) + b2_ref[...]))
        valid = z[:, 33:34]
        fwd = z[:, 34:35]
        lane = lax.broadcasted_iota(jnp.int32, h.shape, 1)
        hi, lo = _split_bf16(h * (valid * jnp.where(lane < fh, fwd, 1.0 - fwd)))
        hh_ref[...] = hi
        hl_ref[...] = lo

    raw = _dot3(hh_ref[...], hl_ref[...], w3h_ref[...], w3l_ref[...])
    window = jnp.exp(-z_ref[:, 0:1] * dabs_ref[...])
    o_ref[...] = raw * window


def _filt_mlp(zpos, w1p, b1, q1, w2d, b2d, q2d, w3h, w3l, dabs):
    n = zpos.shape[0]
    ct = _FFT_CT
    tr = 1024
    per_order = HYENA_WIDTH // ct
    fh = HYENA_FILTER_HIDDEN
    full = lambda shape: pl.BlockSpec(shape, lambda r, i: (0,) * len(shape))
    return pl.pallas_call(
        _filt_mlp_body,
        grid=(n // tr, 2 * per_order),
        in_specs=[
            pl.BlockSpec((tr, 128), lambda r, i: (r, 0)),
            full((128, fh)), full((1, fh)), full((1, fh)),
            full((fh, 2 * fh)), full((1, 2 * fh)), full((1, 2 * fh)),
            pl.BlockSpec((2 * fh, ct), lambda r, i: (0, i)),
            pl.BlockSpec((2 * fh, ct), lambda r, i: (0, i)),
            pl.BlockSpec((1, ct), lambda r, i: (0, i % per_order)),
        ],
        out_specs=pl.BlockSpec((tr, ct), lambda r, i: (r, i)),
        out_shape=jax.ShapeDtypeStruct((n, 2 * HYENA_WIDTH), _F32),
        scratch_shapes=[pltpu.VMEM((tr, 2 * fh), _BF16), pltpu.VMEM((tr, 2 * fh), _BF16)],
        compiler_params=_cparams(("arbitrary", "arbitrary")),
        name="filt_mlp",
    )(zpos, w1p, b1, q1, w2d, b2d, q2d, w3h, w3l, dabs)


_F1_GROUP = 8


def _stage_b_operands(s_ref, f0, n1):
    tre = jnp.swapaxes(s_ref[:, pl.ds(f0, _F1_GROUP), :], 0, 1)
    tim = jnp.swapaxes(s_ref[:, pl.ds(n1 + f0, _F1_GROUP), :], 0, 1)
    return [jnp.concatenate([tre[k], tim[k]], axis=0).astype(_BF16) for k in range(_F1_GROUP)]


def _filt_fft_body(kern_ref, fa_ref, fb_ref, o_ref, s_ref, *, n_in, n1, f1pp):
    s = pl.program_id(1)

    @pl.when(s < n_in)
    def _():
        for i in range(_T2_PER_PHASE):
            u = kern_ref[i].astype(_BF16)
            s_ref[s * _T2_PER_PHASE + i] = jnp.dot(fa_ref[...], u, preferred_element_type=_F32)

    @pl.when(s >= n_in)
    def _():
        def body(g, carry):
            fl0 = g * _F1_GROUP
            f0 = pl.multiple_of((s - n_in) * f1pp + fl0, _F1_GROUP)
            for k, v in enumerate(_stage_b_operands(s_ref, f0, n1)):
                row = pl.multiple_of((fl0 + k) * 2 * _N2, 2 * _N2)
                o_ref[pl.ds(row, 2 * _N2), :] = jnp.dot(fb_ref[fl0 + k], v, preferred_element_type=_F32)
            return carry

        lax.fori_loop(0, f1pp // _F1_GROUP, body, 0)


def _filt_fft(kern3, fa_full, fb):
    n2, n1, cols = kern3.shape
    ct = _FFT_CT
    n_in = n2 // _T2_PER_PHASE
    f1pp = min(_F1_PER_PHASE, n1)
    n_mid = n1 // f1pp
    body = functools.partial(_filt_fft_body, n_in=n_in, n1=n1, f1pp=f1pp)
    return pl.pallas_call(
        body,
        grid=(cols // ct, n_in + n_mid),
        in_specs=[
            pl.BlockSpec((_T2_PER_PHASE, n1, ct), lambda j, s: (jnp.minimum(s, n_in - 1), 0, j)),
            pl.BlockSpec((2 * n1, n1), lambda j, s: (0, 0)),
            pl.BlockSpec((f1pp, 2 * n2, 2 * n2), lambda j, s: (jnp.maximum(s - n_in, 0), 0, 0)),
        ],
        out_specs=pl.BlockSpec((f1pp * 2 * n2, ct), lambda j, s: (jnp.maximum(s - n_in, 0), j)),
        out_shape=jax.ShapeDtypeStruct((n1 * 2 * n2, cols), _F32),
        scratch_shapes=[pltpu.VMEM((n2, 2 * n1, ct), _F32)],
        compiler_params=_cparams(("arbitrary", "arbitrary")),
        name="filt_fft",
    )(kern3, fa_full, fb)


def _hyena_conv_body(u_ref, u2_ref, g_ref, k_ref, bias_ref, fa_ref, fai_ref, fb_ref, fbi_ref,
                     o_ref, s_ref, *, n_in, n_mid, n1, f1pp, natural_out):
    s = pl.program_id(2)
    h1 = n1 // 2

    @pl.when(s < n_in)
    def _():
        for i in range(_T2_PER_PHASE):
            u = jnp.concatenate([u_ref[0, i], u_ref[1, i]], axis=0).astype(_BF16)
            s_ref[s * _T2_PER_PHASE + i] = jnp.dot(fa_ref[...], u, preferred_element_type=_F32)

    @pl.when(jnp.logical_and(s >= n_in, s < n_in + n_mid))
    def _():
        def body(g, carry):
            fl0 = g * _F1_GROUP
            f0 = pl.multiple_of((s - n_in) * f1pp + fl0, _F1_GROUP)
            wre, wim = [], []
            for k, v in enumerate(_stage_b_operands(s_ref, f0, n1)):
                x = jnp.dot(fb_ref[fl0 + k], v, preferred_element_type=_F32)
                xre, xim = x[:_N2], x[_N2:]
                row = pl.multiple_of((fl0 + k) * 2 * _N2, 2 * _N2)
                kre = k_ref[pl.ds(row, _N2), :]
                kim = k_ref[pl.ds(row + _N2, _N2), :]
                y = jnp.concatenate([xre * kre - xim * kim, xre * kim + xim * kre], axis=0).astype(_BF16)
                w = jnp.dot(fbi_ref[fl0 + k], y, preferred_element_type=_F32)
                wre.append(w[:_N2])
                wim.append(w[_N2:])
            s_ref[:, pl.ds(f0, _F1_GROUP), :] = jnp.swapaxes(jnp.stack(wre, axis=0), 0, 1)
            s_ref[:, pl.ds(n1 + f0, _F1_GROUP), :] = jnp.swapaxes(jnp.stack(wim, axis=0), 0, 1)
            return carry

        lax.fori_loop(0, f1pp // _F1_GROUP, body, 0)

    @pl.when(s >= n_in + n_mid)
    def _():
        bias = bias_ref[...]
        outs = [[], []]
        for i in range(_T2_PER_PHASE):
            bv = s_ref[(s - n_in - n_mid) * _T2_PER_PHASE + i].astype(_BF16)
            y = jnp.dot(fai_ref[...], bv, preferred_element_type=_F32)
            for m in range(2):
                res = g_ref[m, i] * (y[m * h1:(m + 1) * h1] + u2_ref[m, i] * bias)
                if natural_out:
                    outs[m].append(res)
                else:
                    o_ref[m, i] = res.astype(o_ref.dtype)
        if natural_out:
            for m in range(2):
                o_ref[m] = jnp.swapaxes(jnp.stack(outs[m], axis=0), 0, 1).astype(o_ref.dtype)


def _hyena_conv(u4, ucol, g4, gcol, kspec, kcol, bias, tabs, natural_out):
    fa, fai, fb, fbi = tabs
    b, n2, h1, _ = u4.shape
    n1 = 2 * h1
    ct = _FFT_CT
    n_in = n2 // _T2_PER_PHASE
    f1pp = min(_F1_PER_PHASE, n1)
    n_mid = n1 // f1pp
    n_out = n_in
    tiles = HYENA_WIDTH // ct
    body = functools.partial(_hyena_conv_body, n_in=n_in, n_mid=n_mid, n1=n1, f1pp=f1pp,
                             natural_out=natural_out)
    mid = lambda s: jnp.clip(s - n_in, 0, n_mid - 1)
    last = lambda s: jnp.clip(s - n_in - n_mid, 0, n_out - 1)
    blk = (2, _T2_PER_PHASE, h1, ct)
    if natural_out:
        out_spec = pl.BlockSpec((2, h1, _T2_PER_PHASE, ct), lambda j, p, s: (p, 0, last(s), j))
        out_shape = jax.ShapeDtypeStruct((b, h1, n2, HYENA_WIDTH), _F32)
    else:
        out_spec = pl.BlockSpec(blk, lambda j, p, s: (p, last(s), 0, j))
        out_shape = jax.ShapeDtypeStruct((b, n2, h1, HYENA_WIDTH), _F32)
    return pl.pallas_call(
        body,
        grid=(tiles, b // 2, n_in + n_mid + n_out),
        in_specs=[
            pl.BlockSpec(blk, lambda j, p, s: (p, jnp.minimum(s, n_in - 1), 0, ucol + j)),
            pl.BlockSpec(blk, lambda j, p, s: (p, last(s), 0, ucol + j)),
            pl.BlockSpec(blk, lambda j, p, s: (p, last(s), 0, gcol + j)),
            pl.BlockSpec((f1pp * 2 * n2, ct), lambda j, p, s: (mid(s), kcol + j)),
            pl.BlockSpec((1, ct), lambda j, p, s: (0, j)),
            pl.BlockSpec((2 * n1, n1), lambda j, p, s: (0, 0)),
            pl.BlockSpec((n1, 2 * n1), lambda j, p, s: (0, 0)),
            pl.BlockSpec((f1pp, 2 * n2, 2 * n2), lambda j, p, s: (mid(s), 0, 0)),
            pl.BlockSpec((f1pp, 2 * n2, 2 * n2), lambda j, p, s: (mid(s), 0, 0)),
        ],
        out_specs=out_spec,
        out_shape=out_shape,
        scratch_shapes=[pltpu.VMEM((n2, 2 * n1, ct), _F32)],
        compiler_params=_cparams(("arbitrary", "arbitrary", "arbitrary")),
        name="hyena_conv",
    )(u4, u4, g4, kspec, bias, fa, fai, fb, fbi)


def _gate_prep_body(g_ref, bias_ref, col_ref, row_ref,
                    b_scr, a_scr, tot_scr, mloc_scr, mprev_scr, *, nc):
    ch = MLSTM_CHUNK
    lane = lax.broadcasted_iota(jnp.int32, (1, 128), 1)
    is_fwd = lane < MLSTM_HEADS
    live = lane < 2 * MLSTM_HEADS
    jj = lax.broadcasted_iota(jnp.int32, (ch, ch), 0)
    ss = lax.broadcasted_iota(jnp.int32, (ch, ch), 1)
    t_lo = (ss <= jj).astype(_F32)
    t_up = (ss >= jj).astype(_F32)

    def chunk_stats(c, carry):
        r0 = pl.multiple_of(c * ch, ch)
        gi = g_ref[pl.ds(r0, ch), :] + bias_ref[...]
        gf = pltpu.roll(gi, 128 - 2 * MLSTM_HEADS, axis=1)
        logf = jnp.minimum(gf, 0.0) - jnp.log1p(jnp.exp(-jnp.abs(gf)))
        cs_lo = jnp.dot(t_lo, logf, precision=_HIGHEST, preferred_element_type=_F32)
        cs_up = jnp.dot(t_up, logf, precision=_HIGHEST, preferred_element_type=_F32)
        bcs = jnp.where(is_fwd, cs_lo, cs_up)
        tot = cs_lo[ch - 1:ch, :]
        a = tot - bcs + gi
        b_scr[pl.ds(r0, ch), :] = bcs
        a_scr[pl.ds(r0, ch), :] = a
        tot_scr[pl.ds(c, 1), :] = tot
        mloc_scr[pl.ds(c, 1), :] = jnp.max(a, axis=0, keepdims=True)
        row_ref[:, pl.ds(r0, ch)] = (gi - bcs).T[0:8, :]
        return carry

    lax.fori_loop(0, nc, chunk_stats, 0)

    def scan_f(c, m):
        mprev_scr[pl.ds(c, 1), :] = jnp.where(is_fwd, m, mprev_scr[pl.ds(c, 1), :])
        return jnp.maximum(tot_scr[pl.ds(c, 1), :] + m, mloc_scr[pl.ds(c, 1), :])

    def scan_b(i, m):
        c = nc - 1 - i
        mprev_scr[pl.ds(c, 1), :] = jnp.where(is_fwd, mprev_scr[pl.ds(c, 1), :], m)
        return jnp.maximum(tot_scr[pl.ds(c, 1), :] + m, mloc_scr[pl.ds(c, 1), :])

    mprev_scr[...] = jnp.zeros_like(mprev_scr)
    lax.fori_loop(0, nc, scan_f, jnp.zeros((1, 128), _F32))
    lax.fori_loop(0, nc, scan_b, jnp.zeros((1, 128), _F32))

    def emit(c, carry):
        r0 = pl.multiple_of(c * ch, ch)
        mprev = mprev_scr[pl.ds(c, 1), :]
        tot = tot_scr[pl.ds(c, 1), :]
        mnew = jnp.maximum(tot + mprev, mloc_scr[pl.ds(c, 1), :])
        w = jnp.exp(a_scr[pl.ds(r0, ch), :] - mnew)
        keep = jnp.exp(tot + mprev - mnew)
        zero = jnp.zeros((ch, 128), _F32)
        pack = (jnp.where(live, b_scr[pl.ds(r0, ch), :], 0.0)
                + pltpu.roll(jnp.where(live, w, 0.0), 8, axis=1)
                + pltpu.roll(jnp.where(live, mprev + zero, 0.0), 16, axis=1)
                + pltpu.roll(jnp.where(live, keep + zero, 0.0), 24, axis=1))
        col_ref[pl.ds(r0, ch), :] = pack
        return carry

    lax.fori_loop(0, nc, emit, 0)


def _gate_prep(g3, bias):
    b, l, _ = g3.shape
    nc = l // MLSTM_CHUNK
    body = functools.partial(_gate_prep_body, nc=nc)
    return pl.pallas_call(
        body,
        grid=(b,),
        in_specs=[
            pl.BlockSpec((None, l, 128), lambda i: (i, 0, 0)),
            pl.BlockSpec((1, 128), lambda i: (0, 0)),
        ],
        out_specs=[
            pl.BlockSpec((None, l, 128), lambda i: (i, 0, 0)),
            pl.BlockSpec((None, 8, l), lambda i: (i, 0, 0)),
        ],
        out_shape=[
            jax.ShapeDtypeStruct((b, l, 128), _F32),
            jax.ShapeDtypeStruct((b, 8, l), _F32),
        ],
        scratch_shapes=[
            pltpu.VMEM((l, 128), _F32), pltpu.VMEM((l, 128), _F32),
            pltpu.VMEM((nc, 128), _F32), pltpu.VMEM((nc, 128), _F32), pltpu.VMEM((nc, 128), _F32),
        ],
        compiler_params=_cparams(("arbitrary",)),
        name="gate_prep",
    )(g3, bias)


def _mlstm_body(qf_ref, kf_ref, vf_ref, colf_ref, rowf_ref, qb_ref, kb_ref, vb_ref, colb_ref, rowb_ref,
                hf_ref, hb_ref, ct_ref, n_ref):
    @pl.when(pl.program_id(1) == 0)
    def _():
        ct_ref[...] = jnp.zeros_like(ct_ref)
        n_ref[...] = jnp.zeros_like(n_ref)

    ch, dh = MLSTM_CHUNK, MLSTM_HEAD_DIM
    jj = lax.broadcasted_iota(jnp.int32, (ch, ch), 0)
    ss = lax.broadcasted_iota(jnp.int32, (ch, ch), 1)
    dirs = ((qf_ref, kf_ref, vf_ref, colf_ref, rowf_ref, hf_ref, ss <= jj),
            (qb_ref, kb_ref, vb_ref, colb_ref, rowb_ref, hb_ref, ss >= jj))
    for d, (q_ref, k_ref, v_ref, col_ref, row_ref, o_ref, mask) in enumerate(dirs):
        col = col_ref[...]
        for h in range(MLSTM_HEADS):
            hd = d * MLSTM_HEADS + h
            bj = col[:, hd:hd + 1]
            wj = col[:, 8 + hd:9 + hd]
            mp = col[:, 16 + hd:17 + hd]
            keep = col[0:1, 24 + hd:25 + hd]
            r = row_ref[hd:hd + 1, :]
            lo, hi = h * dh, (h + 1) * dh
            qh, kh, vh = q_ref[:, lo:hi], k_ref[:, lo:hi], v_ref[:, lo:hi]
            sc = lax.dot_general(qh, kh, (((1,), (1,)), ((), ())), preferred_element_type=_F32)
            dl = jnp.where(mask, bj + r, -jnp.inf)
            inter = bj + mp
            mj = jnp.maximum(inter, jnp.max(dl, axis=1, keepdims=True))
            p = sc * jnp.exp(dl - mj)
            iw = jnp.exp(inter - mj)
            ct = ct_ref[hd]
            nv = n_ref[hd]
            num = (jnp.dot(p.astype(_BF16), vh, preferred_element_type=_F32)
                   + iw * jnp.dot(qh, ct.astype(_BF16), preferred_element_type=_F32))
            den = (jnp.sum(p, axis=1, keepdims=True)
                   + iw * jnp.sum(qh.astype(_F32) * nv, axis=1, keepdims=True))
            o_ref[:, lo:hi] = num / jnp.maximum(jnp.abs(den), jnp.exp(-mj))
            vw = (vh.astype(_F32) * wj).astype(_BF16)
            ct_ref[hd] = keep * ct + lax.dot_general(kh, vw, (((0,), (0,)), ((), ())),
                                                     preferred_element_type=_F32)
            n_ref[hd] = keep * nv + jnp.sum(kh.astype(_F32) * wj, axis=0, keepdims=True)


def _mlstm(qk, proj3, col, row):
    b, l, _ = qk.shape
    ch, mw = MLSTM_CHUNK, MLSTM_WIDTH
    nc = l // ch
    fw = lambda c: c
    bw = lambda c: nc - 1 - c

    def specs(cm):
        return [
            pl.BlockSpec((None, ch, mw), lambda i, c: (i, cm(c), 0)),
            pl.BlockSpec((None, ch, mw), lambda i, c: (i, cm(c), 1)),
            pl.BlockSpec((None, ch, mw), lambda i, c: (i, cm(c), _COL_V)),
            pl.BlockSpec((None, ch, 128), lambda i, c: (i, cm(c), 0)),
            pl.BlockSpec((None, 8, ch), lambda i, c: (i, 0, cm(c))),
        ]

    return pl.pallas_call(
        _mlstm_body,
        grid=(b, nc),
        in_specs=specs(fw) + specs(bw),
        out_specs=[
            pl.BlockSpec((None, ch, mw), lambda i, c: (i, c, 0)),
            pl.BlockSpec((None, ch, mw), lambda i, c: (i, nc - 1 - c, 0)),
        ],
        out_shape=[jax.ShapeDtypeStruct((b, l, mw), _F32), jax.ShapeDtypeStruct((b, l, mw), _F32)],
        scratch_shapes=[
            pltpu.VMEM((2 * MLSTM_HEADS, MLSTM_HEAD_DIM, MLSTM_HEAD_DIM), _F32),
            pltpu.VMEM((2 * MLSTM_HEADS, 1, MLSTM_HEAD_DIM), _F32),
        ],
        compiler_params=_cparams(("arbitrary", "arbitrary")),
        name="mlstm",
    )(qk, qk, proj3, col, row, qk, qk, proj3, col, row)


def _sigmoid(x):
    return 1.0 / (1.0 + jnp.exp(-x))


def _merge_body(x_ref, ya_ref, hf_ref, hb_ref, o_ref, ma_ref, mb_ref, wa_ref, wb_ref, wo_ref, out_ref):
    yb = (_sigmoid(o_ref[...].astype(_F32)) * (hf_ref[...] + hb_ref[...])).astype(_BF16)
    pa = jnp.dot(ya_ref[...].astype(_BF16), wa_ref[...], preferred_element_type=_F32)
    pb = jnp.dot(yb, wb_ref[...], preferred_element_type=_F32)
    mixed = _sigmoid(ma_ref[...].astype(_F32)) * pa + _sigmoid(mb_ref[...].astype(_F32)) * pb
    out_ref[...] = x_ref[...] + jnp.dot(mixed.astype(_BF16), wo_ref[...], preferred_element_type=_F32)


def _merge(x2, ya2, hf2, hb2, proj, wa, wb, wo):
    m = x2.shape[0]
    tm = 256
    d, hw, mw = D_MODEL, HYENA_WIDTH, MLSTM_WIDTH
    const = lambda shape: pl.BlockSpec(shape, lambda i: (0, 0), pipeline_mode=pl.Buffered(1))
    return pl.pallas_call(
        _merge_body,
        grid=(m // tm,),
        in_specs=[
            pl.BlockSpec((tm, d), lambda i: (i, 0)),
            pl.BlockSpec((tm, hw), lambda i: (i, 0)),
            pl.BlockSpec((tm, mw), lambda i: (i, 0)),
            pl.BlockSpec((tm, mw), lambda i: (i, 0)),
            pl.BlockSpec((tm, mw), lambda i: (i, _COL_O)),
            pl.BlockSpec((tm, d), lambda i: (i, _COL_MA // 2)),
            pl.BlockSpec((tm, d), lambda i: (i, _COL_MB // 2)),
            const((hw, d)), const((mw, d)), const((d, d)),
        ],
        out_specs=pl.BlockSpec((tm, d), lambda i: (i, 0)),
        out_shape=jax.ShapeDtypeStruct((m, d), _F32),
        compiler_params=_cparams(("arbitrary",)),
        name="merge",
    )(x2, ya2, hf2, hb2, proj, proj, proj, wa, wb, wo)


def _ffn_body(x_ref, n2_ref, wg_ref, wu_ref, wd_ref, nf_ref, o_ref, hn_ref):
    f = pl.program_id(1)

    @pl.when(f == 0)
    def _():
        x = x_ref[...]
        hn = x * lax.rsqrt(jnp.mean(x * x, axis=-1, keepdims=True) + RMS_EPS) * n2_ref[...]
        hn_ref[...] = hn.astype(_BF16)
        o_ref[...] = x

    hn = hn_ref[...]
    g = jnp.dot(hn, wg_ref[...], preferred_element_type=_F32)
    u = jnp.dot(hn, wu_ref[...], preferred_element_type=_F32)
    a = (g * _sigmoid(g) * u).astype(_BF16)
    o_ref[...] += jnp.dot(a, wd_ref[...], preferred_element_type=_F32)

    @pl.when(f == pl.num_programs(1) - 1)
    def _():
        y = o_ref[...]
        o_ref[...] = y * lax.rsqrt(jnp.mean(y * y, axis=-1, keepdims=True) + RMS_EPS) * nf_ref[...]


def _ffn(x2, norm2_w, w_gate_up, w_down, norm_f_w):
    m = x2.shape[0]
    tm, tf = 1024, 512
    d = D_MODEL
    nf = FFN_HIDDEN // tf
    return pl.pallas_call(
        _ffn_body,
        grid=(m // tm, nf),
        in_specs=[
            pl.BlockSpec((tm, d), lambda i, f: (i, 0)),
            pl.BlockSpec((1, d), lambda i, f: (0, 0)),
            pl.BlockSpec((d, tf), lambda i, f: (0, f)),
            pl.BlockSpec((d, tf), lambda i, f: (0, nf + f)),
            pl.BlockSpec((tf, d), lambda i, f: (f, 0)),
            pl.BlockSpec((1, d), lambda i, f: (0, 0)),
        ],
        out_specs=pl.BlockSpec((tm, d), lambda i, f: (i, 0)),
        out_shape=jax.ShapeDtypeStruct((m, d), _F32),
        scratch_shapes=[pltpu.VMEM((tm, d), _BF16)],
        compiler_params=_cparams(("arbitrary", "arbitrary")),
        name="ffn",
    )(x2, norm2_w, w_gate_up, w_gate_up, w_down, norm_f_w)


def kernel(x, norm1_w, w_in, conv_w, conv_b, filt_w1, filt_b1, filt_freq1, filt_w2, filt_b2, filt_freq2,
           filt_w3, hyena_bias, mlstm_gate_bias, w_branch_a, w_branch_b, w_out, norm2_w, w_gate_up, w_down,
           norm_f_w):
    b, l, d = x.shape
    assert d == D_MODEL and b % 2 == 0 and l % (_N2 * _T2_PER_PHASE) == 0
    assert norm1_w.shape[0] == 1, "single-layer block"
    hw, mw, nh = HYENA_WIDTH, MLSTM_WIDTH, MLSTM_HEADS
    m = b * l
    n = 2 * l
    n1 = n // _N2
    sc_cols = 3 * hw + 2 * mw
    g0 = sc_cols + 2 * mw

    w_in0 = w_in[0]
    w_merge = w_in0[:, g0 + 4 * nh:].astype(_BF16)
    wg = w_in0[:, g0:g0 + 4 * nh]
    pad = lambda a: jnp.pad(a, ((0, 0), (0, 128 - a.shape[1])))
    gate_order = lambda a: jnp.concatenate(
        [a[..., 0:nh], a[..., 2 * nh:3 * nh], a[..., nh:2 * nh], a[..., 3 * nh:4 * nh]], axis=-1)
    wg_hi, wg_lo = _split_bf16(pad(gate_order(wg)))
    gate_bias = pad(gate_order(mlstm_gate_bias[0].astype(_F32).reshape(1, 4 * nh)))

    x2 = x.reshape(m, d)
    w_first = w_in0[:, :g0].astype(_BF16)
    proj, gates = _inproj(x2, norm1_w[0][None, :], w_merge, w_first, wg_hi, wg_lo)
    proj3 = proj.reshape(b, l, _MAIN_COLS)

    cw, cb = conv_w[0], conv_b[0][None, :]
    hy4 = _shortconv(proj3, cw, cb, _COL_HV * 1024, 0, 3 * hw, _F32, False, True)
    qk = _shortconv(proj3, cw, cb, _COL_Q * 1024, 3 * hw, 2 * mw, _BF16, True, False)

    fa, fa_full, fai, fb, fbi = _dft_tables(n)
    zpos = _filter_positions(l)
    w1p = jnp.pad(filt_w1[0].astype(_F32), ((0, 128 - HYENA_EMB), (0, 0)))
    max_decay = math.log(HYENA_TARGET) / HYENA_FAST_DECAY
    min_decay = math.log(HYENA_TARGET) / HYENA_SLOW_DECAY
    dabs = jnp.asarray(np.abs(np.linspace(min_decay, max_decay, hw, dtype=np.float32))[None, :])
    dup = lambda a: jnp.concatenate([a, a], axis=-1)
    fh = HYENA_FILTER_HIDDEN
    w3 = filt_w3[0].astype(_F32).reshape(fh, 2, 2, hw).transpose(2, 0, 1, 3).reshape(2 * fh, 2 * hw)
    w3_hi, w3_lo = _split_bf16(w3)
    kern = _filt_mlp(zpos, w1p, filt_b1[0][None, :], filt_freq1[0][None, :], dup(filt_w2[0]),
                     dup(filt_b2[0][None, :]), dup(filt_freq2[0][None, :]), w3_hi, w3_lo, dabs)
    kspec = _filt_fft(kern.reshape(_N2, n1, 2 * hw), fa_full, fb)
    tabs = (fa, fai, fb, fbi)
    tiles = hw // _FFT_CT
    hbias = hyena_bias[0].astype(_F32)
    z4 = _hyena_conv(hy4, 0, hy4, tiles, kspec, 0, hbias[0][None, :], tabs, False)
    ya4 = _hyena_conv(z4, 0, hy4, 2 * tiles, kspec, tiles, hbias[1][None, :], tabs, True)
    ya2 = ya4.reshape(m, hw)

    col, row = _gate_prep(gates.reshape(b, l, 128), gate_bias)
    hf, hb = _mlstm(qk, proj3, col, row)

    x_mid = _merge(x2, ya2, hf.reshape(m, mw), hb.reshape(m, mw), proj,
                   w_branch_a[0].astype(_BF16), w_branch_b[0].astype(_BF16), w_out[0].astype(_BF16))
    out = _ffn(x_mid, norm2_w[0][None, :], w_gate_up[0].astype(_BF16), w_down[0].astype(_BF16),
               norm_f_w[None, :])
    return out.reshape(b, l, d)
```

```python
import functools
import math

import numpy as np
import jax
import jax.numpy as jnp
from jax import lax
from jax.experimental import pallas as pl
from jax.experimental.pallas import tpu as pltpu

_F32 = jnp.float32
_BF16 = jnp.bfloat16
_HIGHEST = lax.Precision.HIGHEST

D_MODEL = 2048
HYENA_WIDTH = 1024
HYENA_EMB = 33
HYENA_FILTER_HIDDEN = 64
HYENA_FAST_DECAY = 0.3
HYENA_SLOW_DECAY = 1.5
HYENA_TARGET = 1e-2
MLSTM_WIDTH = 1024
MLSTM_HEADS = 4
MLSTM_HEAD_DIM = 256
MLSTM_CHUNK = 128
FFN_HIDDEN = 5632
RMS_EPS = 1e-6

_COL_MA, _COL_MB, _COL_HV, _COL_Q, _COL_V, _COL_O = 0, 2, 4, 7, 9, 10
_MAIN_COLS = 11 * 1024

_N2 = 32
_T2_PER_PHASE = 8
_F1_PER_PHASE = 64
_FFT_CT = 256

_VMEM_LIMIT = 56 * 1024 * 1024


def _cparams(sem, vmem=_VMEM_LIMIT):
    return pltpu.CompilerParams(dimension_semantics=sem, vmem_limit_bytes=vmem)


def _split_bf16(a):
    hi = a.astype(_BF16)
    lo = (a - hi.astype(_F32)).astype(_BF16)
    return hi, lo


def _dot_nt(a, b):
    return lax.dot_general(a, b, (((1,), (1,)), ((), ())), preferred_element_type=_F32)


def _dot3(a_hi, a_lo, b_hi, b_lo):
    d = functools.partial(jnp.dot, preferred_element_type=_F32)
    return d(a_hi, b_hi) + (d(a_hi, b_lo) + d(a_lo, b_hi))


def _inproj_body(x_ref, nw_ref, wm_ref, w_ref, wgh_ref, wgl_ref, o_ref, g_ref, hn_ref, *, n_merge):
    j = pl.program_id(1)

    @pl.when(j == 0)
    def _():
        rows = 256
        for r in range(0, x_ref.shape[0], rows):
            x = x_ref[r:r + rows, :]
            hn = x * lax.rsqrt(jnp.mean(x * x, axis=-1, keepdims=True) + RMS_EPS) * nw_ref[...]
            hn_hi, hn_lo = _split_bf16(hn)
            hn_ref[r:r + rows, :] = hn_hi
            g_ref[r:r + rows, :] = (_dot_nt(hn_hi, wgh_ref[...])
                                    + (_dot_nt(hn_hi, wgl_ref[...]) + _dot_nt(hn_lo, wgh_ref[...])))

    @pl.when(j < n_merge)
    def _():
        o_ref[...] = _dot_nt(hn_ref[...], wm_ref[...]).astype(o_ref.dtype)

    @pl.when(j >= n_merge)
    def _():
        o_ref[...] = _dot_nt(hn_ref[...], w_ref[...]).astype(o_ref.dtype)


def _inproj(x2, norm_w, w_merge_t, w_first_t, wg_hi_t, wg_lo_t):
    m = x2.shape[0]
    tm, tn = 1024, 1024
    n_merge = w_merge_t.shape[0] // tn
    body = functools.partial(_inproj_body, n_merge=n_merge)
    return pl.pallas_call(
        body,
        grid=(m // tm, _MAIN_COLS // tn),
        in_specs=[
            pl.BlockSpec((tm, D_MODEL), lambda i, j: (i, 0)),
            pl.BlockSpec((1, D_MODEL), lambda i, j: (0, 0)),
            pl.BlockSpec((tn, D_MODEL), lambda i, j: (jnp.minimum(j, n_merge - 1), 0)),
            pl.BlockSpec((tn, D_MODEL), lambda i, j: (jnp.maximum(j - n_merge, 0), 0)),
            pl.BlockSpec((128, D_MODEL), lambda i, j: (0, 0)),
            pl.BlockSpec((128, D_MODEL), lambda i, j: (0, 0)),
        ],
        out_specs=[
            pl.BlockSpec((tm, tn), lambda i, j: (i, j)),
            pl.BlockSpec((tm, 128), lambda i, j: (i, 0)),
        ],
        out_shape=[
            jax.ShapeDtypeStruct((m, _MAIN_COLS), _BF16),
            jax.ShapeDtypeStruct((m, 128), _F32),
        ],
        scratch_shapes=[pltpu.VMEM((tm, D_MODEL), _BF16)],
        compiler_params=_cparams(("arbitrary", "arbitrary")),
        name="inproj",
    )(x2, norm_w, w_merge_t, w_first_t, wg_hi_t, wg_lo_t)


def _shortconv_body(u_ref, w_ref, b_ref, o_ref, *, silu_cols, k_scale, t2_major):
    u = u_ref[...].astype(_F32)
    n = u.shape[0]
    row = lax.broadcasted_iota(jnp.int32, u.shape, 0)
    up = jnp.where(row == 0, 0.0, pltpu.roll(u, 1, axis=0))
    un = jnp.where(row == n - 1, 0.0, pltpu.roll(u, n - 1, axis=0))
    w = w_ref[...]
    y = up * w[0:1, :] + b_ref[...] + u * w[1:2, :] + un * w[2:3, :]
    if silu_cols:
        y = y * (1.0 / (1.0 + jnp.exp(-y)))
        is_k = pl.program_id(1) >= (MLSTM_WIDTH // u.shape[1])
        y = y * jnp.where(is_k, k_scale, 1.0)
    if t2_major:
        g = 16
        rows = g * _N2
        for k in range(n // rows):
            blk = y[k * rows:(k + 1) * rows].reshape(g, _N2, y.shape[1])
            o_ref[:, k * g:(k + 1) * g, :] = jnp.swapaxes(blk, 0, 1).astype(o_ref.dtype)
    else:
        o_ref[...] = y.astype(o_ref.dtype)


def _shortconv(proj3, conv_w, conv_b, proj_col0, conv_col0, ncols, out_dtype, silu_cols, t2_major):
    b, l, _ = proj3.shape
    ct = 256
    p0, c0 = proj_col0 // ct, conv_col0 // ct
    body = functools.partial(_shortconv_body, silu_cols=silu_cols, k_scale=MLSTM_HEAD_DIM ** -0.5,
                             t2_major=t2_major)
    if t2_major:
        out_spec = pl.BlockSpec((None, _N2, l // _N2, ct), lambda i, j: (i, 0, 0, j))
        out_shape = jax.ShapeDtypeStruct((b, _N2, l // _N2, ncols), out_dtype)
    else:
        out_spec = pl.BlockSpec((None, l, ct), lambda i, j: (i, 0, j))
        out_shape = jax.ShapeDtypeStruct((b, l, ncols), out_dtype)
    return pl.pallas_call(
        body,
        grid=(b, ncols // ct),
        in_specs=[
            pl.BlockSpec((None, l, ct), lambda i, j: (i, 0, p0 + j)),
            pl.BlockSpec((3, ct), lambda i, j: (0, c0 + j)),
            pl.BlockSpec((1, ct), lambda i, j: (0, c0 + j)),
        ],
        out_specs=out_spec,
        out_shape=out_shape,
        compiler_params=_cparams(("arbitrary", "arbitrary")),
        name="shortconv_silu" if silu_cols else "shortconv",
    )(proj3, conv_w, conv_b)


def _dft_tables(n):
    n2 = _N2
    n1 = n // n2
    h1 = n1 // 2
    f1 = np.arange(n1)[:, None]
    t1 = np.arange(n1)[None, :]
    ang = 2.0 * np.pi * ((f1 * t1) % n1) / n1
    c, s = np.cos(ang), np.sin(ang)
    ch, sh = c[:, :h1], s[:, :h1]
    fa = np.block([[ch, sh], [-sh, ch]])
    fa_full = np.concatenate([c, -s], axis=0)
    fai = np.block([[ch.T, -sh.T], [sh.T, ch.T]]) / n
    f1v = np.arange(n1)[:, None, None]
    f2 = np.arange(n2)[None, :, None]
    t2 = np.arange(n2)[None, None, :]
    ph = 2.0 * np.pi * ((f1v * t2 + f2 * t2 * n1) % n) / n
    tc, ts = np.cos(ph), np.sin(ph)
    fb = np.concatenate([np.concatenate([tc, ts], axis=2),
                         np.concatenate([-ts, tc], axis=2)], axis=1)
    tct, tst = np.swapaxes(tc, 1, 2), np.swapaxes(ts, 1, 2)
    fbi = np.concatenate([np.concatenate([tct, -tst], axis=2),
                          np.concatenate([tst, tct], axis=2)], axis=1)
    to = lambda a: jnp.asarray(a.astype(np.float32)).astype(_BF16)
    return to(fa), to(fa_full), to(fai), to(fb), to(fbi)


def _filter_positions(l):
    n = 2 * l
    n1 = n // _N2
    r = (np.arange(_N2)[:, None] + _N2 * np.arange(n1)[None, :]).reshape(-1)
    pos = np.where(r < l, r, n - r)
    pos = np.where(r == l, 0, pos)
    bands = (HYENA_EMB - 1) // 2
    tt = np.linspace(0.0, 1.0, l, dtype=np.float32).astype(np.float64)[pos]
    omega = (2.0 * math.pi * np.arange(l, dtype=np.float32) / l).astype(np.float32)
    freqs = np.linspace(1e-4, bands - 1, bands, dtype=np.float32)
    ang = (omega[:, None] * freqs[None, :]).astype(np.float64)[pos]
    z = np.zeros((n, 128), np.float64)
    z[:, 0] = tt
    z[:, 1:1 + bands] = np.cos(ang)
    z[:, 1 + bands:1 + 2 * bands] = -np.sin(ang)
    z[:, 33] = (r != l)
    z[:, 34] = (r < l)
    return jnp.asarray(z.astype(np.float32))


def _filt_mlp_body(z_ref, w1_ref, b1_ref, q1_ref, w2_ref, b2_ref, q2_ref, w3h_ref, w3l_ref, dabs_ref,
                   o_ref, hh_ref, hl_ref):
    fh = HYENA_FILTER_HIDDEN

    @pl.when(pl.program_id(1) == 0)
    def _():
        z = z_ref[...]
        h = jnp.sin(q1_ref[...] * (jnp.dot(z, w1_ref[...], precision=_HIGHEST,
                                           preferred_element_type=_F32) + b1_ref[...]))
        h = jnp.sin(q2_ref[...] * (jnp.dot(h, w2_ref[...], precision=_HIGHEST,
                                           preferred_element_type=_F32) + b2_ref[...]))
        valid = z[:, 33:34]
        fwd = z[:, 34:35]
        lane = lax.broadcasted_iota(jnp.int32, h.shape, 1)
        hi, lo = _split_bf16(h * (valid * jnp.where(lane < fh, fwd, 1.0 - fwd)))
        hh_ref[...] = hi
        hl_ref[...] = lo

    raw = _dot3(hh_ref[...], hl_ref[...], w3h_ref[...], w3l_ref[...])
    window = jnp.exp(-z_ref[:, 0:1] * dabs_ref[...])
    o_ref[...] = raw * window


def _filt_mlp(zpos, w1p, b1, q1, w2d, b2d, q2d, w3h, w3l, dabs):
    n = zpos.shape[0]
    ct = _FFT_CT
    tr = 1024
    per_order = HYENA_WIDTH // ct
    fh = HYENA_FILTER_HIDDEN
    full = lambda shape: pl.BlockSpec(shape, lambda r, i: (0,) * len(shape))
    return pl.pallas_call(
        _filt_mlp_body,
        grid=(n // tr, 2 * per_order),
        in_specs=[
            pl.BlockSpec((tr, 128), lambda r, i: (r, 0)),
            full((128, fh)), full((1, fh)), full((1, fh)),
            full((fh, 2 * fh)), full((1, 2 * fh)), full((1, 2 * fh)),
            pl.BlockSpec((2 * fh, ct), lambda r, i: (0, i)),
            pl.BlockSpec((2 * fh, ct), lambda r, i: (0, i)),
            pl.BlockSpec((1, ct), lambda r, i: (0, i % per_order)),
        ],
        out_specs=pl.BlockSpec((tr, ct), lambda r, i: (r, i)),
        out_shape=jax.ShapeDtypeStruct((n, 2 * HYENA_WIDTH), _F32),
        scratch_shapes=[pltpu.VMEM((tr, 2 * fh), _BF16), pltpu.VMEM((tr, 2 * fh), _BF16)],
        compiler_params=_cparams(("arbitrary", "arbitrary")),
        name="filt_mlp",
    )(zpos, w1p, b1, q1, w2d, b2d, q2d, w3h, w3l, dabs)


_F1_GROUP = 8


def _stage_b_operands(s_ref, f0, n1):
    tre = jnp.swapaxes(s_ref[:, pl.ds(f0, _F1_GROUP), :], 0, 1)
    tim = jnp.swapaxes(s_ref[:, pl.ds(n1 + f0, _F1_GROUP), :], 0, 1)
    return [jnp.concatenate([tre[k], tim[k]], axis=0).astype(_BF16) for k in range(_F1_GROUP)]


def _filt_fft_body(kern_ref, fa_ref, fb_ref, o_ref, s_ref, *, n_in, n1, f1pp):
    s = pl.program_id(1)

    @pl.when(s < n_in)
    def _():
        for i in range(_T2_PER_PHASE):
            u = kern_ref[i].astype(_BF16)
            s_ref[s * _T2_PER_PHASE + i] = jnp.dot(fa_ref[...], u, preferred_element_type=_F32)

    @pl.when(s >= n_in)
    def _():
        def body(g, carry):
            fl0 = g * _F1_GROUP
            f0 = pl.multiple_of((s - n_in) * f1pp + fl0, _F1_GROUP)
            for k, v in enumerate(_stage_b_operands(s_ref, f0, n1)):
                row = pl.multiple_of((fl0 + k) * 2 * _N2, 2 * _N2)
                o_ref[pl.ds(row, 2 * _N2), :] = jnp.dot(fb_ref[fl0 + k], v, preferred_element_type=_F32)
            return carry

        lax.fori_loop(0, f1pp // _F1_GROUP, body, 0)


def _filt_fft(kern3, fa_full, fb):
    n2, n1, cols = kern3.shape
    ct = _FFT_CT
    n_in = n2 // _T2_PER_PHASE
    f1pp = min(_F1_PER_PHASE, n1)
    n_mid = n1 // f1pp
    body = functools.partial(_filt_fft_body, n_in=n_in, n1=n1, f1pp=f1pp)
    return pl.pallas_call(
        body,
        grid=(cols // ct, n_in + n_mid),
        in_specs=[
            pl.BlockSpec((_T2_PER_PHASE, n1, ct), lambda j, s: (jnp.minimum(s, n_in - 1), 0, j)),
            pl.BlockSpec((2 * n1, n1), lambda j, s: (0, 0)),
            pl.BlockSpec((f1pp, 2 * n2, 2 * n2), lambda j, s: (jnp.maximum(s - n_in, 0), 0, 0)),
        ],
        out_specs=pl.BlockSpec((f1pp * 2 * n2, ct), lambda j, s: (jnp.maximum(s - n_in, 0), j)),
        out_shape=jax.ShapeDtypeStruct((n1 * 2 * n2, cols), _F32),
        scratch_shapes=[pltpu.VMEM((n2, 2 * n1, ct), _F32)],
        compiler_params=_cparams(("arbitrary", "arbitrary")),
        name="filt_fft",
    )(kern3, fa_full, fb)


def _hyena_conv_body(u_ref, u2_ref, g_ref, k_ref, bias_ref, fa_ref, fai_ref, fb_ref, fbi_ref,
                     o_ref, s_ref, *, n_in, n_mid, n1, f1pp, natural_out):
    s = pl.program_id(2)
    h1 = n1 // 2

    @pl.when(s < n_in)
    def _():
        for i in range(_T2_PER_PHASE):
            u = jnp.concatenate([u_ref[0, i], u_ref[1, i]], axis=0)
            s_ref[s * _T2_PER_PHASE + i] = jnp.dot(fa_ref[...], u, preferred_element_type=_F32)

    @pl.when(jnp.logical_and(s >= n_in, s < n_in + n_mid))
    def _():
        def body(g, carry):
            fl0 = g * _F1_GROUP
            f0 = pl.multiple_of((s - n_in) * f1pp + fl0, _F1_GROUP)
            wre, wim = [], []
            for k, v in enumerate(_stage_b_operands(s_ref, f0, n1)):
                x = jnp.dot(fb_ref[fl0 + k], v, preferred_element_type=_F32)
                xre, xim = x[:_N2], x[_N2:]
                row = pl.multiple_of((fl0 + k) * 2 * _N2, 2 * _N2)
                kre = k_ref[pl.ds(row, _N2), :]
                kim = k_ref[pl.ds(row + _N2, _N2), :]
                y = jnp.concatenate([xre * kre - xim * kim, xre * kim + xim * kre], axis=0).astype(_BF16)
                w = jnp.dot(fbi_ref[fl0 + k], y, preferred_element_type=_F32)
                wre.append(w[:_N2])
                wim.append(w[_N2:])
            s_ref[:, pl.ds(f0, _F1_GROUP), :] = jnp.swapaxes(jnp.stack(wre, axis=0), 0, 1)
            s_ref[:, pl.ds(n1 + f0, _F1_GROUP), :] = jnp.swapaxes(jnp.stack(wim, axis=0), 0, 1)
            return carry

        lax.fori_loop(0, f1pp // _F1_GROUP, body, 0)

    @pl.when(s >= n_in + n_mid)
    def _():
        bias = bias_ref[...]
        outs = [[], []]
        for i in range(_T2_PER_PHASE):
            bv = s_ref[(s - n_in - n_mid) * _T2_PER_PHASE + i].astype(_BF16)
            y = jnp.dot(fai_ref[...], bv, preferred_element_type=_F32)
            for m in range(2):
                res = g_ref[m, i].astype(_F32) * (y[m * h1:(m + 1) * h1] + u2_ref[m, i].astype(_F32) * bias)
                if natural_out:
                    outs[m].append(res)
                else:
                    o_ref[m, i] = res.astype(o_ref.dtype)
        if natural_out:
            for m in range(2):
                o_ref[m] = jnp.swapaxes(jnp.stack(outs[m], axis=0), 0, 1).astype(o_ref.dtype)


def _hyena_conv(u4, ucol, g4, gcol, kspec, kcol, bias, tabs, natural_out):
    fa, fai, fb, fbi = tabs
    b, n2, h1, _ = u4.shape
    n1 = 2 * h1
    ct = _FFT_CT
    n_in = n2 // _T2_PER_PHASE
    f1pp = min(_F1_PER_PHASE, n1)
    n_mid = n1 // f1pp
    n_out = n_in
    tiles = HYENA_WIDTH // ct
    body = functools.partial(_hyena_conv_body, n_in=n_in, n_mid=n_mid, n1=n1, f1pp=f1pp,
                             natural_out=natural_out)
    mid = lambda s: jnp.clip(s - n_in, 0, n_mid - 1)
    last = lambda s: jnp.clip(s - n_in - n_mid, 0, n_out - 1)
    blk = (2, _T2_PER_PHASE, h1, ct)
    if natural_out:
        out_spec = pl.BlockSpec((2, h1, _T2_PER_PHASE, ct), lambda j, p, s: (p, 0, last(s), j))
        out_shape = jax.ShapeDtypeStruct((b, h1, n2, HYENA_WIDTH), _F32)
    else:
        out_spec = pl.BlockSpec(blk, lambda j, p, s: (p, last(s), 0, j))
        out_shape = jax.ShapeDtypeStruct((b, n2, h1, HYENA_WIDTH), _BF16)
    return pl.pallas_call(
        body,
        grid=(tiles, b // 2, n_in + n_mid + n_out),
        in_specs=[
            pl.BlockSpec(blk, lambda j, p, s: (p, jnp.minimum(s, n_in - 1), 0, ucol + j)),
            pl.BlockSpec(blk, lambda j, p, s: (p, last(s), 0, ucol + j)),
            pl.BlockSpec(blk, lambda j, p, s: (p, last(s), 0, gcol + j)),
            pl.BlockSpec((f1pp * 2 * n2, ct), lambda j, p, s: (mid(s), kcol + j)),
            pl.BlockSpec((1, ct), lambda j, p, s: (0, j)),
            pl.BlockSpec((2 * n1, n1), lambda j, p, s: (0, 0)),
            pl.BlockSpec((n1, 2 * n1), lambda j, p, s: (0, 0)),
            pl.BlockSpec((f1pp, 2 * n2, 2 * n2), lambda j, p, s: (mid(s), 0, 0)),
            pl.BlockSpec((f1pp, 2 * n2, 2 * n2), lambda j, p, s: (mid(s), 0, 0)),
        ],
        out_specs=out_spec,
        out_shape=out_shape,
        scratch_shapes=[pltpu.VMEM((n2, 2 * n1, ct), _F32)],
        compiler_params=_cparams(("arbitrary", "arbitrary", "arbitrary")),
        name="hyena_conv",
    )(u4, u4, g4, kspec, bias, fa, fai, fb, fbi)


def _gate_prep_body(g_ref, bias_ref, col_ref, row_ref,
                    b_scr, a_scr, tot_scr, mloc_scr, mprev_scr, *, nc):
    ch = MLSTM_CHUNK
    lane = lax.broadcasted_iota(jnp.int32, (1, 128), 1)
    is_fwd = lane < MLSTM_HEADS
    live = lane < 2 * MLSTM_HEADS
    jj = lax.broadcasted_iota(jnp.int32, (ch, ch), 0)
    ss = lax.broadcasted_iota(jnp.int32, (ch, ch), 1)
    t_lo = (ss <= jj).astype(_F32)
    t_up = (ss >= jj).astype(_F32)

    def chunk_stats(c, carry):
        r0 = pl.multiple_of(c * ch, ch)
        gi = g_ref[pl.ds(r0, ch), :] + bias_ref[...]
        gf = pltpu.roll(gi, 128 - 2 * MLSTM_HEADS, axis=1)
        logf = jnp.minimum(gf, 0.0) - jnp.log1p(jnp.exp(-jnp.abs(gf)))
        cs_lo = jnp.dot(t_lo, logf, precision=_HIGHEST, preferred_element_type=_F32)
        cs_up = jnp.dot(t_up, logf, precision=_HIGHEST, preferred_element_type=_F32)
        bcs = jnp.where(is_fwd, cs_lo, cs_up)
        tot = cs_lo[ch - 1:ch, :]
        a = tot - bcs + gi
        b_scr[pl.ds(r0, ch), :] = bcs
        a_scr[pl.ds(r0, ch), :] = a
        tot_scr[pl.ds(c, 1), :] = tot
        mloc_scr[pl.ds(c, 1), :] = jnp.max(a, axis=0, keepdims=True)
        row_ref[:, pl.ds(r0, ch)] = (gi - bcs).T[0:8, :]
        return carry

    lax.fori_loop(0, nc, chunk_stats, 0)

    def scan_f(c, m):
        mprev_scr[pl.ds(c, 1), :] = jnp.where(is_fwd, m, mprev_scr[pl.ds(c, 1), :])
        return jnp.maximum(tot_scr[pl.ds(c, 1), :] + m, mloc_scr[pl.ds(c, 1), :])

    def scan_b(i, m):
        c = nc - 1 - i
        mprev_scr[pl.ds(c, 1), :] = jnp.where(is_fwd, mprev_scr[pl.ds(c, 1), :], m)
        return jnp.maximum(tot_scr[pl.ds(c, 1), :] + m, mloc_scr[pl.ds(c, 1), :])

    mprev_scr[...] = jnp.zeros_like(mprev_scr)
    lax.fori_loop(0, nc, scan_f, jnp.zeros((1, 128), _F32))
    lax.fori_loop(0, nc, scan_b, jnp.zeros((1, 128), _F32))

    def emit(c, carry):
        r0 = pl.multiple_of(c * ch, ch)
        mprev = mprev_scr[pl.ds(c, 1), :]
        tot = tot_scr[pl.ds(c, 1), :]
        mnew = jnp.maximum(tot + mprev, mloc_scr[pl.ds(c, 1), :])
        w = jnp.exp(a_scr[pl.ds(r0, ch), :] - mnew)
        keep = jnp.exp(tot + mprev - mnew)
        zero = jnp.zeros((ch, 128), _F32)
        pack = (jnp.where(live, b_scr[pl.ds(r0, ch), :], 0.0)
                + pltpu.roll(jnp.where(live, w, 0.0), 8, axis=1)
                + pltpu.roll(jnp.where(live, mprev + zero, 0.0), 16, axis=1)
                + pltpu.roll(jnp.where(live, keep + zero, 0.0), 24, axis=1))
        col_ref[pl.ds(r0, ch), :] = pack
        return carry

    lax.fori_loop(0, nc, emit, 0)


def _gate_prep(g3, bias):
    b, l, _ = g3.shape
    nc = l // MLSTM_CHUNK
    body = functools.partial(_gate_prep_body, nc=nc)
    return pl.pallas_call(
        body,
        grid=(b,),
        in_specs=[
            pl.BlockSpec((None, l, 128), lambda i: (i, 0, 0)),
            pl.BlockSpec((1, 128), lambda i: (0, 0)),
        ],
        out_specs=[
            pl.BlockSpec((None, l, 128), lambda i: (i, 0, 0)),
            pl.BlockSpec((None, 8, l), lambda i: (i, 0, 0)),
        ],
        out_shape=[
            jax.ShapeDtypeStruct((b, l, 128), _F32),
            jax.ShapeDtypeStruct((b, 8, l), _F32),
        ],
        scratch_shapes=[
            pltpu.VMEM((l, 128), _F32), pltpu.VMEM((l, 128), _F32),
            pltpu.VMEM((nc, 128), _F32), pltpu.VMEM((nc, 128), _F32), pltpu.VMEM((nc, 128), _F32),
        ],
        compiler_params=_cparams(("arbitrary",)),
        name="gate_prep",
    )(g3, bias)


def _mlstm_body(qf_ref, kf_ref, vf_ref, colf_ref, rowf_ref, qb_ref, kb_ref, vb_ref, colb_ref, rowb_ref,
                hf_ref, hb_ref, ct_ref, n_ref):
    @pl.when(pl.program_id(1) == 0)
    def _():
        ct_ref[...] = jnp.zeros_like(ct_ref)
        n_ref[...] = jnp.zeros_like(n_ref)

    ch, dh = MLSTM_CHUNK, MLSTM_HEAD_DIM
    jj = lax.broadcasted_iota(jnp.int32, (ch, ch), 0)
    ss = lax.broadcasted_iota(jnp.int32, (ch, ch), 1)
    dirs = ((qf_ref, kf_ref, vf_ref, colf_ref, rowf_ref, hf_ref, ss <= jj),
            (qb_ref, kb_ref, vb_ref, colb_ref, rowb_ref, hb_ref, ss >= jj))
    for d, (q_ref, k_ref, v_ref, col_ref, row_ref, o_ref, mask) in enumerate(dirs):
        col = col_ref[...]
        for h in range(MLSTM_HEADS):
            hd = d * MLSTM_HEADS + h
            bj = col[:, hd:hd + 1]
            wj = col[:, 8 + hd:9 + hd]
            mp = col[:, 16 + hd:17 + hd]
            keep = col[0:1, 24 + hd:25 + hd]
            r = row_ref[hd:hd + 1, :]
            lo, hi = h * dh, (h + 1) * dh
            qh, kh, vh = q_ref[:, lo:hi], k_ref[:, lo:hi], v_ref[:, lo:hi]
            sc = lax.dot_general(qh, kh, (((1,), (1,)), ((), ())), preferred_element_type=_F32)
            dl = jnp.where(mask, bj + r, -jnp.inf)
            inter = bj + mp
            mj = jnp.maximum(inter, jnp.max(dl, axis=1, keepdims=True))
            p = sc * jnp.exp(dl - mj)
            iw = jnp.exp(inter - mj)
            ct = ct_ref[hd]
            nv = n_ref[hd]
            num = (jnp.dot(p.astype(_BF16), vh, preferred_element_type=_F32)
                   + iw * jnp.dot(qh, ct.astype(_BF16), preferred_element_type=_F32))
            den = (jnp.sum(p, axis=1, keepdims=True)
                   + iw * jnp.sum(qh.astype(_F32) * nv, axis=1, keepdims=True))
            o_ref[:, lo:hi] = (num / jnp.maximum(jnp.abs(den), jnp.exp(-mj))).astype(o_ref.dtype)
            vw = (vh.astype(_F32) * wj).astype(_BF16)
            ct_ref[hd] = keep * ct + lax.dot_general(kh, vw, (((0,), (0,)), ((), ())),
                                                     preferred_element_type=_F32)
            n_ref[hd] = keep * nv + jnp.sum(kh.astype(_F32) * wj, axis=0, keepdims=True)


def _mlstm(qk, proj3, col, row):
    b, l, _ = qk.shape
    ch, mw = MLSTM_CHUNK, MLSTM_WIDTH
    nc = l // ch
    fw = lambda c: c
    bw = lambda c: nc - 1 - c

    def specs(cm):
        return [
            pl.BlockSpec((None, ch, mw), lambda i, c: (i, cm(c), 0)),
            pl.BlockSpec((None, ch, mw), lambda i, c: (i, cm(c), 1)),
            pl.BlockSpec((None, ch, mw), lambda i, c: (i, cm(c), _COL_V)),
            pl.BlockSpec((None, ch, 128), lambda i, c: (i, cm(c), 0)),
            pl.BlockSpec((None, 8, ch), lambda i, c: (i, 0, cm(c))),
        ]

    return pl.pallas_call(
        _mlstm_body,
        grid=(b, nc),
        in_specs=specs(fw) + specs(bw),
        out_specs=[
            pl.BlockSpec((None, ch, mw), lambda i, c: (i, c, 0)),
            pl.BlockSpec((None, ch, mw), lambda i, c: (i, nc - 1 - c, 0)),
        ],
        out_shape=[jax.ShapeDtypeStruct((b, l, mw), _BF16), jax.ShapeDtypeStruct((b, l, mw), _BF16)],
        scratch_shapes=[
            pltpu.VMEM((2 * MLSTM_HEADS, MLSTM_HEAD_DIM, MLSTM_HEAD_DIM), _F32),
            pltpu.VMEM((2 * MLSTM_HEADS, 1, MLSTM_HEAD_DIM), _F32),
        ],
        compiler_params=_cparams(("arbitrary", "arbitrary")),
        name="mlstm",
    )(qk, qk, proj3, col, row, qk, qk, proj3, col, row)


def _sigmoid(x):
    return 1.0 / (1.0 + jnp.exp(-x))


def _merge_body(x_ref, ya_ref, hf_ref, hb_ref, o_ref, ma_ref, mb_ref, wa_ref, wb_ref, wo_ref, out_ref):
    hsum = hf_ref[...].astype(_F32) + hb_ref[...].astype(_F32)
    yb = (_sigmoid(o_ref[...].astype(_F32)) * hsum).astype(_BF16)
    pa = jnp.dot(ya_ref[...].astype(_BF16), wa_ref[...], preferred_element_type=_F32)
    pb = jnp.dot(yb, wb_ref[...], preferred_element_type=_F32)
    mixed = _sigmoid(ma_ref[...].astype(_F32)) * pa + _sigmoid(mb_ref[...].astype(_F32)) * pb
    out_ref[...] = x_ref[...] + jnp.dot(mixed.astype(_BF16), wo_ref[...], preferred_element_type=_F32)


def _merge(x2, ya2, hf2, hb2, proj, wa, wb, wo):
    m = x2.shape[0]
    tm = 256
    d, hw, mw = D_MODEL, HYENA_WIDTH, MLSTM_WIDTH
    const = lambda shape: pl.BlockSpec(shape, lambda i: (0, 0), pipeline_mode=pl.Buffered(1))
    return pl.pallas_call(
        _merge_body,
        grid=(m // tm,),
        in_specs=[
            pl.BlockSpec((tm, d), lambda i: (i, 0)),
            pl.BlockSpec((tm, hw), lambda i: (i, 0)),
            pl.BlockSpec((tm, mw), lambda i: (i, 0)),
            pl.BlockSpec((tm, mw), lambda i: (i, 0)),
            pl.BlockSpec((tm, mw), lambda i: (i, _COL_O)),
            pl.BlockSpec((tm, d), lambda i: (i, _COL_MA // 2)),
            pl.BlockSpec((tm, d), lambda i: (i, _COL_MB // 2)),
            const((hw, d)), const((mw, d)), const((d, d)),
        ],
        out_specs=pl.BlockSpec((tm, d), lambda i: (i, 0)),
        out_shape=jax.ShapeDtypeStruct((m, d), _F32),
        compiler_params=_cparams(("arbitrary",)),
        name="merge",
    )(x2, ya2, hf2, hb2, proj, proj, proj, wa, wb, wo)


def _ffn_body(x_ref, n2_ref, wg_ref, wu_ref, wd_ref, nf_ref, o_ref, hn_ref):
    f = pl.program_id(1)

    @pl.when(f == 0)
    def _():
        x = x_ref[...]
        hn = x * lax.rsqrt(jnp.mean(x * x, axis=-1, keepdims=True) + RMS_EPS) * n2_ref[...]
        hn_ref[...] = hn.astype(_BF16)
        o_ref[...] = x

    hn = hn_ref[...]
    g = jnp.dot(hn, wg_ref[...], preferred_element_type=_F32)
    u = jnp.dot(hn, wu_ref[...], preferred_element_type=_F32)
    a = (g * _sigmoid(g) * u).astype(_BF16)
    o_ref[...] += jnp.dot(a, wd_ref[...], preferred_element_type=_F32)

    @pl.when(f == pl.num_programs(1) - 1)
    def _():
        y = o_ref[...]
        o_ref[...] = y * lax.rsqrt(jnp.mean(y * y, axis=-1, keepdims=True) + RMS_EPS) * nf_ref[...]


def _ffn(x2, norm2_w, w_gate_up, w_down, norm_f_w):
    m = x2.shape[0]
    tm, tf = 1024, 512
    d = D_MODEL
    nf = FFN_HIDDEN // tf
    return pl.pallas_call(
        _ffn_body,
        grid=(m // tm, nf),
        in_specs=[
            pl.BlockSpec((tm, d), lambda i, f: (i, 0)),
            pl.BlockSpec((1, d), lambda i, f: (0, 0)),
            pl.BlockSpec((d, tf), lambda i, f: (0, f)),
            pl.BlockSpec((d, tf), lambda i, f: (0, nf + f)),
            pl.BlockSpec((tf, d), lambda i, f: (f, 0)),
            pl.BlockSpec((1, d), lambda i, f: (0, 0)),
        ],
        out_specs=pl.BlockSpec((tm, d), lambda i, f: (i, 0)),
        out_shape=jax.ShapeDtypeStruct((m, d), _F32),
        scratch_shapes=[pltpu.VMEM((tm, d), _BF16)],
        compiler_params=_cparams(("arbitrary", "arbitrary")),
        name="ffn",
    )(x2, norm2_w, w_gate_up, w_gate_up, w_down, norm_f_w)


def kernel(x, norm1_w, w_in, conv_w, conv_b, filt_w1, filt_b1, filt_freq1, filt_w2, filt_b2, filt_freq2,
           filt_w3, hyena_bias, mlstm_gate_bias, w_branch_a, w_branch_b, w_out, norm2_w, w_gate_up, w_down,
           norm_f_w):
    b, l, d = x.shape
    assert d == D_MODEL and b % 2 == 0 and l % (_N2 * _T2_PER_PHASE) == 0
    assert norm1_w.shape[0] == 1, "single-layer block"
    hw, mw, nh = HYENA_WIDTH, MLSTM_WIDTH, MLSTM_HEADS
    m = b * l
    n = 2 * l
    n1 = n // _N2
    sc_cols = 3 * hw + 2 * mw
    g0 = sc_cols + 2 * mw

    w_t = jnp.swapaxes(w_in[0], 0, 1)
    w_first_t = w_t[:g0].astype(_BF16)
    w_merge_t = w_t[g0 + 4 * nh:].astype(_BF16)
    wg_t = w_t[g0:g0 + 4 * nh]
    gate_order = lambda a: jnp.concatenate(
        [a[0:nh], a[2 * nh:3 * nh], a[nh:2 * nh], a[3 * nh:4 * nh]], axis=0)
    wg_hi_t, wg_lo_t = _split_bf16(jnp.pad(gate_order(wg_t), ((0, 128 - 4 * nh), (0, 0))))
    gate_bias = jnp.pad(gate_order(mlstm_gate_bias[0].astype(_F32).reshape(4 * nh))[None, :],
                        ((0, 0), (0, 128 - 4 * nh)))

    x2 = x.reshape(m, d)
    proj, gates = _inproj(x2, norm1_w[0][None, :], w_merge_t, w_first_t, wg_hi_t, wg_lo_t)
    proj3 = proj.reshape(b, l, _MAIN_COLS)

    cw, cb = conv_w[0], conv_b[0][None, :]
    hy4 = _shortconv(proj3, cw, cb, _COL_HV * 1024, 0, 3 * hw, _BF16, False, True)
    qk = _shortconv(proj3, cw, cb, _COL_Q * 1024, 3 * hw, 2 * mw, _BF16, True, False)

    fa, fa_full, fai, fb, fbi = _dft_tables(n)
    zpos = _filter_positions(l)
    w1p = jnp.pad(filt_w1[0].astype(_F32), ((0, 128 - HYENA_EMB), (0, 0)))
    max_decay = math.log(HYENA_TARGET) / HYENA_FAST_DECAY
    min_decay = math.log(HYENA_TARGET) / HYENA_SLOW_DECAY
    dabs = jnp.asarray(np.abs(np.linspace(min_decay, max_decay, hw, dtype=np.float32))[None, :])
    dup = lambda a: jnp.concatenate([a, a], axis=-1)
    fh = HYENA_FILTER_HIDDEN
    w3 = filt_w3[0].astype(_F32).reshape(fh, 2, 2, hw).transpose(2, 0, 1, 3).reshape(2 * fh, 2 * hw)
    w3_hi, w3_lo = _split_bf16(w3)
    kern = _filt_mlp(zpos, w1p, filt_b1[0][None, :], filt_freq1[0][None, :], dup(filt_w2[0]),
                     dup(filt_b2[0][None, :]), dup(filt_freq2[0][None, :]), w3_hi, w3_lo, dabs)
    kspec = _filt_fft(kern.reshape(_N2, n1, 2 * hw), fa_full, fb)
    tabs = (fa, fai, fb, fbi)
    tiles = hw // _FFT_CT
    hbias = hyena_bias[0].astype(_F32)
    z4 = _hyena_conv(hy4, 0, hy4, tiles, kspec, 0, hbias[0][None, :], tabs, False)
    ya4 = _hyena_conv(z4, 0, hy4, 2 * tiles, kspec, tiles, hbias[1][None, :], tabs, True)
    ya2 = ya4.reshape(m, hw)

    col, row = _gate_prep(gates.reshape(b, l, 128), gate_bias)
    hf, hb = _mlstm(qk, proj3, col, row)

    x_mid = _merge(x2, ya2, hf.reshape(m, mw), hb.reshape(m, mw), proj,
                   w_branch_a[0].astype(_BF16), w_branch_b[0].astype(_BF16), w_out[0].astype(_BF16))
    out = _ffn(x_mid, norm2_w[0][None, :], w_gate_up[0].astype(_BF16), w_down[0].astype(_BF16),
               norm_f_w[None, :])
    return out.reshape(b, l, d)
```

```python
import functools
import math

import numpy as np
import jax
import jax.numpy as jnp
from jax import lax
from jax.experimental import pallas as pl
from jax.experimental.pallas import tpu as pltpu

_F32 = jnp.float32
_BF16 = jnp.bfloat16
_HIGHEST = lax.Precision.HIGHEST

D_MODEL = 2048
HYENA_WIDTH = 1024
HYENA_EMB = 33
HYENA_FILTER_HIDDEN = 64
HYENA_FAST_DECAY = 0.3
HYENA_SLOW_DECAY = 1.5
HYENA_TARGET = 1e-2
MLSTM_WIDTH = 1024
MLSTM_HEADS = 4
MLSTM_HEAD_DIM = 256
MLSTM_CHUNK = 128
FFN_HIDDEN = 5632
RMS_EPS = 1e-6

_COL_MA, _COL_MB, _COL_HV, _COL_Q, _COL_V, _COL_O = 0, 2, 4, 7, 9, 10
_MAIN_COLS = 11 * 1024

_N2 = 32
_T2_PER_PHASE = 8
_F1_PER_PHASE = 64
_FFT_CT = 256

_VMEM_LIMIT = 56 * 1024 * 1024


def _cparams(sem, vmem=_VMEM_LIMIT):
    return pltpu.CompilerParams(dimension_semantics=sem, vmem_limit_bytes=vmem)


def _split_bf16(a):
    hi = a.astype(_BF16)
    lo = (a - hi.astype(_F32)).astype(_BF16)
    return hi, lo


def _dot_nt(a, b):
    return lax.dot_general(a, b, (((1,), (1,)), ((), ())), preferred_element_type=_F32)


def _dot3(a_hi, a_lo, b_hi, b_lo):
    d = functools.partial(jnp.dot, preferred_element_type=_F32)
    return d(a_hi, b_hi) + (d(a_hi, b_lo) + d(a_lo, b_hi))


def _inproj_body(x_ref, nw_ref, wm_ref, w_ref, wgh_ref, wgl_ref, o_ref, g_ref, hn_ref, *, n_merge):
    j = pl.program_id(1)

    @pl.when(j == 0)
    def _():
        rows = 256
        for r in range(0, x_ref.shape[0], rows):
            x = x_ref[r:r + rows, :]
            hn = x * lax.rsqrt(jnp.mean(x * x, axis=-1, keepdims=True) + RMS_EPS) * nw_ref[...]
            hn_hi, hn_lo = _split_bf16(hn)
            hn_ref[r:r + rows, :] = hn_hi
            g_ref[r:r + rows, :] = (_dot_nt(hn_hi, wgh_ref[...])
                                    + (_dot_nt(hn_hi, wgl_ref[...]) + _dot_nt(hn_lo, wgh_ref[...])))

    @pl.when(j < n_merge)
    def _():
        o_ref[...] = _dot_nt(hn_ref[...], wm_ref[...]).astype(o_ref.dtype)

    @pl.when(j >= n_merge)
    def _():
        o_ref[...] = _dot_nt(hn_ref[...], w_ref[...]).astype(o_ref.dtype)


def _inproj(x2, norm_w, w_merge_t, w_first_t, wg_hi_t, wg_lo_t):
    m = x2.shape[0]
    tm, tn = 1024, 1024
    n_merge = w_merge_t.shape[0] // tn
    body = functools.partial(_inproj_body, n_merge=n_merge)
    return pl.pallas_call(
        body,
        grid=(m // tm, _MAIN_COLS // tn),
        in_specs=[
            pl.BlockSpec((tm, D_MODEL), lambda i, j: (i, 0)),
            pl.BlockSpec((1, D_MODEL), lambda i, j: (0, 0)),
            pl.BlockSpec((tn, D_MODEL), lambda i, j: (jnp.minimum(j, n_merge - 1), 0)),
            pl.BlockSpec((tn, D_MODEL), lambda i, j: (jnp.maximum(j - n_merge, 0), 0)),
            pl.BlockSpec((128, D_MODEL), lambda i, j: (0, 0)),
            pl.BlockSpec((128, D_MODEL), lambda i, j: (0, 0)),
        ],
        out_specs=[
            pl.BlockSpec((tm, tn), lambda i, j: (i, j)),
            pl.BlockSpec((tm, 128), lambda i, j: (i, 0)),
        ],
        out_shape=[
            jax.ShapeDtypeStruct((m, _MAIN_COLS), _BF16),
            jax.ShapeDtypeStruct((m, 128), _F32),
        ],
        scratch_shapes=[pltpu.VMEM((tm, D_MODEL), _BF16)],
        compiler_params=_cparams(("arbitrary", "arbitrary")),
        name="inproj",
    )(x2, norm_w, w_merge_t, w_first_t, wg_hi_t, wg_lo_t)


def _shortconv_body(u_ref, w_ref, b_ref, o_ref, *, silu_cols, k_scale, t2_major):
    u = u_ref[...].astype(_F32)
    n = u.shape[0]
    row = lax.broadcasted_iota(jnp.int32, u.shape, 0)
    up = jnp.where(row == 0, 0.0, pltpu.roll(u, 1, axis=0))
    un = jnp.where(row == n - 1, 0.0, pltpu.roll(u, n - 1, axis=0))
    w = w_ref[...]
    y = up * w[0:1, :] + b_ref[...] + u * w[1:2, :] + un * w[2:3, :]
    if silu_cols:
        y = y * (1.0 / (1.0 + jnp.exp(-y)))
        is_k = pl.program_id(1) >= (MLSTM_WIDTH // u.shape[1])
        y = y * jnp.where(is_k, k_scale, 1.0)
    if t2_major:
        g = 16
        rows = g * _N2
        for k in range(n // rows):
            blk = y[k * rows:(k + 1) * rows].reshape(g, _N2, y.shape[1])
            o_ref[:, k * g:(k + 1) * g, :] = jnp.swapaxes(blk, 0, 1).astype(o_ref.dtype)
    else:
        o_ref[...] = y.astype(o_ref.dtype)


def _shortconv(proj3, conv_w, conv_b, proj_col0, conv_col0, ncols, out_dtype, silu_cols, t2_major):
    b, l, _ = proj3.shape
    ct = 256
    p0, c0 = proj_col0 // ct, conv_col0 // ct
    body = functools.partial(_shortconv_body, silu_cols=silu_cols, k_scale=MLSTM_HEAD_DIM ** -0.5,
                             t2_major=t2_major)
    if t2_major:
        out_spec = pl.BlockSpec((None, _N2, l // _N2, ct), lambda i, j: (i, 0, 0, j))
        out_shape = jax.ShapeDtypeStruct((b, _N2, l // _N2, ncols), out_dtype)
    else:
        out_spec = pl.BlockSpec((None, l, ct), lambda i, j: (i, 0, j))
        out_shape = jax.ShapeDtypeStruct((b, l, ncols), out_dtype)
    return pl.pallas_call(
        body,
        grid=(b, ncols // ct),
        in_specs=[
            pl.BlockSpec((None, l, ct), lambda i, j: (i, 0, p0 + j)),
            pl.BlockSpec((3, ct), lambda i, j: (0, c0 + j)),
            pl.BlockSpec((1, ct), lambda i, j: (0, c0 + j)),
        ],
        out_specs=out_spec,
        out_shape=out_shape,
        compiler_params=_cparams(("arbitrary", "arbitrary")),
        name="shortconv_silu" if silu_cols else "shortconv",
    )(proj3, conv_w, conv_b)


def _dft_tables(n):
    n2 = _N2
    n1 = n // n2
    h1 = n1 // 2
    f1 = np.arange(n1)[:, None]
    t1 = np.arange(n1)[None, :]
    ang = 2.0 * np.pi * ((f1 * t1) % n1) / n1
    c, s = np.cos(ang), np.sin(ang)
    ch, sh = c[:, :h1], s[:, :h1]
    fa = np.block([[ch, sh], [-sh, ch]])
    fa_full = np.concatenate([c, -s], axis=0)
    fai = np.block([[ch.T, -sh.T], [sh.T, ch.T]]) / n
    f1v = np.arange(n1)[:, None, None]
    f2 = np.arange(n2)[None, :, None]
    t2 = np.arange(n2)[None, None, :]
    ph = 2.0 * np.pi * ((f1v * t2 + f2 * t2 * n1) % n) / n
    tc, ts = np.cos(ph), np.sin(ph)
    fb = np.concatenate([np.concatenate([tc, ts], axis=2),
                         np.concatenate([-ts, tc], axis=2)], axis=1)
    tct, tst = np.swapaxes(tc, 1, 2), np.swapaxes(ts, 1, 2)
    fbi = np.concatenate([np.concatenate([tct, -tst], axis=2),
                          np.concatenate([tst, tct], axis=2)], axis=1)
    to = lambda a: jnp.asarray(a.astype(np.float32)).astype(_BF16)
    return to(fa), to(fa_full), to(fai), to(fb), to(fbi)


def _filter_positions(l):
    n = 2 * l
    n1 = n // _N2
    r = (np.arange(_N2)[:, None] + _N2 * np.arange(n1)[None, :]).reshape(-1)
    pos = np.where(r < l, r, n - r)
    pos = np.where(r == l, 0, pos)
    bands = (HYENA_EMB - 1) // 2
    tt = np.linspace(0.0, 1.0, l, dtype=np.float32).astype(np.float64)[pos]
    omega = (2.0 * math.pi * np.arange(l, dtype=np.float32) / l).astype(np.float32)
    freqs = np.linspace(1e-4, bands - 1, bands, dtype=np.float32)
    ang = (omega[:, None] * freqs[None, :]).astype(np.float64)[pos]
    z = np.zeros((n, 128), np.float64)
    z[:, 0] = tt
    z[:, 1:1 + bands] = np.cos(ang)
    z[:, 1 + bands:1 + 2 * bands] = -np.sin(ang)
    z[:, 33] = (r != l)
    z[:, 34] = (r < l)
    return jnp.asarray(z.astype(np.float32))


def _filt_mlp_body(z_ref, w1_ref, b1_ref, q1_ref, w2_ref, b2_ref, q2_ref, w3h_ref, w3l_ref, dabs_ref,
                   o_ref, hh_ref, hl_ref):
    fh = HYENA_FILTER_HIDDEN

    @pl.when(pl.program_id(1) == 0)
    def _():
        z = z_ref[...]
        h = jnp.sin(q1_ref[...] * (jnp.dot(z, w1_ref[...], precision=_HIGHEST,
                                           preferred_element_type=_F32) + b1_ref[...]))
        h = jnp.sin(q2_ref[...] * (jnp.dot(h, w2_ref[...], precision=_HIGHEST,
                                           preferred_element_type=_F32) + b2_ref[...]))
        valid = z[:, 33:34]
        fwd = z[:, 34:35]
        lane = lax.broadcasted_iota(jnp.int32, h.shape, 1)
        hi, lo = _split_bf16(h * (valid * jnp.where(lane < fh, fwd, 1.0 - fwd)))
        hh_ref[...] = hi
        hl_ref[...] = lo

    raw = _dot3(hh_ref[...], hl_ref[...], w3h_ref[...], w3l_ref[...])
    window = jnp.exp(-z_ref[:, 0:1] * dabs_ref[...])
    o_ref[...] = raw * window


def _filt_mlp(zpos, w1p, b1, q1, w2d, b2d, q2d, w3h, w3l, dabs):
    n = zpos.shape[0]
    ct = _FFT_CT
    tr = 1024
    per_order = HYENA_WIDTH // ct
    fh = HYENA_FILTER_HIDDEN
    full = lambda shape: pl.BlockSpec(shape, lambda r, i: (0,) * len(shape))
    return pl.pallas_call(
        _filt_mlp_body,
        grid=(n // tr, 2 * per_order),
        in_specs=[
            pl.BlockSpec((tr, 128), lambda r, i: (r, 0)),
            full((128, fh)), full((1, fh)), full((1, fh)),
            full((fh, 2 * fh)), full((1, 2 * fh)), full((1, 2 * fh)),
            pl.BlockSpec((2 * fh, ct), lambda r, i: (0, i)),
            pl.BlockSpec((2 * fh, ct), lambda r, i: (0, i)),
            pl.BlockSpec((1, ct), lambda r, i: (0, i % per_order)),
        ],
        out_specs=pl.BlockSpec((tr, ct), lambda r, i: (r, i)),
        out_shape=jax.ShapeDtypeStruct((n, 2 * HYENA_WIDTH), _F32),
        scratch_shapes=[pltpu.VMEM((tr, 2 * fh), _BF16), pltpu.VMEM((tr, 2 * fh), _BF16)],
        compiler_params=_cparams(("arbitrary", "arbitrary")),
        name="filt_mlp",
    )(zpos, w1p, b1, q1, w2d, b2d, q2d, w3h, w3l, dabs)


_F1_GROUP = 8


def _stage_b_operands(s_ref, f0, n1):
    tre = jnp.swapaxes(s_ref[:, pl.ds(f0, _F1_GROUP), :], 0, 1)
    tim = jnp.swapaxes(s_ref[:, pl.ds(n1 + f0, _F1_GROUP), :], 0, 1)
    return [jnp.concatenate([tre[k], tim[k]], axis=0).astype(_BF16) for k in range(_F1_GROUP)]


def _filt_fft_body(kern_ref, fa_ref, fb_ref, o_ref, s_ref, *, n_in, n1, f1pp):
    s = pl.program_id(1)

    @pl.when(s < n_in)
    def _():
        for i in range(_T2_PER_PHASE):
            u = kern_ref[i].astype(_BF16)
            s_ref[s * _T2_PER_PHASE + i] = jnp.dot(fa_ref[...], u, preferred_element_type=_F32)

    @pl.when(s >= n_in)
    def _():
        def body(g, carry):
            fl0 = g * _F1_GROUP
            f0 = pl.multiple_of((s - n_in) * f1pp + fl0, _F1_GROUP)
            for k, v in enumerate(_stage_b_operands(s_ref, f0, n1)):
                row = pl.multiple_of((fl0 + k) * 2 * _N2, 2 * _N2)
                o_ref[pl.ds(row, 2 * _N2), :] = jnp.dot(fb_ref[fl0 + k], v, preferred_element_type=_F32)
            return carry

        lax.fori_loop(0, f1pp // _F1_GROUP, body, 0)


def _filt_fft(kern3, fa_full, fb):
    n2, n1, cols = kern3.shape
    ct = _FFT_CT
    n_in = n2 // _T2_PER_PHASE
    f1pp = min(_F1_PER_PHASE, n1)
    n_mid = n1 // f1pp
    body = functools.partial(_filt_fft_body, n_in=n_in, n1=n1, f1pp=f1pp)
    return pl.pallas_call(
        body,
        grid=(cols // ct, n_in + n_mid),
        in_specs=[
            pl.BlockSpec((_T2_PER_PHASE, n1, ct), lambda j, s: (jnp.minimum(s, n_in - 1), 0, j)),
            pl.BlockSpec((2 * n1, n1), lambda j, s: (0, 0)),
            pl.BlockSpec((f1pp, 2 * n2, 2 * n2), lambda j, s: (jnp.maximum(s - n_in, 0), 0, 0)),
        ],
        out_specs=pl.BlockSpec((f1pp * 2 * n2, ct), lambda j, s: (jnp.maximum(s - n_in, 0), j)),
        out_shape=jax.ShapeDtypeStruct((n1 * 2 * n2, cols), _F32),
        scratch_shapes=[pltpu.VMEM((n2, 2 * n1, ct), _F32)],
        compiler_params=_cparams(("arbitrary", "arbitrary")),
        name="filt_fft",
    )(kern3, fa_full, fb)


def _hyena_conv_body(u_ref, u2_ref, g_ref, k_ref, bias_ref, fa_ref, fai_ref, fb_ref, fbi_ref,
                     o_ref, s_ref, *, n_in, n_mid, n1, f1pp, natural_out):
    s = pl.program_id(2)
    h1 = n1 // 2

    @pl.when(s < n_in)
    def _():
        for i in range(_T2_PER_PHASE):
            u = jnp.concatenate([u_ref[0, i], u_ref[1, i]], axis=0)
            s_ref[s * _T2_PER_PHASE + i] = jnp.dot(fa_ref[...], u, preferred_element_type=_F32)

    @pl.when(jnp.logical_and(s >= n_in, s < n_in + n_mid))
    def _():
        def body(g, carry):
            fl0 = g * _F1_GROUP
            f0 = pl.multiple_of((s - n_in) * f1pp + fl0, _F1_GROUP)
            wre, wim = [], []
            for k, v in enumerate(_stage_b_operands(s_ref, f0, n1)):
                x = jnp.dot(fb_ref[fl0 + k], v, preferred_element_type=_F32)
                xre, xim = x[:_N2], x[_N2:]
                row = pl.multiple_of((fl0 + k) * 2 * _N2, 2 * _N2)
                kre = k_ref[pl.ds(row, _N2), :]
                kim = k_ref[pl.ds(row + _N2, _N2), :]
                y = jnp.concatenate([xre * kre - xim * kim, xre * kim + xim * kre], axis=0).astype(_BF16)
                w = jnp.dot(fbi_ref[fl0 + k], y, preferred_element_type=_F32)
                wre.append(w[:_N2])
                wim.append(w[_N2:])
            s_ref[:, pl.ds(f0, _F1_GROUP), :] = jnp.swapaxes(jnp.stack(wre, axis=0), 0, 1)
            s_ref[:, pl.ds(n1 + f0, _F1_GROUP), :] = jnp.swapaxes(jnp.stack(wim, axis=0), 0, 1)
            return carry

        lax.fori_loop(0, f1pp // _F1_GROUP, body, 0)

    @pl.when(s >= n_in + n_mid)
    def _():
        bias = bias_ref[...]
        outs = [[], []]
        for i in range(_T2_PER_PHASE):
            bv = s_ref[(s - n_in - n_mid) * _T2_PER_PHASE + i].astype(_BF16)
            y = jnp.dot(fai_ref[...], bv, preferred_element_type=_F32)
            for m in range(2):
                res = g_ref[m, i].astype(_F32) * (y[m * h1:(m + 1) * h1] + u2_ref[m, i].astype(_F32) * bias)
                if natural_out:
                    outs[m].append(res)
                else:
                    o_ref[m, i] = res.astype(o_ref.dtype)
        if natural_out:
            for m in range(2):
                o_ref[m] = jnp.swapaxes(jnp.stack(outs[m], axis=0), 0, 1).astype(o_ref.dtype)


def _hyena_conv(u4, ucol, g4, gcol, kspec, kcol, bias, tabs, natural_out):
    fa, fai, fb, fbi = tabs
    b, n2, h1, _ = u4.shape
    n1 = 2 * h1
    ct = _FFT_CT
    n_in = n2 // _T2_PER_PHASE
    f1pp = min(_F1_PER_PHASE, n1)
    n_mid = n1 // f1pp
    n_out = n_in
    tiles = HYENA_WIDTH // ct
    body = functools.partial(_hyena_conv_body, n_in=n_in, n_mid=n_mid, n1=n1, f1pp=f1pp,
                             natural_out=natural_out)
    mid = lambda s: jnp.clip(s - n_in, 0, n_mid - 1)
    last = lambda s: jnp.clip(s - n_in - n_mid, 0, n_out - 1)
    blk = (2, _T2_PER_PHASE, h1, ct)
    if natural_out:
        out_spec = pl.BlockSpec((2, h1, _T2_PER_PHASE, ct), lambda j, p, s: (p, 0, last(s), j))
        out_shape = jax.ShapeDtypeStruct((b, h1, n2, HYENA_WIDTH), _F32)
    else:
        out_spec = pl.BlockSpec(blk, lambda j, p, s: (p, last(s), 0, j))
        out_shape = jax.ShapeDtypeStruct((b, n2, h1, HYENA_WIDTH), _BF16)
    return pl.pallas_call(
        body,
        grid=(tiles, b // 2, n_in + n_mid + n_out),
        in_specs=[
            pl.BlockSpec(blk, lambda j, p, s: (p, jnp.minimum(s, n_in - 1), 0, ucol + j)),
            pl.BlockSpec(blk, lambda j, p, s: (p, last(s), 0, ucol + j)),
            pl.BlockSpec(blk, lambda j, p, s: (p, last(s), 0, gcol + j)),
            pl.BlockSpec((f1pp * 2 * n2, ct), lambda j, p, s: (mid(s), kcol + j)),
            pl.BlockSpec((1, ct), lambda j, p, s: (0, j)),
            pl.BlockSpec((2 * n1, n1), lambda j, p, s: (0, 0)),
            pl.BlockSpec((n1, 2 * n1), lambda j, p, s: (0, 0)),
            pl.BlockSpec((f1pp, 2 * n2, 2 * n2), lambda j, p, s: (mid(s), 0, 0)),
            pl.BlockSpec((f1pp, 2 * n2, 2 * n2), lambda j, p, s: (mid(s), 0, 0)),
        ],
        out_specs=out_spec,
        out_shape=out_shape,
        scratch_shapes=[pltpu.VMEM((n2, 2 * n1, ct), _F32)],
        compiler_params=_cparams(("arbitrary", "arbitrary", "arbitrary")),
        name="hyena_conv",
    )(u4, u4, g4, kspec, bias, fa, fai, fb, fbi)


_GATE_LANES = 2 * MLSTM_HEADS
_TERM_R = 12


def _split3(x):
    hi = x.astype(_BF16).astype(_F32)
    mid = (x - hi).astype(_BF16).astype(_F32)
    lo = (x - hi - mid).astype(_BF16).astype(_F32)
    return hi, mid, lo


def _scan_order_max(x, is_fwd):
    n = x.shape[0]
    row = lax.broadcasted_iota(jnp.int32, x.shape, 0)
    pre, suf = x, x
    shift = 1
    while shift < n:
        pre = jnp.maximum(pre, jnp.where(row >= shift, pltpu.roll(pre, shift, axis=0), -jnp.inf))
        suf = jnp.maximum(suf, jnp.where(row < n - shift, pltpu.roll(suf, n - shift, axis=0), -jnp.inf))
        shift *= 2
    return jnp.where(is_fwd, pre, suf)


def _gate_prep_body(g_ref, bias_ref, a_ref, row_ref, keep_ref,
                    b_scr, a_scr, pm_scr, tot_scr, mloc_scr, mprev_scr, *, nc):
    ch = MLSTM_CHUNK
    lane = lax.broadcasted_iota(jnp.int32, (1, 128), 1)
    is_fwd = lane < MLSTM_HEADS
    live = lane < _GATE_LANES
    jj = lax.broadcasted_iota(jnp.int32, (ch, ch), 0)
    ss = lax.broadcasted_iota(jnp.int32, (ch, ch), 1)
    t_lo = (ss <= jj).astype(_F32)
    t_up = (ss >= jj).astype(_F32)
    row_ref[...] = jnp.zeros_like(row_ref)

    def chunk_stats(c, carry):
        r0 = pl.multiple_of(c * ch, ch)
        gi = g_ref[pl.ds(r0, ch), :] + bias_ref[...]
        gf = pltpu.roll(gi, 128 - 2 * MLSTM_HEADS, axis=1)
        logf = jnp.minimum(gf, 0.0) - jnp.log1p(jnp.exp(-jnp.abs(gf)))
        cs_lo = jnp.dot(t_lo, logf, precision=_HIGHEST, preferred_element_type=_F32)
        cs_up = jnp.dot(t_up, logf, precision=_HIGHEST, preferred_element_type=_F32)
        bcs = jnp.where(is_fwd, cs_lo, cs_up)
        tot = cs_lo[ch - 1:ch, :]
        a = tot - bcs + gi
        b_scr[pl.ds(r0, ch), :] = bcs
        a_scr[pl.ds(r0, ch), :] = a
        tot_scr[pl.ds(c, 1), :] = tot
        mloc_scr[pl.ds(c, 1), :] = jnp.max(a, axis=0, keepdims=True)
        r = gi - bcs
        pm_scr[pl.ds(r0, ch), :] = _scan_order_max(r, is_fwd)
        hi, mid, lo = _split3(r.T[0:_GATE_LANES, :])
        base = _TERM_R * _GATE_LANES
        row_ref[base:base + 16, pl.ds(r0, ch)] = jnp.concatenate([hi, mid], axis=0).astype(_BF16)
        row_ref[base + 16:base + 32, pl.ds(r0, ch)] = jnp.concatenate(
            [lo, jnp.zeros_like(lo)], axis=0).astype(_BF16)
        return carry

    lax.fori_loop(0, nc, chunk_stats, 0)

    def scan_f(c, m):
        mprev_scr[pl.ds(c, 1), :] = jnp.where(is_fwd, m, mprev_scr[pl.ds(c, 1), :])
        return jnp.maximum(tot_scr[pl.ds(c, 1), :] + m, mloc_scr[pl.ds(c, 1), :])

    def scan_b(i, m):
        c = nc - 1 - i
        mprev_scr[pl.ds(c, 1), :] = jnp.where(is_fwd, mprev_scr[pl.ds(c, 1), :], m)
        return jnp.maximum(tot_scr[pl.ds(c, 1), :] + m, mloc_scr[pl.ds(c, 1), :])

    mprev_scr[...] = jnp.zeros_like(mprev_scr)
    lax.fori_loop(0, nc, scan_f, jnp.zeros((1, 128), _F32))
    lax.fori_loop(0, nc, scan_b, jnp.zeros((1, 128), _F32))

    def emit(c, carry):
        r0 = pl.multiple_of(c * ch, ch)
        mprev = mprev_scr[pl.ds(c, 1), :]
        tot = tot_scr[pl.ds(c, 1), :]
        mnew = jnp.maximum(tot + mprev, mloc_scr[pl.ds(c, 1), :])
        keep_ref[pl.ds(c, 1), :] = jnp.exp(tot + mprev - mnew)
        g = jnp.maximum(mprev, pm_scr[pl.ds(r0, ch), :])
        terms = (_split3(-g) + _split3(mprev - g) + _split3(-b_scr[pl.ds(r0, ch), :] - g)
                 + _split3(a_scr[pl.ds(r0, ch), :] - mnew) + 3 * (jnp.ones((ch, 128), _F32),))
        pack = jnp.where(live, terms[0], 0.0)
        for t in range(1, len(terms)):
            pack = pack + pltpu.roll(jnp.where(live, terms[t], 0.0), _GATE_LANES * t, axis=1)
        a_ref[pl.ds(r0, ch), :] = pack.astype(_BF16)
        return carry

    lax.fori_loop(0, nc, emit, 0)


def _gate_prep(g3, bias):
    b, l, _ = g3.shape
    nc = l // MLSTM_CHUNK
    body = functools.partial(_gate_prep_body, nc=nc)
    return pl.pallas_call(
        body,
        grid=(b,),
        in_specs=[
            pl.BlockSpec((None, l, 128), lambda i: (i, 0, 0)),
            pl.BlockSpec((1, 128), lambda i: (0, 0)),
        ],
        out_specs=[
            pl.BlockSpec((None, l, 128), lambda i: (i, 0, 0)),
            pl.BlockSpec((None, 128, l), lambda i: (i, 0, 0)),
            pl.BlockSpec((None, nc, 128), lambda i: (i, 0, 0)),
        ],
        out_shape=[
            jax.ShapeDtypeStruct((b, l, 128), _BF16),
            jax.ShapeDtypeStruct((b, 128, l), _BF16),
            jax.ShapeDtypeStruct((b, nc, 128), _F32),
        ],
        scratch_shapes=[
            pltpu.VMEM((l, 128), _F32), pltpu.VMEM((l, 128), _F32), pltpu.VMEM((l, 128), _F32),
            pltpu.VMEM((nc, 128), _F32), pltpu.VMEM((nc, 128), _F32), pltpu.VMEM((nc, 128), _F32),
        ],
        compiler_params=_cparams(("arbitrary",)),
        name="gate_prep",
    )(g3, bias)


def _gate_spread_matrix():
    bc = np.zeros((128, 4 * 128), np.float32)
    for blk in range(4):
        for t in range(3 * blk, 3 * blk + 3):
            bc[t * _GATE_LANES:(t + 1) * _GATE_LANES, blk * 128:(blk + 1) * 128] = 1.0
    return jnp.asarray(bc).astype(_BF16)


def _mlstm_body(keep_ref, bc_ref, qf_ref, kf_ref, vf_ref, af_ref, rf_ref, qb_ref, kb_ref, vb_ref, ab_ref,
                rb_ref, hf_ref, hb_ref, ct_ref, nm_ref, *, nc):
    bi, c = pl.program_id(0), pl.program_id(1)

    @pl.when(c == 0)
    def _():
        ct_ref[...] = jnp.zeros_like(ct_ref)
        nm_ref[...] = jnp.zeros_like(nm_ref)

    ch, dh = MLSTM_CHUNK, MLSTM_HEAD_DIM
    jj = lax.broadcasted_iota(jnp.int32, (ch, ch), 0)
    ss = lax.broadcasted_iota(jnp.int32, (ch, ch), 1)
    head_lane = ss % _GATE_LANES
    ones_rhs = jnp.ones((ch, 128), _BF16)
    twice = lambda a: jnp.concatenate([a, a], axis=1)
    dirs = ((qf_ref, kf_ref, vf_ref, af_ref, rf_ref, hf_ref, c, ss <= jj),
            (qb_ref, kb_ref, vb_ref, ab_ref, rb_ref, hb_ref, nc - 1 - c, ss >= jj))
    for d, (q_ref, k_ref, v_ref, a_ref, r_ref, o_ref, chunk, mask) in enumerate(dirs):
        bmat = jnp.concatenate([bc_ref[:, 0:128] + r_ref[...], bc_ref[:, 128:]], axis=1)
        a_all = a_ref[...].astype(_F32)
        kbase = (bi * nc + chunk) * _GATE_LANES
        for h in range(MLSTM_HEADS):
            hd = d * MLSTM_HEADS + h
            a_h = jnp.where(head_lane == hd, a_all, 0.0).astype(_BF16)
            e = jnp.dot(a_h, bmat, preferred_element_type=_F32)
            decay = jnp.exp(jnp.where(mask, e[:, 0:128], -jnp.inf))
            rest = jnp.exp(e[:, 128:])
            iw, clamp, wrep = rest[:, 0:128], rest[:, 128:256], rest[:, 256:384]
            keep = keep_ref[kbase + hd]
            lo, hi = h * dh, (h + 1) * dh
            qh, kh, vh = q_ref[:, lo:hi], k_ref[:, lo:hi], v_ref[:, lo:hi]
            p = (_dot_nt(qh, kh) * decay).astype(_BF16)
            ct, nm = ct_ref[hd], nm_ref[hd]
            pv = jnp.dot(p, jnp.concatenate([vh, ones_rhs], axis=1), preferred_element_type=_F32)
            qc = jnp.dot(qh, jnp.concatenate([ct, nm], axis=1).astype(_BF16), preferred_element_type=_F32)
            num = pv[:, :dh] + twice(iw) * qc[:, :dh]
            den = pv[:, dh:] + iw * qc[:, dh:]
            inv = 1.0 / jnp.maximum(jnp.abs(den), clamp)
            o_ref[:, lo:hi] = (num * twice(inv)).astype(o_ref.dtype)
            vw = jnp.concatenate([(vh.astype(_F32) * twice(wrep)).astype(_BF16), wrep.astype(_BF16)], axis=1)
            upd = lax.dot_general(kh, vw, (((0,), (0,)), ((), ())), preferred_element_type=_F32)
            ct_ref[hd] = keep * ct + upd[:, :dh]
            nm_ref[hd] = keep * nm + upd[:, dh:]


def _mlstm(qk, proj3, a_mat, r_rows, keep):
    b, l, _ = qk.shape
    ch, mw = MLSTM_CHUNK, MLSTM_WIDTH
    nc = l // ch
    fw = lambda c: c
    bw = lambda c: nc - 1 - c

    def specs(cm):
        return [
            pl.BlockSpec((None, ch, mw), lambda i, c: (i, cm(c), 0)),
            pl.BlockSpec((None, ch, mw), lambda i, c: (i, cm(c), 1)),
            pl.BlockSpec((None, ch, mw), lambda i, c: (i, cm(c), _COL_V)),
            pl.BlockSpec((None, ch, 128), lambda i, c: (i, cm(c), 0)),
            pl.BlockSpec((None, 128, ch), lambda i, c: (i, 0, cm(c))),
        ]

    body = functools.partial(_mlstm_body, nc=nc)
    keep_flat = keep[:, :, :_GATE_LANES].reshape(b * nc * _GATE_LANES)
    return pl.pallas_call(
        body,
        grid=(b, nc),
        in_specs=[pl.BlockSpec(memory_space=pltpu.SMEM),
                  pl.BlockSpec((128, 4 * 128), lambda i, c: (0, 0))] + specs(fw) + specs(bw),
        out_specs=[
            pl.BlockSpec((None, ch, mw), lambda i, c: (i, c, 0)),
            pl.BlockSpec((None, ch, mw), lambda i, c: (i, nc - 1 - c, 0)),
        ],
        out_shape=[jax.ShapeDtypeStruct((b, l, mw), _BF16), jax.ShapeDtypeStruct((b, l, mw), _BF16)],
        scratch_shapes=[
            pltpu.VMEM((2 * MLSTM_HEADS, MLSTM_HEAD_DIM, MLSTM_HEAD_DIM), _F32),
            pltpu.VMEM((2 * MLSTM_HEADS, MLSTM_HEAD_DIM, 128), _F32),
        ],
        compiler_params=_cparams(("arbitrary", "arbitrary")),
        name="mlstm",
    )(keep_flat, _gate_spread_matrix(), qk, qk, proj3, a_mat, r_rows, qk, qk, proj3, a_mat, r_rows)


def _sigmoid(x):
    return 1.0 / (1.0 + jnp.exp(-x))


def _merge_body(x_ref, ya_ref, hf_ref, hb_ref, o_ref, ma_ref, mb_ref, wa_ref, wb_ref, wo_ref, out_ref):
    hsum = hf_ref[...].astype(_F32) + hb_ref[...].astype(_F32)
    yb = (_sigmoid(o_ref[...].astype(_F32)) * hsum).astype(_BF16)
    pa = jnp.dot(ya_ref[...].astype(_BF16), wa_ref[...], preferred_element_type=_F32)
    pb = jnp.dot(yb, wb_ref[...], preferred_element_type=_F32)
    mixed = _sigmoid(ma_ref[...].astype(_F32)) * pa + _sigmoid(mb_ref[...].astype(_F32)) * pb
    out_ref[...] = x_ref[...] + jnp.dot(mixed.astype(_BF16), wo_ref[...], preferred_element_type=_F32)


def _merge(x2, ya2, hf2, hb2, proj, wa, wb, wo):
    m = x2.shape[0]
    tm = 256
    d, hw, mw = D_MODEL, HYENA_WIDTH, MLSTM_WIDTH
    const = lambda shape: pl.BlockSpec(shape, lambda i: (0, 0), pipeline_mode=pl.Buffered(1))
    return pl.pallas_call(
        _merge_body,
        grid=(m // tm,),
        in_specs=[
            pl.BlockSpec((tm, d), lambda i: (i, 0)),
            pl.BlockSpec((tm, hw), lambda i: (i, 0)),
            pl.BlockSpec((tm, mw), lambda i: (i, 0)),
            pl.BlockSpec((tm, mw), lambda i: (i, 0)),
            pl.BlockSpec((tm, mw), lambda i: (i, _COL_O)),
            pl.BlockSpec((tm, d), lambda i: (i, _COL_MA // 2)),
            pl.BlockSpec((tm, d), lambda i: (i, _COL_MB // 2)),
            const((hw, d)), const((mw, d)), const((d, d)),
        ],
        out_specs=pl.BlockSpec((tm, d), lambda i: (i, 0)),
        out_shape=jax.ShapeDtypeStruct((m, d), _F32),
        compiler_params=_cparams(("arbitrary",)),
        name="merge",
    )(x2, ya2, hf2, hb2, proj, proj, proj, wa, wb, wo)


def _ffn_body(x_ref, n2_ref, wg_ref, wu_ref, wd_ref, nf_ref, o_ref, hn_ref):
    f = pl.program_id(1)

    @pl.when(f == 0)
    def _():
        x = x_ref[...]
        hn = x * lax.rsqrt(jnp.mean(x * x, axis=-1, keepdims=True) + RMS_EPS) * n2_ref[...]
        hn_ref[...] = hn.astype(_BF16)
        o_ref[...] = x

    hn = hn_ref[...]
    g = jnp.dot(hn, wg_ref[...], preferred_element_type=_F32)
    u = jnp.dot(hn, wu_ref[...], preferred_element_type=_F32)
    a = (g * _sigmoid(g) * u).astype(_BF16)
    o_ref[...] += jnp.dot(a, wd_ref[...], preferred_element_type=_F32)

    @pl.when(f == pl.num_programs(1) - 1)
    def _():
        y = o_ref[...]
        o_ref[...] = y * lax.rsqrt(jnp.mean(y * y, axis=-1, keepdims=True) + RMS_EPS) * nf_ref[...]


def _ffn(x2, norm2_w, w_gate_up, w_down, norm_f_w):
    m = x2.shape[0]
    tm, tf = 1024, 512
    d = D_MODEL
    nf = FFN_HIDDEN // tf
    return pl.pallas_call(
        _ffn_body,
        grid=(m // tm, nf),
        in_specs=[
            pl.BlockSpec((tm, d), lambda i, f: (i, 0)),
            pl.BlockSpec((1, d), lambda i, f: (0, 0)),
            pl.BlockSpec((d, tf), lambda i, f: (0, f)),
            pl.BlockSpec((d, tf), lambda i, f: (0, nf + f)),
            pl.BlockSpec((tf, d), lambda i, f: (f, 0)),
            pl.BlockSpec((1, d), lambda i, f: (0, 0)),
        ],
        out_specs=pl.BlockSpec((tm, d), lambda i, f: (i, 0)),
        out_shape=jax.ShapeDtypeStruct((m, d), _F32),
        scratch_shapes=[pltpu.VMEM((tm, d), _BF16)],
        compiler_params=_cparams(("arbitrary", "arbitrary")),
        name="ffn",
    )(x2, norm2_w, w_gate_up, w_gate_up, w_down, norm_f_w)


def kernel(x, norm1_w, w_in, conv_w, conv_b, filt_w1, filt_b1, filt_freq1, filt_w2, filt_b2, filt_freq2,
           filt_w3, hyena_bias, mlstm_gate_bias, w_branch_a, w_branch_b, w_out, norm2_w, w_gate_up, w_down,
           norm_f_w):
    b, l, d = x.shape
    assert d == D_MODEL and b % 2 == 0 and l % (_N2 * _T2_PER_PHASE) == 0
    assert norm1_w.shape[0] == 1, "single-layer block"
    hw, mw, nh = HYENA_WIDTH, MLSTM_WIDTH, MLSTM_HEADS
    m = b * l
    n = 2 * l
    n1 = n // _N2
    sc_cols = 3 * hw + 2 * mw
    g0 = sc_cols + 2 * mw

    w_t = jnp.swapaxes(w_in[0], 0, 1)
    w_first_t = w_t[:g0].astype(_BF16)
    w_merge_t = w_t[g0 + 4 * nh:].astype(_BF16)
    wg_t = w_t[g0:g0 + 4 * nh]
    gate_order = lambda a: jnp.concatenate(
        [a[0:nh], a[2 * nh:3 * nh], a[nh:2 * nh], a[3 * nh:4 * nh]], axis=0)
    wg_hi_t, wg_lo_t = _split_bf16(jnp.pad(gate_order(wg_t), ((0, 128 - 4 * nh), (0, 0))))
    gate_bias = jnp.pad(gate_order(mlstm_gate_bias[0].astype(_F32).reshape(4 * nh))[None, :],
                        ((0, 0), (0, 128 - 4 * nh)))

    x2 = x.reshape(m, d)
    proj, gates = _inproj(x2, norm1_w[0][None, :], w_merge_t, w_first_t, wg_hi_t, wg_lo_t)
    proj3 = proj.reshape(b, l, _MAIN_COLS)

    cw, cb = conv_w[0], conv_b[0][None, :]
    hy4 = _shortconv(proj3, cw, cb, _COL_HV * 1024, 0, 3 * hw, _BF16, False, True)
    qk = _shortconv(proj3, cw, cb, _COL_Q * 1024, 3 * hw, 2 * mw, _BF16, True, False)

    fa, fa_full, fai, fb, fbi = _dft_tables(n)
    zpos = _filter_positions(l)
    w1p = jnp.pad(filt_w1[0].astype(_F32), ((0, 128 - HYENA_EMB), (0, 0)))
    max_decay = math.log(HYENA_TARGET) / HYENA_FAST_DECAY
    min_decay = math.log(HYENA_TARGET) / HYENA_SLOW_DECAY
    dabs = jnp.asarray(np.abs(np.linspace(min_decay, max_decay, hw, dtype=np.float32))[None, :])
    dup = lambda a: jnp.concatenate([a, a], axis=-1)
    fh = HYENA_FILTER_HIDDEN
    w3 = filt_w3[0].astype(_F32).reshape(fh, 2, 2, hw).transpose(2, 0, 1, 3).reshape(2 * fh, 2 * hw)
    w3_hi, w3_lo = _split_bf16(w3)
    kern = _filt_mlp(zpos, w1p, filt_b1[0][None, :], filt_freq1[0][None, :], dup(filt_w2[0]),
                     dup(filt_b2[0][None, :]), dup(filt_freq2[0][None, :]), w3_hi, w3_lo, dabs)
    kspec = _filt_fft(kern.reshape(_N2, n1, 2 * hw), fa_full, fb)
    tabs = (fa, fai, fb, fbi)
    tiles = hw // _FFT_CT
    hbias = hyena_bias[0].astype(_F32)
    z4 = _hyena_conv(hy4, 0, hy4, tiles, kspec, 0, hbias[0][None, :], tabs, False)
    ya4 = _hyena_conv(z4, 0, hy4, 2 * tiles, kspec, tiles, hbias[1][None, :], tabs, True)
    ya2 = ya4.reshape(m, hw)

    a_mat, r_rows, keep = _gate_prep(gates.reshape(b, l, 128), gate_bias)
    hf, hb = _mlstm(qk, proj3, a_mat, r_rows, keep)

    x_mid = _merge(x2, ya2, hf.reshape(m, mw), hb.reshape(m, mw), proj,
                   w_branch_a[0].astype(_BF16), w_branch_b[0].astype(_BF16), w_out[0].astype(_BF16))
    out = _ffn(x_mid, norm2_w[0][None, :], w_gate_up[0].astype(_BF16), w_down[0].astype(_BF16),
               norm_f_w[None, :])
    return out.reshape(b, l, d)
```

```python
import functools
import math

import numpy as np
import jax
import jax.numpy as jnp
from jax import lax
from jax.experimental import pallas as pl
from jax.experimental.pallas import tpu as pltpu

_F32 = jnp.float32
_BF16 = jnp.bfloat16
_HIGHEST = lax.Precision.HIGHEST

D_MODEL = 2048
HYENA_WIDTH = 1024
HYENA_EMB = 33
HYENA_FILTER_HIDDEN = 64
HYENA_FAST_DECAY = 0.3
HYENA_SLOW_DECAY = 1.5
HYENA_TARGET = 1e-2
MLSTM_WIDTH = 1024
MLSTM_HEADS = 4
MLSTM_HEAD_DIM = 256
MLSTM_CHUNK = 128
FFN_HIDDEN = 5632
RMS_EPS = 1e-6

_COL_MA, _COL_MB, _COL_HV, _COL_Q, _COL_V, _COL_O = 0, 2, 4, 7, 9, 10
_MAIN_COLS = 11 * 1024

_N2 = 32
_T2_PER_PHASE = 8
_F1_PER_PHASE = 64
_FFT_CT = 256

_VMEM_LIMIT = 56 * 1024 * 1024


def _cparams(sem, vmem=_VMEM_LIMIT):
    return pltpu.CompilerParams(dimension_semantics=sem, vmem_limit_bytes=vmem)


def _split_bf16(a):
    hi = a.astype(_BF16)
    lo = (a - hi.astype(_F32)).astype(_BF16)
    return hi, lo


def _dot_nt(a, b):
    return lax.dot_general(a, b, (((1,), (1,)), ((), ())), preferred_element_type=_F32)


def _dot3(a_hi, a_lo, b_hi, b_lo):
    d = functools.partial(jnp.dot, preferred_element_type=_F32)
    return d(a_hi, b_hi) + (d(a_hi, b_lo) + d(a_lo, b_hi))


def _inproj_body(x_ref, nw_ref, w_ref, wgh_ref, wgl_ref, o_ref, g_ref, hn_ref):
    @pl.when(pl.program_id(1) == 0)
    def _():
        rows = 256
        for r in range(0, x_ref.shape[0], rows):
            x = x_ref[r:r + rows, :]
            hn = x * lax.rsqrt(jnp.mean(x * x, axis=-1, keepdims=True) + RMS_EPS) * nw_ref[...]
            hn_hi, hn_lo = _split_bf16(hn)
            hn_ref[r:r + rows, :] = hn_hi
            g_ref[r:r + rows, :] = (_dot_nt(hn_hi, wgh_ref[...])
                                    + (_dot_nt(hn_hi, wgl_ref[...]) + _dot_nt(hn_lo, wgh_ref[...])))

    o_ref[...] = _dot_nt(hn_ref[...], w_ref[...]).astype(o_ref.dtype)


def _inproj(x2, norm_w, w_t, wg_hi_t, wg_lo_t):
    m = x2.shape[0]
    tm, tn = 1024, 1408
    return pl.pallas_call(
        _inproj_body,
        grid=(m // tm, _MAIN_COLS // tn),
        in_specs=[
            pl.BlockSpec((tm, D_MODEL), lambda i, j: (i, 0)),
            pl.BlockSpec((1, D_MODEL), lambda i, j: (0, 0)),
            pl.BlockSpec((tn, D_MODEL), lambda i, j: (j, 0)),
            pl.BlockSpec((128, D_MODEL), lambda i, j: (0, 0)),
            pl.BlockSpec((128, D_MODEL), lambda i, j: (0, 0)),
        ],
        out_specs=[
            pl.BlockSpec((tm, tn), lambda i, j: (i, j)),
            pl.BlockSpec((tm, 128), lambda i, j: (i, 0)),
        ],
        out_shape=[
            jax.ShapeDtypeStruct((m, _MAIN_COLS), _BF16),
            jax.ShapeDtypeStruct((m, 128), _F32),
        ],
        scratch_shapes=[pltpu.VMEM((tm, D_MODEL), _BF16)],
        compiler_params=_cparams(("arbitrary", "arbitrary")),
        name="inproj",
    )(x2, norm_w, w_t, wg_hi_t, wg_lo_t)


def _shortconv_silu_body(u_ref, w_ref, b_ref, o_ref, *, k_scale):
    u = u_ref[...].astype(_F32)
    n = u.shape[0]
    row = lax.broadcasted_iota(jnp.int32, u.shape, 0)
    up = jnp.where(row == 0, 0.0, pltpu.roll(u, 1, axis=0))
    un = jnp.where(row == n - 1, 0.0, pltpu.roll(u, n - 1, axis=0))
    w = w_ref[...]
    y = up * w[0:1, :] + b_ref[...] + u * w[1:2, :] + un * w[2:3, :]
    y = y * (1.0 / (1.0 + jnp.exp(-y)))
    is_k = pl.program_id(1) >= (MLSTM_WIDTH // u.shape[1])
    o_ref[...] = (y * jnp.where(is_k, k_scale, 1.0)).astype(o_ref.dtype)


def _shortconv_t2_body(u_ref, w_ref, b_ref, o_ref, ut_ref):
    n, c = u_ref.shape
    h1 = n // _N2
    g = 16
    for k in range(h1 // g):
        blk = u_ref[k * g * _N2:(k + 1) * g * _N2, :].reshape(g, _N2, c)
        ut_ref[:, k * g:(k + 1) * g, :] = jnp.swapaxes(blk, 0, 1)
    w = w_ref[...]
    w0, w1, w2, bias = w[0:1, :], w[1:2, :], w[2:3, :], b_ref[...]
    slab = lambda t2: ut_ref[t2].astype(_F32)
    row = lax.broadcasted_iota(jnp.int32, (h1, c), 0)
    before_first = jnp.where(row == 0, 0.0, pltpu.roll(slab(_N2 - 1), 1, axis=0))
    after_last = jnp.where(row == h1 - 1, 0.0, pltpu.roll(slab(0), h1 - 1, axis=0))
    for t2 in range(_N2):
        up = before_first if t2 == 0 else slab(t2 - 1)
        un = after_last if t2 == _N2 - 1 else slab(t2 + 1)
        o_ref[t2] = (up * w0 + bias + slab(t2) * w1 + un * w2).astype(o_ref.dtype)


def _shortconv(proj3, conv_w, conv_b, proj_col0, conv_col0, ncols, t2_major):
    b, l, _ = proj3.shape
    ct = 256
    p0, c0 = proj_col0 // ct, conv_col0 // ct
    if t2_major:
        body = _shortconv_t2_body
        out_spec = pl.BlockSpec((None, _N2, l // _N2, ct), lambda i, j: (i, 0, 0, j))
        out_shape = jax.ShapeDtypeStruct((b, _N2, l // _N2, ncols), _BF16)
        scratch = [pltpu.VMEM((_N2, l // _N2, ct), _BF16)]
    else:
        body = functools.partial(_shortconv_silu_body, k_scale=MLSTM_HEAD_DIM ** -0.5)
        out_spec = pl.BlockSpec((None, l, ct), lambda i, j: (i, 0, j))
        out_shape = jax.ShapeDtypeStruct((b, l, ncols), _BF16)
        scratch = []
    return pl.pallas_call(
        body,
        grid=(b, ncols // ct),
        in_specs=[
            pl.BlockSpec((None, l, ct), lambda i, j: (i, 0, p0 + j)),
            pl.BlockSpec((3, ct), lambda i, j: (0, c0 + j)),
            pl.BlockSpec((1, ct), lambda i, j: (0, c0 + j)),
        ],
        out_specs=out_spec,
        out_shape=out_shape,
        scratch_shapes=scratch,
        compiler_params=_cparams(("arbitrary", "arbitrary")),
        name="shortconv" if t2_major else "shortconv_silu",
    )(proj3, conv_w, conv_b)


def _dft_tables(n):
    n2 = _N2
    n1 = n // n2
    h1 = n1 // 2
    f1 = np.arange(n1)[:, None]
    t1 = np.arange(n1)[None, :]
    ang = 2.0 * np.pi * ((f1 * t1) % n1) / n1
    c, s = np.cos(ang), np.sin(ang)
    ch, sh = c[:, :h1], s[:, :h1]
    fa = np.block([[ch, sh], [-sh, ch]])
    fa_full = np.concatenate([c, -s], axis=0)
    fai = np.block([[ch.T, -sh.T], [sh.T, ch.T]]) / n
    f1v = np.arange(n1)[:, None, None]
    f2 = np.arange(n2)[None, :, None]
    t2 = np.arange(n2)[None, None, :]
    ph = 2.0 * np.pi * ((f1v * t2 + f2 * t2 * n1) % n) / n
    tc, ts = np.cos(ph), np.sin(ph)
    fb = np.concatenate([np.concatenate([tc, ts], axis=2),
                         np.concatenate([-ts, tc], axis=2)], axis=1)
    tct, tst = np.swapaxes(tc, 1, 2), np.swapaxes(ts, 1, 2)
    fbi = np.concatenate([np.concatenate([tct, -tst], axis=2),
                          np.concatenate([tst, tct], axis=2)], axis=1)
    to = lambda a: jnp.asarray(a.astype(np.float32)).astype(_BF16)
    return to(fa), to(fa_full), to(fai), to(fb), to(fbi)


def _filter_positions(l):
    n = 2 * l
    n1 = n // _N2
    r = (np.arange(_N2)[:, None] + _N2 * np.arange(n1)[None, :]).reshape(-1)
    pos = np.where(r < l, r, n - r)
    pos = np.where(r == l, 0, pos)
    bands = (HYENA_EMB - 1) // 2
    tt = np.linspace(0.0, 1.0, l, dtype=np.float32).astype(np.float64)[pos]
    omega = (2.0 * math.pi * np.arange(l, dtype=np.float32) / l).astype(np.float32)
    freqs = np.linspace(1e-4, bands - 1, bands, dtype=np.float32)
    ang = (omega[:, None] * freqs[None, :]).astype(np.float64)[pos]
    z = np.zeros((n, 128), np.float64)
    z[:, 0] = tt
    z[:, 1:1 + bands] = np.cos(ang)
    z[:, 1 + bands:1 + 2 * bands] = -np.sin(ang)
    z[:, 33] = (r != l)
    z[:, 34] = (r < l)
    return jnp.asarray(z.astype(np.float32))


def _filt_mlp_body(z_ref, w1_ref, b1_ref, q1_ref, w2_ref, b2_ref, q2_ref, w3h_ref, w3l_ref, dabs_ref,
                   o_ref, hh_ref, hl_ref):
    fh = HYENA_FILTER_HIDDEN

    @pl.when(pl.program_id(1) == 0)
    def _():
        z = z_ref[...]
        h = jnp.sin(q1_ref[...] * (jnp.dot(z, w1_ref[...], precision=_HIGHEST,
                                           preferred_element_type=_F32) + b1_ref[...]))
        h = jnp.sin(q2_ref[...] * (jnp.dot(h, w2_ref[...], precision=_HIGHEST,
                                           preferred_element_type=_F32) + b2_ref[...]))
        valid = z[:, 33:34]
        fwd = z[:, 34:35]
        lane = lax.broadcasted_iota(jnp.int32, h.shape, 1)
        hi, lo = _split_bf16(h * (valid * jnp.where(lane < fh, fwd, 1.0 - fwd)))
        hh_ref[...] = hi
        hl_ref[...] = lo

    raw = _dot3(hh_ref[...], hl_ref[...], w3h_ref[...], w3l_ref[...])
    window = jnp.exp(-z_ref[:, 0:1] * dabs_ref[...])
    o_ref[...] = raw * window


def _filt_mlp(zpos, w1p, b1, q1, w2d, b2d, q2d, w3h, w3l, dabs):
    n = zpos.shape[0]
    ct = _FFT_CT
    tr = 1024
    per_order = HYENA_WIDTH // ct
    fh = HYENA_FILTER_HIDDEN
    full = lambda shape: pl.BlockSpec(shape, lambda r, i: (0,) * len(shape))
    return pl.pallas_call(
        _filt_mlp_body,
        grid=(n // tr, 2 * per_order),
        in_specs=[
            pl.BlockSpec((tr, 128), lambda r, i: (r, 0)),
            full((128, fh)), full((1, fh)), full((1, fh)),
            full((fh, 2 * fh)), full((1, 2 * fh)), full((1, 2 * fh)),
            pl.BlockSpec((2 * fh, ct), lambda r, i: (0, i)),
            pl.BlockSpec((2 * fh, ct), lambda r, i: (0, i)),
            pl.BlockSpec((1, ct), lambda r, i: (0, i % per_order)),
        ],
        out_specs=pl.BlockSpec((tr, ct), lambda r, i: (r, i)),
        out_shape=jax.ShapeDtypeStruct((n, 2 * HYENA_WIDTH), _F32),
        scratch_shapes=[pltpu.VMEM((tr, 2 * fh), _BF16), pltpu.VMEM((tr, 2 * fh), _BF16)],
        compiler_params=_cparams(("arbitrary", "arbitrary")),
        name="filt_mlp",
    )(zpos, w1p, b1, q1, w2d, b2d, q2d, w3h, w3l, dabs)


_F1_GROUP = 16


def _f1_tiles(f_base, f1pp):
    return [(slice(g * _F1_GROUP, (g + 1) * _F1_GROUP), pl.multiple_of(f_base + g * _F1_GROUP, _F1_GROUP))
            for g in range(f1pp // _F1_GROUP)]


def _gather_f1(s_ref, vre_ref, vim_ref, f_base, f1pp, n1):
    for sl, f0 in _f1_tiles(f_base, f1pp):
        vre_ref[sl] = jnp.swapaxes(s_ref[:, pl.ds(f0, _F1_GROUP), :], 0, 1)
        vim_ref[sl] = jnp.swapaxes(s_ref[:, pl.ds(n1 + f0, _F1_GROUP), :], 0, 1)


def _filt_fft_body(kern_ref, fa_ref, fb_ref, o_ref, s_ref, vre_ref, vim_ref, *, n_in, n1, f1pp):
    s = pl.program_id(1)

    @pl.when(s < n_in)
    def _():
        for i in range(_T2_PER_PHASE):
            u = kern_ref[i].astype(_BF16)
            s_ref[s * _T2_PER_PHASE + i] = jnp.dot(fa_ref[...], u,
                                                   preferred_element_type=_F32).astype(_BF16)

    @pl.when(s >= n_in)
    def _():
        _gather_f1(s_ref, vre_ref, vim_ref, (s - n_in) * f1pp, f1pp, n1)

        for f in range(f1pp):
            v = jnp.concatenate([vre_ref[f], vim_ref[f]], axis=0)
            o_ref[f * 2 * _N2:(f + 1) * 2 * _N2, :] = jnp.dot(fb_ref[f], v, preferred_element_type=_F32)


def _filt_fft(kern3, fa_full, fb):
    n2, n1, cols = kern3.shape
    ct = _FFT_CT
    n_in = n2 // _T2_PER_PHASE
    f1pp = min(_F1_PER_PHASE, n1)
    n_mid = n1 // f1pp
    body = functools.partial(_filt_fft_body, n_in=n_in, n1=n1, f1pp=f1pp)
    return pl.pallas_call(
        body,
        grid=(cols // ct, n_in + n_mid),
        in_specs=[
            pl.BlockSpec((_T2_PER_PHASE, n1, ct), lambda j, s: (jnp.minimum(s, n_in - 1), 0, j)),
            pl.BlockSpec((2 * n1, n1), lambda j, s: (0, 0)),
            pl.BlockSpec((f1pp, 2 * n2, 2 * n2), lambda j, s: (jnp.maximum(s - n_in, 0), 0, 0)),
        ],
        out_specs=pl.BlockSpec((f1pp * 2 * n2, ct), lambda j, s: (jnp.maximum(s - n_in, 0), j)),
        out_shape=jax.ShapeDtypeStruct((n1 * 2 * n2, cols), _F32),
        scratch_shapes=[pltpu.VMEM((n2, 2 * n1, ct), _BF16),
                        pltpu.VMEM((f1pp, n2, ct), _BF16), pltpu.VMEM((f1pp, n2, ct), _BF16)],
        compiler_params=_cparams(("arbitrary", "arbitrary")),
        name="filt_fft",
    )(kern3, fa_full, fb)


def _hyena_conv_body(u_ref, u2_ref, g_ref, k_ref, bias_ref, fa_ref, fai_ref, fb_ref, fbi_ref,
                     o_ref, s_ref, vre_ref, vim_ref, wre_ref, wim_ref, y_ref,
                     *, n_in, n_mid, n1, f1pp, natural_out):
    s = pl.program_id(2)
    h1 = n1 // 2

    @pl.when(s < n_in)
    def _():
        for i in range(_T2_PER_PHASE):
            u = jnp.concatenate([u_ref[0, i], u_ref[1, i]], axis=0)
            s_ref[s * _T2_PER_PHASE + i] = jnp.dot(fa_ref[...], u,
                                                   preferred_element_type=_F32).astype(_BF16)

    @pl.when(jnp.logical_and(s >= n_in, s < n_in + n_mid))
    def _():
        tiles = _f1_tiles((s - n_in) * f1pp, f1pp)

        def gather(sl, f0):
            vre_ref[sl] = jnp.swapaxes(s_ref[:, pl.ds(f0, _F1_GROUP), :], 0, 1)
            vim_ref[sl] = jnp.swapaxes(s_ref[:, pl.ds(n1 + f0, _F1_GROUP), :], 0, 1)

        def forward(f):
            v = jnp.concatenate([vre_ref[f], vim_ref[f]], axis=0)
            x = jnp.dot(fb_ref[f], v, preferred_element_type=_F32)
            xre, xim = x[:_N2], x[_N2:]
            kre = k_ref[f * 2 * _N2:f * 2 * _N2 + _N2, :]
            kim = k_ref[f * 2 * _N2 + _N2:(f + 1) * 2 * _N2, :]
            y_ref[f] = jnp.concatenate([xre * kre - xim * kim, xre * kim + xim * kre],
                                       axis=0).astype(_BF16)

        def inverse(f):
            w = jnp.dot(fbi_ref[f], y_ref[f], preferred_element_type=_F32)
            wre_ref[f] = w[:_N2].astype(_BF16)
            wim_ref[f] = w[_N2:].astype(_BF16)

        def scatter(sl, f0):
            s_ref[:, pl.ds(f0, _F1_GROUP), :] = jnp.swapaxes(wre_ref[sl], 0, 1)
            s_ref[:, pl.ds(n1 + f0, _F1_GROUP), :] = jnp.swapaxes(wim_ref[sl], 0, 1)

        nt = len(tiles)
        for t in tiles:
            gather(*t)
        for g in range(nt + 1):
            for k in range(_F1_GROUP):
                if g < nt:
                    forward(g * _F1_GROUP + k)
                if g >= 1:
                    inverse((g - 1) * _F1_GROUP + k)
            if g >= 1:
                scatter(*tiles[g - 1])

    @pl.when(s >= n_in + n_mid)
    def _():
        bias = bias_ref[...]
        outs = [[], []]
        for i in range(_T2_PER_PHASE):
            bv = s_ref[(s - n_in - n_mid) * _T2_PER_PHASE + i]
            y = jnp.dot(fai_ref[...], bv, preferred_element_type=_F32)
            for m in range(2):
                res = g_ref[m, i].astype(_F32) * (y[m * h1:(m + 1) * h1] + u2_ref[m, i].astype(_F32) * bias)
                if natural_out:
                    outs[m].append(res)
                else:
                    o_ref[m, i] = res.astype(o_ref.dtype)
        if natural_out:
            for m in range(2):
                o_ref[m] = jnp.swapaxes(jnp.stack(outs[m], axis=0), 0, 1).astype(o_ref.dtype)


def _hyena_conv(u4, ucol, g4, gcol, kspec, kcol, bias, tabs, natural_out):
    fa, fai, fb, fbi = tabs
    b, n2, h1, _ = u4.shape
    n1 = 2 * h1
    ct = _FFT_CT
    n_in = n2 // _T2_PER_PHASE
    f1pp = min(_F1_PER_PHASE, n1)
    n_mid = n1 // f1pp
    n_out = n_in
    tiles = HYENA_WIDTH // ct
    body = functools.partial(_hyena_conv_body, n_in=n_in, n_mid=n_mid, n1=n1, f1pp=f1pp,
                             natural_out=natural_out)
    mid = lambda s: jnp.clip(s - n_in, 0, n_mid - 1)
    last = lambda s: jnp.clip(s - n_in - n_mid, 0, n_out - 1)
    blk = (2, _T2_PER_PHASE, h1, ct)
    if natural_out:
        out_spec = pl.BlockSpec((2, h1, _T2_PER_PHASE, ct), lambda j, p, s: (p, 0, last(s), j))
        out_shape = jax.ShapeDtypeStruct((b, h1, n2, HYENA_WIDTH), _F32)
    else:
        out_spec = pl.BlockSpec(blk, lambda j, p, s: (p, last(s), 0, j))
        out_shape = jax.ShapeDtypeStruct((b, n2, h1, HYENA_WIDTH), _BF16)
    return pl.pallas_call(
        body,
        grid=(tiles, b // 2, n_in + n_mid + n_out),
        in_specs=[
            pl.BlockSpec(blk, lambda j, p, s: (p, jnp.minimum(s, n_in - 1), 0, ucol + j)),
            pl.BlockSpec(blk, lambda j, p, s: (p, last(s), 0, ucol + j)),
            pl.BlockSpec(blk, lambda j, p, s: (p, last(s), 0, gcol + j)),
            pl.BlockSpec((f1pp * 2 * n2, ct), lambda j, p, s: (mid(s), kcol + j)),
            pl.BlockSpec((1, ct), lambda j, p, s: (0, j)),
            pl.BlockSpec((2 * n1, n1), lambda j, p, s: (0, 0)),
            pl.BlockSpec((n1, 2 * n1), lambda j, p, s: (0, 0)),
            pl.BlockSpec((f1pp, 2 * n2, 2 * n2), lambda j, p, s: (mid(s), 0, 0)),
            pl.BlockSpec((f1pp, 2 * n2, 2 * n2), lambda j, p, s: (mid(s), 0, 0)),
        ],
        out_specs=out_spec,
        out_shape=out_shape,
        scratch_shapes=([pltpu.VMEM((n2, 2 * n1, ct), _BF16)] + 4 * [pltpu.VMEM((f1pp, n2, ct), _BF16)]
                        + [pltpu.VMEM((f1pp, 2 * n2, ct), _BF16)]),
        compiler_params=_cparams(("arbitrary", "arbitrary", "arbitrary")),
        name="hyena_conv",
    )(u4, u4, g4, kspec, bias, fa, fai, fb, fbi)


_GATE_LANES = 2 * MLSTM_HEADS
_TERM_R = 12


def _split3(x):
    hi = x.astype(_BF16).astype(_F32)
    mid = (x - hi).astype(_BF16).astype(_F32)
    lo = (x - hi - mid).astype(_BF16).astype(_F32)
    return hi, mid, lo


def _scan_order_max(x, is_fwd):
    n = x.shape[0]
    row = lax.broadcasted_iota(jnp.int32, x.shape, 0)
    pre, suf = x, x
    shift = 1
    while shift < n:
        pre = jnp.maximum(pre, jnp.where(row >= shift, pltpu.roll(pre, shift, axis=0), -jnp.inf))
        suf = jnp.maximum(suf, jnp.where(row < n - shift, pltpu.roll(suf, n - shift, axis=0), -jnp.inf))
        shift *= 2
    return jnp.where(is_fwd, pre, suf)


def _gate_prep_body(g_ref, bias_ref, a_ref, row_ref, keep_ref,
                    b_scr, a_scr, pm_scr, tot_scr, mloc_scr, mprev_scr, *, nc):
    ch = MLSTM_CHUNK
    lane = lax.broadcasted_iota(jnp.int32, (1, 128), 1)
    is_fwd = lane < MLSTM_HEADS
    live = lane < _GATE_LANES
    jj = lax.broadcasted_iota(jnp.int32, (ch, ch), 0)
    ss = lax.broadcasted_iota(jnp.int32, (ch, ch), 1)
    t_lo = (ss <= jj).astype(_F32)
    t_up = (ss >= jj).astype(_F32)
    row_ref[...] = jnp.zeros_like(row_ref)

    def chunk_stats(c, carry):
        r0 = pl.multiple_of(c * ch, ch)
        gi = g_ref[pl.ds(r0, ch), :] + bias_ref[...]
        gf = pltpu.roll(gi, 128 - 2 * MLSTM_HEADS, axis=1)
        logf = jnp.minimum(gf, 0.0) - jnp.log1p(jnp.exp(-jnp.abs(gf)))
        cs_lo = jnp.dot(t_lo, logf, precision=_HIGHEST, preferred_element_type=_F32)
        cs_up = jnp.dot(t_up, logf, precision=_HIGHEST, preferred_element_type=_F32)
        bcs = jnp.where(is_fwd, cs_lo, cs_up)
        tot = cs_lo[ch - 1:ch, :]
        a = tot - bcs + gi
        b_scr[pl.ds(r0, ch), :] = bcs
        a_scr[pl.ds(r0, ch), :] = a
        tot_scr[pl.ds(c, 1), :] = tot
        mloc_scr[pl.ds(c, 1), :] = jnp.max(a, axis=0, keepdims=True)
        r = gi - bcs
        pm_scr[pl.ds(r0, ch), :] = _scan_order_max(r, is_fwd)
        hi, mid, lo = _split3(r.T[0:_GATE_LANES, :])
        base = _TERM_R * _GATE_LANES
        row_ref[base:base + 16, pl.ds(r0, ch)] = jnp.concatenate([hi, mid], axis=0).astype(_BF16)
        row_ref[base + 16:base + 32, pl.ds(r0, ch)] = jnp.concatenate(
            [lo, jnp.zeros_like(lo)], axis=0).astype(_BF16)
        return carry

    lax.fori_loop(0, nc, chunk_stats, 0)

    def scan_f(c, m):
        mprev_scr[pl.ds(c, 1), :] = jnp.where(is_fwd, m, mprev_scr[pl.ds(c, 1), :])
        return jnp.maximum(tot_scr[pl.ds(c, 1), :] + m, mloc_scr[pl.ds(c, 1), :])

    def scan_b(i, m):
        c = nc - 1 - i
        mprev_scr[pl.ds(c, 1), :] = jnp.where(is_fwd, mprev_scr[pl.ds(c, 1), :], m)
        return jnp.maximum(tot_scr[pl.ds(c, 1), :] + m, mloc_scr[pl.ds(c, 1), :])

    mprev_scr[...] = jnp.zeros_like(mprev_scr)
    lax.fori_loop(0, nc, scan_f, jnp.zeros((1, 128), _F32))
    lax.fori_loop(0, nc, scan_b, jnp.zeros((1, 128), _F32))

    def emit(c, carry):
        r0 = pl.multiple_of(c * ch, ch)
        mprev = mprev_scr[pl.ds(c, 1), :]
        tot = tot_scr[pl.ds(c, 1), :]
        mnew = jnp.maximum(tot + mprev, mloc_scr[pl.ds(c, 1), :])
        keep_ref[pl.ds(c, 1), :] = jnp.exp(tot + mprev - mnew)
        g = jnp.maximum(mprev, pm_scr[pl.ds(r0, ch), :])
        terms = (_split3(-g) + _split3(mprev - g) + _split3(-b_scr[pl.ds(r0, ch), :] - g)
                 + _split3(a_scr[pl.ds(r0, ch), :] - mnew) + 3 * (jnp.ones((ch, 128), _F32),))
        pack = jnp.where(live, terms[0], 0.0)
        for t in range(1, len(terms)):
            pack = pack + pltpu.roll(jnp.where(live, terms[t], 0.0), _GATE_LANES * t, axis=1)
        a_ref[pl.ds(r0, ch), :] = pack.astype(_BF16)
        return carry

    lax.fori_loop(0, nc, emit, 0)


def _gate_prep(g3, bias):
    b, l, _ = g3.shape
    nc = l // MLSTM_CHUNK
    body = functools.partial(_gate_prep_body, nc=nc)
    return pl.pallas_call(
        body,
        grid=(b,),
        in_specs=[
            pl.BlockSpec((None, l, 128), lambda i: (i, 0, 0)),
            pl.BlockSpec((1, 128), lambda i: (0, 0)),
        ],
        out_specs=[
            pl.BlockSpec((None, l, 128), lambda i: (i, 0, 0)),
            pl.BlockSpec((None, 128, l), lambda i: (i, 0, 0)),
            pl.BlockSpec((None, nc, 128), lambda i: (i, 0, 0)),
        ],
        out_shape=[
            jax.ShapeDtypeStruct((b, l, 128), _BF16),
            jax.ShapeDtypeStruct((b, 128, l), _BF16),
            jax.ShapeDtypeStruct((b, nc, 128), _F32),
        ],
        scratch_shapes=[
            pltpu.VMEM((l, 128), _F32), pltpu.VMEM((l, 128), _F32), pltpu.VMEM((l, 128), _F32),
            pltpu.VMEM((nc, 128), _F32), pltpu.VMEM((nc, 128), _F32), pltpu.VMEM((nc, 128), _F32),
        ],
        compiler_params=_cparams(("arbitrary",)),
        name="gate_prep",
    )(g3, bias)


def _gate_spread_matrix():
    bc = np.zeros((128, 4 * 128), np.float32)
    for blk in range(4):
        for t in range(3 * blk, 3 * blk + 3):
            bc[t * _GATE_LANES:(t + 1) * _GATE_LANES, blk * 128:(blk + 1) * 128] = 1.0
    return jnp.asarray(bc).astype(_BF16)


def _mlstm_body(keep_ref, bc_ref, qf_ref, kf_ref, vf_ref, af_ref, rf_ref, qb_ref, kb_ref, vb_ref, ab_ref,
                rb_ref, hf_ref, hb_ref, ct_ref, nm_ref, *, nc):
    bi, c = pl.program_id(0), pl.program_id(1)

    @pl.when(c == 0)
    def _():
        ct_ref[...] = jnp.zeros_like(ct_ref)
        nm_ref[...] = jnp.zeros_like(nm_ref)

    ch, dh = MLSTM_CHUNK, MLSTM_HEAD_DIM
    jj = lax.broadcasted_iota(jnp.int32, (ch, ch), 0)
    ss = lax.broadcasted_iota(jnp.int32, (ch, ch), 1)
    head_lane = ss % _GATE_LANES
    ones_rhs = jnp.ones((ch, 128), _BF16)
    twice = lambda a: jnp.concatenate([a, a], axis=1)
    dirs = ((qf_ref, kf_ref, vf_ref, af_ref, rf_ref, hf_ref, c, ss <= jj),
            (qb_ref, kb_ref, vb_ref, ab_ref, rb_ref, hb_ref, nc - 1 - c, ss >= jj))
    for d, (q_ref, k_ref, v_ref, a_ref, r_ref, o_ref, chunk, mask) in enumerate(dirs):
        bmat = jnp.concatenate([bc_ref[:, 0:128] + r_ref[...], bc_ref[:, 128:]], axis=1)
        a_all = a_ref[...].astype(_F32)
        kbase = (bi * nc + chunk) * _GATE_LANES
        for h in range(MLSTM_HEADS):
            hd = d * MLSTM_HEADS + h
            a_h = jnp.where(head_lane == hd, a_all, 0.0).astype(_BF16)
            e = jnp.dot(a_h, bmat, preferred_element_type=_F32)
            decay = jnp.exp(jnp.where(mask, e[:, 0:128], -jnp.inf))
            rest = jnp.exp(e[:, 128:])
            iw, clamp, wrep = rest[:, 0:128], rest[:, 128:256], rest[:, 256:384]
            keep = keep_ref[kbase + hd]
            lo, hi = h * dh, (h + 1) * dh
            qh, kh, vh = q_ref[:, lo:hi], k_ref[:, lo:hi], v_ref[:, lo:hi]
            p = (_dot_nt(qh, kh) * decay).astype(_BF16)
            ct, nm = ct_ref[hd], nm_ref[hd]
            pv = jnp.dot(p, jnp.concatenate([vh, ones_rhs], axis=1), preferred_element_type=_F32)
            qc = jnp.dot(qh, jnp.concatenate([ct, nm], axis=1).astype(_BF16), preferred_element_type=_F32)
            num = pv[:, :dh] + twice(iw) * qc[:, :dh]
            den = pv[:, dh:] + iw * qc[:, dh:]
            inv = 1.0 / jnp.maximum(jnp.abs(den), clamp)
            o_ref[:, lo:hi] = (num * twice(inv)).astype(o_ref.dtype)
            vw = jnp.concatenate([(vh.astype(_F32) * twice(wrep)).astype(_BF16), wrep.astype(_BF16)], axis=1)
            upd = lax.dot_general(kh, vw, (((0,), (0,)), ((), ())), preferred_element_type=_F32)
            ct_ref[hd] = keep * ct + upd[:, :dh]
            nm_ref[hd] = keep * nm + upd[:, dh:]


def _mlstm(qk, proj3, a_mat, r_rows, keep):
    b, l, _ = qk.shape
    ch, mw = MLSTM_CHUNK, MLSTM_WIDTH
    nc = l // ch
    fw = lambda c: c
    bw = lambda c: nc - 1 - c

    def specs(cm):
        return [
            pl.BlockSpec((None, ch, mw), lambda i, c: (i, cm(c), 0)),
            pl.BlockSpec((None, ch, mw), lambda i, c: (i, cm(c), 1)),
            pl.BlockSpec((None, ch, mw), lambda i, c: (i, cm(c), _COL_V)),
            pl.BlockSpec((None, ch, 128), lambda i, c: (i, cm(c), 0)),
            pl.BlockSpec((None, 128, ch), lambda i, c: (i, 0, cm(c))),
        ]

    body = functools.partial(_mlstm_body, nc=nc)
    keep_flat = keep[:, :, :_GATE_LANES].reshape(b * nc * _GATE_LANES)
    return pl.pallas_call(
        body,
        grid=(b, nc),
        in_specs=[pl.BlockSpec(memory_space=pltpu.SMEM),
                  pl.BlockSpec((128, 4 * 128), lambda i, c: (0, 0))] + specs(fw) + specs(bw),
        out_specs=[
            pl.BlockSpec((None, ch, mw), lambda i, c: (i, c, 0)),
            pl.BlockSpec((None, ch, mw), lambda i, c: (i, nc - 1 - c, 0)),
        ],
        out_shape=[jax.ShapeDtypeStruct((b, l, mw), _BF16), jax.ShapeDtypeStruct((b, l, mw), _BF16)],
        scratch_shapes=[
            pltpu.VMEM((2 * MLSTM_HEADS, MLSTM_HEAD_DIM, MLSTM_HEAD_DIM), _F32),
            pltpu.VMEM((2 * MLSTM_HEADS, MLSTM_HEAD_DIM, 128), _F32),
        ],
        compiler_params=_cparams(("arbitrary", "arbitrary")),
        name="mlstm",
    )(keep_flat, _gate_spread_matrix(), qk, qk, proj3, a_mat, r_rows, qk, qk, proj3, a_mat, r_rows)


def _sigmoid(x):
    return 1.0 / (1.0 + jnp.exp(-x))


def _merge_body(x_ref, ya_ref, hf_ref, hb_ref, o_ref, ma_ref, mb_ref, wa_ref, wb_ref, wo_ref, out_ref):
    hsum = hf_ref[...].astype(_F32) + hb_ref[...].astype(_F32)
    yb = (_sigmoid(o_ref[...].astype(_F32)) * hsum).astype(_BF16)
    pa = jnp.dot(ya_ref[...].astype(_BF16), wa_ref[...], preferred_element_type=_F32)
    pb = jnp.dot(yb, wb_ref[...], preferred_element_type=_F32)
    mixed = _sigmoid(ma_ref[...].astype(_F32)) * pa + _sigmoid(mb_ref[...].astype(_F32)) * pb
    out_ref[...] = x_ref[...] + jnp.dot(mixed.astype(_BF16), wo_ref[...], preferred_element_type=_F32)


def _merge(x2, ya2, hf2, hb2, proj, wa, wb, wo):
    m = x2.shape[0]
    tm = 256
    d, hw, mw = D_MODEL, HYENA_WIDTH, MLSTM_WIDTH
    const = lambda shape: pl.BlockSpec(shape, lambda i: (0, 0), pipeline_mode=pl.Buffered(1))
    return pl.pallas_call(
        _merge_body,
        grid=(m // tm,),
        in_specs=[
            pl.BlockSpec((tm, d), lambda i: (i, 0)),
            pl.BlockSpec((tm, hw), lambda i: (i, 0)),
            pl.BlockSpec((tm, mw), lambda i: (i, 0)),
            pl.BlockSpec((tm, mw), lambda i: (i, 0)),
            pl.BlockSpec((tm, mw), lambda i: (i, _COL_O)),
            pl.BlockSpec((tm, d), lambda i: (i, _COL_MA // 2)),
            pl.BlockSpec((tm, d), lambda i: (i, _COL_MB // 2)),
            const((hw, d)), const((mw, d)), const((d, d)),
        ],
        out_specs=pl.BlockSpec((tm, d), lambda i: (i, 0)),
        out_shape=jax.ShapeDtypeStruct((m, d), _F32),
        compiler_params=_cparams(("arbitrary",)),
        name="merge",
    )(x2, ya2, hf2, hb2, proj, proj, proj, wa, wb, wo)


def _ffn_body(x_ref, n2_ref, wg_ref, wu_ref, wd_ref, nf_ref, o_ref, hn_ref):
    f = pl.program_id(1)

    @pl.when(f == 0)
    def _():
        x = x_ref[...]
        hn = x * lax.rsqrt(jnp.mean(x * x, axis=-1, keepdims=True) + RMS_EPS) * n2_ref[...]
        hn_ref[...] = hn.astype(_BF16)
        o_ref[...] = x

    hn = hn_ref[...]
    g = jnp.dot(hn, wg_ref[...], preferred_element_type=_F32)
    u = jnp.dot(hn, wu_ref[...], preferred_element_type=_F32)
    a = (g * _sigmoid(g) * u).astype(_BF16)
    o_ref[...] += jnp.dot(a, wd_ref[...], preferred_element_type=_F32)

    @pl.when(f == pl.num_programs(1) - 1)
    def _():
        y = o_ref[...]
        o_ref[...] = y * lax.rsqrt(jnp.mean(y * y, axis=-1, keepdims=True) + RMS_EPS) * nf_ref[...]


def _ffn(x2, norm2_w, w_gate_up, w_down, norm_f_w):
    m = x2.shape[0]
    tm, tf = 1024, 512
    d = D_MODEL
    nf = FFN_HIDDEN // tf
    return pl.pallas_call(
        _ffn_body,
        grid=(m // tm, nf),
        in_specs=[
            pl.BlockSpec((tm, d), lambda i, f: (i, 0)),
            pl.BlockSpec((1, d), lambda i, f: (0, 0)),
            pl.BlockSpec((d, tf), lambda i, f: (0, f)),
            pl.BlockSpec((d, tf), lambda i, f: (0, nf + f)),
            pl.BlockSpec((tf, d), lambda i, f: (f, 0)),
            pl.BlockSpec((1, d), lambda i, f: (0, 0)),
        ],
        out_specs=pl.BlockSpec((tm, d), lambda i, f: (i, 0)),
        out_shape=jax.ShapeDtypeStruct((m, d), _F32),
        scratch_shapes=[pltpu.VMEM((tm, d), _BF16)],
        compiler_params=_cparams(("arbitrary", "arbitrary")),
        name="ffn",
    )(x2, norm2_w, w_gate_up, w_gate_up, w_down, norm_f_w)


def kernel(x, norm1_w, w_in, conv_w, conv_b, filt_w1, filt_b1, filt_freq1, filt_w2, filt_b2, filt_freq2,
           filt_w3, hyena_bias, mlstm_gate_bias, w_branch_a, w_branch_b, w_out, norm2_w, w_gate_up, w_down,
           norm_f_w):
    b, l, d = x.shape
    assert d == D_MODEL and b % 2 == 0 and l % (_N2 * _T2_PER_PHASE) == 0
    assert norm1_w.shape[0] == 1, "single-layer block"
    hw, mw, nh = HYENA_WIDTH, MLSTM_WIDTH, MLSTM_HEADS
    m = b * l
    n = 2 * l
    n1 = n // _N2
    sc_cols = 3 * hw + 2 * mw
    g0 = sc_cols + 2 * mw

    w_t = jnp.swapaxes(w_in[0], 0, 1)
    w_main_t = jnp.concatenate([w_t[g0 + 4 * nh:], w_t[:g0]], axis=0).astype(_BF16)
    wg_t = w_t[g0:g0 + 4 * nh]
    gate_order = lambda a: jnp.concatenate(
        [a[0:nh], a[2 * nh:3 * nh], a[nh:2 * nh], a[3 * nh:4 * nh]], axis=0)
    wg_hi_t, wg_lo_t = _split_bf16(jnp.pad(gate_order(wg_t), ((0, 128 - 4 * nh), (0, 0))))
    gate_bias = jnp.pad(gate_order(mlstm_gate_bias[0].astype(_F32).reshape(4 * nh))[None, :],
                        ((0, 0), (0, 128 - 4 * nh)))

    x2 = x.reshape(m, d)
    proj, gates = _inproj(x2, norm1_w[0][None, :], w_main_t, wg_hi_t, wg_lo_t)
    proj3 = proj.reshape(b, l, _MAIN_COLS)

    cw, cb = conv_w[0], conv_b[0][None, :]
    hy4 = _shortconv(proj3, cw, cb, _COL_HV * 1024, 0, 3 * hw, True)
    qk = _shortconv(proj3, cw, cb, _COL_Q * 1024, 3 * hw, 2 * mw, False)

    fa, fa_full, fai, fb, fbi = _dft_tables(n)
    zpos = _filter_positions(l)
    w1p = jnp.pad(filt_w1[0].astype(_F32), ((0, 128 - HYENA_EMB), (0, 0)))
    max_decay = math.log(HYENA_TARGET) / HYENA_FAST_DECAY
    min_decay = math.log(HYENA_TARGET) / HYENA_SLOW_DECAY
    dabs = jnp.asarray(np.abs(np.linspace(min_decay, max_decay, hw, dtype=np.float32))[None, :])
    dup = lambda a: jnp.concatenate([a, a], axis=-1)
    fh = HYENA_FILTER_HIDDEN
    w3 = filt_w3[0].astype(_F32).reshape(fh, 2, 2, hw).transpose(2, 0, 1, 3).reshape(2 * fh, 2 * hw)
    w3_hi, w3_lo = _split_bf16(w3)
    kern = _filt_mlp(zpos, w1p, filt_b1[0][None, :], filt_freq1[0][None, :], dup(filt_w2[0]),
                     dup(filt_b2[0][None, :]), dup(filt_freq2[0][None, :]), w3_hi, w3_lo, dabs)
    kspec = _filt_fft(kern.reshape(_N2, n1, 2 * hw), fa_full, fb)
    tabs = (fa, fai, fb, fbi)
    tiles = hw // _FFT_CT
    hbias = hyena_bias[0].astype(_F32)
    z4 = _hyena_conv(hy4, 0, hy4, tiles, kspec, 0, hbias[0][None, :], tabs, False)
    ya4 = _hyena_conv(z4, 0, hy4, 2 * tiles, kspec, tiles, hbias[1][None, :], tabs, True)
    ya2 = ya4.reshape(m, hw)

    a_mat, r_rows, keep = _gate_prep(gates.reshape(b, l, 128), gate_bias)
    hf, hb = _mlstm(qk, proj3, a_mat, r_rows, keep)

    x_mid = _merge(x2, ya2, hf.reshape(m, mw), hb.reshape(m, mw), proj,
                   w_branch_a[0].astype(_BF16), w_branch_b[0].astype(_BF16), w_out[0].astype(_BF16))
    out = _ffn(x_mid, norm2_w[0][None, :], w_gate_up[0].astype(_BF16), w_down[0].astype(_BF16),
               norm_f_w[None, :])
    return out.reshape(b, l, d)
```

```python
import functools
import math

import numpy as np
import jax
import jax.numpy as jnp
from jax import lax
from jax.experimental import pallas as pl
from jax.experimental.pallas import tpu as pltpu

_F32 = jnp.float32
_BF16 = jnp.bfloat16
_HIGHEST = lax.Precision.HIGHEST

D_MODEL = 2048
HYENA_WIDTH = 1024
HYENA_EMB = 33
HYENA_FILTER_HIDDEN = 64
HYENA_FAST_DECAY = 0.3
HYENA_SLOW_DECAY = 1.5
HYENA_TARGET = 1e-2
MLSTM_WIDTH = 1024
MLSTM_HEADS = 4
MLSTM_HEAD_DIM = 256
MLSTM_CHUNK = 128
FFN_HIDDEN = 5632
RMS_EPS = 1e-6

_COL_MA, _COL_MB, _COL_HV, _COL_Q, _COL_V, _COL_O = 0, 2, 4, 7, 9, 10
_MAIN_COLS = 11 * 1024

_N2 = 32
_T2_PER_PHASE = 16
_F1_PER_PHASE = 64
_FFT_CT = 256

_VMEM_LIMIT = 56 * 1024 * 1024


def _cparams(sem, vmem=_VMEM_LIMIT):
    return pltpu.CompilerParams(dimension_semantics=sem, vmem_limit_bytes=vmem)


def _split_bf16(a):
    hi = a.astype(_BF16)
    lo = (a - hi.astype(_F32)).astype(_BF16)
    return hi, lo


def _dot_nt(a, b):
    return lax.dot_general(a, b, (((1,), (1,)), ((), ())), preferred_element_type=_F32)


def _dot3(a_hi, a_lo, b_hi, b_lo):
    d = functools.partial(jnp.dot, preferred_element_type=_F32)
    return d(a_hi, b_hi) + (d(a_hi, b_lo) + d(a_lo, b_hi))


def _inproj_body(x_ref, nw_ref, wm_ref, w_ref, wgh_ref, wgl_ref, o_ref, g_ref, hn_ref, *, n_merge):
    j = pl.program_id(1)

    @pl.when(j == 0)
    def _():
        rows = 256
        for r in range(0, x_ref.shape[0], rows):
            x = x_ref[r:r + rows, :]
            hn = x * lax.rsqrt(jnp.mean(x * x, axis=-1, keepdims=True) + RMS_EPS) * nw_ref[...]
            hn_hi, hn_lo = _split_bf16(hn)
            hn_ref[r:r + rows, :] = hn_hi
            g_ref[r:r + rows, :] = (_dot_nt(hn_hi, wgh_ref[...])
                                    + (_dot_nt(hn_hi, wgl_ref[...]) + _dot_nt(hn_lo, wgh_ref[...])))

    @pl.when(j < n_merge)
    def _():
        o_ref[...] = _dot_nt(hn_ref[...], wm_ref[...]).astype(o_ref.dtype)

    @pl.when(j >= n_merge)
    def _():
        o_ref[...] = _dot_nt(hn_ref[...], w_ref[...]).astype(o_ref.dtype)


def _inproj(x2, norm_w, w_merge_t, w_all_t, wg_hi_t, wg_lo_t):
    m = x2.shape[0]
    tm, tn = 1024, 1024
    n_merge = w_merge_t.shape[0] // tn
    body = functools.partial(_inproj_body, n_merge=n_merge)
    return pl.pallas_call(
        body,
        grid=(m // tm, _MAIN_COLS // tn),
        in_specs=[
            pl.BlockSpec((tm, D_MODEL), lambda i, j: (i, 0)),
            pl.BlockSpec((1, D_MODEL), lambda i, j: (0, 0)),
            pl.BlockSpec((tn, D_MODEL), lambda i, j: (jnp.minimum(j, n_merge - 1), 0)),
            pl.BlockSpec((tn, D_MODEL), lambda i, j: (jnp.maximum(j - n_merge, 0), 0)),
            pl.BlockSpec((128, D_MODEL), lambda i, j: (0, 0)),
            pl.BlockSpec((128, D_MODEL), lambda i, j: (0, 0)),
        ],
        out_specs=[
            pl.BlockSpec((tm, tn), lambda i, j: (i, j)),
            pl.BlockSpec((tm, 128), lambda i, j: (i, 0)),
        ],
        out_shape=[
            jax.ShapeDtypeStruct((m, _MAIN_COLS), _BF16),
            jax.ShapeDtypeStruct((m, 128), _F32),
        ],
        scratch_shapes=[pltpu.VMEM((tm, D_MODEL), _BF16)],
        compiler_params=_cparams(("arbitrary", "arbitrary")),
        name="inproj",
    )(x2, norm_w, w_merge_t, w_all_t, wg_hi_t, wg_lo_t)


def _shortconv_silu_body(u_ref, w_ref, b_ref, o_ref, *, k_scale):
    u = u_ref[...].astype(_F32)
    n = u.shape[0]
    row = lax.broadcasted_iota(jnp.int32, u.shape, 0)
    up = jnp.where(row == 0, 0.0, pltpu.roll(u, 1, axis=0))
    un = jnp.where(row == n - 1, 0.0, pltpu.roll(u, n - 1, axis=0))
    w = w_ref[...]
    y = up * w[0:1, :] + b_ref[...] + u * w[1:2, :] + un * w[2:3, :]
    y = y * (1.0 / (1.0 + jnp.exp(-y)))
    is_k = pl.program_id(1) >= (MLSTM_WIDTH // u.shape[1])
    o_ref[...] = (y * jnp.where(is_k, k_scale, 1.0)).astype(o_ref.dtype)


def _shortconv_t2_body(u_ref, w_ref, b_ref, o_ref, ut_ref):
    n, c = u_ref.shape
    h1 = n // _N2
    g = 16
    for k in range(h1 // g):
        blk = u_ref[k * g * _N2:(k + 1) * g * _N2, :].reshape(g, _N2, c)
        ut_ref[:, k * g:(k + 1) * g, :] = jnp.swapaxes(blk, 0, 1)
    w = w_ref[...]
    w0, w1, w2, bias = w[0:1, :], w[1:2, :], w[2:3, :], b_ref[...]
    slab = lambda t2: ut_ref[t2].astype(_F32)
    row = lax.broadcasted_iota(jnp.int32, (h1, c), 0)
    before_first = jnp.where(row == 0, 0.0, pltpu.roll(slab(_N2 - 1), 1, axis=0))
    after_last = jnp.where(row == h1 - 1, 0.0, pltpu.roll(slab(0), h1 - 1, axis=0))
    for t2 in range(_N2):
        up = before_first if t2 == 0 else slab(t2 - 1)
        un = after_last if t2 == _N2 - 1 else slab(t2 + 1)
        o_ref[t2] = (up * w0 + bias + slab(t2) * w1 + un * w2).astype(o_ref.dtype)


def _shortconv(proj3, conv_w, conv_b, proj_col0, conv_col0, ncols, t2_major):
    b, l, _ = proj3.shape
    ct = 256
    p0, c0 = proj_col0 // ct, conv_col0 // ct
    if t2_major:
        body = _shortconv_t2_body
        out_spec = pl.BlockSpec((None, _N2, l // _N2, ct), lambda i, j: (i, 0, 0, j))
        out_shape = jax.ShapeDtypeStruct((b, _N2, l // _N2, ncols), _BF16)
        scratch = [pltpu.VMEM((_N2, l // _N2, ct), _BF16)]
    else:
        body = functools.partial(_shortconv_silu_body, k_scale=MLSTM_HEAD_DIM ** -0.5)
        out_spec = pl.BlockSpec((None, l, ct), lambda i, j: (i, 0, j))
        out_shape = jax.ShapeDtypeStruct((b, l, ncols), _BF16)
        scratch = []
    return pl.pallas_call(
        body,
        grid=(b, ncols // ct),
        in_specs=[
            pl.BlockSpec((None, l, ct), lambda i, j: (i, 0, p0 + j)),
            pl.BlockSpec((3, ct), lambda i, j: (0, c0 + j)),
            pl.BlockSpec((1, ct), lambda i, j: (0, c0 + j)),
        ],
        out_specs=out_spec,
        out_shape=out_shape,
        scratch_shapes=scratch,
        compiler_params=_cparams(("arbitrary", "arbitrary")),
        name="shortconv" if t2_major else "shortconv_silu",
    )(proj3, conv_w, conv_b)


def _dft_tables(n):
    n2 = _N2
    n1 = n // n2
    h1 = n1 // 2
    f1 = np.arange(n1)[:, None]
    t1 = np.arange(n1)[None, :]
    ang = 2.0 * np.pi * ((f1 * t1) % n1) / n1
    c, s = np.cos(ang), np.sin(ang)
    ch, sh = c[:, :h1], s[:, :h1]
    fa = np.block([[ch, sh], [-sh, ch]])
    fa_full = np.concatenate([c, -s], axis=0)
    fai = np.block([[ch.T, -sh.T], [sh.T, ch.T]]) / n
    f1v = np.arange(n1)[:, None, None]
    f2 = np.arange(n2)[None, :, None]
    t2 = np.arange(n2)[None, None, :]
    ph = 2.0 * np.pi * ((f1v * t2 + f2 * t2 * n1) % n) / n
    tc, ts = np.cos(ph), np.sin(ph)
    fb = np.concatenate([np.concatenate([tc, ts], axis=2),
                         np.concatenate([-ts, tc], axis=2)], axis=1)
    tct, tst = np.swapaxes(tc, 1, 2), np.swapaxes(ts, 1, 2)
    fbi = np.concatenate([np.concatenate([tct, -tst], axis=2),
                          np.concatenate([tst, tct], axis=2)], axis=1)
    to = lambda a: jnp.asarray(a.astype(np.float32)).astype(_BF16)
    return to(fa), to(fa_full), to(fai), to(fb), to(fbi)


def _filter_positions(l):
    n = 2 * l
    n1 = n // _N2
    r = (np.arange(_N2)[:, None] + _N2 * np.arange(n1)[None, :]).reshape(-1)
    pos = np.where(r < l, r, n - r)
    pos = np.where(r == l, 0, pos)
    bands = (HYENA_EMB - 1) // 2
    tt = np.linspace(0.0, 1.0, l, dtype=np.float32).astype(np.float64)[pos]
    omega = (2.0 * math.pi * np.arange(l, dtype=np.float32) / l).astype(np.float32)
    freqs = np.linspace(1e-4, bands - 1, bands, dtype=np.float32)
    ang = (omega[:, None] * freqs[None, :]).astype(np.float64)[pos]
    z = np.zeros((n, 128), np.float64)
    z[:, 0] = tt
    z[:, 1:1 + bands] = np.cos(ang)
    z[:, 1 + bands:1 + 2 * bands] = -np.sin(ang)
    z[:, 33] = (r != l)
    z[:, 34] = (r < l)
    return jnp.asarray(z.astype(np.float32))


def _filt_mlp_body(z_ref, w1_ref, b1_ref, q1_ref, w2_ref, b2_ref, q2_ref, w3h_ref, w3l_ref, dabs_ref,
                   o_ref, hh_ref, hl_ref):
    fh = HYENA_FILTER_HIDDEN

    @pl.when(pl.program_id(1) == 0)
    def _():
        z = z_ref[...]
        h = jnp.sin(q1_ref[...] * (jnp.dot(z, w1_ref[...], precision=_HIGHEST,
                                           preferred_element_type=_F32) + b1_ref[...]))
        h = jnp.sin(q2_ref[...] * (jnp.dot(h, w2_ref[...], precision=_HIGHEST,
                                           preferred_element_type=_F32) + b2_ref[...]))
        valid = z[:, 33:34]
        fwd = z[:, 34:35]
        lane = lax.broadcasted_iota(jnp.int32, h.shape, 1)
        hi, lo = _split_bf16(h * (valid * jnp.where(lane < fh, fwd, 1.0 - fwd)))
        hh_ref[...] = hi
        hl_ref[...] = lo

    raw = _dot3(hh_ref[...], hl_ref[...], w3h_ref[...], w3l_ref[...])
    window = jnp.exp(-z_ref[:, 0:1] * dabs_ref[...])
    o_ref[...] = raw * window


def _filt_mlp(zpos, w1p, b1, q1, w2d, b2d, q2d, w3h, w3l, dabs):
    n = zpos.shape[0]
    ct = _FFT_CT
    tr = 1024
    per_order = HYENA_WIDTH // ct
    fh = HYENA_FILTER_HIDDEN
    full = lambda shape: pl.BlockSpec(shape, lambda r, i: (0,) * len(shape))
    return pl.pallas_call(
        _filt_mlp_body,
        grid=(n // tr, 2 * per_order),
        in_specs=[
            pl.BlockSpec((tr, 128), lambda r, i: (r, 0)),
            full((128, fh)), full((1, fh)), full((1, fh)),
            full((fh, 2 * fh)), full((1, 2 * fh)), full((1, 2 * fh)),
            pl.BlockSpec((2 * fh, ct), lambda r, i: (0, i)),
            pl.BlockSpec((2 * fh, ct), lambda r, i: (0, i)),
            pl.BlockSpec((1, ct), lambda r, i: (0, i % per_order)),
        ],
        out_specs=pl.BlockSpec((tr, ct), lambda r, i: (r, i)),
        out_shape=jax.ShapeDtypeStruct((n, 2 * HYENA_WIDTH), _F32),
        scratch_shapes=[pltpu.VMEM((tr, 2 * fh), _BF16), pltpu.VMEM((tr, 2 * fh), _BF16)],
        compiler_params=_cparams(("arbitrary", "arbitrary")),
        name="filt_mlp",
    )(zpos, w1p, b1, q1, w2d, b2d, q2d, w3h, w3l, dabs)


_F1_GROUP = 16


def _f1_tiles(f_base, f1pp):
    return [(slice(g * _F1_GROUP, (g + 1) * _F1_GROUP), pl.multiple_of(f_base + g * _F1_GROUP, _F1_GROUP))
            for g in range(f1pp // _F1_GROUP)]


def _gather_f1(s_ref, vre_ref, vim_ref, f_base, f1pp, n1):
    for sl, f0 in _f1_tiles(f_base, f1pp):
        vre_ref[sl] = jnp.swapaxes(s_ref[:, pl.ds(f0, _F1_GROUP), :], 0, 1)
        vim_ref[sl] = jnp.swapaxes(s_ref[:, pl.ds(n1 + f0, _F1_GROUP), :], 0, 1)


def _filt_fft_body(kern_ref, fa_ref, fb_ref, o_ref, s_ref, vre_ref, vim_ref, *, n_in, n1, f1pp):
    s = pl.program_id(1)

    @pl.when(s < n_in)
    def _():
        for i in range(_T2_PER_PHASE):
            u = kern_ref[i].astype(_BF16)
            s_ref[s * _T2_PER_PHASE + i] = jnp.dot(fa_ref[...], u,
                                                   preferred_element_type=_F32).astype(_BF16)

    @pl.when(s >= n_in)
    def _():
        _gather_f1(s_ref, vre_ref, vim_ref, (s - n_in) * f1pp, f1pp, n1)

        for f in range(f1pp):
            v = jnp.concatenate([vre_ref[f], vim_ref[f]], axis=0)
            o_ref[f * 2 * _N2:(f + 1) * 2 * _N2, :] = jnp.dot(fb_ref[f], v, preferred_element_type=_F32)


def _filt_fft(kern3, fa_full, fb):
    n2, n1, cols = kern3.shape
    ct = _FFT_CT
    n_in = n2 // _T2_PER_PHASE
    f1pp = min(_F1_PER_PHASE, n1)
    n_mid = n1 // f1pp
    body = functools.partial(_filt_fft_body, n_in=n_in, n1=n1, f1pp=f1pp)
    return pl.pallas_call(
        body,
        grid=(cols // ct, n_in + n_mid),
        in_specs=[
            pl.BlockSpec((_T2_PER_PHASE, n1, ct), lambda j, s: (jnp.minimum(s, n_in - 1), 0, j)),
            pl.BlockSpec((2 * n1, n1), lambda j, s: (0, 0)),
            pl.BlockSpec((f1pp, 2 * n2, 2 * n2), lambda j, s: (jnp.maximum(s - n_in, 0), 0, 0)),
        ],
        out_specs=pl.BlockSpec((f1pp * 2 * n2, ct), lambda j, s: (jnp.maximum(s - n_in, 0), j)),
        out_shape=jax.ShapeDtypeStruct((n1 * 2 * n2, cols), _F32),
        scratch_shapes=[pltpu.VMEM((n2, 2 * n1, ct), _BF16),
                        pltpu.VMEM((f1pp, n2, ct), _BF16), pltpu.VMEM((f1pp, n2, ct), _BF16)],
        compiler_params=_cparams(("arbitrary", "arbitrary")),
        name="filt_fft",
    )(kern3, fa_full, fb)


def _hyena_conv_body(u_ref, u2_ref, g_ref, k_ref, bias_ref, fa_ref, fai_ref, fb_ref, fbi_ref,
                     o_ref, s_ref, vre_ref, vim_ref, wre_ref, wim_ref, y_ref,
                     *, n_in, n_mid, n1, f1pp, natural_out):
    s = pl.program_id(2)
    h1 = n1 // 2

    @pl.when(s < n_in)
    def _():
        for i in range(_T2_PER_PHASE):
            u = jnp.concatenate([u_ref[0, i], u_ref[1, i]], axis=0)
            s_ref[s * _T2_PER_PHASE + i] = jnp.dot(fa_ref[...], u,
                                                   preferred_element_type=_F32).astype(_BF16)

    @pl.when(jnp.logical_and(s >= n_in, s < n_in + n_mid))
    def _():
        tiles = _f1_tiles((s - n_in) * f1pp, f1pp)

        def gather(sl, f0):
            vre_ref[sl] = jnp.swapaxes(s_ref[:, pl.ds(f0, _F1_GROUP), :], 0, 1)
            vim_ref[sl] = jnp.swapaxes(s_ref[:, pl.ds(n1 + f0, _F1_GROUP), :], 0, 1)

        def forward(f):
            v = jnp.concatenate([vre_ref[f], vim_ref[f]], axis=0)
            x = jnp.dot(fb_ref[f], v, preferred_element_type=_F32)
            xre, xim = x[:_N2], x[_N2:]
            kre = k_ref[f * 2 * _N2:f * 2 * _N2 + _N2, :]
            kim = k_ref[f * 2 * _N2 + _N2:(f + 1) * 2 * _N2, :]
            y_ref[f] = jnp.concatenate([xre * kre - xim * kim, xre * kim + xim * kre],
                                       axis=0).astype(_BF16)

        def inverse(f):
            w = jnp.dot(fbi_ref[f], y_ref[f], preferred_element_type=_F32)
            wre_ref[f] = w[:_N2].astype(_BF16)
            wim_ref[f] = w[_N2:].astype(_BF16)

        def scatter(sl, f0):
            s_ref[:, pl.ds(f0, _F1_GROUP), :] = jnp.swapaxes(wre_ref[sl], 0, 1)
            s_ref[:, pl.ds(n1 + f0, _F1_GROUP), :] = jnp.swapaxes(wim_ref[sl], 0, 1)

        nt = len(tiles)
        for t in tiles:
            gather(*t)
        for g in range(nt + 1):
            for k in range(_F1_GROUP):
                if g < nt:
                    forward(g * _F1_GROUP + k)
                if g >= 1:
                    inverse((g - 1) * _F1_GROUP + k)
            if g >= 1:
                scatter(*tiles[g - 1])

    @pl.when(s >= n_in + n_mid)
    def _():
        bias = bias_ref[...]
        outs = [[], []]
        for i in range(_T2_PER_PHASE):
            bv = s_ref[(s - n_in - n_mid) * _T2_PER_PHASE + i]
            y = jnp.dot(fai_ref[...], bv, preferred_element_type=_F32)
            for m in range(2):
                res = g_ref[m, i].astype(_F32) * (y[m * h1:(m + 1) * h1] + u2_ref[m, i].astype(_F32) * bias)
                if natural_out:
                    outs[m].append(res.astype(o_ref.dtype))
                else:
                    o_ref[m, i] = res.astype(o_ref.dtype)
        if natural_out:
            for m in range(2):
                o_ref[m] = jnp.swapaxes(jnp.stack(outs[m], axis=0), 0, 1)


def _hyena_conv(u4, ucol, g4, gcol, kspec, kcol, bias, tabs, natural_out):
    fa, fai, fb, fbi = tabs
    b, n2, h1, _ = u4.shape
    n1 = 2 * h1
    ct = _FFT_CT
    n_in = n2 // _T2_PER_PHASE
    f1pp = min(_F1_PER_PHASE, n1)
    n_mid = n1 // f1pp
    n_out = n_in
    tiles = HYENA_WIDTH // ct
    body = functools.partial(_hyena_conv_body, n_in=n_in, n_mid=n_mid, n1=n1, f1pp=f1pp,
                             natural_out=natural_out)
    mid = lambda s: jnp.clip(s - n_in, 0, n_mid - 1)
    last = lambda s: jnp.clip(s - n_in - n_mid, 0, n_out - 1)
    blk = (2, _T2_PER_PHASE, h1, ct)
    if natural_out:
        out_spec = pl.BlockSpec((2, h1, _T2_PER_PHASE, ct), lambda j, p, s: (p, 0, last(s), j))
        out_shape = jax.ShapeDtypeStruct((b, h1, n2, HYENA_WIDTH), _BF16)
    else:
        out_spec = pl.BlockSpec(blk, lambda j, p, s: (p, last(s), 0, j))
        out_shape = jax.ShapeDtypeStruct((b, n2, h1, HYENA_WIDTH), _BF16)
    return pl.pallas_call(
        body,
        grid=(tiles, b // 2, n_in + n_mid + n_out),
        in_specs=[
            pl.BlockSpec(blk, lambda j, p, s: (p, jnp.minimum(s, n_in - 1), 0, ucol + j)),
            pl.BlockSpec(blk, lambda j, p, s: (p, last(s), 0, ucol + j)),
            pl.BlockSpec(blk, lambda j, p, s: (p, last(s), 0, gcol + j)),
            pl.BlockSpec((f1pp * 2 * n2, ct), lambda j, p, s: (mid(s), kcol + j)),
            pl.BlockSpec((1, ct), lambda j, p, s: (0, j)),
            pl.BlockSpec((2 * n1, n1), lambda j, p, s: (0, 0)),
            pl.BlockSpec((n1, 2 * n1), lambda j, p, s: (0, 0)),
            pl.BlockSpec((f1pp, 2 * n2, 2 * n2), lambda j, p, s: (mid(s), 0, 0)),
            pl.BlockSpec((f1pp, 2 * n2, 2 * n2), lambda j, p, s: (mid(s), 0, 0)),
        ],
        out_specs=out_spec,
        out_shape=out_shape,
        scratch_shapes=([pltpu.VMEM((n2, 2 * n1, ct), _BF16)] + 4 * [pltpu.VMEM((f1pp, n2, ct), _BF16)]
                        + [pltpu.VMEM((f1pp, 2 * n2, ct), _BF16)]),
        compiler_params=_cparams(("arbitrary", "arbitrary", "arbitrary")),
        name="hyena_conv",
    )(u4, u4, g4, kspec, bias, fa, fai, fb, fbi)


_GATE_LANES = 2 * MLSTM_HEADS
_TERM_R = 12


def _split3(x):
    hi = x.astype(_BF16).astype(_F32)
    mid = (x - hi).astype(_BF16).astype(_F32)
    lo = (x - hi - mid).astype(_BF16).astype(_F32)
    return hi, mid, lo


def _scan_order_max(x, is_fwd):
    n = x.shape[0]
    row = lax.broadcasted_iota(jnp.int32, x.shape, 0)
    pre, suf = x, x
    shift = 1
    while shift < n:
        pre = jnp.maximum(pre, jnp.where(row >= shift, pltpu.roll(pre, shift, axis=0), -jnp.inf))
        suf = jnp.maximum(suf, jnp.where(row < n - shift, pltpu.roll(suf, n - shift, axis=0), -jnp.inf))
        shift *= 2
    return jnp.where(is_fwd, pre, suf)


def _gate_prep_body(g_ref, bias_ref, a_ref, row_ref, keep_ref,
                    b_scr, a_scr, pm_scr, tot_scr, mloc_scr, mprev_scr, *, nc):
    ch = MLSTM_CHUNK
    lane = lax.broadcasted_iota(jnp.int32, (1, 128), 1)
    is_fwd = lane < MLSTM_HEADS
    live = lane < _GATE_LANES
    jj = lax.broadcasted_iota(jnp.int32, (ch, ch), 0)
    ss = lax.broadcasted_iota(jnp.int32, (ch, ch), 1)
    t_lo = (ss <= jj).astype(_F32)
    t_up = (ss >= jj).astype(_F32)
    row_ref[...] = jnp.zeros_like(row_ref)

    def chunk_stats(c, carry):
        r0 = pl.multiple_of(c * ch, ch)
        gi = g_ref[pl.ds(r0, ch), :] + bias_ref[...]
        gf = pltpu.roll(gi, 128 - 2 * MLSTM_HEADS, axis=1)
        logf = jnp.minimum(gf, 0.0) - jnp.log1p(jnp.exp(-jnp.abs(gf)))
        cs_lo = jnp.dot(t_lo, logf, precision=_HIGHEST, preferred_element_type=_F32)
        cs_up = jnp.dot(t_up, logf, precision=_HIGHEST, preferred_element_type=_F32)
        bcs = jnp.where(is_fwd, cs_lo, cs_up)
        tot = cs_lo[ch - 1:ch, :]
        a = tot - bcs + gi
        b_scr[pl.ds(r0, ch), :] = bcs
        a_scr[pl.ds(r0, ch), :] = a
        tot_scr[pl.ds(c, 1), :] = tot
        mloc_scr[pl.ds(c, 1), :] = jnp.max(a, axis=0, keepdims=True)
        r = gi - bcs
        pm_scr[pl.ds(r0, ch), :] = _scan_order_max(r, is_fwd)
        hi, mid, lo = _split3(r.T[0:_GATE_LANES, :])
        base = _TERM_R * _GATE_LANES
        row_ref[base:base + 16, pl.ds(r0, ch)] = jnp.concatenate([hi, mid], axis=0).astype(_BF16)
        row_ref[base + 16:base + 32, pl.ds(r0, ch)] = jnp.concatenate(
            [lo, jnp.zeros_like(lo)], axis=0).astype(_BF16)
        return carry

    lax.fori_loop(0, nc, chunk_stats, 0)

    def scan_f(c, m):
        mprev_scr[pl.ds(c, 1), :] = jnp.where(is_fwd, m, mprev_scr[pl.ds(c, 1), :])
        return jnp.maximum(tot_scr[pl.ds(c, 1), :] + m, mloc_scr[pl.ds(c, 1), :])

    def scan_b(i, m):
        c = nc - 1 - i
        mprev_scr[pl.ds(c, 1), :] = jnp.where(is_fwd, mprev_scr[pl.ds(c, 1), :], m)
        return jnp.maximum(tot_scr[pl.ds(c, 1), :] + m, mloc_scr[pl.ds(c, 1), :])

    mprev_scr[...] = jnp.zeros_like(mprev_scr)
    lax.fori_loop(0, nc, scan_f, jnp.zeros((1, 128), _F32))
    lax.fori_loop(0, nc, scan_b, jnp.zeros((1, 128), _F32))

    def emit(c, carry):
        r0 = pl.multiple_of(c * ch, ch)
        mprev = mprev_scr[pl.ds(c, 1), :]
        tot = tot_scr[pl.ds(c, 1), :]
        mnew = jnp.maximum(tot + mprev, mloc_scr[pl.ds(c, 1), :])
        keep_ref[pl.ds(c, 1), :] = jnp.exp(tot + mprev - mnew)
        g = jnp.maximum(mprev, pm_scr[pl.ds(r0, ch), :])
        terms = (_split3(-g) + _split3(mprev - g) + _split3(-b_scr[pl.ds(r0, ch), :] - g)
                 + _split3(a_scr[pl.ds(r0, ch), :] - mnew) + 3 * (jnp.ones((ch, 128), _F32),))
        pack = jnp.where(live, terms[0], 0.0)
        for t in range(1, len(terms)):
            pack = pack + pltpu.roll(jnp.where(live, terms[t], 0.0), _GATE_LANES * t, axis=1)
        a_ref[pl.ds(r0, ch), :] = pack.astype(_BF16)
        return carry

    lax.fori_loop(0, nc, emit, 0)


def _gate_prep(g3, bias):
    b, l, _ = g3.shape
    nc = l // MLSTM_CHUNK
    body = functools.partial(_gate_prep_body, nc=nc)
    return pl.pallas_call(
        body,
        grid=(b,),
        in_specs=[
            pl.BlockSpec((None, l, 128), lambda i: (i, 0, 0)),
            pl.BlockSpec((1, 128), lambda i: (0, 0)),
        ],
        out_specs=[
            pl.BlockSpec((None, l, 128), lambda i: (i, 0, 0)),
            pl.BlockSpec((None, 128, l), lambda i: (i, 0, 0)),
            pl.BlockSpec((None, nc, 128), lambda i: (i, 0, 0)),
        ],
        out_shape=[
            jax.ShapeDtypeStruct((b, l, 128), _BF16),
            jax.ShapeDtypeStruct((b, 128, l), _BF16),
            jax.ShapeDtypeStruct((b, nc, 128), _F32),
        ],
        scratch_shapes=[
            pltpu.VMEM((l, 128), _F32), pltpu.VMEM((l, 128), _F32), pltpu.VMEM((l, 128), _F32),
            pltpu.VMEM((nc, 128), _F32), pltpu.VMEM((nc, 128), _F32), pltpu.VMEM((nc, 128), _F32),
        ],
        compiler_params=_cparams(("arbitrary",)),
        name="gate_prep",
    )(g3, bias)


def _gate_spread_matrix():
    bc = np.zeros((128, 4 * 128), np.float32)
    for blk in range(4):
        for t in range(3 * blk, 3 * blk + 3):
            bc[t * _GATE_LANES:(t + 1) * _GATE_LANES, blk * 128:(blk + 1) * 128] = 1.0
    return jnp.asarray(bc).astype(_BF16)


def _mlstm_body(keep_ref, bc_ref, qf_ref, kf_ref, vf_ref, af_ref, rf_ref, qb_ref, kb_ref, vb_ref, ab_ref,
                rb_ref, hf_ref, hb_ref, ct_ref, nm_ref, *, nc):
    bi, c = pl.program_id(0), pl.program_id(1)

    @pl.when(c == 0)
    def _():
        ct_ref[...] = jnp.zeros_like(ct_ref)
        nm_ref[...] = jnp.zeros_like(nm_ref)

    ch, dh = MLSTM_CHUNK, MLSTM_HEAD_DIM
    jj = lax.broadcasted_iota(jnp.int32, (ch, ch), 0)
    ss = lax.broadcasted_iota(jnp.int32, (ch, ch), 1)
    head_lane = ss % _GATE_LANES
    ones_rhs = jnp.ones((ch, 128), _BF16)
    twice = lambda a: jnp.concatenate([a, a], axis=1)
    dirs = ((qf_ref, kf_ref, vf_ref, af_ref, rf_ref, hf_ref, c, ss <= jj),
            (qb_ref, kb_ref, vb_ref, ab_ref, rb_ref, hb_ref, nc - 1 - c, ss >= jj))
    for d, (q_ref, k_ref, v_ref, a_ref, r_ref, o_ref, chunk, mask) in enumerate(dirs):
        bmat = jnp.concatenate([bc_ref[:, 0:128] + r_ref[...], bc_ref[:, 128:]], axis=1)
        a_all = a_ref[...].astype(_F32)
        kbase = (bi * nc + chunk) * _GATE_LANES
        for h in range(MLSTM_HEADS):
            hd = d * MLSTM_HEADS + h
            a_h = jnp.where(head_lane == hd, a_all, 0.0).astype(_BF16)
            e = jnp.dot(a_h, bmat, preferred_element_type=_F32)
            decay = jnp.exp(jnp.where(mask, e[:, 0:128], -jnp.inf))
            rest = jnp.exp(e[:, 128:])
            iw, clamp, wrep = rest[:, 0:128], rest[:, 128:256], rest[:, 256:384]
            keep = keep_ref[kbase + hd]
            lo, hi = h * dh, (h + 1) * dh
            qh, kh, vh = q_ref[:, lo:hi], k_ref[:, lo:hi], v_ref[:, lo:hi]
            p = (_dot_nt(qh, kh) * decay).astype(_BF16)
            ct, nm = ct_ref[hd], nm_ref[hd]
            pv = jnp.dot(p, jnp.concatenate([vh, ones_rhs], axis=1), preferred_element_type=_F32)
            qc = jnp.dot(qh, jnp.concatenate([ct, nm], axis=1).astype(_BF16), preferred_element_type=_F32)
            num = pv[:, :dh] + twice(iw) * qc[:, :dh]
            den = pv[:, dh:] + iw * qc[:, dh:]
            inv = 1.0 / jnp.maximum(jnp.abs(den), clamp)
            o_ref[:, lo:hi] = (num * twice(inv)).astype(o_ref.dtype)
            vw = jnp.concatenate([(vh.astype(_F32) * twice(wrep)).astype(_BF16), wrep.astype(_BF16)], axis=1)
            upd = lax.dot_general(kh, vw, (((0,), (0,)), ((), ())), preferred_element_type=_F32)
            ct_ref[hd] = keep * ct + upd[:, :dh]
            nm_ref[hd] = keep * nm + upd[:, dh:]


def _mlstm(qk, proj3, a_mat, r_rows, keep):
    b, l, _ = qk.shape
    ch, mw = MLSTM_CHUNK, MLSTM_WIDTH
    nc = l // ch
    fw = lambda c: c
    bw = lambda c: nc - 1 - c

    def specs(cm):
        return [
            pl.BlockSpec((None, ch, mw), lambda i, c: (i, cm(c), 0)),
            pl.BlockSpec((None, ch, mw), lambda i, c: (i, cm(c), 1)),
            pl.BlockSpec((None, ch, mw), lambda i, c: (i, cm(c), _COL_V)),
            pl.BlockSpec((None, ch, 128), lambda i, c: (i, cm(c), 0)),
            pl.BlockSpec((None, 128, ch), lambda i, c: (i, 0, cm(c))),
        ]

    body = functools.partial(_mlstm_body, nc=nc)
    keep_flat = keep[:, :, :_GATE_LANES].reshape(b * nc * _GATE_LANES)
    return pl.pallas_call(
        body,
        grid=(b, nc),
        in_specs=[pl.BlockSpec(memory_space=pltpu.SMEM),
                  pl.BlockSpec((128, 4 * 128), lambda i, c: (0, 0))] + specs(fw) + specs(bw),
        out_specs=[
            pl.BlockSpec((None, ch, mw), lambda i, c: (i, c, 0)),
            pl.BlockSpec((None, ch, mw), lambda i, c: (i, nc - 1 - c, 0)),
        ],
        out_shape=[jax.ShapeDtypeStruct((b, l, mw), _BF16), jax.ShapeDtypeStruct((b, l, mw), _BF16)],
        scratch_shapes=[
            pltpu.VMEM((2 * MLSTM_HEADS, MLSTM_HEAD_DIM, MLSTM_HEAD_DIM), _F32),
            pltpu.VMEM((2 * MLSTM_HEADS, MLSTM_HEAD_DIM, 128), _F32),
        ],
        compiler_params=_cparams(("arbitrary", "arbitrary")),
        name="mlstm",
    )(keep_flat, _gate_spread_matrix(), qk, qk, proj3, a_mat, r_rows, qk, qk, proj3, a_mat, r_rows)


def _sigmoid(x):
    return 1.0 / (1.0 + jnp.exp(-x))


def _merge_body(x_ref, ya_ref, hf_ref, hb_ref, o_ref, ma_ref, mb_ref, wa_ref, wb_ref, wo_ref, out_ref):
    hsum = hf_ref[...].astype(_F32) + hb_ref[...].astype(_F32)
    yb = (_sigmoid(o_ref[...].astype(_F32)) * hsum).astype(_BF16)
    pa = jnp.dot(ya_ref[...], wa_ref[...], preferred_element_type=_F32)
    pb = jnp.dot(yb, wb_ref[...], preferred_element_type=_F32)
    mixed = _sigmoid(ma_ref[...].astype(_F32)) * pa + _sigmoid(mb_ref[...].astype(_F32)) * pb
    out_ref[...] = x_ref[...] + jnp.dot(mixed.astype(_BF16), wo_ref[...], preferred_element_type=_F32)


def _merge(x2, ya2, hf2, hb2, proj, wa, wb, wo):
    m = x2.shape[0]
    tm = 256
    d, hw, mw = D_MODEL, HYENA_WIDTH, MLSTM_WIDTH
    const = lambda shape: pl.BlockSpec(shape, lambda i: (0, 0), pipeline_mode=pl.Buffered(1))
    return pl.pallas_call(
        _merge_body,
        grid=(m // tm,),
        in_specs=[
            pl.BlockSpec((tm, d), lambda i: (i, 0)),
            pl.BlockSpec((tm, hw), lambda i: (i, 0)),
            pl.BlockSpec((tm, mw), lambda i: (i, 0)),
            pl.BlockSpec((tm, mw), lambda i: (i, 0)),
            pl.BlockSpec((tm, mw), lambda i: (i, _COL_O)),
            pl.BlockSpec((tm, d), lambda i: (i, _COL_MA // 2)),
            pl.BlockSpec((tm, d), lambda i: (i, _COL_MB // 2)),
            const((hw, d)), const((mw, d)), const((d, d)),
        ],
        out_specs=pl.BlockSpec((tm, d), lambda i: (i, 0)),
        out_shape=jax.ShapeDtypeStruct((m, d), _F32),
        compiler_params=_cparams(("arbitrary",)),
        name="merge",
    )(x2, ya2, hf2, hb2, proj, proj, proj, wa, wb, wo)


def _ffn_body(x_ref, n2_ref, wg_ref, wu_ref, wd_ref, nf_ref, o_ref, hn_ref):
    f = pl.program_id(1)

    @pl.when(f == 0)
    def _():
        x = x_ref[...]
        hn = x * lax.rsqrt(jnp.mean(x * x, axis=-1, keepdims=True) + RMS_EPS) * n2_ref[...]
        hn_ref[...] = hn.astype(_BF16)
        o_ref[...] = x

    hn = hn_ref[...]
    g = jnp.dot(hn, wg_ref[...], preferred_element_type=_F32)
    u = jnp.dot(hn, wu_ref[...], preferred_element_type=_F32)
    a = (g * _sigmoid(g) * u).astype(_BF16)
    o_ref[...] += jnp.dot(a, wd_ref[...], preferred_element_type=_F32)

    @pl.when(f == pl.num_programs(1) - 1)
    def _():
        y = o_ref[...]
        o_ref[...] = y * lax.rsqrt(jnp.mean(y * y, axis=-1, keepdims=True) + RMS_EPS) * nf_ref[...]


def _ffn(x2, norm2_w, w_gate_up, w_down, norm_f_w):
    m = x2.shape[0]
    tm, tf = 1024, 512
    d = D_MODEL
    nf = FFN_HIDDEN // tf
    return pl.pallas_call(
        _ffn_body,
        grid=(m // tm, nf),
        in_specs=[
            pl.BlockSpec((tm, d), lambda i, f: (i, 0)),
            pl.BlockSpec((1, d), lambda i, f: (0, 0)),
            pl.BlockSpec((d, tf), lambda i, f: (0, f)),
            pl.BlockSpec((d, tf), lambda i, f: (0, nf + f)),
            pl.BlockSpec((tf, d), lambda i, f: (f, 0)),
            pl.BlockSpec((1, d), lambda i, f: (0, 0)),
        ],
        out_specs=pl.BlockSpec((tm, d), lambda i, f: (i, 0)),
        out_shape=jax.ShapeDtypeStruct((m, d), _F32),
        scratch_shapes=[pltpu.VMEM((tm, d), _BF16)],
        compiler_params=_cparams(("arbitrary", "arbitrary")),
        name="ffn",
    )(x2, norm2_w, w_gate_up, w_gate_up, w_down, norm_f_w)


def kernel(x, norm1_w, w_in, conv_w, conv_b, filt_w1, filt_b1, filt_freq1, filt_w2, filt_b2, filt_freq2,
           filt_w3, hyena_bias, mlstm_gate_bias, w_branch_a, w_branch_b, w_out, norm2_w, w_gate_up, w_down,
           norm_f_w):
    b, l, d = x.shape
    assert d == D_MODEL and b % 2 == 0 and l % (_N2 * _T2_PER_PHASE) == 0
    assert norm1_w.shape[0] == 1, "single-layer block"
    hw, mw, nh = HYENA_WIDTH, MLSTM_WIDTH, MLSTM_HEADS
    m = b * l
    n = 2 * l
    n1 = n // _N2
    sc_cols = 3 * hw + 2 * mw
    g0 = sc_cols + 2 * mw

    w_t = jnp.swapaxes(w_in[0], 0, 1)
    w_all_t = w_t.astype(_BF16)
    w_merge_t = w_all_t[g0 + 4 * nh:]
    wg_t = w_t[g0:g0 + 4 * nh]
    gate_order = lambda a: jnp.concatenate(
        [a[0:nh], a[2 * nh:3 * nh], a[nh:2 * nh], a[3 * nh:4 * nh]], axis=0)
    wg_hi_t, wg_lo_t = _split_bf16(jnp.pad(gate_order(wg_t), ((0, 128 - 4 * nh), (0, 0))))
    gate_bias = jnp.pad(gate_order(mlstm_gate_bias[0].astype(_F32).reshape(4 * nh))[None, :],
                        ((0, 0), (0, 128 - 4 * nh)))

    x2 = x.reshape(m, d)
    proj, gates = _inproj(x2, norm1_w[0][None, :], w_merge_t, w_all_t, wg_hi_t, wg_lo_t)
    proj3 = proj.reshape(b, l, _MAIN_COLS)

    cw, cb = conv_w[0], conv_b[0][None, :]
    hy4 = _shortconv(proj3, cw, cb, _COL_HV * 1024, 0, 3 * hw, True)
    qk = _shortconv(proj3, cw, cb, _COL_Q * 1024, 3 * hw, 2 * mw, False)

    fa, fa_full, fai, fb, fbi = _dft_tables(n)
    zpos = _filter_positions(l)
    w1p = jnp.pad(filt_w1[0].astype(_F32), ((0, 128 - HYENA_EMB), (0, 0)))
    max_decay = math.log(HYENA_TARGET) / HYENA_FAST_DECAY
    min_decay = math.log(HYENA_TARGET) / HYENA_SLOW_DECAY
    dabs = jnp.asarray(np.abs(np.linspace(min_decay, max_decay, hw, dtype=np.float32))[None, :])
    dup = lambda a: jnp.concatenate([a, a], axis=-1)
    fh = HYENA_FILTER_HIDDEN
    w3 = filt_w3[0].astype(_F32).reshape(fh, 2, 2, hw).transpose(2, 0, 1, 3).reshape(2 * fh, 2 * hw)
    w3_hi, w3_lo = _split_bf16(w3)
    kern = _filt_mlp(zpos, w1p, filt_b1[0][None, :], filt_freq1[0][None, :], dup(filt_w2[0]),
                     dup(filt_b2[0][None, :]), dup(filt_freq2[0][None, :]), w3_hi, w3_lo, dabs)
    kspec = _filt_fft(kern.reshape(_N2, n1, 2 * hw), fa_full, fb)
    tabs = (fa, fai, fb, fbi)
    tiles = hw // _FFT_CT
    hbias = hyena_bias[0].astype(_F32)
    z4 = _hyena_conv(hy4, 0, hy4, tiles, kspec, 0, hbias[0][None, :], tabs, False)
    ya4 = _hyena_conv(z4, 0, hy4, 2 * tiles, kspec, tiles, hbias[1][None, :], tabs, True)
    ya2 = ya4.reshape(m, hw)

    a_mat, r_rows, keep = _gate_prep(gates.reshape(b, l, 128), gate_bias)
    hf, hb = _mlstm(qk, proj3, a_mat, r_rows, keep)

    x_mid = _merge(x2, ya2, hf.reshape(m, mw), hb.reshape(m, mw), proj,
                   w_branch_a[0].astype(_BF16), w_branch_b[0].astype(_BF16), w_out[0].astype(_BF16))
    out = _ffn(x_mid, norm2_w[0][None, :], w_gate_up[0].astype(_BF16), w_down[0].astype(_BF16),
               norm_f_w[None, :])
    return out.reshape(b, l, d)
```

```python
import functools
import math

import numpy as np
import jax
import jax.numpy as jnp
from jax import lax
from jax.experimental import pallas as pl
from jax.experimental.pallas import tpu as pltpu

_F32 = jnp.float32
_BF16 = jnp.bfloat16
_HIGHEST = lax.Precision.HIGHEST

D_MODEL = 2048
HYENA_WIDTH = 1024
HYENA_EMB = 33
HYENA_FILTER_HIDDEN = 64
HYENA_FAST_DECAY = 0.3
HYENA_SLOW_DECAY = 1.5
HYENA_TARGET = 1e-2
MLSTM_WIDTH = 1024
MLSTM_HEADS = 4
MLSTM_HEAD_DIM = 256
MLSTM_CHUNK = 128
FFN_HIDDEN = 5632
RMS_EPS = 1e-6

_COL_MA, _COL_MB, _COL_HV, _COL_Q, _COL_V, _COL_O = 0, 2, 4, 7, 9, 10
_MAIN_COLS = 11 * 1024

_N2 = 32
_T2_PER_PHASE = 16
_F1_PER_PHASE = 64
_FFT_CT = 256

_VMEM_LIMIT = 56 * 1024 * 1024


def _cparams(sem, vmem=_VMEM_LIMIT):
    return pltpu.CompilerParams(dimension_semantics=sem, vmem_limit_bytes=vmem)


def _split_bf16(a):
    hi = a.astype(_BF16)
    lo = (a - hi.astype(_F32)).astype(_BF16)
    return hi, lo


def _dot_nt(a, b):
    return lax.dot_general(a, b, (((1,), (1,)), ((), ())), preferred_element_type=_F32)


def _dot3(a_hi, a_lo, b_hi, b_lo):
    d = functools.partial(jnp.dot, preferred_element_type=_F32)
    return d(a_hi, b_hi) + (d(a_hi, b_lo) + d(a_lo, b_hi))


def _inproj_body(x_ref, nw_ref, wm_ref, w_ref, wgh_ref, wgl_ref, o_ref, g_ref, hn_ref, *, n_merge):
    j = pl.program_id(1)

    @pl.when(j == 0)
    def _():
        rows = 256
        for r in range(0, x_ref.shape[0], rows):
            x = x_ref[r:r + rows, :]
            hn = x * lax.rsqrt(jnp.mean(x * x, axis=-1, keepdims=True) + RMS_EPS) * nw_ref[...]
            hn_hi, hn_lo = _split_bf16(hn)
            hn_ref[r:r + rows, :] = hn_hi
            g_ref[r:r + rows, :] = (_dot_nt(hn_hi, wgh_ref[...])
                                    + (_dot_nt(hn_hi, wgl_ref[...]) + _dot_nt(hn_lo, wgh_ref[...])))

    @pl.when(j < n_merge)
    def _():
        o_ref[...] = _dot_nt(hn_ref[...], wm_ref[...]).astype(o_ref.dtype)

    @pl.when(j >= n_merge)
    def _():
        o_ref[...] = _dot_nt(hn_ref[...], w_ref[...]).astype(o_ref.dtype)


def _inproj(x2, norm_w, w_merge_t, w_all_t, wg_hi_t, wg_lo_t):
    m = x2.shape[0]
    tm, tn = 1024, 1024
    n_merge = w_merge_t.shape[0] // tn
    body = functools.partial(_inproj_body, n_merge=n_merge)
    return pl.pallas_call(
        body,
        grid=(m // tm, _MAIN_COLS // tn),
        in_specs=[
            pl.BlockSpec((tm, D_MODEL), lambda i, j: (i, 0)),
            pl.BlockSpec((1, D_MODEL), lambda i, j: (0, 0)),
            pl.BlockSpec((tn, D_MODEL), lambda i, j: (jnp.minimum(j, n_merge - 1), 0)),
            pl.BlockSpec((tn, D_MODEL), lambda i, j: (jnp.maximum(j - n_merge, 0), 0)),
            pl.BlockSpec((128, D_MODEL), lambda i, j: (0, 0)),
            pl.BlockSpec((128, D_MODEL), lambda i, j: (0, 0)),
        ],
        out_specs=[
            pl.BlockSpec((tm, tn), lambda i, j: (i, j)),
            pl.BlockSpec((tm, 128), lambda i, j: (i, 0)),
        ],
        out_shape=[
            jax.ShapeDtypeStruct((m, _MAIN_COLS), _BF16),
            jax.ShapeDtypeStruct((m, 128), _F32),
        ],
        scratch_shapes=[pltpu.VMEM((tm, D_MODEL), _BF16)],
        compiler_params=_cparams(("arbitrary", "arbitrary")),
        name="inproj",
    )(x2, norm_w, w_merge_t, w_all_t, wg_hi_t, wg_lo_t)


def _shortconv_silu_body(u_ref, w_ref, b_ref, o_ref, *, k_scale):
    u = u_ref[...].astype(_F32)
    n = u.shape[0]
    row = lax.broadcasted_iota(jnp.int32, u.shape, 0)
    up = jnp.where(row == 0, 0.0, pltpu.roll(u, 1, axis=0))
    un = jnp.where(row == n - 1, 0.0, pltpu.roll(u, n - 1, axis=0))
    w = w_ref[...]
    y = up * w[0:1, :] + b_ref[...] + u * w[1:2, :] + un * w[2:3, :]
    y = y * (1.0 / (1.0 + jnp.exp(-y)))
    is_k = pl.program_id(1) >= (MLSTM_WIDTH // u.shape[1])
    o_ref[...] = (y * jnp.where(is_k, k_scale, 1.0)).astype(o_ref.dtype)


def _shortconv_t2_body(u_ref, w_ref, b_ref, o_ref, ut_ref):
    n, c = u_ref.shape
    h1 = n // _N2
    g = 16
    for k in range(h1 // g):
        blk = u_ref[k * g * _N2:(k + 1) * g * _N2, :].reshape(g, _N2, c)
        ut_ref[:, k * g:(k + 1) * g, :] = jnp.swapaxes(blk, 0, 1)
    w = w_ref[...]
    w0, w1, w2, bias = w[0:1, :], w[1:2, :], w[2:3, :], b_ref[...]
    slab = lambda t2: ut_ref[t2].astype(_F32)
    row = lax.broadcasted_iota(jnp.int32, (h1, c), 0)
    before_first = jnp.where(row == 0, 0.0, pltpu.roll(slab(_N2 - 1), 1, axis=0))
    after_last = jnp.where(row == h1 - 1, 0.0, pltpu.roll(slab(0), h1 - 1, axis=0))
    for t2 in range(_N2):
        up = before_first if t2 == 0 else slab(t2 - 1)
        un = after_last if t2 == _N2 - 1 else slab(t2 + 1)
        o_ref[t2] = (up * w0 + bias + slab(t2) * w1 + un * w2).astype(o_ref.dtype)


def _shortconv(proj3, conv_w, conv_b, proj_col0, conv_col0, ncols, t2_major):
    b, l, _ = proj3.shape
    ct = 256
    p0, c0 = proj_col0 // ct, conv_col0 // ct
    if t2_major:
        body = _shortconv_t2_body
        out_spec = pl.BlockSpec((None, _N2, l // _N2, ct), lambda i, j: (i, 0, 0, j))
        out_shape = jax.ShapeDtypeStruct((b, _N2, l // _N2, ncols), _BF16)
        scratch = [pltpu.VMEM((_N2, l // _N2, ct), _BF16)]
    else:
        body = functools.partial(_shortconv_silu_body, k_scale=MLSTM_HEAD_DIM ** -0.5)
        out_spec = pl.BlockSpec((None, l, ct), lambda i, j: (i, 0, j))
        out_shape = jax.ShapeDtypeStruct((b, l, ncols), _BF16)
        scratch = []
    return pl.pallas_call(
        body,
        grid=(b, ncols // ct),
        in_specs=[
            pl.BlockSpec((None, l, ct), lambda i, j: (i, 0, p0 + j)),
            pl.BlockSpec((3, ct), lambda i, j: (0, c0 + j)),
            pl.BlockSpec((1, ct), lambda i, j: (0, c0 + j)),
        ],
        out_specs=out_spec,
        out_shape=out_shape,
        scratch_shapes=scratch,
        compiler_params=_cparams(("arbitrary", "arbitrary")),
        name="shortconv" if t2_major else "shortconv_silu",
    )(proj3, conv_w, conv_b)


def _dft_tables(n):
    n2 = _N2
    n1 = n // n2
    h1 = n1 // 2
    f1 = np.arange(n1)[:, None]
    t1 = np.arange(n1)[None, :]
    ang = 2.0 * np.pi * ((f1 * t1) % n1) / n1
    c, s = np.cos(ang), np.sin(ang)
    ch, sh = c[:, :h1], s[:, :h1]
    fa = np.block([[ch, sh], [-sh, ch]])
    fa_full = np.concatenate([c, -s], axis=0)
    fai = np.block([[ch.T, -sh.T], [sh.T, ch.T]]) / n
    f1v = np.arange(n1)[:, None, None]
    f2 = np.arange(n2)[None, :, None]
    t2 = np.arange(n2)[None, None, :]
    ph = 2.0 * np.pi * ((f1v * t2 + f2 * t2 * n1) % n) / n
    tc, ts = np.cos(ph), np.sin(ph)
    fb = np.concatenate([np.concatenate([tc, ts], axis=2),
                         np.concatenate([-ts, tc], axis=2)], axis=1)
    tct, tst = np.swapaxes(tc, 1, 2), np.swapaxes(ts, 1, 2)
    fbi = np.concatenate([np.concatenate([tct, -tst], axis=2),
                          np.concatenate([tst, tct], axis=2)], axis=1)
    to = lambda a: jnp.asarray(a.astype(np.float32)).astype(_BF16)
    return to(fa), to(fa_full), to(fai), to(fb), to(fbi)


def _filter_positions(l):
    n = 2 * l
    n1 = n // _N2
    r = (np.arange(_N2)[:, None] + _N2 * np.arange(n1)[None, :]).reshape(-1)
    pos = np.where(r < l, r, n - r)
    pos = np.where(r == l, 0, pos)
    bands = (HYENA_EMB - 1) // 2
    tt = np.linspace(0.0, 1.0, l, dtype=np.float32).astype(np.float64)[pos]
    omega = (2.0 * math.pi * np.arange(l, dtype=np.float32) / l).astype(np.float32)
    freqs = np.linspace(1e-4, bands - 1, bands, dtype=np.float32)
    ang = (omega[:, None] * freqs[None, :]).astype(np.float64)[pos]
    z = np.zeros((n, 128), np.float64)
    z[:, 0] = tt
    z[:, 1:1 + bands] = np.cos(ang)
    z[:, 1 + bands:1 + 2 * bands] = -np.sin(ang)
    z[:, 33] = (r != l)
    z[:, 34] = (r < l)
    return jnp.asarray(z.astype(np.float32))


def _filt_mlp_body(z_ref, w1_ref, b1_ref, q1_ref, w2_ref, b2_ref, q2_ref, w3h_ref, w3l_ref, dabs_ref,
                   o_ref, hh_ref, hl_ref):
    fh = HYENA_FILTER_HIDDEN

    @pl.when(pl.program_id(1) == 0)
    def _():
        z = z_ref[...]
        h = jnp.sin(q1_ref[...] * (jnp.dot(z, w1_ref[...], precision=_HIGHEST,
                                           preferred_element_type=_F32) + b1_ref[...]))
        h = jnp.sin(q2_ref[...] * (jnp.dot(h, w2_ref[...], precision=_HIGHEST,
                                           preferred_element_type=_F32) + b2_ref[...]))
        valid = z[:, 33:34]
        fwd = z[:, 34:35]
        lane = lax.broadcasted_iota(jnp.int32, h.shape, 1)
        hi, lo = _split_bf16(h * (valid * jnp.where(lane < fh, fwd, 1.0 - fwd)))
        hh_ref[...] = hi
        hl_ref[...] = lo

    raw = _dot3(hh_ref[...], hl_ref[...], w3h_ref[...], w3l_ref[...])
    window = jnp.exp(-z_ref[:, 0:1] * dabs_ref[...])
    o_ref[...] = raw * window


def _filt_mlp(zpos, w1p, b1, q1, w2d, b2d, q2d, w3h, w3l, dabs):
    n = zpos.shape[0]
    ct = _FFT_CT
    tr = 1024
    per_order = HYENA_WIDTH // ct
    fh = HYENA_FILTER_HIDDEN
    full = lambda shape: pl.BlockSpec(shape, lambda r, i: (0,) * len(shape))
    return pl.pallas_call(
        _filt_mlp_body,
        grid=(n // tr, 2 * per_order),
        in_specs=[
            pl.BlockSpec((tr, 128), lambda r, i: (r, 0)),
            full((128, fh)), full((1, fh)), full((1, fh)),
            full((fh, 2 * fh)), full((1, 2 * fh)), full((1, 2 * fh)),
            pl.BlockSpec((2 * fh, ct), lambda r, i: (0, i)),
            pl.BlockSpec((2 * fh, ct), lambda r, i: (0, i)),
            pl.BlockSpec((1, ct), lambda r, i: (0, i % per_order)),
        ],
        out_specs=pl.BlockSpec((tr, ct), lambda r, i: (r, i)),
        out_shape=jax.ShapeDtypeStruct((n, 2 * HYENA_WIDTH), _F32),
        scratch_shapes=[pltpu.VMEM((tr, 2 * fh), _BF16), pltpu.VMEM((tr, 2 * fh), _BF16)],
        compiler_params=_cparams(("arbitrary", "arbitrary")),
        name="filt_mlp",
    )(zpos, w1p, b1, q1, w2d, b2d, q2d, w3h, w3l, dabs)


_F1_GROUP = 16


def _f1_tiles(f_base, f1pp):
    return [(slice(g * _F1_GROUP, (g + 1) * _F1_GROUP), pl.multiple_of(f_base + g * _F1_GROUP, _F1_GROUP))
            for g in range(f1pp // _F1_GROUP)]


def _gather_f1(s_ref, vre_ref, vim_ref, f_base, f1pp, n1):
    for sl, f0 in _f1_tiles(f_base, f1pp):
        vre_ref[sl] = jnp.swapaxes(s_ref[:, pl.ds(f0, _F1_GROUP), :], 0, 1)
        vim_ref[sl] = jnp.swapaxes(s_ref[:, pl.ds(n1 + f0, _F1_GROUP), :], 0, 1)


def _filt_fft_body(kern_ref, fa_ref, fb_ref, o_ref, s_ref, vre_ref, vim_ref, *, n_in, n1, f1pp):
    s = pl.program_id(1)

    @pl.when(s < n_in)
    def _():
        for i in range(_T2_PER_PHASE):
            u = kern_ref[i].astype(_BF16)
            s_ref[s * _T2_PER_PHASE + i] = jnp.dot(fa_ref[...], u,
                                                   preferred_element_type=_F32).astype(_BF16)

    @pl.when(s >= n_in)
    def _():
        _gather_f1(s_ref, vre_ref, vim_ref, (s - n_in) * f1pp, f1pp, n1)

        for f in range(f1pp):
            v = jnp.concatenate([vre_ref[f], vim_ref[f]], axis=0)
            o_ref[f * 2 * _N2:(f + 1) * 2 * _N2, :] = jnp.dot(fb_ref[f], v, preferred_element_type=_F32)


def _filt_fft(kern3, fa_full, fb):
    n2, n1, cols = kern3.shape
    ct = _FFT_CT
    n_in = n2 // _T2_PER_PHASE
    f1pp = min(_F1_PER_PHASE, n1)
    n_mid = n1 // f1pp
    body = functools.partial(_filt_fft_body, n_in=n_in, n1=n1, f1pp=f1pp)
    return pl.pallas_call(
        body,
        grid=(cols // ct, n_in + n_mid),
        in_specs=[
            pl.BlockSpec((_T2_PER_PHASE, n1, ct), lambda j, s: (jnp.minimum(s, n_in - 1), 0, j)),
            pl.BlockSpec((2 * n1, n1), lambda j, s: (0, 0)),
            pl.BlockSpec((f1pp, 2 * n2, 2 * n2), lambda j, s: (jnp.maximum(s - n_in, 0), 0, 0)),
        ],
        out_specs=pl.BlockSpec((f1pp * 2 * n2, ct), lambda j, s: (jnp.maximum(s - n_in, 0), j)),
        out_shape=jax.ShapeDtypeStruct((n1 * 2 * n2, cols), _F32),
        scratch_shapes=[pltpu.VMEM((n2, 2 * n1, ct), _BF16),
                        pltpu.VMEM((f1pp, n2, ct), _BF16), pltpu.VMEM((f1pp, n2, ct), _BF16)],
        compiler_params=_cparams(("arbitrary", "arbitrary")),
        name="filt_fft",
    )(kern3, fa_full, fb)


def _hyena_conv_body(u_ref, g_ref, k_ref, bias_ref, fa_ref, fai_ref, fb_ref, fbi_ref,
                     o_ref, s_ref, vre_ref, vim_ref, wre_ref, wim_ref, y_ref, ukeep_ref,
                     *, n_in, n_mid, n1, f1pp, natural_out):
    s = pl.program_id(2)
    h1 = n1 // 2

    @pl.when(s < n_in)
    def _():
        for i in range(_T2_PER_PHASE):
            u = jnp.concatenate([u_ref[0, i], u_ref[1, i]], axis=0)
            ukeep_ref[s * _T2_PER_PHASE + i] = u
            s_ref[s * _T2_PER_PHASE + i] = jnp.dot(fa_ref[...], u,
                                                   preferred_element_type=_F32).astype(_BF16)

    @pl.when(jnp.logical_and(s >= n_in, s < n_in + n_mid))
    def _():
        tiles = _f1_tiles((s - n_in) * f1pp, f1pp)

        def gather(sl, f0):
            vre_ref[sl] = jnp.swapaxes(s_ref[:, pl.ds(f0, _F1_GROUP), :], 0, 1)
            vim_ref[sl] = jnp.swapaxes(s_ref[:, pl.ds(n1 + f0, _F1_GROUP), :], 0, 1)

        def forward(f):
            v = jnp.concatenate([vre_ref[f], vim_ref[f]], axis=0)
            x = jnp.dot(fb_ref[f], v, preferred_element_type=_F32)
            xre, xim = x[:_N2], x[_N2:]
            kre = k_ref[f * 2 * _N2:f * 2 * _N2 + _N2, :]
            kim = k_ref[f * 2 * _N2 + _N2:(f + 1) * 2 * _N2, :]
            y_ref[f] = jnp.concatenate([xre * kre - xim * kim, xre * kim + xim * kre],
                                       axis=0).astype(_BF16)

        def inverse(f):
            w = jnp.dot(fbi_ref[f], y_ref[f], preferred_element_type=_F32)
            wre_ref[f] = w[:_N2].astype(_BF16)
            wim_ref[f] = w[_N2:].astype(_BF16)

        def scatter(sl, f0):
            s_ref[:, pl.ds(f0, _F1_GROUP), :] = jnp.swapaxes(wre_ref[sl], 0, 1)
            s_ref[:, pl.ds(n1 + f0, _F1_GROUP), :] = jnp.swapaxes(wim_ref[sl], 0, 1)

        nt = len(tiles)
        for t in tiles:
            gather(*t)
        for g in range(nt + 1):
            for k in range(_F1_GROUP):
                if g < nt:
                    forward(g * _F1_GROUP + k)
                if g >= 1:
                    inverse((g - 1) * _F1_GROUP + k)
            if g >= 1:
                scatter(*tiles[g - 1])

    @pl.when(s >= n_in + n_mid)
    def _():
        bias = bias_ref[...]
        outs = [[], []]
        for i in range(_T2_PER_PHASE):
            t2 = (s - n_in - n_mid) * _T2_PER_PHASE + i
            y = jnp.dot(fai_ref[...], s_ref[t2], preferred_element_type=_F32)
            y = y + ukeep_ref[t2].astype(_F32) * bias
            for m in range(2):
                res = g_ref[m, i].astype(_F32) * y[m * h1:(m + 1) * h1]
                if natural_out:
                    outs[m].append(res.astype(o_ref.dtype))
                else:
                    o_ref[m, i] = res.astype(o_ref.dtype)
        if natural_out:
            for m in range(2):
                o_ref[m] = jnp.swapaxes(jnp.stack(outs[m], axis=0), 0, 1)


def _hyena_conv(u4, ucol, g4, gcol, kspec, kcol, bias, tabs, natural_out):
    fa, fai, fb, fbi = tabs
    b, n2, h1, _ = u4.shape
    n1 = 2 * h1
    ct = _FFT_CT
    n_in = n2 // _T2_PER_PHASE
    f1pp = min(_F1_PER_PHASE, n1)
    n_mid = n1 // f1pp
    n_out = n_in
    tiles = HYENA_WIDTH // ct
    body = functools.partial(_hyena_conv_body, n_in=n_in, n_mid=n_mid, n1=n1, f1pp=f1pp,
                             natural_out=natural_out)
    mid = lambda s: jnp.clip(s - n_in, 0, n_mid - 1)
    last = lambda s: jnp.clip(s - n_in - n_mid, 0, n_out - 1)
    blk = (2, _T2_PER_PHASE, h1, ct)
    if natural_out:
        out_spec = pl.BlockSpec((2, h1, _T2_PER_PHASE, ct), lambda j, p, s: (p, 0, last(s), j))
        out_shape = jax.ShapeDtypeStruct((b, h1, n2, HYENA_WIDTH), _BF16)
    else:
        out_spec = pl.BlockSpec(blk, lambda j, p, s: (p, last(s), 0, j))
        out_shape = jax.ShapeDtypeStruct((b, n2, h1, HYENA_WIDTH), _BF16)
    return pl.pallas_call(
        body,
        grid=(tiles, b // 2, n_in + n_mid + n_out),
        in_specs=[
            pl.BlockSpec(blk, lambda j, p, s: (p, jnp.minimum(s, n_in - 1), 0, ucol + j)),
            pl.BlockSpec(blk, lambda j, p, s: (p, last(s), 0, gcol + j)),
            pl.BlockSpec((f1pp * 2 * n2, ct), lambda j, p, s: (mid(s), kcol + j)),
            pl.BlockSpec((1, ct), lambda j, p, s: (0, j)),
            pl.BlockSpec((2 * n1, n1), lambda j, p, s: (0, 0)),
            pl.BlockSpec((n1, 2 * n1), lambda j, p, s: (0, 0)),
            pl.BlockSpec((f1pp, 2 * n2, 2 * n2), lambda j, p, s: (mid(s), 0, 0)),
            pl.BlockSpec((f1pp, 2 * n2, 2 * n2), lambda j, p, s: (mid(s), 0, 0)),
        ],
        out_specs=out_spec,
        out_shape=out_shape,
        scratch_shapes=([pltpu.VMEM((n2, 2 * n1, ct), _BF16)] + 4 * [pltpu.VMEM((f1pp, n2, ct), _BF16)]
                        + [pltpu.VMEM((f1pp, 2 * n2, ct), _BF16), pltpu.VMEM((n2, n1, ct), _BF16)]),
        compiler_params=_cparams(("arbitrary", "arbitrary", "arbitrary")),
        name="hyena_conv",
    )(u4, g4, kspec, bias, fa, fai, fb, fbi)


_GATE_LANES = 2 * MLSTM_HEADS
_TERM_R = 12
_MLSTM_SUB = 2


def _split3(x):
    hi = x.astype(_BF16).astype(_F32)
    mid = (x - hi).astype(_BF16).astype(_F32)
    lo = (x - hi - mid).astype(_BF16).astype(_F32)
    return hi, mid, lo


def _scan_order_max(x, is_fwd):
    n = x.shape[0]
    row = lax.broadcasted_iota(jnp.int32, x.shape, 0)
    pre, suf = x, x
    shift = 1
    while shift < n:
        pre = jnp.maximum(pre, jnp.where(row >= shift, pltpu.roll(pre, shift, axis=0), -jnp.inf))
        suf = jnp.maximum(suf, jnp.where(row < n - shift, pltpu.roll(suf, n - shift, axis=0), -jnp.inf))
        shift *= 2
    return jnp.where(is_fwd, pre, suf)


def _gate_prep_body(g_ref, bias_ref, a_ref, row_ref, keep_ref,
                    b_scr, a_scr, pm_scr, tot_scr, mloc_scr, mprev_scr, *, nc):
    ch = MLSTM_CHUNK
    lane = lax.broadcasted_iota(jnp.int32, (1, 128), 1)
    is_fwd = lane < MLSTM_HEADS
    live = lane < _GATE_LANES
    jj = lax.broadcasted_iota(jnp.int32, (ch, ch), 0)
    ss = lax.broadcasted_iota(jnp.int32, (ch, ch), 1)
    t_lo = (ss <= jj).astype(_F32)
    row_ref[...] = jnp.zeros_like(row_ref)

    def chunk_stats(c, carry):
        r0 = pl.multiple_of(c * ch, ch)
        gi = g_ref[pl.ds(r0, ch), :] + bias_ref[...]
        gf = pltpu.roll(gi, 128 - 2 * MLSTM_HEADS, axis=1)
        logf = jnp.minimum(gf, 0.0) - jnp.log1p(jnp.exp(-jnp.abs(gf)))
        cs_lo = jnp.dot(t_lo, logf, precision=_HIGHEST, preferred_element_type=_F32)
        tot = cs_lo[ch - 1:ch, :]
        cs_up = tot - cs_lo + logf
        bcs = jnp.where(is_fwd, cs_lo, cs_up)
        a = tot - bcs + gi
        b_scr[pl.ds(r0, ch), :] = bcs
        a_scr[pl.ds(r0, ch), :] = a
        tot_scr[pl.ds(c, 1), :] = tot
        mloc_scr[pl.ds(c, 1), :] = jnp.max(a, axis=0, keepdims=True)
        r = gi - bcs
        pm_scr[pl.ds(r0, ch), :] = _scan_order_max(r, is_fwd)
        hi, mid, lo = _split3(r.T[0:_GATE_LANES, :])
        base = _TERM_R * _GATE_LANES
        row_ref[base:base + 16, pl.ds(r0, ch)] = jnp.concatenate([hi, mid], axis=0).astype(_BF16)
        row_ref[base + 16:base + 32, pl.ds(r0, ch)] = jnp.concatenate(
            [lo, jnp.zeros_like(lo)], axis=0).astype(_BF16)
        return carry

    lax.fori_loop(0, nc, chunk_stats, 0, unroll=4)

    def scan_f(c, m):
        mprev_scr[pl.ds(c, 1), :] = jnp.where(is_fwd, m, mprev_scr[pl.ds(c, 1), :])
        return jnp.maximum(tot_scr[pl.ds(c, 1), :] + m, mloc_scr[pl.ds(c, 1), :])

    def scan_b(i, m):
        c = nc - 1 - i
        mprev_scr[pl.ds(c, 1), :] = jnp.where(is_fwd, mprev_scr[pl.ds(c, 1), :], m)
        return jnp.maximum(tot_scr[pl.ds(c, 1), :] + m, mloc_scr[pl.ds(c, 1), :])

    mprev_scr[...] = jnp.zeros_like(mprev_scr)
    lax.fori_loop(0, nc, scan_f, jnp.zeros((1, 128), _F32))
    lax.fori_loop(0, nc, scan_b, jnp.zeros((1, 128), _F32))

    def emit(c, carry):
        r0 = pl.multiple_of(c * ch, ch)
        mprev = mprev_scr[pl.ds(c, 1), :]
        tot = tot_scr[pl.ds(c, 1), :]
        mnew = jnp.maximum(tot + mprev, mloc_scr[pl.ds(c, 1), :])
        keep_ref[pl.ds(c, 1), :] = jnp.exp(tot + mprev - mnew)
        g = jnp.maximum(mprev, pm_scr[pl.ds(r0, ch), :])
        terms = (_split3(-g) + _split3(mprev - g) + _split3(-b_scr[pl.ds(r0, ch), :] - g)
                 + _split3(a_scr[pl.ds(r0, ch), :] - mnew) + 3 * (jnp.ones((ch, 128), _F32),))
        pack = jnp.where(live, terms[0], 0.0)
        for t in range(1, len(terms)):
            pack = pack + pltpu.roll(jnp.where(live, terms[t], 0.0), _GATE_LANES * t, axis=1)
        a_ref[pl.ds(r0, ch), :] = pack.astype(_BF16)
        return carry

    lax.fori_loop(0, nc, emit, 0, unroll=4)


def _gate_prep(g3, bias):
    b, l, _ = g3.shape
    nc = l // MLSTM_CHUNK
    body = functools.partial(_gate_prep_body, nc=nc)
    return pl.pallas_call(
        body,
        grid=(b,),
        in_specs=[
            pl.BlockSpec((None, l, 128), lambda i: (i, 0, 0)),
            pl.BlockSpec((1, 128), lambda i: (0, 0)),
        ],
        out_specs=[
            pl.BlockSpec((None, l, 128), lambda i: (i, 0, 0)),
            pl.BlockSpec((None, 128, l), lambda i: (i, 0, 0)),
            pl.BlockSpec((None, nc, 128), lambda i: (i, 0, 0)),
        ],
        out_shape=[
            jax.ShapeDtypeStruct((b, l, 128), _BF16),
            jax.ShapeDtypeStruct((b, 128, l), _BF16),
            jax.ShapeDtypeStruct((b, nc, 128), _F32),
        ],
        scratch_shapes=[
            pltpu.VMEM((l, 128), _F32), pltpu.VMEM((l, 128), _F32), pltpu.VMEM((l, 128), _F32),
            pltpu.VMEM((nc, 128), _F32), pltpu.VMEM((nc, 128), _F32), pltpu.VMEM((nc, 128), _F32),
        ],
        compiler_params=_cparams(("arbitrary",)),
        name="gate_prep",
    )(g3, bias)


def _gate_spread_matrix():
    bc = np.zeros((128, 4 * 128), np.float32)
    for blk in range(4):
        for t in range(3 * blk, 3 * blk + 3):
            bc[t * _GATE_LANES:(t + 1) * _GATE_LANES, blk * 128:(blk + 1) * 128] = 1.0
    return jnp.asarray(bc).astype(_BF16)


def _mlstm_body(keep_ref, bc_ref, qf_ref, kf_ref, vf_ref, af_ref, rf_ref, qb_ref, kb_ref, vb_ref, ab_ref,
                rb_ref, hf_ref, hb_ref, ct_ref, nm_ref, *, nc):
    bi, c = pl.program_id(0), pl.program_id(1)

    @pl.when(c == 0)
    def _():
        ct_ref[...] = jnp.zeros_like(ct_ref)
        nm_ref[...] = jnp.zeros_like(nm_ref)

    ch, dh = MLSTM_CHUNK, MLSTM_HEAD_DIM
    jj = lax.broadcasted_iota(jnp.int32, (ch, ch), 0)
    ss = lax.broadcasted_iota(jnp.int32, (ch, ch), 1)
    head_lane = ss % _GATE_LANES
    ones_rhs = jnp.ones((ch, 128), _BF16)
    twice = lambda a: jnp.concatenate([a, a], axis=1)
    dirs = ((qf_ref, kf_ref, vf_ref, af_ref, rf_ref, hf_ref, ss <= jj),
            (qb_ref, kb_ref, vb_ref, ab_ref, rb_ref, hb_ref, ss >= jj))
    for sub, d in [(sub, d) for sub in range(_MLSTM_SUB) for d in range(2)]:
        q_blk, k_blk, v_blk, a_blk, r_blk, o_blk, mask = dirs[d]
        pos = sub if d == 0 else _MLSTM_SUB - 1 - sub
        chunk = c * _MLSTM_SUB + pos if d == 0 else nc - (c + 1) * _MLSTM_SUB + pos
        rows = slice(pos * ch, (pos + 1) * ch)
        q_ref, k_ref, v_ref, o_ref = q_blk.at[rows], k_blk.at[rows], v_blk.at[rows], o_blk.at[rows]
        bmat = jnp.concatenate([bc_ref[:, 0:128] + r_blk[:, rows], bc_ref[:, 128:]], axis=1)
        a_all = a_blk[rows, :].astype(_F32)
        kbase = (bi * nc + chunk) * _GATE_LANES
        for h in range(MLSTM_HEADS):
            hd = d * MLSTM_HEADS + h
            a_h = jnp.where(head_lane == hd, a_all, 0.0).astype(_BF16)
            e = jnp.dot(a_h, bmat, preferred_element_type=_F32)
            decay = jnp.exp(jnp.where(mask, e[:, 0:128], -jnp.inf))
            rest = jnp.exp(e[:, 128:])
            iw, clamp, wrep = rest[:, 0:128], rest[:, 128:256], rest[:, 256:384]
            keep = keep_ref[kbase + hd]
            lo, hi = h * dh, (h + 1) * dh
            qh, kh, vh = q_ref[:, lo:hi], k_ref[:, lo:hi], v_ref[:, lo:hi]
            p = (_dot_nt(qh, kh) * decay).astype(_BF16)
            ct, nm = ct_ref[hd], nm_ref[hd]
            pv = jnp.dot(p, jnp.concatenate([vh, ones_rhs], axis=1), preferred_element_type=_F32)
            qc = jnp.dot(qh, jnp.concatenate([ct, nm], axis=1).astype(_BF16), preferred_element_type=_F32)
            num = pv[:, :dh] + twice(iw) * qc[:, :dh]
            den = pv[:, dh:] + iw * qc[:, dh:]
            inv = 1.0 / jnp.maximum(jnp.abs(den), clamp)
            o_ref[:, lo:hi] = (num * twice(inv)).astype(o_ref.dtype)
            vw = jnp.concatenate([(vh.astype(_F32) * twice(wrep)).astype(_BF16), wrep.astype(_BF16)], axis=1)
            upd = lax.dot_general(kh, vw, (((0,), (0,)), ((), ())), preferred_element_type=_F32)
            ct_ref[hd] = keep * ct + upd[:, :dh]
            nm_ref[hd] = keep * nm + upd[:, dh:]


def _mlstm(qk, proj3, a_mat, r_rows, keep):
    b, l, _ = qk.shape
    mw = MLSTM_WIDTH
    nc = l // MLSTM_CHUNK
    rows = _MLSTM_SUB * MLSTM_CHUNK
    nb = l // rows
    fw = lambda c: c
    bw = lambda c: nb - 1 - c

    def specs(cm):
        return [
            pl.BlockSpec((None, rows, mw), lambda i, c: (i, cm(c), 0)),
            pl.BlockSpec((None, rows, mw), lambda i, c: (i, cm(c), 1)),
            pl.BlockSpec((None, rows, mw), lambda i, c: (i, cm(c), _COL_V)),
            pl.BlockSpec((None, rows, 128), lambda i, c: (i, cm(c), 0)),
            pl.BlockSpec((None, 128, rows), lambda i, c: (i, 0, cm(c))),
        ]

    body = functools.partial(_mlstm_body, nc=nc)
    keep_flat = keep[:, :, :_GATE_LANES].reshape(b * nc * _GATE_LANES)
    return pl.pallas_call(
        body,
        grid=(b, nb),
        in_specs=[pl.BlockSpec(memory_space=pltpu.SMEM),
                  pl.BlockSpec((128, 4 * 128), lambda i, c: (0, 0))] + specs(fw) + specs(bw),
        out_specs=[
            pl.BlockSpec((None, rows, mw), lambda i, c: (i, c, 0)),
            pl.BlockSpec((None, rows, mw), lambda i, c: (i, nb - 1 - c, 0)),
        ],
        out_shape=[jax.ShapeDtypeStruct((b, l, mw), _BF16), jax.ShapeDtypeStruct((b, l, mw), _BF16)],
        scratch_shapes=[
            pltpu.VMEM((2 * MLSTM_HEADS, MLSTM_HEAD_DIM, MLSTM_HEAD_DIM), _F32),
            pltpu.VMEM((2 * MLSTM_HEADS, MLSTM_HEAD_DIM, 128), _F32),
        ],
        compiler_params=_cparams(("arbitrary", "arbitrary")),
        name="mlstm",
    )(keep_flat, _gate_spread_matrix(), qk, qk, proj3, a_mat, r_rows, qk, qk, proj3, a_mat, r_rows)


def _sigmoid(x):
    return 1.0 / (1.0 + jnp.exp(-x))


def _merge_body(x_ref, ya_ref, hf_ref, hb_ref, o_ref, ma_ref, mb_ref, wa_ref, wb_ref, wo_ref, out_ref):
    hsum = hf_ref[...].astype(_F32) + hb_ref[...].astype(_F32)
    yb = (_sigmoid(o_ref[...].astype(_F32)) * hsum).astype(_BF16)
    pa = jnp.dot(ya_ref[...], wa_ref[...], preferred_element_type=_F32)
    pb = jnp.dot(yb, wb_ref[...], preferred_element_type=_F32)
    mixed = _sigmoid(ma_ref[...].astype(_F32)) * pa + _sigmoid(mb_ref[...].astype(_F32)) * pb
    out_ref[...] = x_ref[...] + jnp.dot(mixed.astype(_BF16), wo_ref[...], preferred_element_type=_F32)


def _merge(x2, ya2, hf2, hb2, proj, wa, wb, wo):
    m = x2.shape[0]
    tm = 256
    d, hw, mw = D_MODEL, HYENA_WIDTH, MLSTM_WIDTH
    const = lambda shape: pl.BlockSpec(shape, lambda i: (0, 0), pipeline_mode=pl.Buffered(1))
    return pl.pallas_call(
        _merge_body,
        grid=(m // tm,),
        in_specs=[
            pl.BlockSpec((tm, d), lambda i: (i, 0)),
            pl.BlockSpec((tm, hw), lambda i: (i, 0)),
            pl.BlockSpec((tm, mw), lambda i: (i, 0)),
            pl.BlockSpec((tm, mw), lambda i: (i, 0)),
            pl.BlockSpec((tm, mw), lambda i: (i, _COL_O)),
            pl.BlockSpec((tm, d), lambda i: (i, _COL_MA // 2)),
            pl.BlockSpec((tm, d), lambda i: (i, _COL_MB // 2)),
            const((hw, d)), const((mw, d)), const((d, d)),
        ],
        out_specs=pl.BlockSpec((tm, d), lambda i: (i, 0)),
        out_shape=jax.ShapeDtypeStruct((m, d), _F32),
        compiler_params=_cparams(("arbitrary",)),
        name="merge",
    )(x2, ya2, hf2, hb2, proj, proj, proj, wa, wb, wo)


def _ffn_body(x_ref, n2_ref, wg_ref, wu_ref, wd_ref, nf_ref, o_ref, hn_ref):
    f = pl.program_id(1)

    @pl.when(f == 0)
    def _():
        x = x_ref[...]
        hn = x * lax.rsqrt(jnp.mean(x * x, axis=-1, keepdims=True) + RMS_EPS) * n2_ref[...]
        hn_ref[...] = hn.astype(_BF16)
        o_ref[...] = x

    hn = hn_ref[...]
    g = jnp.dot(hn, wg_ref[...], preferred_element_type=_F32)
    u = jnp.dot(hn, wu_ref[...], preferred_element_type=_F32)
    a = (g * _sigmoid(g) * u).astype(_BF16)
    o_ref[...] += jnp.dot(a, wd_ref[...], preferred_element_type=_F32)

    @pl.when(f == pl.num_programs(1) - 1)
    def _():
        y = o_ref[...]
        o_ref[...] = y * lax.rsqrt(jnp.mean(y * y, axis=-1, keepdims=True) + RMS_EPS) * nf_ref[...]


def _ffn(x2, norm2_w, w_gate_up, w_down, norm_f_w):
    m = x2.shape[0]
    tm, tf = 1024, 512
    d = D_MODEL
    nf = FFN_HIDDEN // tf
    return pl.pallas_call(
        _ffn_body,
        grid=(m // tm, nf),
        in_specs=[
            pl.BlockSpec((tm, d), lambda i, f: (i, 0)),
            pl.BlockSpec((1, d), lambda i, f: (0, 0)),
            pl.BlockSpec((d, tf), lambda i, f: (0, f)),
            pl.BlockSpec((d, tf), lambda i, f: (0, nf + f)),
            pl.BlockSpec((tf, d), lambda i, f: (f, 0)),
            pl.BlockSpec((1, d), lambda i, f: (0, 0)),
        ],
        out_specs=pl.BlockSpec((tm, d), lambda i, f: (i, 0)),
        out_shape=jax.ShapeDtypeStruct((m, d), _F32),
        scratch_shapes=[pltpu.VMEM((tm, d), _BF16)],
        compiler_params=_cparams(("arbitrary", "arbitrary")),
        name="ffn",
    )(x2, norm2_w, w_gate_up, w_gate_up, w_down, norm_f_w)


def kernel(x, norm1_w, w_in, conv_w, conv_b, filt_w1, filt_b1, filt_freq1, filt_w2, filt_b2, filt_freq2,
           filt_w3, hyena_bias, mlstm_gate_bias, w_branch_a, w_branch_b, w_out, norm2_w, w_gate_up, w_down,
           norm_f_w):
    b, l, d = x.shape
    assert d == D_MODEL and b % 2 == 0 and l % (_N2 * _T2_PER_PHASE) == 0
    assert norm1_w.shape[0] == 1, "single-layer block"
    hw, mw, nh = HYENA_WIDTH, MLSTM_WIDTH, MLSTM_HEADS
    m = b * l
    n = 2 * l
    n1 = n // _N2
    sc_cols = 3 * hw + 2 * mw
    g0 = sc_cols + 2 * mw

    w_t = jnp.swapaxes(w_in[0], 0, 1)
    w_all_t = w_t.astype(_BF16)
    w_merge_t = w_all_t[g0 + 4 * nh:]
    wg_t = w_t[g0:g0 + 4 * nh]
    gate_order = lambda a: jnp.concatenate(
        [a[0:nh], a[2 * nh:3 * nh], a[nh:2 * nh], a[3 * nh:4 * nh]], axis=0)
    wg_hi_t, wg_lo_t = _split_bf16(jnp.pad(gate_order(wg_t), ((0, 128 - 4 * nh), (0, 0))))
    gate_bias = jnp.pad(gate_order(mlstm_gate_bias[0].astype(_F32).reshape(4 * nh))[None, :],
                        ((0, 0), (0, 128 - 4 * nh)))

    x2 = x.reshape(m, d)
    proj, gates = _inproj(x2, norm1_w[0][None, :], w_merge_t, w_all_t, wg_hi_t, wg_lo_t)
    proj3 = proj.reshape(b, l, _MAIN_COLS)

    cw, cb = conv_w[0], conv_b[0][None, :]
    hy4 = _shortconv(proj3, cw, cb, _COL_HV * 1024, 0, 3 * hw, True)
    qk = _shortconv(proj3, cw, cb, _COL_Q * 1024, 3 * hw, 2 * mw, False)

    fa, fa_full, fai, fb, fbi = _dft_tables(n)
    zpos = _filter_positions(l)
    w1p = jnp.pad(filt_w1[0].astype(_F32), ((0, 128 - HYENA_EMB), (0, 0)))
    max_decay = math.log(HYENA_TARGET) / HYENA_FAST_DECAY
    min_decay = math.log(HYENA_TARGET) / HYENA_SLOW_DECAY
    dabs = jnp.asarray(np.abs(np.linspace(min_decay, max_decay, hw, dtype=np.float32))[None, :])
    dup = lambda a: jnp.concatenate([a, a], axis=-1)
    fh = HYENA_FILTER_HIDDEN
    w3 = filt_w3[0].astype(_F32).reshape(fh, 2, 2, hw).transpose(2, 0, 1, 3).reshape(2 * fh, 2 * hw)
    w3_hi, w3_lo = _split_bf16(w3)
    kern = _filt_mlp(zpos, w1p, filt_b1[0][None, :], filt_freq1[0][None, :], dup(filt_w2[0]),
                     dup(filt_b2[0][None, :]), dup(filt_freq2[0][None, :]), w3_hi, w3_lo, dabs)
    kspec = _filt_fft(kern.reshape(_N2, n1, 2 * hw), fa_full, fb)
    tabs = (fa, fai, fb, fbi)
    tiles = hw // _FFT_CT
    hbias = hyena_bias[0].astype(_F32)
    z4 = _hyena_conv(hy4, 0, hy4, tiles, kspec, 0, hbias[0][None, :], tabs, False)
    ya4 = _hyena_conv(z4, 0, hy4, 2 * tiles, kspec, tiles, hbias[1][None, :], tabs, True)
    ya2 = ya4.reshape(m, hw)

    a_mat, r_rows, keep = _gate_prep(gates.reshape(b, l, 128), gate_bias)
    hf, hb = _mlstm(qk, proj3, a_mat, r_rows, keep)

    x_mid = _merge(x2, ya2, hf.reshape(m, mw), hb.reshape(m, mw), proj,
                   w_branch_a[0].astype(_BF16), w_branch_b[0].astype(_BF16), w_out[0].astype(_BF16))
    out = _ffn(x_mid, norm2_w[0][None, :], w_gate_up[0].astype(_BF16), w_down[0].astype(_BF16),
               norm_f_w[None, :])
    return out.reshape(b, l, d)
```

```python
import functools
import math

import numpy as np
import jax
import jax.numpy as jnp
from jax import lax
from jax.experimental import pallas as pl
from jax.experimental.pallas import tpu as pltpu

_F32 = jnp.float32
_BF16 = jnp.bfloat16
_HIGHEST = lax.Precision.HIGHEST

D_MODEL = 2048
HYENA_WIDTH = 1024
HYENA_EMB = 33
HYENA_FILTER_HIDDEN = 64
HYENA_FAST_DECAY = 0.3
HYENA_SLOW_DECAY = 1.5
HYENA_TARGET = 1e-2
MLSTM_WIDTH = 1024
MLSTM_HEADS = 4
MLSTM_HEAD_DIM = 256
MLSTM_CHUNK = 128
FFN_HIDDEN = 5632
RMS_EPS = 1e-6

_COL_MA, _COL_MB, _COL_HV, _COL_Q, _COL_V, _COL_O = 0, 2, 4, 7, 9, 10
_MAIN_COLS = 11 * 1024

_N2 = 32
_T2_PER_PHASE = 16
_F1_PER_PHASE = 64
_FFT_CT = 256

_VMEM_LIMIT = 56 * 1024 * 1024


def _cparams(sem, vmem=_VMEM_LIMIT):
    return pltpu.CompilerParams(dimension_semantics=sem, vmem_limit_bytes=vmem)


def _split_bf16(a):
    hi = a.astype(_BF16)
    lo = (a - hi.astype(_F32)).astype(_BF16)
    return hi, lo


def _dot_nt(a, b):
    return lax.dot_general(a, b, (((1,), (1,)), ((), ())), preferred_element_type=_F32)


def _dot3(a_hi, a_lo, b_hi, b_lo):
    d = functools.partial(jnp.dot, preferred_element_type=_F32)
    return d(a_hi, b_hi) + (d(a_hi, b_lo) + d(a_lo, b_hi))


def _inproj_body(x_ref, nw_ref, wm_ref, w_ref, wgh_ref, wgl_ref, o_ref, g_ref, hn_ref, *, n_merge):
    j = pl.program_id(1)

    @pl.when(j == 0)
    def _():
        rows = 256
        for r in range(0, x_ref.shape[0], rows):
            x = x_ref[r:r + rows, :]
            hn = x * lax.rsqrt(jnp.mean(x * x, axis=-1, keepdims=True) + RMS_EPS) * nw_ref[...]
            hn_hi, hn_lo = _split_bf16(hn)
            hn_ref[r:r + rows, :] = hn_hi
            g_ref[r:r + rows, :] = (_dot_nt(hn_hi, wgh_ref[...])
                                    + (_dot_nt(hn_hi, wgl_ref[...]) + _dot_nt(hn_lo, wgh_ref[...])))

    @pl.when(j < n_merge)
    def _():
        o_ref[...] = _dot_nt(hn_ref[...], wm_ref[...]).astype(o_ref.dtype)

    @pl.when(j >= n_merge)
    def _():
        o_ref[...] = _dot_nt(hn_ref[...], w_ref[...]).astype(o_ref.dtype)


def _inproj(x2, norm_w, w_merge_t, w_all_t, wg_hi_t, wg_lo_t):
    m = x2.shape[0]
    tm, tn = 1024, 1024
    n_merge = w_merge_t.shape[0] // tn
    body = functools.partial(_inproj_body, n_merge=n_merge)
    return pl.pallas_call(
        body,
        grid=(m // tm, _MAIN_COLS // tn),
        in_specs=[
            pl.BlockSpec((tm, D_MODEL), lambda i, j: (i, 0)),
            pl.BlockSpec((1, D_MODEL), lambda i, j: (0, 0)),
            pl.BlockSpec((tn, D_MODEL), lambda i, j: (jnp.minimum(j, n_merge - 1), 0)),
            pl.BlockSpec((tn, D_MODEL), lambda i, j: (jnp.maximum(j - n_merge, 0), 0)),
            pl.BlockSpec((128, D_MODEL), lambda i, j: (0, 0)),
            pl.BlockSpec((128, D_MODEL), lambda i, j: (0, 0)),
        ],
        out_specs=[
            pl.BlockSpec((tm, tn), lambda i, j: (i, j)),
            pl.BlockSpec((tm, 128), lambda i, j: (i, 0)),
        ],
        out_shape=[
            jax.ShapeDtypeStruct((m, _MAIN_COLS), _BF16),
            jax.ShapeDtypeStruct((m, 128), _F32),
        ],
        scratch_shapes=[pltpu.VMEM((tm, D_MODEL), _BF16)],
        compiler_params=_cparams(("arbitrary", "arbitrary")),
        name="inproj",
    )(x2, norm_w, w_merge_t, w_all_t, wg_hi_t, wg_lo_t)


def _shortconv_silu_body(u_ref, w_ref, b_ref, o_ref, *, k_scale):
    u = u_ref[...].astype(_F32)
    n = u.shape[0]
    row = lax.broadcasted_iota(jnp.int32, u.shape, 0)
    up = jnp.where(row == 0, 0.0, pltpu.roll(u, 1, axis=0))
    un = jnp.where(row == n - 1, 0.0, pltpu.roll(u, n - 1, axis=0))
    w = w_ref[...]
    y = up * w[0:1, :] + b_ref[...] + u * w[1:2, :] + un * w[2:3, :]
    y = y * (1.0 / (1.0 + jnp.exp(-y)))
    is_k = pl.program_id(1) >= (MLSTM_WIDTH // u.shape[1])
    o_ref[...] = (y * jnp.where(is_k, k_scale, 1.0)).astype(o_ref.dtype)


def _shortconv_t2_body(u_ref, w_ref, b_ref, o_ref, ut_ref):
    n, c = u_ref.shape
    h1 = n // _N2
    g = 16
    for k in range(h1 // g):
        blk = u_ref[k * g * _N2:(k + 1) * g * _N2, :].reshape(g, _N2, c)
        ut_ref[:, k * g:(k + 1) * g, :] = jnp.swapaxes(blk, 0, 1)
    w = w_ref[...]
    w0, w1, w2, bias = w[0:1, :], w[1:2, :], w[2:3, :], b_ref[...]
    slab = lambda t2: ut_ref[t2].astype(_F32)
    row = lax.broadcasted_iota(jnp.int32, (h1, c), 0)
    before_first = jnp.where(row == 0, 0.0, pltpu.roll(slab(_N2 - 1), 1, axis=0))
    after_last = jnp.where(row == h1 - 1, 0.0, pltpu.roll(slab(0), h1 - 1, axis=0))
    for t2 in range(_N2):
        up = before_first if t2 == 0 else slab(t2 - 1)
        un = after_last if t2 == _N2 - 1 else slab(t2 + 1)
        o_ref[t2] = (up * w0 + bias + slab(t2) * w1 + un * w2).astype(o_ref.dtype)


def _shortconv(proj3, conv_w, conv_b, proj_col0, conv_col0, ncols, t2_major):
    b, l, _ = proj3.shape
    ct = 256
    p0, c0 = proj_col0 // ct, conv_col0 // ct
    if t2_major:
        body = _shortconv_t2_body
        out_spec = pl.BlockSpec((None, _N2, l // _N2, ct), lambda i, j: (i, 0, 0, j))
        out_shape = jax.ShapeDtypeStruct((b, _N2, l // _N2, ncols), _BF16)
        scratch = [pltpu.VMEM((_N2, l // _N2, ct), _BF16)]
    else:
        body = functools.partial(_shortconv_silu_body, k_scale=MLSTM_HEAD_DIM ** -0.5)
        out_spec = pl.BlockSpec((None, l, ct), lambda i, j: (i, 0, j))
        out_shape = jax.ShapeDtypeStruct((b, l, ncols), _BF16)
        scratch = []
    return pl.pallas_call(
        body,
        grid=(b, ncols // ct),
        in_specs=[
            pl.BlockSpec((None, l, ct), lambda i, j: (i, 0, p0 + j)),
            pl.BlockSpec((3, ct), lambda i, j: (0, c0 + j)),
            pl.BlockSpec((1, ct), lambda i, j: (0, c0 + j)),
        ],
        out_specs=out_spec,
        out_shape=out_shape,
        scratch_shapes=scratch,
        compiler_params=_cparams(("arbitrary", "arbitrary")),
        name="shortconv" if t2_major else "shortconv_silu",
    )(proj3, conv_w, conv_b)


def _dft_tables(n):
    n2 = _N2
    n1 = n // n2
    h1 = n1 // 2
    f1 = np.arange(n1)[:, None]
    t1 = np.arange(n1)[None, :]
    ang = 2.0 * np.pi * ((f1 * t1) % n1) / n1
    c, s = np.cos(ang), np.sin(ang)
    ch, sh = c[:, :h1], s[:, :h1]
    fa = np.block([[ch, sh], [-sh, ch]])
    fa_full = np.concatenate([c, -s], axis=0)
    fai = np.block([[ch.T, -sh.T], [sh.T, ch.T]]) / n
    f1v = np.arange(n1)[:, None, None]
    f2 = np.arange(n2)[None, :, None]
    t2 = np.arange(n2)[None, None, :]
    ph = 2.0 * np.pi * ((f1v * t2 + f2 * t2 * n1) % n) / n
    tc, ts = np.cos(ph), np.sin(ph)
    fb = np.concatenate([np.concatenate([tc, ts], axis=2),
                         np.concatenate([-ts, tc], axis=2)], axis=1)
    tct, tst = np.swapaxes(tc, 1, 2), np.swapaxes(ts, 1, 2)
    fbi = np.concatenate([np.concatenate([tct, -tst], axis=2),
                          np.concatenate([tst, tct], axis=2)], axis=1)
    to = lambda a: jnp.asarray(a.astype(np.float32)).astype(_BF16)
    return to(fa), to(fa_full), to(fai), to(fb), to(fbi)


def _filter_positions(l):
    n = 2 * l
    n1 = n // _N2
    r = (np.arange(_N2)[:, None] + _N2 * np.arange(n1)[None, :]).reshape(-1)
    pos = np.where(r < l, r, n - r)
    pos = np.where(r == l, 0, pos)
    bands = (HYENA_EMB - 1) // 2
    tt = np.linspace(0.0, 1.0, l, dtype=np.float32).astype(np.float64)[pos]
    omega = (2.0 * math.pi * np.arange(l, dtype=np.float32) / l).astype(np.float32)
    freqs = np.linspace(1e-4, bands - 1, bands, dtype=np.float32)
    ang = (omega[:, None] * freqs[None, :]).astype(np.float64)[pos]
    z = np.zeros((n, 128), np.float64)
    z[:, 0] = tt
    z[:, 1:1 + bands] = np.cos(ang)
    z[:, 1 + bands:1 + 2 * bands] = -np.sin(ang)
    z[:, 33] = (r != l)
    z[:, 34] = (r < l)
    return jnp.asarray(z.astype(np.float32))


def _filt_mlp_body(z_ref, zp_ref, w1_ref, b1_ref, q1_ref, w2_ref, b2_ref, q2_ref, w3h_ref, w3l_ref, dabs_ref,
                   o_ref, hh_ref, hl_ref):
    fh = HYENA_FILTER_HIDDEN

    @pl.when(pl.program_id(1) == 0)
    def _():
        hp = jnp.sin(q1_ref[...] * (jnp.dot(zp_ref[...], w1_ref[...], precision=_HIGHEST,
                                            preferred_element_type=_F32) + b1_ref[...]))
        hp = jnp.sin(q2_ref[...] * (jnp.dot(hp, w2_ref[...], precision=_HIGHEST,
                                            preferred_element_type=_F32) + b2_ref[...]))
        lane_p = lax.broadcasted_iota(jnp.int32, hp.shape, 1)
        swapped = pltpu.roll(hp, fh, axis=1)
        h = jnp.concatenate([jnp.where(lane_p < fh, hp, swapped), jnp.where(lane_p < fh, swapped, hp)], axis=0)
        z = z_ref[...]
        valid = z[:, 33:34]
        fwd = z[:, 34:35]
        lane = lax.broadcasted_iota(jnp.int32, h.shape, 1)
        hi, lo = _split_bf16(h * (valid * jnp.where(lane < fh, fwd, 1.0 - fwd)))
        hh_ref[...] = hi
        hl_ref[...] = lo

    raw = _dot3(hh_ref[...], hl_ref[...], w3h_ref[...], w3l_ref[...])
    window = jnp.exp(-z_ref[:, 0:1] * dabs_ref[...])
    o_ref[...] = (raw * window).astype(o_ref.dtype)


_FILT_ROWS = 1024


def _filt_mlp(zpos, w1bd, b1d, q1d, w2bd, b2d, q2d, w3h, w3l, dabs):
    n = zpos.shape[0]
    ct = _FFT_CT
    tr = _FILT_ROWS
    per_order = HYENA_WIDTH // ct
    fh = HYENA_FILTER_HIDDEN
    zpacked = zpos.reshape(n // tr, 2, tr // 2, 128).transpose(0, 2, 1, 3).reshape(n // 2, 256)
    full = lambda shape: pl.BlockSpec(shape, lambda r, i: (0,) * len(shape))
    return pl.pallas_call(
        _filt_mlp_body,
        grid=(n // tr, 2 * per_order),
        in_specs=[
            pl.BlockSpec((tr, 128), lambda r, i: (r, 0)),
            pl.BlockSpec((tr // 2, 256), lambda r, i: (r, 0)),
            full((256, 2 * fh)), full((1, 2 * fh)), full((1, 2 * fh)),
            full((2 * fh, 2 * fh)), full((1, 2 * fh)), full((1, 2 * fh)),
            pl.BlockSpec((2 * fh, ct), lambda r, i: (0, i)),
            pl.BlockSpec((2 * fh, ct), lambda r, i: (0, i)),
            pl.BlockSpec((1, ct), lambda r, i: (0, i % per_order)),
        ],
        out_specs=pl.BlockSpec((tr, ct), lambda r, i: (r, i)),
        out_shape=jax.ShapeDtypeStruct((n, 2 * HYENA_WIDTH), _BF16),
        scratch_shapes=[pltpu.VMEM((tr, 2 * fh), _BF16), pltpu.VMEM((tr, 2 * fh), _BF16)],
        compiler_params=_cparams(("arbitrary", "arbitrary")),
        name="filt_mlp",
    )(zpos, zpacked, w1bd, b1d, q1d, w2bd, b2d, q2d, w3h, w3l, dabs)


_F1_GROUP = 16


def _f1_tiles(f_base, f1pp):
    return [(slice(g * _F1_GROUP, (g + 1) * _F1_GROUP), pl.multiple_of(f_base + g * _F1_GROUP, _F1_GROUP))
            for g in range(f1pp // _F1_GROUP)]


def _gather_f1(s_ref, vre_ref, vim_ref, f_base, f1pp, n1):
    for sl, f0 in _f1_tiles(f_base, f1pp):
        vre_ref[sl] = jnp.swapaxes(s_ref[:, pl.ds(f0, _F1_GROUP), :], 0, 1)
        vim_ref[sl] = jnp.swapaxes(s_ref[:, pl.ds(n1 + f0, _F1_GROUP), :], 0, 1)


def _filt_fft_body(kern_ref, fa_ref, fb_ref, o_ref, s_ref, vre_ref, vim_ref, *, n_in, n1, f1pp):
    s = pl.program_id(1)

    @pl.when(s < n_in)
    def _():
        for i in range(_T2_PER_PHASE):
            s_ref[s * _T2_PER_PHASE + i] = jnp.dot(fa_ref[...], kern_ref[i],
                                                   preferred_element_type=_F32).astype(_BF16)

    @pl.when(s >= n_in)
    def _():
        _gather_f1(s_ref, vre_ref, vim_ref, (s - n_in) * f1pp, f1pp, n1)

        for f in range(f1pp):
            v = jnp.concatenate([vre_ref[f], vim_ref[f]], axis=0)
            o_ref[f * 2 * _N2:(f + 1) * 2 * _N2, :] = jnp.dot(
                fb_ref[f], v, preferred_element_type=_F32).astype(o_ref.dtype)


def _filt_fft(kern3, fa_full, fb):
    n2, n1, cols = kern3.shape
    ct = _FFT_CT
    n_in = n2 // _T2_PER_PHASE
    f1pp = min(_F1_PER_PHASE, n1)
    n_mid = n1 // f1pp
    body = functools.partial(_filt_fft_body, n_in=n_in, n1=n1, f1pp=f1pp)
    return pl.pallas_call(
        body,
        grid=(cols // ct, n_in + n_mid),
        in_specs=[
            pl.BlockSpec((_T2_PER_PHASE, n1, ct), lambda j, s: (jnp.minimum(s, n_in - 1), 0, j)),
            pl.BlockSpec((2 * n1, n1), lambda j, s: (0, 0)),
            pl.BlockSpec((f1pp, 2 * n2, 2 * n2), lambda j, s: (jnp.maximum(s - n_in, 0), 0, 0)),
        ],
        out_specs=pl.BlockSpec((f1pp * 2 * n2, ct), lambda j, s: (jnp.maximum(s - n_in, 0), j)),
        out_shape=jax.ShapeDtypeStruct((n1 * 2 * n2, cols), _BF16),
        scratch_shapes=[pltpu.VMEM((n2, 2 * n1, ct), _BF16),
                        pltpu.VMEM((f1pp, n2, ct), _BF16), pltpu.VMEM((f1pp, n2, ct), _BF16)],
        compiler_params=_cparams(("arbitrary", "arbitrary")),
        name="filt_fft",
    )(kern3, fa_full, fb)


def _hyena_conv_body(u_ref, g_ref, k_ref, bias_ref, fa_ref, fai_ref, fb_ref, fbi_ref,
                     o_ref, s_ref, vre_ref, vim_ref, wre_ref, wim_ref, y_ref, ukeep_ref,
                     *, n_in, n_mid, n1, f1pp, natural_out):
    s = pl.program_id(2)
    h1 = n1 // 2

    @pl.when(s < n_in)
    def _():
        for i in range(_T2_PER_PHASE):
            u = jnp.concatenate([u_ref[0, i], u_ref[1, i]], axis=0)
            ukeep_ref[s * _T2_PER_PHASE + i] = u
            s_ref[s * _T2_PER_PHASE + i] = jnp.dot(fa_ref[...], u,
                                                   preferred_element_type=_F32).astype(_BF16)

    @pl.when(jnp.logical_and(s >= n_in, s < n_in + n_mid))
    def _():
        tiles = _f1_tiles((s - n_in) * f1pp, f1pp)

        def gather(sl, f0):
            vre_ref[sl] = jnp.swapaxes(s_ref[:, pl.ds(f0, _F1_GROUP), :], 0, 1)
            vim_ref[sl] = jnp.swapaxes(s_ref[:, pl.ds(n1 + f0, _F1_GROUP), :], 0, 1)

        def forward(f):
            v = jnp.concatenate([vre_ref[f], vim_ref[f]], axis=0)
            x = jnp.dot(fb_ref[f], v, preferred_element_type=_F32)
            xre, xim = x[:_N2], x[_N2:]
            kre = k_ref[f * 2 * _N2:f * 2 * _N2 + _N2, :].astype(_F32)
            kim = k_ref[f * 2 * _N2 + _N2:(f + 1) * 2 * _N2, :].astype(_F32)
            y_ref[f] = jnp.concatenate([xre * kre - xim * kim, xre * kim + xim * kre],
                                       axis=0).astype(_BF16)

        def inverse(f):
            w = jnp.dot(fbi_ref[f], y_ref[f], preferred_element_type=_F32)
            wre_ref[f] = w[:_N2].astype(_BF16)
            wim_ref[f] = w[_N2:].astype(_BF16)

        def scatter(sl, f0):
            s_ref[:, pl.ds(f0, _F1_GROUP), :] = jnp.swapaxes(wre_ref[sl], 0, 1)
            s_ref[:, pl.ds(n1 + f0, _F1_GROUP), :] = jnp.swapaxes(wim_ref[sl], 0, 1)

        nt = len(tiles)
        for t in tiles:
            gather(*t)
        for g in range(nt + 1):
            for k in range(_F1_GROUP):
                if g < nt:
                    forward(g * _F1_GROUP + k)
                if g >= 1:
                    inverse((g - 1) * _F1_GROUP + k)
            if g >= 1:
                scatter(*tiles[g - 1])

    @pl.when(s >= n_in + n_mid)
    def _():
        bias = bias_ref[...]
        outs = [[], []]
        for i in range(_T2_PER_PHASE):
            t2 = (s - n_in - n_mid) * _T2_PER_PHASE + i
            y = jnp.dot(fai_ref[...], s_ref[t2], preferred_element_type=_F32)
            y = y + ukeep_ref[t2].astype(_F32) * bias
            for m in range(2):
                res = g_ref[m, i].astype(_F32) * y[m * h1:(m + 1) * h1]
                if natural_out:
                    outs[m].append(res.astype(o_ref.dtype))
                else:
                    o_ref[m, i] = res.astype(o_ref.dtype)
        if natural_out:
            for m in range(2):
                o_ref[m] = jnp.swapaxes(jnp.stack(outs[m], axis=0), 0, 1)


def _hyena_conv(u4, ucol, g4, gcol, kspec, kcol, bias, tabs, natural_out):
    fa, fai, fb, fbi = tabs
    b, n2, h1, _ = u4.shape
    n1 = 2 * h1
    ct = _FFT_CT
    n_in = n2 // _T2_PER_PHASE
    f1pp = min(_F1_PER_PHASE, n1)
    n_mid = n1 // f1pp
    n_out = n_in
    tiles = HYENA_WIDTH // ct
    body = functools.partial(_hyena_conv_body, n_in=n_in, n_mid=n_mid, n1=n1, f1pp=f1pp,
                             natural_out=natural_out)
    mid = lambda s: jnp.clip(s - n_in, 0, n_mid - 1)
    last = lambda s: jnp.clip(s - n_in - n_mid, 0, n_out - 1)
    blk = (2, _T2_PER_PHASE, h1, ct)
    if natural_out:
        out_spec = pl.BlockSpec((2, h1, _T2_PER_PHASE, ct), lambda j, p, s: (p, 0, last(s), j))
        out_shape = jax.ShapeDtypeStruct((b, h1, n2, HYENA_WIDTH), _BF16)
    else:
        out_spec = pl.BlockSpec(blk, lambda j, p, s: (p, last(s), 0, j))
        out_shape = jax.ShapeDtypeStruct((b, n2, h1, HYENA_WIDTH), _BF16)
    return pl.pallas_call(
        body,
        grid=(tiles, b // 2, n_in + n_mid + n_out),
        in_specs=[
            pl.BlockSpec(blk, lambda j, p, s: (p, jnp.minimum(s, n_in - 1), 0, ucol + j)),
            pl.BlockSpec(blk, lambda j, p, s: (p, last(s), 0, gcol + j)),
            pl.BlockSpec((f1pp * 2 * n2, ct), lambda j, p, s: (mid(s), kcol + j)),
            pl.BlockSpec((1, ct), lambda j, p, s: (0, j)),
            pl.BlockSpec((2 * n1, n1), lambda j, p, s: (0, 0)),
            pl.BlockSpec((n1, 2 * n1), lambda j, p, s: (0, 0)),
            pl.BlockSpec((f1pp, 2 * n2, 2 * n2), lambda j, p, s: (mid(s), 0, 0)),
            pl.BlockSpec((f1pp, 2 * n2, 2 * n2), lambda j, p, s: (mid(s), 0, 0)),
        ],
        out_specs=out_spec,
        out_shape=out_shape,
        scratch_shapes=([pltpu.VMEM((n2, 2 * n1, ct), _BF16)] + 4 * [pltpu.VMEM((f1pp, n2, ct), _BF16)]
                        + [pltpu.VMEM((f1pp, 2 * n2, ct), _BF16), pltpu.VMEM((n2, n1, ct), _BF16)]),
        compiler_params=_cparams(("arbitrary", "arbitrary", "arbitrary")),
        name="hyena_conv",
    )(u4, g4, kspec, bias, fa, fai, fb, fbi)


_GATE_LANES = 2 * MLSTM_HEADS
_TERM_R = 12
_MLSTM_SUB = 2


def _split3(x):
    hi = x.astype(_BF16).astype(_F32)
    mid = (x - hi).astype(_BF16).astype(_F32)
    lo = (x - hi - mid).astype(_BF16).astype(_F32)
    return hi, mid, lo


def _scan_order_max(x, is_fwd):
    n = x.shape[0]
    row = lax.broadcasted_iota(jnp.int32, x.shape, 0)
    pre, suf = x, x
    shift = 1
    while shift < n:
        pre = jnp.maximum(pre, jnp.where(row >= shift, pltpu.roll(pre, shift, axis=0), -jnp.inf))
        suf = jnp.maximum(suf, jnp.where(row < n - shift, pltpu.roll(suf, n - shift, axis=0), -jnp.inf))
        shift *= 2
    return jnp.where(is_fwd, pre, suf)


def _gate_prep_body(g_ref, bias_ref, a_ref, row_ref, keep_ref,
                    b_scr, a_scr, pm_scr, tot_scr, mloc_scr, mprev_scr, *, nc):
    ch = MLSTM_CHUNK
    lane = lax.broadcasted_iota(jnp.int32, (1, 128), 1)
    is_fwd = lane < MLSTM_HEADS
    live = lane < _GATE_LANES
    jj = lax.broadcasted_iota(jnp.int32, (ch, ch), 0)
    ss = lax.broadcasted_iota(jnp.int32, (ch, ch), 1)
    t_lo = (ss <= jj).astype(_F32)
    row_ref[...] = jnp.zeros_like(row_ref)

    def chunk_stats(c, carry):
        r0 = pl.multiple_of(c * ch, ch)
        gi = g_ref[pl.ds(r0, ch), :] + bias_ref[...]
        gf = pltpu.roll(gi, 128 - 2 * MLSTM_HEADS, axis=1)
        logf = jnp.minimum(gf, 0.0) - jnp.log1p(jnp.exp(-jnp.abs(gf)))
        cs_lo = jnp.dot(t_lo, logf, precision=_HIGHEST, preferred_element_type=_F32)
        tot = cs_lo[ch - 1:ch, :]
        cs_up = tot - cs_lo + logf
        bcs = jnp.where(is_fwd, cs_lo, cs_up)
        a = tot - bcs + gi
        b_scr[pl.ds(r0, ch), :] = bcs
        a_scr[pl.ds(r0, ch), :] = a
        tot_scr[pl.ds(c, 1), :] = tot
        mloc_scr[pl.ds(c, 1), :] = jnp.max(a, axis=0, keepdims=True)
        r = gi - bcs
        pm_scr[pl.ds(r0, ch), :] = _scan_order_max(r, is_fwd)
        hi, mid, lo = _split3(r.T[0:_GATE_LANES, :])
        base = _TERM_R * _GATE_LANES
        row_ref[base:base + 16, pl.ds(r0, ch)] = jnp.concatenate([hi, mid], axis=0).astype(_BF16)
        row_ref[base + 16:base + 32, pl.ds(r0, ch)] = jnp.concatenate(
            [lo, jnp.zeros_like(lo)], axis=0).astype(_BF16)
        return carry

    lax.fori_loop(0, nc, chunk_stats, 0, unroll=4)

    def scan_f(c, m):
        mprev_scr[pl.ds(c, 1), :] = jnp.where(is_fwd, m, mprev_scr[pl.ds(c, 1), :])
        return jnp.maximum(tot_scr[pl.ds(c, 1), :] + m, mloc_scr[pl.ds(c, 1), :])

    def scan_b(i, m):
        c = nc - 1 - i
        mprev_scr[pl.ds(c, 1), :] = jnp.where(is_fwd, mprev_scr[pl.ds(c, 1), :], m)
        return jnp.maximum(tot_scr[pl.ds(c, 1), :] + m, mloc_scr[pl.ds(c, 1), :])

    mprev_scr[...] = jnp.zeros_like(mprev_scr)
    lax.fori_loop(0, nc, scan_f, jnp.zeros((1, 128), _F32))
    lax.fori_loop(0, nc, scan_b, jnp.zeros((1, 128), _F32))

    def emit(c, carry):
        r0 = pl.multiple_of(c * ch, ch)
        mprev = mprev_scr[pl.ds(c, 1), :]
        tot = tot_scr[pl.ds(c, 1), :]
        mnew = jnp.maximum(tot + mprev, mloc_scr[pl.ds(c, 1), :])
        keep_ref[pl.ds(c, 1), :] = jnp.exp(tot + mprev - mnew)
        g = jnp.maximum(mprev, pm_scr[pl.ds(r0, ch), :])
        terms = (_split3(-g) + _split3(mprev - g) + _split3(-b_scr[pl.ds(r0, ch), :] - g)
                 + _split3(a_scr[pl.ds(r0, ch), :] - mnew) + 3 * (jnp.ones((ch, 128), _F32),))
        pack = jnp.where(live, terms[0], 0.0)
        for t in range(1, len(terms)):
            pack = pack + pltpu.roll(jnp.where(live, terms[t], 0.0), _GATE_LANES * t, axis=1)
        a_ref[pl.ds(r0, ch), :] = pack.astype(_BF16)
        return carry

    lax.fori_loop(0, nc, emit, 0, unroll=4)


def _gate_prep(g3, bias):
    b, l, _ = g3.shape
    nc = l // MLSTM_CHUNK
    body = functools.partial(_gate_prep_body, nc=nc)
    return pl.pallas_call(
        body,
        grid=(b,),
        in_specs=[
            pl.BlockSpec((None, l, 128), lambda i: (i, 0, 0)),
            pl.BlockSpec((1, 128), lambda i: (0, 0)),
        ],
        out_specs=[
            pl.BlockSpec((None, l, 128), lambda i: (i, 0, 0)),
            pl.BlockSpec((None, 128, l), lambda i: (i, 0, 0)),
            pl.BlockSpec((None, nc, 128), lambda i: (i, 0, 0)),
        ],
        out_shape=[
            jax.ShapeDtypeStruct((b, l, 128), _BF16),
            jax.ShapeDtypeStruct((b, 128, l), _BF16),
            jax.ShapeDtypeStruct((b, nc, 128), _F32),
        ],
        scratch_shapes=[
            pltpu.VMEM((l, 128), _F32), pltpu.VMEM((l, 128), _F32), pltpu.VMEM((l, 128), _F32),
            pltpu.VMEM((nc, 128), _F32), pltpu.VMEM((nc, 128), _F32), pltpu.VMEM((nc, 128), _F32),
        ],
        compiler_params=_cparams(("arbitrary",)),
        name="gate_prep",
    )(g3, bias)


def _gate_spread_matrix():
    bc = np.zeros((128, 4 * 128), np.float32)
    for blk in range(4):
        for t in range(3 * blk, 3 * blk + 3):
            bc[t * _GATE_LANES:(t + 1) * _GATE_LANES, blk * 128:(blk + 1) * 128] = 1.0
    return jnp.asarray(bc).astype(_BF16)


def _mlstm_body(keep_ref, bc_ref, qf_ref, kf_ref, vf_ref, af_ref, rf_ref, qb_ref, kb_ref, vb_ref, ab_ref,
                rb_ref, hf_ref, hb_ref, ct_ref, nm_ref, *, nc):
    bi, c = pl.program_id(0), pl.program_id(1)

    @pl.when(c == 0)
    def _():
        ct_ref[...] = jnp.zeros_like(ct_ref)
        nm_ref[...] = jnp.zeros_like(nm_ref)

    ch, dh = MLSTM_CHUNK, MLSTM_HEAD_DIM
    jj = lax.broadcasted_iota(jnp.int32, (ch, ch), 0)
    ss = lax.broadcasted_iota(jnp.int32, (ch, ch), 1)
    head_lane = ss % _GATE_LANES
    ones_rhs = jnp.ones((ch, 128), _BF16)
    twice = lambda a: jnp.concatenate([a, a], axis=1)
    dirs = ((qf_ref, kf_ref, vf_ref, af_ref, rf_ref, hf_ref, ss <= jj),
            (qb_ref, kb_ref, vb_ref, ab_ref, rb_ref, hb_ref, ss >= jj))
    for sub, d in [(sub, d) for sub in range(_MLSTM_SUB) for d in range(2)]:
        q_blk, k_blk, v_blk, a_blk, r_blk, o_blk, mask = dirs[d]
        pos = sub if d == 0 else _MLSTM_SUB - 1 - sub
        chunk = c * _MLSTM_SUB + pos if d == 0 else nc - (c + 1) * _MLSTM_SUB + pos
        rows = slice(pos * ch, (pos + 1) * ch)
        q_ref, k_ref, v_ref, o_ref = q_blk.at[rows], k_blk.at[rows], v_blk.at[rows], o_blk.at[rows]
        bmat = jnp.concatenate([bc_ref[:, 0:128] + r_blk[:, rows], bc_ref[:, 128:]], axis=1)
        a_all = a_blk[rows, :].astype(_F32)
        kbase = (bi * nc + chunk) * _GATE_LANES
        for h in range(MLSTM_HEADS):
            hd = d * MLSTM_HEADS + h
            a_h = jnp.where(head_lane == hd, a_all, 0.0).astype(_BF16)
            e = jnp.dot(a_h, bmat, preferred_element_type=_F32)
            decay = jnp.exp(jnp.where(mask, e[:, 0:128], -jnp.inf))
            rest = jnp.exp(e[:, 128:])
            iw, clamp, wrep = rest[:, 0:128], rest[:, 128:256], rest[:, 256:384]
            keep = keep_ref[kbase + hd]
            lo, hi = h * dh, (h + 1) * dh
            qh, kh, vh = q_ref[:, lo:hi], k_ref[:, lo:hi], v_ref[:, lo:hi]
            p = (_dot_nt(qh, kh) * decay).astype(_BF16)
            ct, nm = ct_ref[hd], nm_ref[hd]
            pv = jnp.dot(p, jnp.concatenate([vh, ones_rhs], axis=1), preferred_element_type=_F32)
            qc = jnp.dot(qh, jnp.concatenate([ct, nm], axis=1).astype(_BF16), preferred_element_type=_F32)
            num = pv[:, :dh] + twice(iw) * qc[:, :dh]
            den = pv[:, dh:] + iw * qc[:, dh:]
            inv = 1.0 / jnp.maximum(jnp.abs(den), clamp)
            o_ref[:, lo:hi] = (num * twice(inv)).astype(o_ref.dtype)
            vw = jnp.concatenate([(vh.astype(_F32) * twice(wrep)).astype(_BF16), wrep.astype(_BF16)], axis=1)
            upd = lax.dot_general(kh, vw, (((0,), (0,)), ((), ())), preferred_element_type=_F32)
            ct_ref[hd] = keep * ct + upd[:, :dh]
            nm_ref[hd] = keep * nm + upd[:, dh:]


def _mlstm(qk, proj3, a_mat, r_rows, keep):
    b, l, _ = qk.shape
    mw = MLSTM_WIDTH
    nc = l // MLSTM_CHUNK
    rows = _MLSTM_SUB * MLSTM_CHUNK
    nb = l // rows
    fw = lambda c: c
    bw = lambda c: nb - 1 - c

    def specs(cm):
        return [
            pl.BlockSpec((None, rows, mw), lambda i, c: (i, cm(c), 0)),
            pl.BlockSpec((None, rows, mw), lambda i, c: (i, cm(c), 1)),
            pl.BlockSpec((None, rows, mw), lambda i, c: (i, cm(c), _COL_V)),
            pl.BlockSpec((None, rows, 128), lambda i, c: (i, cm(c), 0)),
            pl.BlockSpec((None, 128, rows), lambda i, c: (i, 0, cm(c))),
        ]

    body = functools.partial(_mlstm_body, nc=nc)
    keep_flat = keep[:, :, :_GATE_LANES].reshape(b * nc * _GATE_LANES)
    return pl.pallas_call(
        body,
        grid=(b, nb),
        in_specs=[pl.BlockSpec(memory_space=pltpu.SMEM),
                  pl.BlockSpec((128, 4 * 128), lambda i, c: (0, 0))] + specs(fw) + specs(bw),
        out_specs=[
            pl.BlockSpec((None, rows, mw), lambda i, c: (i, c, 0)),
            pl.BlockSpec((None, rows, mw), lambda i, c: (i, nb - 1 - c, 0)),
        ],
        out_shape=[jax.ShapeDtypeStruct((b, l, mw), _BF16), jax.ShapeDtypeStruct((b, l, mw), _BF16)],
        scratch_shapes=[
            pltpu.VMEM((2 * MLSTM_HEADS, MLSTM_HEAD_DIM, MLSTM_HEAD_DIM), _F32),
            pltpu.VMEM((2 * MLSTM_HEADS, MLSTM_HEAD_DIM, 128), _F32),
        ],
        compiler_params=_cparams(("arbitrary", "arbitrary")),
        name="mlstm",
    )(keep_flat, _gate_spread_matrix(), qk, qk, proj3, a_mat, r_rows, qk, qk, proj3, a_mat, r_rows)


def _sigmoid(x):
    return 1.0 / (1.0 + jnp.exp(-x))


def _merge_body(x_ref, ya_ref, hf_ref, hb_ref, o_ref, ma_ref, mb_ref, wa_ref, wb_ref, wo_ref, out_ref):
    hsum = hf_ref[...].astype(_F32) + hb_ref[...].astype(_F32)
    yb = (_sigmoid(o_ref[...].astype(_F32)) * hsum).astype(_BF16)
    pa = jnp.dot(ya_ref[...], wa_ref[...], preferred_element_type=_F32)
    pb = jnp.dot(yb, wb_ref[...], preferred_element_type=_F32)
    mixed = _sigmoid(ma_ref[...].astype(_F32)) * pa + _sigmoid(mb_ref[...].astype(_F32)) * pb
    out_ref[...] = x_ref[...] + jnp.dot(mixed.astype(_BF16), wo_ref[...], preferred_element_type=_F32)


def _merge(x2, ya2, hf2, hb2, proj, wa, wb, wo):
    m = x2.shape[0]
    tm = 256
    d, hw, mw = D_MODEL, HYENA_WIDTH, MLSTM_WIDTH
    const = lambda shape: pl.BlockSpec(shape, lambda i: (0, 0), pipeline_mode=pl.Buffered(1))
    return pl.pallas_call(
        _merge_body,
        grid=(m // tm,),
        in_specs=[
            pl.BlockSpec((tm, d), lambda i: (i, 0)),
            pl.BlockSpec((tm, hw), lambda i: (i, 0)),
            pl.BlockSpec((tm, mw), lambda i: (i, 0)),
            pl.BlockSpec((tm, mw), lambda i: (i, 0)),
            pl.BlockSpec((tm, mw), lambda i: (i, _COL_O)),
            pl.BlockSpec((tm, d), lambda i: (i, _COL_MA // 2)),
            pl.BlockSpec((tm, d), lambda i: (i, _COL_MB // 2)),
            const((hw, d)), const((mw, d)), const((d, d)),
        ],
        out_specs=pl.BlockSpec((tm, d), lambda i: (i, 0)),
        out_shape=jax.ShapeDtypeStruct((m, d), _F32),
        compiler_params=_cparams(("arbitrary",)),
        name="merge",
    )(x2, ya2, hf2, hb2, proj, proj, proj, wa, wb, wo)


def _ffn_body(x_ref, n2_ref, wg_ref, wu_ref, wd_ref, nf_ref, o_ref, hn_ref):
    f = pl.program_id(1)

    @pl.when(f == 0)
    def _():
        x = x_ref[...]
        hn = x * lax.rsqrt(jnp.mean(x * x, axis=-1, keepdims=True) + RMS_EPS) * n2_ref[...]
        hn_ref[...] = hn.astype(_BF16)
        o_ref[...] = x

    hn = hn_ref[...]
    g = jnp.dot(hn, wg_ref[...], preferred_element_type=_F32)
    u = jnp.dot(hn, wu_ref[...], preferred_element_type=_F32)
    a = (g * _sigmoid(g) * u).astype(_BF16)
    o_ref[...] += jnp.dot(a, wd_ref[...], preferred_element_type=_F32)

    @pl.when(f == pl.num_programs(1) - 1)
    def _():
        y = o_ref[...]
        o_ref[...] = y * lax.rsqrt(jnp.mean(y * y, axis=-1, keepdims=True) + RMS_EPS) * nf_ref[...]


def _ffn(x2, norm2_w, w_gate_up, w_down, norm_f_w):
    m = x2.shape[0]
    tm, tf = 1024, 512
    d = D_MODEL
    nf = FFN_HIDDEN // tf
    return pl.pallas_call(
        _ffn_body,
        grid=(m // tm, nf),
        in_specs=[
            pl.BlockSpec((tm, d), lambda i, f: (i, 0)),
            pl.BlockSpec((1, d), lambda i, f: (0, 0)),
            pl.BlockSpec((d, tf), lambda i, f: (0, f)),
            pl.BlockSpec((d, tf), lambda i, f: (0, nf + f)),
            pl.BlockSpec((tf, d), lambda i, f: (f, 0)),
            pl.BlockSpec((1, d), lambda i, f: (0, 0)),
        ],
        out_specs=pl.BlockSpec((tm, d), lambda i, f: (i, 0)),
        out_shape=jax.ShapeDtypeStruct((m, d), _F32),
        scratch_shapes=[pltpu.VMEM((tm, d), _BF16)],
        compiler_params=_cparams(("arbitrary", "arbitrary")),
        name="ffn",
    )(x2, norm2_w, w_gate_up, w_gate_up, w_down, norm_f_w)


def kernel(x, norm1_w, w_in, conv_w, conv_b, filt_w1, filt_b1, filt_freq1, filt_w2, filt_b2, filt_freq2,
           filt_w3, hyena_bias, mlstm_gate_bias, w_branch_a, w_branch_b, w_out, norm2_w, w_gate_up, w_down,
           norm_f_w):
    b, l, d = x.shape
    assert d == D_MODEL and b % 2 == 0 and l % (_N2 * _T2_PER_PHASE) == 0
    assert norm1_w.shape[0] == 1, "single-layer block"
    hw, mw, nh = HYENA_WIDTH, MLSTM_WIDTH, MLSTM_HEADS
    m = b * l
    n = 2 * l
    n1 = n // _N2
    sc_cols = 3 * hw + 2 * mw
    g0 = sc_cols + 2 * mw

    w_t = jnp.swapaxes(w_in[0], 0, 1)
    w_all_t = w_t.astype(_BF16)
    w_merge_t = w_all_t[g0 + 4 * nh:]
    wg_t = w_t[g0:g0 + 4 * nh]
    gate_order = lambda a: jnp.concatenate(
        [a[0:nh], a[2 * nh:3 * nh], a[nh:2 * nh], a[3 * nh:4 * nh]], axis=0)
    wg_hi_t, wg_lo_t = _split_bf16(jnp.pad(gate_order(wg_t), ((0, 128 - 4 * nh), (0, 0))))
    gate_bias = jnp.pad(gate_order(mlstm_gate_bias[0].astype(_F32).reshape(4 * nh))[None, :],
                        ((0, 0), (0, 128 - 4 * nh)))

    x2 = x.reshape(m, d)
    proj, gates = _inproj(x2, norm1_w[0][None, :], w_merge_t, w_all_t, wg_hi_t, wg_lo_t)
    proj3 = proj.reshape(b, l, _MAIN_COLS)

    cw, cb = conv_w[0], conv_b[0][None, :]
    hy4 = _shortconv(proj3, cw, cb, _COL_HV * 1024, 0, 3 * hw, True)
    qk = _shortconv(proj3, cw, cb, _COL_Q * 1024, 3 * hw, 2 * mw, False)

    fa, fa_full, fai, fb, fbi = _dft_tables(n)
    zpos = _filter_positions(l)
    w1p = jnp.pad(filt_w1[0].astype(_F32), ((0, 128 - HYENA_EMB), (0, 0)))
    max_decay = math.log(HYENA_TARGET) / HYENA_FAST_DECAY
    min_decay = math.log(HYENA_TARGET) / HYENA_SLOW_DECAY
    dabs = jnp.asarray(np.abs(np.linspace(min_decay, max_decay, hw, dtype=np.float32))[None, :])
    dup = lambda a: jnp.concatenate([a, a], axis=-1)
    fh = HYENA_FILTER_HIDDEN
    w3 = filt_w3[0].astype(_F32).reshape(fh, 2, 2, hw).transpose(2, 0, 1, 3).reshape(2 * fh, 2 * hw)
    w3_hi, w3_lo = _split_bf16(w3)
    blockdiag = lambda a: jnp.concatenate(
        [jnp.concatenate([a, jnp.zeros_like(a)], axis=1), jnp.concatenate([jnp.zeros_like(a), a], axis=1)], axis=0)
    kern = _filt_mlp(zpos, blockdiag(w1p), dup(filt_b1[0][None, :]), dup(filt_freq1[0][None, :]),
                     blockdiag(filt_w2[0].astype(_F32)), dup(filt_b2[0][None, :]),
                     dup(filt_freq2[0][None, :]), w3_hi, w3_lo, dabs)
    kspec = _filt_fft(kern.reshape(_N2, n1, 2 * hw), fa_full, fb)
    tabs = (fa, fai, fb, fbi)
    tiles = hw // _FFT_CT
    hbias = hyena_bias[0].astype(_F32)
    z4 = _hyena_conv(hy4, 0, hy4, tiles, kspec, 0, hbias[0][None, :], tabs, False)
    ya4 = _hyena_conv(z4, 0, hy4, 2 * tiles, kspec, tiles, hbias[1][None, :], tabs, True)
    ya2 = ya4.reshape(m, hw)

    a_mat, r_rows, keep = _gate_prep(gates.reshape(b, l, 128), gate_bias)
    hf, hb = _mlstm(qk, proj3, a_mat, r_rows, keep)

    x_mid = _merge(x2, ya2, hf.reshape(m, mw), hb.reshape(m, mw), proj,
                   w_branch_a[0].astype(_BF16), w_branch_b[0].astype(_BF16), w_out[0].astype(_BF16))
    out = _ffn(x_mid, norm2_w[0][None, :], w_gate_up[0].astype(_BF16), w_down[0].astype(_BF16),
               norm_f_w[None, :])
    return out.reshape(b, l, d)
```

```python
import functools
import math

import numpy as np
import jax
import jax.numpy as jnp
from jax import lax
from jax.experimental import pallas as pl
from jax.experimental.pallas import tpu as pltpu

_F32 = jnp.float32
_BF16 = jnp.bfloat16
_HIGHEST = lax.Precision.HIGHEST

D_MODEL = 2048
HYENA_WIDTH = 1024
HYENA_EMB = 33
HYENA_FILTER_HIDDEN = 64
HYENA_FAST_DECAY = 0.3
HYENA_SLOW_DECAY = 1.5
HYENA_TARGET = 1e-2
MLSTM_WIDTH = 1024
MLSTM_HEADS = 4
MLSTM_HEAD_DIM = 256
MLSTM_CHUNK = 128
FFN_HIDDEN = 5632
RMS_EPS = 1e-6

_COL_MA, _COL_MB, _COL_HV, _COL_Q, _COL_V, _COL_O = 0, 2, 4, 7, 9, 10
_MAIN_COLS = 11 * 1024

_N2 = 32
_T2_PER_PHASE = 16
_F1_PER_PHASE = 64
_FFT_CT = 256

_VMEM_LIMIT = 56 * 1024 * 1024


def _cparams(sem, vmem=_VMEM_LIMIT):
    return pltpu.CompilerParams(dimension_semantics=sem, vmem_limit_bytes=vmem)


def _split_bf16(a):
    hi = a.astype(_BF16)
    lo = (a - hi.astype(_F32)).astype(_BF16)
    return hi, lo


def _dot_nt(a, b):
    return lax.dot_general(a, b, (((1,), (1,)), ((), ())), preferred_element_type=_F32)


def _dot3(a_hi, a_lo, b_hi, b_lo):
    d = functools.partial(jnp.dot, preferred_element_type=_F32)
    return d(a_hi, b_hi) + (d(a_hi, b_lo) + d(a_lo, b_hi))


def _inproj_body(x_ref, nw_ref, wm_ref, w_ref, wg_ref, o_ref, g_ref, hn_ref, *, n_merge):
    j = pl.program_id(1)

    @pl.when(j == 0)
    def _():
        rows = 256
        for r in range(0, x_ref.shape[0], rows):
            x = x_ref[r:r + rows, :]
            hn = x * lax.rsqrt(jnp.mean(x * x, axis=-1, keepdims=True) + RMS_EPS) * nw_ref[...]
            hn_hi, hn_lo = _split_bf16(hn)
            hn_ref[r:r + rows, :] = hn_hi
            hh_hl = _dot_nt(hn_hi, wg_ref[...])
            g_ref[r:r + rows, :] = hh_hl[:, :128] + (hh_hl[:, 128:] + _dot_nt(hn_lo, wg_ref[0:128, :]))
            o_ref[r:r + rows, :] = _dot_nt(hn_hi, wm_ref[...]).astype(o_ref.dtype)

    @pl.when(jnp.logical_and(j > 0, j < n_merge))
    def _():
        o_ref[...] = _dot_nt(hn_ref[...], wm_ref[...]).astype(o_ref.dtype)

    @pl.when(j >= n_merge)
    def _():
        o_ref[...] = _dot_nt(hn_ref[...], w_ref[...]).astype(o_ref.dtype)


def _inproj(x2, norm_w, w_merge_t, w_all_t, wg_t):
    m = x2.shape[0]
    tm, tn = 1024, 1024
    n_merge = w_merge_t.shape[0] // tn
    body = functools.partial(_inproj_body, n_merge=n_merge)
    return pl.pallas_call(
        body,
        grid=(m // tm, _MAIN_COLS // tn),
        in_specs=[
            pl.BlockSpec((tm, D_MODEL), lambda i, j: (i, 0)),
            pl.BlockSpec((1, D_MODEL), lambda i, j: (0, 0)),
            pl.BlockSpec((tn, D_MODEL), lambda i, j: (jnp.minimum(j, n_merge - 1), 0)),
            pl.BlockSpec((tn, D_MODEL), lambda i, j: (jnp.maximum(j - n_merge, 0), 0)),
            pl.BlockSpec((256, D_MODEL), lambda i, j: (0, 0)),
        ],
        out_specs=[
            pl.BlockSpec((tm, tn), lambda i, j: (i, j)),
            pl.BlockSpec((tm, 128), lambda i, j: (i, 0)),
        ],
        out_shape=[
            jax.ShapeDtypeStruct((m, _MAIN_COLS), _BF16),
            jax.ShapeDtypeStruct((m, 128), _F32),
        ],
        scratch_shapes=[pltpu.VMEM((tm, D_MODEL), _BF16)],
        compiler_params=_cparams(("arbitrary", "arbitrary")),
        name="inproj",
    )(x2, norm_w, w_merge_t, w_all_t, wg_t)


def _shortconv_silu_body(u_ref, w_ref, b_ref, o_ref, *, k_scale):
    u = u_ref[...].astype(_F32)
    n = u.shape[0]
    row = lax.broadcasted_iota(jnp.int32, u.shape, 0)
    up = jnp.where(row == 0, 0.0, pltpu.roll(u, 1, axis=0))
    un = jnp.where(row == n - 1, 0.0, pltpu.roll(u, n - 1, axis=0))
    w = w_ref[...]
    y = up * w[0:1, :] + b_ref[...] + u * w[1:2, :] + un * w[2:3, :]
    y = y * (1.0 / (1.0 + jnp.exp(-y)))
    is_k = pl.program_id(1) >= (MLSTM_WIDTH // u.shape[1])
    o_ref[...] = (y * jnp.where(is_k, k_scale, 1.0)).astype(o_ref.dtype)


def _shortconv_t2_body(u_ref, w_ref, b_ref, o_ref, ut_ref):
    n, c = u_ref.shape
    h1 = n // _N2
    g = 16
    for k in range(h1 // g):
        blk = u_ref[k * g * _N2:(k + 1) * g * _N2, :].reshape(g, _N2, c)
        ut_ref[:, k * g:(k + 1) * g, :] = jnp.swapaxes(blk, 0, 1)
    w = w_ref[...]
    w0, w1, w2, bias = w[0:1, :], w[1:2, :], w[2:3, :], b_ref[...]
    slab = lambda t2: ut_ref[t2].astype(_F32)
    row = lax.broadcasted_iota(jnp.int32, (h1, c), 0)
    before_first = jnp.where(row == 0, 0.0, pltpu.roll(slab(_N2 - 1), 1, axis=0))
    after_last = jnp.where(row == h1 - 1, 0.0, pltpu.roll(slab(0), h1 - 1, axis=0))
    for t2 in range(_N2):
        up = before_first if t2 == 0 else slab(t2 - 1)
        un = after_last if t2 == _N2 - 1 else slab(t2 + 1)
        o_ref[t2] = (up * w0 + bias + slab(t2) * w1 + un * w2).astype(o_ref.dtype)


def _shortconv(proj3, conv_w, conv_b, proj_col0, conv_col0, ncols, t2_major):
    b, l, _ = proj3.shape
    ct = 256
    p0, c0 = proj_col0 // ct, conv_col0 // ct
    if t2_major:
        body = _shortconv_t2_body
        out_spec = pl.BlockSpec((None, _N2, l // _N2, ct), lambda i, j: (i, 0, 0, j))
        out_shape = jax.ShapeDtypeStruct((b, _N2, l // _N2, ncols), _BF16)
        scratch = [pltpu.VMEM((_N2, l // _N2, ct), _BF16)]
    else:
        body = functools.partial(_shortconv_silu_body, k_scale=MLSTM_HEAD_DIM ** -0.5)
        out_spec = pl.BlockSpec((None, l, ct), lambda i, j: (i, 0, j))
        out_shape = jax.ShapeDtypeStruct((b, l, ncols), _BF16)
        scratch = []
    return pl.pallas_call(
        body,
        grid=(b, ncols // ct),
        in_specs=[
            pl.BlockSpec((None, l, ct), lambda i, j: (i, 0, p0 + j)),
            pl.BlockSpec((3, ct), lambda i, j: (0, c0 + j)),
            pl.BlockSpec((1, ct), lambda i, j: (0, c0 + j)),
        ],
        out_specs=out_spec,
        out_shape=out_shape,
        scratch_shapes=scratch,
        compiler_params=_cparams(("arbitrary", "arbitrary")),
        name="shortconv" if t2_major else "shortconv_silu",
    )(proj3, conv_w, conv_b)


def _dft_tables(n):
    n2 = _N2
    n1 = n // n2
    h1 = n1 // 2
    f1 = np.arange(n1)[:, None]
    t1 = np.arange(n1)[None, :]
    ang = 2.0 * np.pi * ((f1 * t1) % n1) / n1
    c, s = np.cos(ang), np.sin(ang)
    ch, sh = c[:, :h1], s[:, :h1]
    fa = np.block([[ch, sh], [-sh, ch]])
    fa_full = np.concatenate([c, -s], axis=0)
    fai = np.block([[ch.T, -sh.T], [sh.T, ch.T]]) / n
    f1v = np.arange(n1)[:, None, None]
    f2 = np.arange(n2)[None, :, None]
    t2 = np.arange(n2)[None, None, :]
    ph = 2.0 * np.pi * ((f1v * t2 + f2 * t2 * n1) % n) / n
    tc, ts = np.cos(ph), np.sin(ph)
    fb = np.concatenate([np.concatenate([tc, ts], axis=2),
                         np.concatenate([-ts, tc], axis=2)], axis=1)
    tct, tst = np.swapaxes(tc, 1, 2), np.swapaxes(ts, 1, 2)
    fbi = np.concatenate([np.concatenate([tct, -tst], axis=2),
                          np.concatenate([tst, tct], axis=2)], axis=1)
    to = lambda a: jnp.asarray(a.astype(np.float32)).astype(_BF16)
    return to(fa), to(fa_full), to(fai), to(fb), to(fbi)


def _filter_positions(l):
    n = 2 * l
    n1 = n // _N2
    r = (np.arange(_N2)[:, None] + _N2 * np.arange(n1)[None, :]).reshape(-1)
    pos = np.where(r < l, r, n - r)
    pos = np.where(r == l, 0, pos)
    bands = (HYENA_EMB - 1) // 2
    tt = np.linspace(0.0, 1.0, l, dtype=np.float32).astype(np.float64)[pos]
    omega = (2.0 * math.pi * np.arange(l, dtype=np.float32) / l).astype(np.float32)
    freqs = np.linspace(1e-4, bands - 1, bands, dtype=np.float32)
    ang = (omega[:, None] * freqs[None, :]).astype(np.float64)[pos]
    z = np.zeros((n, 128), np.float64)
    z[:, 0] = tt
    z[:, 1:1 + bands] = np.cos(ang)
    z[:, 1 + bands:1 + 2 * bands] = -np.sin(ang)
    z[:, 33] = (r != l)
    z[:, 34] = (r < l)
    return jnp.asarray(z.astype(np.float32))


def _filt_mlp_body(z_ref, zp_ref, w1_ref, b1_ref, q1_ref, w2_ref, b2_ref, q2_ref, w3h_ref, w3l_ref, dabs_ref,
                   o_ref, hh_ref, hl_ref):
    fh = HYENA_FILTER_HIDDEN

    @pl.when(pl.program_id(1) == 0)
    def _():
        hp = jnp.sin(q1_ref[...] * (jnp.dot(zp_ref[...], w1_ref[...], precision=_HIGHEST,
                                            preferred_element_type=_F32) + b1_ref[...]))
        hp = jnp.sin(q2_ref[...] * (jnp.dot(hp, w2_ref[...], precision=_HIGHEST,
                                            preferred_element_type=_F32) + b2_ref[...]))
        lane_p = lax.broadcasted_iota(jnp.int32, hp.shape, 1)
        swapped = pltpu.roll(hp, fh, axis=1)
        h = jnp.concatenate([jnp.where(lane_p < fh, hp, swapped), jnp.where(lane_p < fh, swapped, hp)], axis=0)
        z = z_ref[...]
        valid = z[:, 33:34]
        fwd = z[:, 34:35]
        lane = lax.broadcasted_iota(jnp.int32, h.shape, 1)
        hi, lo = _split_bf16(h * (valid * jnp.where(lane < fh, fwd, 1.0 - fwd)))
        hh_ref[...] = hi
        hl_ref[...] = lo

    raw = _dot3(hh_ref[...], hl_ref[...], w3h_ref[...], w3l_ref[...])
    window = jnp.exp(-z_ref[:, 0:1] * dabs_ref[...])
    o_ref[...] = (raw * window).astype(o_ref.dtype)


_FILT_ROWS = 1024


def _filt_mlp(zpos, w1bd, b1d, q1d, w2bd, b2d, q2d, w3h, w3l, dabs):
    n = zpos.shape[0]
    ct = _FFT_CT
    tr = _FILT_ROWS
    per_order = HYENA_WIDTH // ct
    fh = HYENA_FILTER_HIDDEN
    zpacked = zpos.reshape(n // tr, 2, tr // 2, 128).transpose(0, 2, 1, 3).reshape(n // 2, 256)
    full = lambda shape: pl.BlockSpec(shape, lambda r, i: (0,) * len(shape))
    return pl.pallas_call(
        _filt_mlp_body,
        grid=(n // tr, 2 * per_order),
        in_specs=[
            pl.BlockSpec((tr, 128), lambda r, i: (r, 0)),
            pl.BlockSpec((tr // 2, 256), lambda r, i: (r, 0)),
            full((256, 2 * fh)), full((1, 2 * fh)), full((1, 2 * fh)),
            full((2 * fh, 2 * fh)), full((1, 2 * fh)), full((1, 2 * fh)),
            pl.BlockSpec((2 * fh, ct), lambda r, i: (0, i)),
            pl.BlockSpec((2 * fh, ct), lambda r, i: (0, i)),
            pl.BlockSpec((1, ct), lambda r, i: (0, i % per_order)),
        ],
        out_specs=pl.BlockSpec((tr, ct), lambda r, i: (r, i)),
        out_shape=jax.ShapeDtypeStruct((n, 2 * HYENA_WIDTH), _BF16),
        scratch_shapes=[pltpu.VMEM((tr, 2 * fh), _BF16), pltpu.VMEM((tr, 2 * fh), _BF16)],
        compiler_params=_cparams(("arbitrary", "arbitrary")),
        name="filt_mlp",
    )(zpos, zpacked, w1bd, b1d, q1d, w2bd, b2d, q2d, w3h, w3l, dabs)


_F1_GROUP = 16


def _f1_tiles(f_base, f1pp):
    return [(slice(g * _F1_GROUP, (g + 1) * _F1_GROUP), pl.multiple_of(f_base + g * _F1_GROUP, _F1_GROUP))
            for g in range(f1pp // _F1_GROUP)]


def _gather_f1(s_ref, vre_ref, vim_ref, f_base, f1pp, n1):
    for sl, f0 in _f1_tiles(f_base, f1pp):
        vre_ref[sl] = jnp.swapaxes(s_ref[:, pl.ds(f0, _F1_GROUP), :], 0, 1)
        vim_ref[sl] = jnp.swapaxes(s_ref[:, pl.ds(n1 + f0, _F1_GROUP), :], 0, 1)


def _filt_fft_body(kern_ref, fa_ref, fb_ref, o_ref, s_ref, vre_ref, vim_ref, *, n_in, n1, f1pp):
    s = pl.program_id(1)

    @pl.when(s < n_in)
    def _():
        for i in range(_T2_PER_PHASE):
            s_ref[s * _T2_PER_PHASE + i] = jnp.dot(fa_ref[...], kern_ref[i],
                                                   preferred_element_type=_F32).astype(_BF16)

    @pl.when(s >= n_in)
    def _():
        _gather_f1(s_ref, vre_ref, vim_ref, (s - n_in) * f1pp, f1pp, n1)

        for f in range(f1pp):
            v = jnp.concatenate([vre_ref[f], vim_ref[f]], axis=0)
            o_ref[f * 2 * _N2:(f + 1) * 2 * _N2, :] = jnp.dot(
                fb_ref[f], v, preferred_element_type=_F32).astype(o_ref.dtype)


def _filt_fft(kern3, fa_full, fb):
    n2, n1, cols = kern3.shape
    ct = _FFT_CT
    n_in = n2 // _T2_PER_PHASE
    f1pp = min(_F1_PER_PHASE, n1)
    n_mid = n1 // f1pp
    body = functools.partial(_filt_fft_body, n_in=n_in, n1=n1, f1pp=f1pp)
    return pl.pallas_call(
        body,
        grid=(cols // ct, n_in + n_mid),
        in_specs=[
            pl.BlockSpec((_T2_PER_PHASE, n1, ct), lambda j, s: (jnp.minimum(s, n_in - 1), 0, j)),
            pl.BlockSpec((2 * n1, n1), lambda j, s: (0, 0)),
            pl.BlockSpec((f1pp, 2 * n2, 2 * n2), lambda j, s: (jnp.maximum(s - n_in, 0), 0, 0)),
        ],
        out_specs=pl.BlockSpec((f1pp * 2 * n2, ct), lambda j, s: (jnp.maximum(s - n_in, 0), j)),
        out_shape=jax.ShapeDtypeStruct((n1 * 2 * n2, cols), _BF16),
        scratch_shapes=[pltpu.VMEM((n2, 2 * n1, ct), _BF16),
                        pltpu.VMEM((f1pp, n2, ct), _BF16), pltpu.VMEM((f1pp, n2, ct), _BF16)],
        compiler_params=_cparams(("arbitrary", "arbitrary")),
        name="filt_fft",
    )(kern3, fa_full, fb)


def _hyena_conv_body(u_ref, g_ref, k_ref, bias_ref, fa_ref, fai_ref, fb_ref, fbi_ref,
                     o_ref, s_ref, vre_ref, vim_ref, wre_ref, wim_ref, y_ref, ukeep_ref,
                     *, n_in, n_mid, n1, f1pp, natural_out):
    s = pl.program_id(2)
    h1 = n1 // 2

    @pl.when(s < n_in)
    def _():
        for i in range(_T2_PER_PHASE):
            u = jnp.concatenate([u_ref[0, i], u_ref[1, i]], axis=0)
            ukeep_ref[s * _T2_PER_PHASE + i] = u
            s_ref[s * _T2_PER_PHASE + i] = jnp.dot(fa_ref[...], u,
                                                   preferred_element_type=_F32).astype(_BF16)

    @pl.when(jnp.logical_and(s >= n_in, s < n_in + n_mid))
    def _():
        tiles = _f1_tiles((s - n_in) * f1pp, f1pp)

        def gather(sl, f0):
            vre_ref[sl] = jnp.swapaxes(s_ref[:, pl.ds(f0, _F1_GROUP), :], 0, 1)
            vim_ref[sl] = jnp.swapaxes(s_ref[:, pl.ds(n1 + f0, _F1_GROUP), :], 0, 1)

        def forward(f):
            v = jnp.concatenate([vre_ref[f], vim_ref[f]], axis=0)
            x = jnp.dot(fb_ref[f], v, preferred_element_type=_F32)
            xre, xim = x[:_N2], x[_N2:]
            kre = k_ref[f * 2 * _N2:f * 2 * _N2 + _N2, :].astype(_F32)
            kim = k_ref[f * 2 * _N2 + _N2:(f + 1) * 2 * _N2, :].astype(_F32)
            y_ref[f] = jnp.concatenate([xre * kre - xim * kim, xre * kim + xim * kre],
                                       axis=0).astype(_BF16)

        def inverse(f):
            w = jnp.dot(fbi_ref[f], y_ref[f], preferred_element_type=_F32)
            wre_ref[f] = w[:_N2].astype(_BF16)
            wim_ref[f] = w[_N2:].astype(_BF16)

        def scatter(sl, f0):
            s_ref[:, pl.ds(f0, _F1_GROUP), :] = jnp.swapaxes(wre_ref[sl], 0, 1)
            s_ref[:, pl.ds(n1 + f0, _F1_GROUP), :] = jnp.swapaxes(wim_ref[sl], 0, 1)

        nt = len(tiles)
        for t in tiles:
            gather(*t)
        for g in range(nt + 1):
            for k in range(_F1_GROUP):
                if g < nt:
                    forward(g * _F1_GROUP + k)
                if g >= 1:
                    inverse((g - 1) * _F1_GROUP + k)
            if g >= 1:
                scatter(*tiles[g - 1])

    @pl.when(s >= n_in + n_mid)
    def _():
        bias = bias_ref[...]
        outs = [[], []]
        for i in range(_T2_PER_PHASE):
            t2 = (s - n_in - n_mid) * _T2_PER_PHASE + i
            y = jnp.dot(fai_ref[...], s_ref[t2], preferred_element_type=_F32)
            y = y + ukeep_ref[t2].astype(_F32) * bias
            for m in range(2):
                res = g_ref[m, i].astype(_F32) * y[m * h1:(m + 1) * h1]
                if natural_out:
                    outs[m].append(res.astype(o_ref.dtype))
                else:
                    o_ref[m, i] = res.astype(o_ref.dtype)
        if natural_out:
            for m in range(2):
                o_ref[m] = jnp.swapaxes(jnp.stack(outs[m], axis=0), 0, 1)


def _hyena_conv(u4, ucol, g4, gcol, kspec, kcol, bias, tabs, natural_out):
    fa, fai, fb, fbi = tabs
    b, n2, h1, _ = u4.shape
    n1 = 2 * h1
    ct = _FFT_CT
    n_in = n2 // _T2_PER_PHASE
    f1pp = min(_F1_PER_PHASE, n1)
    n_mid = n1 // f1pp
    n_out = n_in
    tiles = HYENA_WIDTH // ct
    body = functools.partial(_hyena_conv_body, n_in=n_in, n_mid=n_mid, n1=n1, f1pp=f1pp,
                             natural_out=natural_out)
    mid = lambda s: jnp.clip(s - n_in, 0, n_mid - 1)
    last = lambda s: jnp.clip(s - n_in - n_mid, 0, n_out - 1)
    blk = (2, _T2_PER_PHASE, h1, ct)
    if natural_out:
        out_spec = pl.BlockSpec((2, h1, _T2_PER_PHASE, ct), lambda j, p, s: (p, 0, last(s), j))
        out_shape = jax.ShapeDtypeStruct((b, h1, n2, HYENA_WIDTH), _BF16)
    else:
        out_spec = pl.BlockSpec(blk, lambda j, p, s: (p, last(s), 0, j))
        out_shape = jax.ShapeDtypeStruct((b, n2, h1, HYENA_WIDTH), _BF16)
    return pl.pallas_call(
        body,
        grid=(tiles, b // 2, n_in + n_mid + n_out),
        in_specs=[
            pl.BlockSpec(blk, lambda j, p, s: (p, jnp.minimum(s, n_in - 1), 0, ucol + j)),
            pl.BlockSpec(blk, lambda j, p, s: (p, last(s), 0, gcol + j)),
            pl.BlockSpec((f1pp * 2 * n2, ct), lambda j, p, s: (mid(s), kcol + j)),
            pl.BlockSpec((1, ct), lambda j, p, s: (0, j)),
            pl.BlockSpec((2 * n1, n1), lambda j, p, s: (0, 0)),
            pl.BlockSpec((n1, 2 * n1), lambda j, p, s: (0, 0)),
            pl.BlockSpec((f1pp, 2 * n2, 2 * n2), lambda j, p, s: (mid(s), 0, 0)),
            pl.BlockSpec((f1pp, 2 * n2, 2 * n2), lambda j, p, s: (mid(s), 0, 0)),
        ],
        out_specs=out_spec,
        out_shape=out_shape,
        scratch_shapes=([pltpu.VMEM((n2, 2 * n1, ct), _BF16)] + 4 * [pltpu.VMEM((f1pp, n2, ct), _BF16)]
                        + [pltpu.VMEM((f1pp, 2 * n2, ct), _BF16), pltpu.VMEM((n2, n1, ct), _BF16)]),
        compiler_params=_cparams(("arbitrary", "arbitrary", "arbitrary")),
        name="hyena_conv",
    )(u4, g4, kspec, bias, fa, fai, fb, fbi)


_GATE_LANES = 2 * MLSTM_HEADS
_TERM_R = 12
_LOG2E = math.log2(math.e)
_MLSTM_SUB = 2


def _split3(x):
    hi = x.astype(_BF16).astype(_F32)
    mid = (x - hi).astype(_BF16).astype(_F32)
    lo = (x - hi - mid).astype(_BF16).astype(_F32)
    return hi, mid, lo


def _scan_order_max(x, is_fwd):
    n = x.shape[0]
    row = lax.broadcasted_iota(jnp.int32, x.shape, 0)
    pre, suf = x, x
    shift = 1
    while shift < n:
        pre = jnp.maximum(pre, jnp.where(row >= shift, pltpu.roll(pre, shift, axis=0), -jnp.inf))
        suf = jnp.maximum(suf, jnp.where(row < n - shift, pltpu.roll(suf, n - shift, axis=0), -jnp.inf))
        shift *= 2
    return jnp.where(is_fwd, pre, suf)


def _gate_prep_body(g_ref, bias_ref, a_ref, row_ref, keep_ref,
                    b_scr, a_scr, pm_scr, tot_scr, mloc_scr, mprev_scr, *, nc):
    ch = MLSTM_CHUNK
    lane = lax.broadcasted_iota(jnp.int32, (1, 128), 1)
    is_fwd = lane < MLSTM_HEADS
    live = lane < _GATE_LANES
    jj = lax.broadcasted_iota(jnp.int32, (ch, ch), 0)
    ss = lax.broadcasted_iota(jnp.int32, (ch, ch), 1)
    t_lo = (ss <= jj).astype(_F32)
    row_ref[...] = jnp.zeros_like(row_ref)

    def chunk_stats(c, carry):
        r0 = pl.multiple_of(c * ch, ch)
        gi = g_ref[pl.ds(r0, ch), :] + bias_ref[...]
        gf = pltpu.roll(gi, 128 - 2 * MLSTM_HEADS, axis=1)
        logf = jnp.minimum(gf, 0.0) - jnp.log1p(jnp.exp(-jnp.abs(gf)))
        cs_lo = jnp.dot(t_lo, logf, precision=_HIGHEST, preferred_element_type=_F32)
        tot = cs_lo[ch - 1:ch, :]
        cs_up = tot - cs_lo + logf
        bcs = jnp.where(is_fwd, cs_lo, cs_up)
        a = tot - bcs + gi
        b_scr[pl.ds(r0, ch), :] = bcs
        a_scr[pl.ds(r0, ch), :] = a
        tot_scr[pl.ds(c, 1), :] = tot
        mloc_scr[pl.ds(c, 1), :] = jnp.max(a, axis=0, keepdims=True)
        r = gi - bcs
        pm_scr[pl.ds(r0, ch), :] = _scan_order_max(r, is_fwd)
        hi, mid, lo = _split3(r.T[0:_GATE_LANES, :] * _LOG2E)
        base = _TERM_R * _GATE_LANES
        row_ref[base:base + 16, pl.ds(r0, ch)] = jnp.concatenate([hi, mid], axis=0).astype(_BF16)
        row_ref[base + 16:base + 32, pl.ds(r0, ch)] = jnp.concatenate(
            [lo, jnp.zeros_like(lo)], axis=0).astype(_BF16)
        return carry

    lax.fori_loop(0, nc, chunk_stats, 0, unroll=4)

    def scan_f(c, m):
        mprev_scr[pl.ds(c, 1), :] = jnp.where(is_fwd, m, mprev_scr[pl.ds(c, 1), :])
        return jnp.maximum(tot_scr[pl.ds(c, 1), :] + m, mloc_scr[pl.ds(c, 1), :])

    def scan_b(i, m):
        c = nc - 1 - i
        mprev_scr[pl.ds(c, 1), :] = jnp.where(is_fwd, mprev_scr[pl.ds(c, 1), :], m)
        return jnp.maximum(tot_scr[pl.ds(c, 1), :] + m, mloc_scr[pl.ds(c, 1), :])

    mprev_scr[...] = jnp.zeros_like(mprev_scr)
    lax.fori_loop(0, nc, scan_f, jnp.zeros((1, 128), _F32))
    lax.fori_loop(0, nc, scan_b, jnp.zeros((1, 128), _F32))

    def emit(c, carry):
        r0 = pl.multiple_of(c * ch, ch)
        mprev = mprev_scr[pl.ds(c, 1), :]
        tot = tot_scr[pl.ds(c, 1), :]
        mnew = jnp.maximum(tot + mprev, mloc_scr[pl.ds(c, 1), :])
        keep_ref[pl.ds(c, 1), :] = jnp.exp(tot + mprev - mnew)
        g = jnp.maximum(mprev, pm_scr[pl.ds(r0, ch), :])
        terms = (_split3(-g * _LOG2E) + _split3((mprev - g) * _LOG2E)
                 + _split3((-b_scr[pl.ds(r0, ch), :] - g) * _LOG2E)
                 + _split3((a_scr[pl.ds(r0, ch), :] - mnew) * _LOG2E) + 3 * (jnp.ones((ch, 128), _F32),))
        pack = jnp.where(live, terms[0], 0.0)
        for t in range(1, len(terms)):
            pack = pack + pltpu.roll(jnp.where(live, terms[t], 0.0), _GATE_LANES * t, axis=1)
        a_ref[pl.ds(r0, ch), :] = pack.astype(_BF16)
        return carry

    lax.fori_loop(0, nc, emit, 0, unroll=4)


def _gate_prep(g3, bias):
    b, l, _ = g3.shape
    nc = l // MLSTM_CHUNK
    body = functools.partial(_gate_prep_body, nc=nc)
    return pl.pallas_call(
        body,
        grid=(b,),
        in_specs=[
            pl.BlockSpec((None, l, 128), lambda i: (i, 0, 0)),
            pl.BlockSpec((1, 128), lambda i: (0, 0)),
        ],
        out_specs=[
            pl.BlockSpec((None, l, 128), lambda i: (i, 0, 0)),
            pl.BlockSpec((None, 128, l), lambda i: (i, 0, 0)),
            pl.BlockSpec((None, nc, 128), lambda i: (i, 0, 0)),
        ],
        out_shape=[
            jax.ShapeDtypeStruct((b, l, 128), _BF16),
            jax.ShapeDtypeStruct((b, 128, l), _BF16),
            jax.ShapeDtypeStruct((b, nc, 128), _F32),
        ],
        scratch_shapes=[
            pltpu.VMEM((l, 128), _F32), pltpu.VMEM((l, 128), _F32), pltpu.VMEM((l, 128), _F32),
            pltpu.VMEM((nc, 128), _F32), pltpu.VMEM((nc, 128), _F32), pltpu.VMEM((nc, 128), _F32),
        ],
        compiler_params=_cparams(("arbitrary",)),
        name="gate_prep",
    )(g3, bias)


def _gate_spread_matrix():
    bc = np.zeros((128, 4 * 128), np.float32)
    for blk in range(4):
        for t in range(3 * blk, 3 * blk + 3):
            bc[t * _GATE_LANES:(t + 1) * _GATE_LANES, blk * 128:(blk + 1) * 128] = 1.0
    return jnp.asarray(bc).astype(_BF16)


def _mlstm_body(keep_ref, bc_ref, qf_ref, kf_ref, vf_ref, af_ref, rf_ref, qb_ref, kb_ref, vb_ref, ab_ref,
                rb_ref, hf_ref, hb_ref, ct_ref, nm_ref, *, nc):
    bi, c = pl.program_id(0), pl.program_id(1)

    @pl.when(c == 0)
    def _():
        ct_ref[...] = jnp.zeros_like(ct_ref)
        nm_ref[...] = jnp.zeros_like(nm_ref)

    ch, dh = MLSTM_CHUNK, MLSTM_HEAD_DIM
    jj = lax.broadcasted_iota(jnp.int32, (ch, ch), 0)
    ss = lax.broadcasted_iota(jnp.int32, (ch, ch), 1)
    head_lane = ss % _GATE_LANES
    ones_rhs = jnp.ones((ch, 128), _BF16)
    twice = lambda a: jnp.concatenate([a, a], axis=1)
    dirs = ((qf_ref, kf_ref, vf_ref, af_ref, rf_ref, hf_ref, ss <= jj),
            (qb_ref, kb_ref, vb_ref, ab_ref, rb_ref, hb_ref, ss >= jj))
    for sub, d in [(sub, d) for sub in range(_MLSTM_SUB) for d in range(2)]:
        q_blk, k_blk, v_blk, a_blk, r_blk, o_blk, mask = dirs[d]
        pos = sub if d == 0 else _MLSTM_SUB - 1 - sub
        chunk = c * _MLSTM_SUB + pos if d == 0 else nc - (c + 1) * _MLSTM_SUB + pos
        rows = slice(pos * ch, (pos + 1) * ch)
        q_ref, k_ref, v_ref, o_ref = q_blk.at[rows], k_blk.at[rows], v_blk.at[rows], o_blk.at[rows]
        bmat = jnp.concatenate([bc_ref[:, 0:128] + r_blk[:, rows], bc_ref[:, 128:]], axis=1)
        a_all = a_blk[rows, :].astype(_F32)
        kbase = (bi * nc + chunk) * _GATE_LANES
        for h in range(MLSTM_HEADS):
            hd = d * MLSTM_HEADS + h
            a_h = jnp.where(head_lane == hd, a_all, 0.0).astype(_BF16)
            e = jnp.dot(a_h, bmat, preferred_element_type=_F32)
            decay = jnp.exp2(jnp.where(mask, e[:, 0:128], -jnp.inf))
            rest = jnp.exp2(e[:, 128:])
            iw, clamp, wrep = rest[:, 0:128], rest[:, 128:256], rest[:, 256:384]
            keep = keep_ref[kbase + hd]
            lo, hi = h * dh, (h + 1) * dh
            qh, kh, vh = q_ref[:, lo:hi], k_ref[:, lo:hi], v_ref[:, lo:hi]
            p = (_dot_nt(qh, kh) * decay).astype(_BF16)
            ct, nm = ct_ref[hd], nm_ref[hd]
            pv = jnp.dot(p, jnp.concatenate([vh, ones_rhs], axis=1), preferred_element_type=_F32)
            qc = jnp.dot(qh, jnp.concatenate([ct, nm], axis=1).astype(_BF16), preferred_element_type=_F32)
            num = pv[:, :dh] + twice(iw) * qc[:, :dh]
            den = pv[:, dh:] + iw * qc[:, dh:]
            inv = 1.0 / jnp.maximum(jnp.abs(den), clamp)
            o_ref[:, lo:hi] = (num * twice(inv)).astype(o_ref.dtype)
            vw = jnp.concatenate([(vh.astype(_F32) * twice(wrep)).astype(_BF16), wrep.astype(_BF16)], axis=1)
            upd = lax.dot_general(kh, vw, (((0,), (0,)), ((), ())), preferred_element_type=_F32)
            ct_ref[hd] = keep * ct + upd[:, :dh]
            nm_ref[hd] = keep * nm + upd[:, dh:]


def _mlstm(qk, proj3, a_mat, r_rows, keep):
    b, l, _ = qk.shape
    mw = MLSTM_WIDTH
    nc = l // MLSTM_CHUNK
    rows = _MLSTM_SUB * MLSTM_CHUNK
    nb = l // rows
    fw = lambda c: c
    bw = lambda c: nb - 1 - c

    def specs(cm):
        return [
            pl.BlockSpec((None, rows, mw), lambda i, c: (i, cm(c), 0)),
            pl.BlockSpec((None, rows, mw), lambda i, c: (i, cm(c), 1)),
            pl.BlockSpec((None, rows, mw), lambda i, c: (i, cm(c), _COL_V)),
            pl.BlockSpec((None, rows, 128), lambda i, c: (i, cm(c), 0)),
            pl.BlockSpec((None, 128, rows), lambda i, c: (i, 0, cm(c))),
        ]

    body = functools.partial(_mlstm_body, nc=nc)
    keep_flat = keep[:, :, :_GATE_LANES].reshape(b * nc * _GATE_LANES)
    return pl.pallas_call(
        body,
        grid=(b, nb),
        in_specs=[pl.BlockSpec(memory_space=pltpu.SMEM),
                  pl.BlockSpec((128, 4 * 128), lambda i, c: (0, 0))] + specs(fw) + specs(bw),
        out_specs=[
            pl.BlockSpec((None, rows, mw), lambda i, c: (i, c, 0)),
            pl.BlockSpec((None, rows, mw), lambda i, c: (i, nb - 1 - c, 0)),
        ],
        out_shape=[jax.ShapeDtypeStruct((b, l, mw), _BF16), jax.ShapeDtypeStruct((b, l, mw), _BF16)],
        scratch_shapes=[
            pltpu.VMEM((2 * MLSTM_HEADS, MLSTM_HEAD_DIM, MLSTM_HEAD_DIM), _F32),
            pltpu.VMEM((2 * MLSTM_HEADS, MLSTM_HEAD_DIM, 128), _F32),
        ],
        compiler_params=_cparams(("arbitrary", "arbitrary")),
        name="mlstm",
    )(keep_flat, _gate_spread_matrix(), qk, qk, proj3, a_mat, r_rows, qk, qk, proj3, a_mat, r_rows)


def _sigmoid(x):
    return 1.0 / (1.0 + jnp.exp(-x))


def _merge_body(x_ref, ya_ref, hf_ref, hb_ref, o_ref, ma_ref, mb_ref, wa_ref, wb_ref, wo_ref, out_ref):
    hsum = hf_ref[...].astype(_F32) + hb_ref[...].astype(_F32)
    yb = (_sigmoid(o_ref[...].astype(_F32)) * hsum).astype(_BF16)
    pa = jnp.dot(ya_ref[...], wa_ref[...], preferred_element_type=_F32)
    pb = jnp.dot(yb, wb_ref[...], preferred_element_type=_F32)
    mixed = _sigmoid(ma_ref[...].astype(_F32)) * pa + _sigmoid(mb_ref[...].astype(_F32)) * pb
    out_ref[...] = x_ref[...] + jnp.dot(mixed.astype(_BF16), wo_ref[...], preferred_element_type=_F32)


def _merge(x2, ya2, hf2, hb2, proj, wa, wb, wo):
    m = x2.shape[0]
    tm = 256
    d, hw, mw = D_MODEL, HYENA_WIDTH, MLSTM_WIDTH
    const = lambda shape: pl.BlockSpec(shape, lambda i: (0, 0), pipeline_mode=pl.Buffered(1))
    return pl.pallas_call(
        _merge_body,
        grid=(m // tm,),
        in_specs=[
            pl.BlockSpec((tm, d), lambda i: (i, 0)),
            pl.BlockSpec((tm, hw), lambda i: (i, 0)),
            pl.BlockSpec((tm, mw), lambda i: (i, 0)),
            pl.BlockSpec((tm, mw), lambda i: (i, 0)),
            pl.BlockSpec((tm, mw), lambda i: (i, _COL_O)),
            pl.BlockSpec((tm, d), lambda i: (i, _COL_MA // 2)),
            pl.BlockSpec((tm, d), lambda i: (i, _COL_MB // 2)),
            const((hw, d)), const((mw, d)), const((d, d)),
        ],
        out_specs=pl.BlockSpec((tm, d), lambda i: (i, 0)),
        out_shape=jax.ShapeDtypeStruct((m, d), _F32),
        compiler_params=_cparams(("arbitrary",)),
        name="merge",
    )(x2, ya2, hf2, hb2, proj, proj, proj, wa, wb, wo)


def _ffn_body(x_ref, n2_ref, wg_ref, wu_ref, wd_ref, nf_ref, o_ref, hn_ref):
    f = pl.program_id(1)
    last = pl.num_programs(1) - 1
    rows = 256

    def swiglu_down(hn):
        g = jnp.dot(hn, wg_ref[...], preferred_element_type=_F32)
        u = jnp.dot(hn, wu_ref[...], preferred_element_type=_F32)
        a = (g * _sigmoid(g) * u).astype(_BF16)
        return jnp.dot(a, wd_ref[...], preferred_element_type=_F32)

    @pl.when(f == 0)
    def _():
        for r in range(0, x_ref.shape[0], rows):
            x = x_ref[r:r + rows, :]
            hn = (x * lax.rsqrt(jnp.mean(x * x, axis=-1, keepdims=True) + RMS_EPS) * n2_ref[...]).astype(_BF16)
            hn_ref[r:r + rows, :] = hn
            o_ref[r:r + rows, :] = x + swiglu_down(hn)

    @pl.when(jnp.logical_and(f > 0, f < last))
    def _():
        o_ref[...] += swiglu_down(hn_ref[...])

    @pl.when(f == last)
    def _():
        for r in range(0, x_ref.shape[0], rows):
            y = o_ref[r:r + rows, :] + swiglu_down(hn_ref[r:r + rows, :])
            o_ref[r:r + rows, :] = y * lax.rsqrt(jnp.mean(y * y, axis=-1, keepdims=True) + RMS_EPS) * nf_ref[...]


def _ffn(x2, norm2_w, w_gate_up, w_down, norm_f_w):
    m = x2.shape[0]
    tm, tf = 1024, 512
    d = D_MODEL
    nf = FFN_HIDDEN // tf
    return pl.pallas_call(
        _ffn_body,
        grid=(m // tm, nf),
        in_specs=[
            pl.BlockSpec((tm, d), lambda i, f: (i, 0)),
            pl.BlockSpec((1, d), lambda i, f: (0, 0)),
            pl.BlockSpec((d, tf), lambda i, f: (0, f)),
            pl.BlockSpec((d, tf), lambda i, f: (0, nf + f)),
            pl.BlockSpec((tf, d), lambda i, f: (f, 0)),
            pl.BlockSpec((1, d), lambda i, f: (0, 0)),
        ],
        out_specs=pl.BlockSpec((tm, d), lambda i, f: (i, 0)),
        out_shape=jax.ShapeDtypeStruct((m, d), _F32),
        scratch_shapes=[pltpu.VMEM((tm, d), _BF16)],
        compiler_params=_cparams(("arbitrary", "arbitrary")),
        name="ffn",
    )(x2, norm2_w, w_gate_up, w_gate_up, w_down, norm_f_w)


def kernel(x, norm1_w, w_in, conv_w, conv_b, filt_w1, filt_b1, filt_freq1, filt_w2, filt_b2, filt_freq2,
           filt_w3, hyena_bias, mlstm_gate_bias, w_branch_a, w_branch_b, w_out, norm2_w, w_gate_up, w_down,
           norm_f_w):
    b, l, d = x.shape
    assert d == D_MODEL and b % 2 == 0 and l % (_N2 * _T2_PER_PHASE) == 0
    assert norm1_w.shape[0] == 1, "single-layer block"
    hw, mw, nh = HYENA_WIDTH, MLSTM_WIDTH, MLSTM_HEADS
    m = b * l
    n = 2 * l
    n1 = n // _N2
    sc_cols = 3 * hw + 2 * mw
    g0 = sc_cols + 2 * mw

    w_t = jnp.swapaxes(w_in[0], 0, 1)
    w_all_t = w_t.astype(_BF16)
    w_merge_t = w_all_t[g0 + 4 * nh:]
    wg_t = w_t[g0:g0 + 4 * nh]
    gate_order = lambda a: jnp.concatenate(
        [a[0:nh], a[2 * nh:3 * nh], a[nh:2 * nh], a[3 * nh:4 * nh]], axis=0)
    wg_split_t = jnp.concatenate(_split_bf16(jnp.pad(gate_order(wg_t), ((0, 128 - 4 * nh), (0, 0)))), axis=0)
    gate_bias = jnp.pad(gate_order(mlstm_gate_bias[0].astype(_F32).reshape(4 * nh))[None, :],
                        ((0, 0), (0, 128 - 4 * nh)))

    x2 = x.reshape(m, d)
    proj, gates = _inproj(x2, norm1_w[0][None, :], w_merge_t, w_all_t, wg_split_t)
    proj3 = proj.reshape(b, l, _MAIN_COLS)

    cw, cb = conv_w[0], conv_b[0][None, :]
    hy4 = _shortconv(proj3, cw, cb, _COL_HV * 1024, 0, 3 * hw, True)
    qk = _shortconv(proj3, cw, cb, _COL_Q * 1024, 3 * hw, 2 * mw, False)

    fa, fa_full, fai, fb, fbi = _dft_tables(n)
    zpos = _filter_positions(l)
    w1p = jnp.pad(filt_w1[0].astype(_F32), ((0, 128 - HYENA_EMB), (0, 0)))
    max_decay = math.log(HYENA_TARGET) / HYENA_FAST_DECAY
    min_decay = math.log(HYENA_TARGET) / HYENA_SLOW_DECAY
    dabs = jnp.asarray(np.abs(np.linspace(min_decay, max_decay, hw, dtype=np.float32))[None, :])
    dup = lambda a: jnp.concatenate([a, a], axis=-1)
    fh = HYENA_FILTER_HIDDEN
    w3 = filt_w3[0].astype(_F32).reshape(fh, 2, 2, hw).transpose(2, 0, 1, 3).reshape(2 * fh, 2 * hw)
    w3_hi, w3_lo = _split_bf16(w3)
    blockdiag = lambda a: jnp.concatenate(
        [jnp.concatenate([a, jnp.zeros_like(a)], axis=1), jnp.concatenate([jnp.zeros_like(a), a], axis=1)], axis=0)
    kern = _filt_mlp(zpos, blockdiag(w1p), dup(filt_b1[0][None, :]), dup(filt_freq1[0][None, :]),
                     blockdiag(filt_w2[0].astype(_F32)), dup(filt_b2[0][None, :]),
                     dup(filt_freq2[0][None, :]), w3_hi, w3_lo, dabs)
    kspec = _filt_fft(kern.reshape(_N2, n1, 2 * hw), fa_full, fb)
    tabs = (fa, fai, fb, fbi)
    tiles = hw // _FFT_CT
    hbias = hyena_bias[0].astype(_F32)
    z4 = _hyena_conv(hy4, 0, hy4, tiles, kspec, 0, hbias[0][None, :], tabs, False)
    ya4 = _hyena_conv(z4, 0, hy4, 2 * tiles, kspec, tiles, hbias[1][None, :], tabs, True)
    ya2 = ya4.reshape(m, hw)

    a_mat, r_rows, keep = _gate_prep(gates.reshape(b, l, 128), gate_bias)
    hf, hb = _mlstm(qk, proj3, a_mat, r_rows, keep)

    x_mid = _merge(x2, ya2, hf.reshape(m, mw), hb.reshape(m, mw), proj,
                   w_branch_a[0].astype(_BF16), w_branch_b[0].astype(_BF16), w_out[0].astype(_BF16))
    out = _ffn(x_mid, norm2_w[0][None, :], w_gate_up[0].astype(_BF16), w_down[0].astype(_BF16),
               norm_f_w[None, :])
    return out.reshape(b, l, d)
```

```python
import functools
import math

import numpy as np
import jax
import jax.numpy as jnp
from jax import lax
from jax.experimental import pallas as pl
from jax.experimental.pallas import tpu as pltpu

_F32 = jnp.float32
_BF16 = jnp.bfloat16
_HIGHEST = lax.Precision.HIGHEST

D_MODEL = 2048
HYENA_WIDTH = 1024
HYENA_EMB = 33
HYENA_FILTER_HIDDEN = 64
HYENA_FAST_DECAY = 0.3
HYENA_SLOW_DECAY = 1.5
HYENA_TARGET = 1e-2
MLSTM_WIDTH = 1024
MLSTM_HEADS = 4
MLSTM_HEAD_DIM = 256
MLSTM_CHUNK = 128
FFN_HIDDEN = 5632
RMS_EPS = 1e-6

_COL_MA, _COL_MB, _COL_HV, _COL_Q, _COL_V, _COL_O = 0, 2, 4, 7, 9, 10
_MAIN_COLS = 11 * 1024

_N2 = 32
_T2_PER_PHASE = 16
_F1_PER_PHASE = 64
_FFT_CT = 256

_VMEM_LIMIT = 56 * 1024 * 1024


def _cparams(sem, vmem=_VMEM_LIMIT):
    return pltpu.CompilerParams(dimension_semantics=sem, vmem_limit_bytes=vmem)


def _split_bf16(a):
    hi = a.astype(_BF16)
    lo = (a - hi.astype(_F32)).astype(_BF16)
    return hi, lo


def _dot_nt(a, b):
    return lax.dot_general(a, b, (((1,), (1,)), ((), ())), preferred_element_type=_F32)


def _dot3(a_hi, a_lo, b_hi, b_lo):
    d = functools.partial(jnp.dot, preferred_element_type=_F32)
    return d(a_hi, b_hi) + (d(a_hi, b_lo) + d(a_lo, b_hi))


def _inproj_body(x_ref, nw_ref, wm_ref, w_ref, wg_ref, o_ref, g_ref, hn_ref, *, n_merge):
    j = pl.program_id(1)

    @pl.when(j == 0)
    def _():
        rows = 256
        for r in range(0, x_ref.shape[0], rows):
            x = x_ref[r:r + rows, :]
            hn = x * lax.rsqrt(jnp.mean(x * x, axis=-1, keepdims=True) + RMS_EPS) * nw_ref[...]
            hn_hi, hn_lo = _split_bf16(hn)
            hn_ref[r:r + rows, :] = hn_hi
            hh_hl = _dot_nt(hn_hi, wg_ref[...])
            g_ref[r:r + rows, :] = hh_hl[:, :128] + (hh_hl[:, 128:] + _dot_nt(hn_lo, wg_ref[0:128, :]))
            o_ref[r:r + rows, :] = _dot_nt(hn_hi, wm_ref[...]).astype(o_ref.dtype)

    @pl.when(jnp.logical_and(j > 0, j < n_merge))
    def _():
        o_ref[...] = _dot_nt(hn_ref[...], wm_ref[...]).astype(o_ref.dtype)

    @pl.when(j >= n_merge)
    def _():
        o_ref[...] = _dot_nt(hn_ref[...], w_ref[...]).astype(o_ref.dtype)


def _inproj(x2, norm_w, w_merge_t, w_all_t, wg_t):
    m = x2.shape[0]
    tm, tn = 1024, 1024
    n_merge = w_merge_t.shape[0] // tn
    body = functools.partial(_inproj_body, n_merge=n_merge)
    return pl.pallas_call(
        body,
        grid=(m // tm, _MAIN_COLS // tn),
        in_specs=[
            pl.BlockSpec((tm, D_MODEL), lambda i, j: (i, 0)),
            pl.BlockSpec((1, D_MODEL), lambda i, j: (0, 0)),
            pl.BlockSpec((tn, D_MODEL), lambda i, j: (jnp.minimum(j, n_merge - 1), 0)),
            pl.BlockSpec((tn, D_MODEL), lambda i, j: (jnp.maximum(j - n_merge, 0), 0)),
            pl.BlockSpec((256, D_MODEL), lambda i, j: (0, 0)),
        ],
        out_specs=[
            pl.BlockSpec((tm, tn), lambda i, j: (i, j)),
            pl.BlockSpec((tm, 128), lambda i, j: (i, 0)),
        ],
        out_shape=[
            jax.ShapeDtypeStruct((m, _MAIN_COLS), _BF16),
            jax.ShapeDtypeStruct((m, 128), _F32),
        ],
        scratch_shapes=[pltpu.VMEM((tm, D_MODEL), _BF16)],
        compiler_params=_cparams(("arbitrary", "arbitrary")),
        name="inproj",
    )(x2, norm_w, w_merge_t, w_all_t, wg_t)


def _shortconv_silu_body(u_ref, w_ref, b_ref, o_ref, *, k_scale):
    u = u_ref[...].astype(_F32)
    n = u.shape[0]
    row = lax.broadcasted_iota(jnp.int32, u.shape, 0)
    up = jnp.where(row == 0, 0.0, pltpu.roll(u, 1, axis=0))
    un = jnp.where(row == n - 1, 0.0, pltpu.roll(u, n - 1, axis=0))
    w = w_ref[...]
    y = up * w[0:1, :] + b_ref[...] + u * w[1:2, :] + un * w[2:3, :]
    y = y * (1.0 / (1.0 + jnp.exp(-y)))
    is_k = pl.program_id(1) >= (MLSTM_WIDTH // u.shape[1])
    o_ref[...] = (y * jnp.where(is_k, k_scale, 1.0)).astype(o_ref.dtype)


def _shortconv_t2_body(u_ref, w_ref, b_ref, o_ref, ut_ref):
    n, c = u_ref.shape
    h1 = n // _N2
    g = 16
    for k in range(h1 // g):
        blk = u_ref[k * g * _N2:(k + 1) * g * _N2, :].reshape(g, _N2, c)
        ut_ref[:, k * g:(k + 1) * g, :] = jnp.swapaxes(blk, 0, 1)
    w = w_ref[...]
    w0, w1, w2, bias = w[0:1, :], w[1:2, :], w[2:3, :], b_ref[...]
    slab = lambda t2: ut_ref[t2].astype(_F32)
    row = lax.broadcasted_iota(jnp.int32, (h1, c), 0)
    before_first = jnp.where(row == 0, 0.0, pltpu.roll(slab(_N2 - 1), 1, axis=0))
    after_last = jnp.where(row == h1 - 1, 0.0, pltpu.roll(slab(0), h1 - 1, axis=0))
    for t2 in range(_N2):
        up = before_first if t2 == 0 else slab(t2 - 1)
        un = after_last if t2 == _N2 - 1 else slab(t2 + 1)
        o_ref[t2] = (up * w0 + bias + slab(t2) * w1 + un * w2).astype(o_ref.dtype)


def _shortconv(proj3, conv_w, conv_b, proj_col0, conv_col0, ncols, t2_major):
    b, l, _ = proj3.shape
    ct = _FFT_CT
    p0, c0 = proj_col0 // ct, conv_col0 // ct
    if t2_major:
        body = _shortconv_t2_body
        out_spec = pl.BlockSpec((None, None, _N2, l // _N2, ct), lambda i, j: (i, j, 0, 0, 0))
        out_shape = jax.ShapeDtypeStruct((b, ncols // ct, _N2, l // _N2, ct), _BF16)
        scratch = [pltpu.VMEM((_N2, l // _N2, ct), _BF16)]
    else:
        body = functools.partial(_shortconv_silu_body, k_scale=MLSTM_HEAD_DIM ** -0.5)
        out_spec = pl.BlockSpec((None, l, ct), lambda i, j: (i, 0, j))
        out_shape = jax.ShapeDtypeStruct((b, l, ncols), _BF16)
        scratch = []
    return pl.pallas_call(
        body,
        grid=(b, ncols // ct),
        in_specs=[
            pl.BlockSpec((None, l, ct), lambda i, j: (i, 0, p0 + j)),
            pl.BlockSpec((3, ct), lambda i, j: (0, c0 + j)),
            pl.BlockSpec((1, ct), lambda i, j: (0, c0 + j)),
        ],
        out_specs=out_spec,
        out_shape=out_shape,
        scratch_shapes=scratch,
        compiler_params=_cparams(("arbitrary", "arbitrary")),
        name="shortconv" if t2_major else "shortconv_silu",
    )(proj3, conv_w, conv_b)


def _dft_tables(n):
    n2 = _N2
    n1 = n // n2
    h1 = n1 // 2
    f1 = np.arange(n1)[:, None]
    t1 = np.arange(n1)[None, :]
    ang = 2.0 * np.pi * ((f1 * t1) % n1) / n1
    c, s = np.cos(ang), np.sin(ang)
    ch, sh = c[:, :h1], s[:, :h1]
    fa = np.block([[ch, sh], [-sh, ch]])
    fa_full = np.concatenate([c, -s], axis=0)
    fai = np.block([[ch.T, -sh.T], [sh.T, ch.T]]) / n
    f1v = np.arange(n1)[:, None, None]
    f2 = np.arange(n2)[None, :, None]
    t2 = np.arange(n2)[None, None, :]
    ph = 2.0 * np.pi * ((f1v * t2 + f2 * t2 * n1) % n) / n
    tc, ts = np.cos(ph), np.sin(ph)
    fb = np.concatenate([np.concatenate([tc, ts], axis=2),
                         np.concatenate([-ts, tc], axis=2)], axis=1)
    tct, tst = np.swapaxes(tc, 1, 2), np.swapaxes(ts, 1, 2)
    fbi = np.concatenate([np.concatenate([tct, -tst], axis=2),
                          np.concatenate([tst, tct], axis=2)], axis=1)
    to = lambda a: jnp.asarray(a.astype(np.float32)).astype(_BF16)
    return to(fa), to(fa_full), to(fai), to(fb), to(fbi)


def _filter_positions(l):
    n = 2 * l
    n1 = n // _N2
    r = (np.arange(_N2)[:, None] + _N2 * np.arange(n1)[None, :]).reshape(-1)
    pos = np.where(r < l, r, n - r)
    pos = np.where(r == l, 0, pos)
    bands = (HYENA_EMB - 1) // 2
    tt = np.linspace(0.0, 1.0, l, dtype=np.float32).astype(np.float64)[pos]
    omega = (2.0 * math.pi * np.arange(l, dtype=np.float32) / l).astype(np.float32)
    freqs = np.linspace(1e-4, bands - 1, bands, dtype=np.float32)
    ang = (omega[:, None] * freqs[None, :]).astype(np.float64)[pos]
    z = np.zeros((n, 128), np.float64)
    z[:, 0] = tt
    z[:, 1:1 + bands] = np.cos(ang)
    z[:, 1 + bands:1 + 2 * bands] = -np.sin(ang)
    z[:, 33] = (r != l)
    z[:, 34] = (r < l)
    return jnp.asarray(z.astype(np.float32))


def _filt_mlp_body(z_ref, zp_ref, w1_ref, b1_ref, q1_ref, w2_ref, b2_ref, q2_ref, w3h_ref, w3l_ref, dabs_ref,
                   o_ref, hh_ref, hl_ref):
    fh = HYENA_FILTER_HIDDEN

    @pl.when(pl.program_id(1) == 0)
    def _():
        hp = jnp.sin(q1_ref[...] * (jnp.dot(zp_ref[...], w1_ref[...], precision=_HIGHEST,
                                            preferred_element_type=_F32) + b1_ref[...]))
        hp = jnp.sin(q2_ref[...] * (jnp.dot(hp, w2_ref[...], precision=_HIGHEST,
                                            preferred_element_type=_F32) + b2_ref[...]))
        lane_p = lax.broadcasted_iota(jnp.int32, hp.shape, 1)
        swapped = pltpu.roll(hp, fh, axis=1)
        h = jnp.concatenate([jnp.where(lane_p < fh, hp, swapped), jnp.where(lane_p < fh, swapped, hp)], axis=0)
        z = z_ref[...]
        valid = z[:, 33:34]
        fwd = z[:, 34:35]
        lane = lax.broadcasted_iota(jnp.int32, h.shape, 1)
        hi, lo = _split_bf16(h * (valid * jnp.where(lane < fh, fwd, 1.0 - fwd)))
        hh_ref[...] = hi
        hl_ref[...] = lo

    raw = _dot3(hh_ref[...], hl_ref[...], w3h_ref[...], w3l_ref[...])
    window = jnp.exp(-z_ref[:, 0:1] * dabs_ref[...])
    o_ref[...] = (raw * window).astype(o_ref.dtype)


_FILT_ROWS = 1024


def _filt_mlp(zpos, w1bd, b1d, q1d, w2bd, b2d, q2d, w3h, w3l, dabs):
    n = zpos.shape[0]
    ct = _FFT_CT
    tr = _FILT_ROWS
    per_order = HYENA_WIDTH // ct
    fh = HYENA_FILTER_HIDDEN
    zpacked = zpos.reshape(n // tr, 2, tr // 2, 128).transpose(0, 2, 1, 3).reshape(n // 2, 256)
    full = lambda shape: pl.BlockSpec(shape, lambda r, i: (0,) * len(shape))
    return pl.pallas_call(
        _filt_mlp_body,
        grid=(n // tr, 2 * per_order),
        in_specs=[
            pl.BlockSpec((tr, 128), lambda r, i: (r, 0)),
            pl.BlockSpec((tr // 2, 256), lambda r, i: (r, 0)),
            full((256, 2 * fh)), full((1, 2 * fh)), full((1, 2 * fh)),
            full((2 * fh, 2 * fh)), full((1, 2 * fh)), full((1, 2 * fh)),
            pl.BlockSpec((2 * fh, ct), lambda r, i: (0, i)),
            pl.BlockSpec((2 * fh, ct), lambda r, i: (0, i)),
            pl.BlockSpec((1, ct), lambda r, i: (0, i % per_order)),
        ],
        out_specs=pl.BlockSpec((None, tr, ct), lambda r, i: (i, r, 0)),
        out_shape=jax.ShapeDtypeStruct((2 * per_order, n, ct), _BF16),
        scratch_shapes=[pltpu.VMEM((tr, 2 * fh), _BF16), pltpu.VMEM((tr, 2 * fh), _BF16)],
        compiler_params=_cparams(("arbitrary", "arbitrary")),
        name="filt_mlp",
    )(zpos, zpacked, w1bd, b1d, q1d, w2bd, b2d, q2d, w3h, w3l, dabs)


_F1_GROUP = 16


def _f1_tiles(f_base, f1pp):
    return [(slice(g * _F1_GROUP, (g + 1) * _F1_GROUP), pl.multiple_of(f_base + g * _F1_GROUP, _F1_GROUP))
            for g in range(f1pp // _F1_GROUP)]


def _gather_f1(s_ref, vre_ref, vim_ref, f_base, f1pp, n1):
    for sl, f0 in _f1_tiles(f_base, f1pp):
        vre_ref[sl] = jnp.swapaxes(s_ref[:, pl.ds(f0, _F1_GROUP), :], 0, 1)
        vim_ref[sl] = jnp.swapaxes(s_ref[:, pl.ds(n1 + f0, _F1_GROUP), :], 0, 1)


def _filt_fft_body(kern_ref, fa_ref, fb_ref, o_ref, s_ref, vre_ref, vim_ref, *, n_in, n1, f1pp):
    s = pl.program_id(1)

    @pl.when(s < n_in)
    def _():
        for i in range(_T2_PER_PHASE):
            s_ref[s * _T2_PER_PHASE + i] = jnp.dot(fa_ref[...], kern_ref[i],
                                                   preferred_element_type=_F32).astype(_BF16)

    @pl.when(s >= n_in)
    def _():
        _gather_f1(s_ref, vre_ref, vim_ref, (s - n_in) * f1pp, f1pp, n1)

        for f in range(f1pp):
            v = jnp.concatenate([vre_ref[f], vim_ref[f]], axis=0)
            o_ref[f * 2 * _N2:(f + 1) * 2 * _N2, :] = jnp.dot(
                fb_ref[f], v, preferred_element_type=_F32).astype(o_ref.dtype)


def _filt_fft(kern4, fa_full, fb):
    tiles, n2, n1, ct = kern4.shape
    n_in = n2 // _T2_PER_PHASE
    f1pp = min(_F1_PER_PHASE, n1)
    n_mid = n1 // f1pp
    body = functools.partial(_filt_fft_body, n_in=n_in, n1=n1, f1pp=f1pp)
    return pl.pallas_call(
        body,
        grid=(tiles, n_in + n_mid),
        in_specs=[
            pl.BlockSpec((None, _T2_PER_PHASE, n1, ct), lambda j, s: (j, jnp.minimum(s, n_in - 1), 0, 0)),
            pl.BlockSpec((2 * n1, n1), lambda j, s: (0, 0)),
            pl.BlockSpec((f1pp, 2 * n2, 2 * n2), lambda j, s: (jnp.maximum(s - n_in, 0), 0, 0)),
        ],
        out_specs=pl.BlockSpec((None, f1pp * 2 * n2, ct), lambda j, s: (j, jnp.maximum(s - n_in, 0), 0)),
        out_shape=jax.ShapeDtypeStruct((tiles, n1 * 2 * n2, ct), _BF16),
        scratch_shapes=[pltpu.VMEM((n2, 2 * n1, ct), _BF16),
                        pltpu.VMEM((f1pp, n2, ct), _BF16), pltpu.VMEM((f1pp, n2, ct), _BF16)],
        compiler_params=_cparams(("arbitrary", "arbitrary")),
        name="filt_fft",
    )(kern4, fa_full, fb)


def _hyena_conv_body(u_ref, g_ref, k_ref, bias_ref, fa_ref, fai_ref, fb_ref, fbi_ref,
                     o_ref, s_ref, vre_ref, vim_ref, wre_ref, wim_ref, y_ref, ukeep_ref,
                     *, n_in, n_mid, n1, f1pp, natural_out):
    s = pl.program_id(2)
    h1 = n1 // 2

    @pl.when(s < n_in)
    def _():
        for i in range(_T2_PER_PHASE):
            u = jnp.concatenate([u_ref[0, i], u_ref[1, i]], axis=0)
            ukeep_ref[s * _T2_PER_PHASE + i] = u
            s_ref[s * _T2_PER_PHASE + i] = jnp.dot(fa_ref[...], u,
                                                   preferred_element_type=_F32).astype(_BF16)

    @pl.when(jnp.logical_and(s >= n_in, s < n_in + n_mid))
    def _():
        tiles = _f1_tiles((s - n_in) * f1pp, f1pp)

        def gather(sl, f0):
            vre_ref[sl] = jnp.swapaxes(s_ref[:, pl.ds(f0, _F1_GROUP), :], 0, 1)
            vim_ref[sl] = jnp.swapaxes(s_ref[:, pl.ds(n1 + f0, _F1_GROUP), :], 0, 1)

        def forward(f):
            v = jnp.concatenate([vre_ref[f], vim_ref[f]], axis=0)
            x = jnp.dot(fb_ref[f], v, preferred_element_type=_F32)
            xre, xim = x[:_N2], x[_N2:]
            kre = k_ref[f * 2 * _N2:f * 2 * _N2 + _N2, :].astype(_F32)
            kim = k_ref[f * 2 * _N2 + _N2:(f + 1) * 2 * _N2, :].astype(_F32)
            y_ref[f] = jnp.concatenate([xre * kre - xim * kim, xre * kim + xim * kre],
                                       axis=0).astype(_BF16)

        def inverse(f):
            w = jnp.dot(fbi_ref[f], y_ref[f], preferred_element_type=_F32)
            wre_ref[f] = w[:_N2].astype(_BF16)
            wim_ref[f] = w[_N2:].astype(_BF16)

        def scatter(sl, f0):
            s_ref[:, pl.ds(f0, _F1_GROUP), :] = jnp.swapaxes(wre_ref[sl], 0, 1)
            s_ref[:, pl.ds(n1 + f0, _F1_GROUP), :] = jnp.swapaxes(wim_ref[sl], 0, 1)

        nt = len(tiles)
        for t in tiles:
            gather(*t)
        for g in range(nt + 1):
            for k in range(_F1_GROUP):
                if g < nt:
                    forward(g * _F1_GROUP + k)
                if g >= 1:
                    inverse((g - 1) * _F1_GROUP + k)
            if g >= 1:
                scatter(*tiles[g - 1])

    @pl.when(s >= n_in + n_mid)
    def _():
        bias = bias_ref[...]
        outs = [[], []]
        for i in range(_T2_PER_PHASE):
            t2 = (s - n_in - n_mid) * _T2_PER_PHASE + i
            y = jnp.dot(fai_ref[...], s_ref[t2], preferred_element_type=_F32)
            y = y + ukeep_ref[t2].astype(_F32) * bias
            for m in range(2):
                res = g_ref[m, i].astype(_F32) * y[m * h1:(m + 1) * h1]
                if natural_out:
                    outs[m].append(res.astype(o_ref.dtype))
                else:
                    o_ref[m, i] = res.astype(o_ref.dtype)
        if natural_out:
            for m in range(2):
                o_ref[m] = jnp.swapaxes(jnp.stack(outs[m], axis=0), 0, 1)


def _hyena_conv(u5, ucol, g5, gcol, kspec, kcol, bias, tabs, natural_out):
    fa, fai, fb, fbi = tabs
    b, _, n2, h1, ct = u5.shape
    n1 = 2 * h1
    n_in = n2 // _T2_PER_PHASE
    f1pp = min(_F1_PER_PHASE, n1)
    n_mid = n1 // f1pp
    n_out = n_in
    tiles = HYENA_WIDTH // ct
    body = functools.partial(_hyena_conv_body, n_in=n_in, n_mid=n_mid, n1=n1, f1pp=f1pp,
                             natural_out=natural_out)
    mid = lambda s: jnp.clip(s - n_in, 0, n_mid - 1)
    last = lambda s: jnp.clip(s - n_in - n_mid, 0, n_out - 1)
    blk = (2, None, _T2_PER_PHASE, h1, ct)
    if natural_out:
        out_spec = pl.BlockSpec((None, 2, h1, _T2_PER_PHASE, ct), lambda j, p, s: (j, p, 0, last(s), 0))
        out_shape = jax.ShapeDtypeStruct((tiles, b, h1, n2, ct), _BF16)
    else:
        out_spec = pl.BlockSpec(blk, lambda j, p, s: (p, j, last(s), 0, 0))
        out_shape = jax.ShapeDtypeStruct((b, tiles, n2, h1, ct), _BF16)
    return pl.pallas_call(
        body,
        grid=(tiles, b // 2, n_in + n_mid + n_out),
        in_specs=[
            pl.BlockSpec(blk, lambda j, p, s: (p, ucol + j, jnp.minimum(s, n_in - 1), 0, 0)),
            pl.BlockSpec(blk, lambda j, p, s: (p, gcol + j, last(s), 0, 0)),
            pl.BlockSpec((None, f1pp * 2 * n2, ct), lambda j, p, s: (kcol + j, mid(s), 0)),
            pl.BlockSpec((1, ct), lambda j, p, s: (0, j)),
            pl.BlockSpec((2 * n1, n1), lambda j, p, s: (0, 0)),
            pl.BlockSpec((n1, 2 * n1), lambda j, p, s: (0, 0)),
            pl.BlockSpec((f1pp, 2 * n2, 2 * n2), lambda j, p, s: (mid(s), 0, 0)),
            pl.BlockSpec((f1pp, 2 * n2, 2 * n2), lambda j, p, s: (mid(s), 0, 0)),
        ],
        out_specs=out_spec,
        out_shape=out_shape,
        scratch_shapes=([pltpu.VMEM((n2, 2 * n1, ct), _BF16)] + 4 * [pltpu.VMEM((f1pp, n2, ct), _BF16)]
                        + [pltpu.VMEM((f1pp, 2 * n2, ct), _BF16), pltpu.VMEM((n2, n1, ct), _BF16)]),
        compiler_params=_cparams(("arbitrary", "arbitrary", "arbitrary")),
        name="hyena_conv",
    )(u5, g5, kspec, bias, fa, fai, fb, fbi)


_GATE_LANES = 2 * MLSTM_HEADS
_TERM_R = 12
_LOG2E = math.log2(math.e)
_MLSTM_SUB = 2


def _split3(x):
    hi = x.astype(_BF16).astype(_F32)
    mid = (x - hi).astype(_BF16).astype(_F32)
    lo = (x - hi - mid).astype(_BF16).astype(_F32)
    return hi, mid, lo


def _scan_order_max(x, is_fwd):
    n = x.shape[0]
    row = lax.broadcasted_iota(jnp.int32, x.shape, 0)
    pre, suf = x, x
    shift = 1
    while shift < n:
        pre = jnp.maximum(pre, jnp.where(row >= shift, pltpu.roll(pre, shift, axis=0), -jnp.inf))
        suf = jnp.maximum(suf, jnp.where(row < n - shift, pltpu.roll(suf, n - shift, axis=0), -jnp.inf))
        shift *= 2
    return jnp.where(is_fwd, pre, suf)


def _gate_prep_body(g_ref, bias_ref, a_ref, row_ref, keep_ref,
                    b_scr, a_scr, pm_scr, tot_scr, mloc_scr, mprev_scr, *, nc):
    ch = MLSTM_CHUNK
    lane = lax.broadcasted_iota(jnp.int32, (1, 128), 1)
    is_fwd = lane < MLSTM_HEADS
    live = lane < _GATE_LANES
    jj = lax.broadcasted_iota(jnp.int32, (ch, ch), 0)
    ss = lax.broadcasted_iota(jnp.int32, (ch, ch), 1)
    t_lo = (ss <= jj).astype(_F32)
    row_ref[...] = jnp.zeros_like(row_ref)

    def chunk_stats(c, carry):
        r0 = pl.multiple_of(c * ch, ch)
        gi = g_ref[pl.ds(r0, ch), :] + bias_ref[...]
        gf = pltpu.roll(gi, 128 - 2 * MLSTM_HEADS, axis=1)
        logf = jnp.minimum(gf, 0.0) - jnp.log1p(jnp.exp(-jnp.abs(gf)))
        cs_lo = jnp.dot(t_lo, logf, precision=_HIGHEST, preferred_element_type=_F32)
        tot = cs_lo[ch - 1:ch, :]
        cs_up = tot - cs_lo + logf
        bcs = jnp.where(is_fwd, cs_lo, cs_up)
        a = tot - bcs + gi
        b_scr[pl.ds(r0, ch), :] = bcs
        a_scr[pl.ds(r0, ch), :] = a
        tot_scr[pl.ds(c, 1), :] = tot
        mloc_scr[pl.ds(c, 1), :] = jnp.max(a, axis=0, keepdims=True)
        r = gi - bcs
        pm_scr[pl.ds(r0, ch), :] = _scan_order_max(r, is_fwd)
        hi, mid, lo = _split3(r.T[0:_GATE_LANES, :] * _LOG2E)
        base = _TERM_R * _GATE_LANES
        row_ref[base:base + 16, pl.ds(r0, ch)] = jnp.concatenate([hi, mid], axis=0).astype(_BF16)
        row_ref[base + 16:base + 32, pl.ds(r0, ch)] = jnp.concatenate(
            [lo, jnp.zeros_like(lo)], axis=0).astype(_BF16)
        return carry

    lax.fori_loop(0, nc, chunk_stats, 0, unroll=4)

    def scan_f(c, m):
        mprev_scr[pl.ds(c, 1), :] = jnp.where(is_fwd, m, mprev_scr[pl.ds(c, 1), :])
        return jnp.maximum(tot_scr[pl.ds(c, 1), :] + m, mloc_scr[pl.ds(c, 1), :])

    def scan_b(i, m):
        c = nc - 1 - i
        mprev_scr[pl.ds(c, 1), :] = jnp.where(is_fwd, mprev_scr[pl.ds(c, 1), :], m)
        return jnp.maximum(tot_scr[pl.ds(c, 1), :] + m, mloc_scr[pl.ds(c, 1), :])

    mprev_scr[...] = jnp.zeros_like(mprev_scr)
    lax.fori_loop(0, nc, scan_f, jnp.zeros((1, 128), _F32))
    lax.fori_loop(0, nc, scan_b, jnp.zeros((1, 128), _F32))

    def emit(c, carry):
        r0 = pl.multiple_of(c * ch, ch)
        mprev = mprev_scr[pl.ds(c, 1), :]
        tot = tot_scr[pl.ds(c, 1), :]
        mnew = jnp.maximum(tot + mprev, mloc_scr[pl.ds(c, 1), :])
        keep_ref[pl.ds(c, 1), :] = jnp.exp(tot + mprev - mnew)
        g = jnp.maximum(mprev, pm_scr[pl.ds(r0, ch), :])
        terms = (_split3(-g * _LOG2E) + _split3((mprev - g) * _LOG2E)
                 + _split3((-b_scr[pl.ds(r0, ch), :] - g) * _LOG2E)
                 + _split3((a_scr[pl.ds(r0, ch), :] - mnew) * _LOG2E) + 3 * (jnp.ones((ch, 128), _F32),))
        pack = jnp.where(live, terms[0], 0.0)
        for t in range(1, len(terms)):
            pack = pack + pltpu.roll(jnp.where(live, terms[t], 0.0), _GATE_LANES * t, axis=1)
        a_ref[pl.ds(r0, ch), :] = pack.astype(_BF16)
        return carry

    lax.fori_loop(0, nc, emit, 0, unroll=4)


def _gate_prep(g3, bias):
    b, l, _ = g3.shape
    nc = l // MLSTM_CHUNK
    body = functools.partial(_gate_prep_body, nc=nc)
    return pl.pallas_call(
        body,
        grid=(b,),
        in_specs=[
            pl.BlockSpec((None, l, 128), lambda i: (i, 0, 0)),
            pl.BlockSpec((1, 128), lambda i: (0, 0)),
        ],
        out_specs=[
            pl.BlockSpec((None, l, 128), lambda i: (i, 0, 0)),
            pl.BlockSpec((None, 128, l), lambda i: (i, 0, 0)),
            pl.BlockSpec((None, nc, 128), lambda i: (i, 0, 0)),
        ],
        out_shape=[
            jax.ShapeDtypeStruct((b, l, 128), _BF16),
            jax.ShapeDtypeStruct((b, 128, l), _BF16),
            jax.ShapeDtypeStruct((b, nc, 128), _F32),
        ],
        scratch_shapes=[
            pltpu.VMEM((l, 128), _F32), pltpu.VMEM((l, 128), _F32), pltpu.VMEM((l, 128), _F32),
            pltpu.VMEM((nc, 128), _F32), pltpu.VMEM((nc, 128), _F32), pltpu.VMEM((nc, 128), _F32),
        ],
        compiler_params=_cparams(("arbitrary",)),
        name="gate_prep",
    )(g3, bias)


def _gate_spread_matrix():
    bc = np.zeros((128, 4 * 128), np.float32)
    for blk in range(4):
        for t in range(3 * blk, 3 * blk + 3):
            bc[t * _GATE_LANES:(t + 1) * _GATE_LANES, blk * 128:(blk + 1) * 128] = 1.0
    return jnp.asarray(bc).astype(_BF16)


def _mlstm_body(keep_ref, bc_ref, qf_ref, kf_ref, vf_ref, af_ref, rf_ref, qb_ref, kb_ref, vb_ref, ab_ref,
                rb_ref, hf_ref, hb_ref, ct_ref, nm_ref, *, nc):
    bi, c = pl.program_id(0), pl.program_id(1)

    @pl.when(c == 0)
    def _():
        ct_ref[...] = jnp.zeros_like(ct_ref)
        nm_ref[...] = jnp.zeros_like(nm_ref)

    ch, dh = MLSTM_CHUNK, MLSTM_HEAD_DIM
    jj = lax.broadcasted_iota(jnp.int32, (ch, ch), 0)
    ss = lax.broadcasted_iota(jnp.int32, (ch, ch), 1)
    head_lane = ss % _GATE_LANES
    ones_rhs = jnp.ones((ch, 128), _BF16)
    twice = lambda a: jnp.concatenate([a, a], axis=1)
    dirs = ((qf_ref, kf_ref, vf_ref, af_ref, rf_ref, hf_ref, ss <= jj),
            (qb_ref, kb_ref, vb_ref, ab_ref, rb_ref, hb_ref, ss >= jj))
    for sub, d in [(sub, d) for sub in range(_MLSTM_SUB) for d in range(2)]:
        q_blk, k_blk, v_blk, a_blk, r_blk, o_blk, mask = dirs[d]
        pos = sub if d == 0 else _MLSTM_SUB - 1 - sub
        chunk = c * _MLSTM_SUB + pos if d == 0 else nc - (c + 1) * _MLSTM_SUB + pos
        rows = slice(pos * ch, (pos + 1) * ch)
        q_ref, k_ref, v_ref, o_ref = q_blk.at[rows], k_blk.at[rows], v_blk.at[rows], o_blk.at[rows]
        bmat = jnp.concatenate([bc_ref[:, 0:128] + r_blk[:, rows], bc_ref[:, 128:]], axis=1)
        a_all = a_blk[rows, :].astype(_F32)
        kbase = (bi * nc + chunk) * _GATE_LANES
        for h in range(MLSTM_HEADS):
            hd = d * MLSTM_HEADS + h
            a_h = jnp.where(head_lane == hd, a_all, 0.0).astype(_BF16)
            e = jnp.dot(a_h, bmat, preferred_element_type=_F32)
            decay = jnp.exp2(jnp.where(mask, e[:, 0:128], -jnp.inf))
            rest = jnp.exp2(e[:, 128:])
            iw, clamp, wrep = rest[:, 0:128], rest[:, 128:256], rest[:, 256:384]
            keep = keep_ref[kbase + hd]
            lo, hi = h * dh, (h + 1) * dh
            qh, kh, vh = q_ref[:, lo:hi], k_ref[:, lo:hi], v_ref[:, lo:hi]
            p = (_dot_nt(qh, kh) * decay).astype(_BF16)
            ct, nm = ct_ref[hd], nm_ref[hd]
            pv = jnp.dot(p, jnp.concatenate([vh, ones_rhs], axis=1), preferred_element_type=_F32)
            qc = jnp.dot(qh, jnp.concatenate([ct, nm], axis=1).astype(_BF16), preferred_element_type=_F32)
            num = pv[:, :dh] + twice(iw) * qc[:, :dh]
            den = pv[:, dh:] + iw * qc[:, dh:]
            inv = 1.0 / jnp.maximum(jnp.abs(den), clamp)
            o_ref[:, lo:hi] = (num * twice(inv)).astype(o_ref.dtype)
            vw = jnp.concatenate([(vh.astype(_F32) * twice(wrep)).astype(_BF16), wrep.astype(_BF16)], axis=1)
            upd = lax.dot_general(kh, vw, (((0,), (0,)), ((), ())), preferred_element_type=_F32)
            ct_ref[hd] = keep * ct + upd[:, :dh]
            nm_ref[hd] = keep * nm + upd[:, dh:]


def _mlstm(qk, proj3, a_mat, r_rows, keep):
    b, l, _ = qk.shape
    mw = MLSTM_WIDTH
    nc = l // MLSTM_CHUNK
    rows = _MLSTM_SUB * MLSTM_CHUNK
    nb = l // rows
    fw = lambda c: c
    bw = lambda c: nb - 1 - c

    def specs(cm):
        return [
            pl.BlockSpec((None, rows, mw), lambda i, c: (i, cm(c), 0)),
            pl.BlockSpec((None, rows, mw), lambda i, c: (i, cm(c), 1)),
            pl.BlockSpec((None, rows, mw), lambda i, c: (i, cm(c), _COL_V)),
            pl.BlockSpec((None, rows, 128), lambda i, c: (i, cm(c), 0)),
            pl.BlockSpec((None, 128, rows), lambda i, c: (i, 0, cm(c))),
        ]

    body = functools.partial(_mlstm_body, nc=nc)
    keep_flat = keep[:, :, :_GATE_LANES].reshape(b * nc * _GATE_LANES)
    return pl.pallas_call(
        body,
        grid=(b, nb),
        in_specs=[pl.BlockSpec(memory_space=pltpu.SMEM),
                  pl.BlockSpec((128, 4 * 128), lambda i, c: (0, 0))] + specs(fw) + specs(bw),
        out_specs=[
            pl.BlockSpec((None, rows, mw), lambda i, c: (i, c, 0)),
            pl.BlockSpec((None, rows, mw), lambda i, c: (i, nb - 1 - c, 0)),
        ],
        out_shape=[jax.ShapeDtypeStruct((b, l, mw), _BF16), jax.ShapeDtypeStruct((b, l, mw), _BF16)],
        scratch_shapes=[
            pltpu.VMEM((2 * MLSTM_HEADS, MLSTM_HEAD_DIM, MLSTM_HEAD_DIM), _F32),
            pltpu.VMEM((2 * MLSTM_HEADS, MLSTM_HEAD_DIM, 128), _F32),
        ],
        compiler_params=_cparams(("arbitrary", "arbitrary")),
        name="mlstm",
    )(keep_flat, _gate_spread_matrix(), qk, qk, proj3, a_mat, r_rows, qk, qk, proj3, a_mat, r_rows)


def _sigmoid(x):
    return 1.0 / (1.0 + jnp.exp(-x))


def _merge_body(x_ref, ya_ref, hf_ref, hb_ref, o_ref, ma_ref, mb_ref, wa_ref, wb_ref, wo_ref, out_ref):
    hsum = hf_ref[...].astype(_F32) + hb_ref[...].astype(_F32)
    yb = (_sigmoid(o_ref[...].astype(_F32)) * hsum).astype(_BF16)
    ya = jnp.concatenate([ya_ref[t] for t in range(ya_ref.shape[0])], axis=1)
    pa = jnp.dot(ya, wa_ref[...], preferred_element_type=_F32)
    pb = jnp.dot(yb, wb_ref[...], preferred_element_type=_F32)
    mixed = _sigmoid(ma_ref[...].astype(_F32)) * pa + _sigmoid(mb_ref[...].astype(_F32)) * pb
    out_ref[...] = x_ref[...] + jnp.dot(mixed.astype(_BF16), wo_ref[...], preferred_element_type=_F32)


def _merge(x2, ya3, hf2, hb2, proj, wa, wb, wo):
    m = x2.shape[0]
    tm = 256
    d, mw = D_MODEL, MLSTM_WIDTH
    const = lambda shape: pl.BlockSpec(shape, lambda i: (0, 0), pipeline_mode=pl.Buffered(1))
    return pl.pallas_call(
        _merge_body,
        grid=(m // tm,),
        in_specs=[
            pl.BlockSpec((tm, d), lambda i: (i, 0)),
            pl.BlockSpec((ya3.shape[0], tm, ya3.shape[2]), lambda i: (0, i, 0)),
            pl.BlockSpec((tm, mw), lambda i: (i, 0)),
            pl.BlockSpec((tm, mw), lambda i: (i, 0)),
            pl.BlockSpec((tm, mw), lambda i: (i, _COL_O)),
            pl.BlockSpec((tm, d), lambda i: (i, _COL_MA // 2)),
            pl.BlockSpec((tm, d), lambda i: (i, _COL_MB // 2)),
            const((HYENA_WIDTH, d)), const((mw, d)), const((d, d)),
        ],
        out_specs=pl.BlockSpec((tm, d), lambda i: (i, 0)),
        out_shape=jax.ShapeDtypeStruct((m, d), _F32),
        compiler_params=_cparams(("arbitrary",)),
        name="merge",
    )(x2, ya3, hf2, hb2, proj, proj, proj, wa, wb, wo)


def _ffn_body(x_ref, n2_ref, wg_ref, wu_ref, wd_ref, nf_ref, o_ref, hn_ref):
    f = pl.program_id(1)
    last = pl.num_programs(1) - 1
    rows = 256

    def swiglu_down(hn):
        g = jnp.dot(hn, wg_ref[...], preferred_element_type=_F32)
        u = jnp.dot(hn, wu_ref[...], preferred_element_type=_F32)
        a = (g * _sigmoid(g) * u).astype(_BF16)
        return jnp.dot(a, wd_ref[...], preferred_element_type=_F32)

    @pl.when(f == 0)
    def _():
        for r in range(0, x_ref.shape[0], rows):
            x = x_ref[r:r + rows, :]
            hn = (x * lax.rsqrt(jnp.mean(x * x, axis=-1, keepdims=True) + RMS_EPS) * n2_ref[...]).astype(_BF16)
            hn_ref[r:r + rows, :] = hn
            o_ref[r:r + rows, :] = x + swiglu_down(hn)

    @pl.when(jnp.logical_and(f > 0, f < last))
    def _():
        o_ref[...] += swiglu_down(hn_ref[...])

    @pl.when(f == last)
    def _():
        for r in range(0, x_ref.shape[0], rows):
            y = o_ref[r:r + rows, :] + swiglu_down(hn_ref[r:r + rows, :])
            o_ref[r:r + rows, :] = y * lax.rsqrt(jnp.mean(y * y, axis=-1, keepdims=True) + RMS_EPS) * nf_ref[...]


def _ffn(x2, norm2_w, w_gate_up, w_down, norm_f_w):
    m = x2.shape[0]
    tm, tf = 1024, 512
    d = D_MODEL
    nf = FFN_HIDDEN // tf
    return pl.pallas_call(
        _ffn_body,
        grid=(m // tm, nf),
        in_specs=[
            pl.BlockSpec((tm, d), lambda i, f: (i, 0)),
            pl.BlockSpec((1, d), lambda i, f: (0, 0)),
            pl.BlockSpec((d, tf), lambda i, f: (0, f)),
            pl.BlockSpec((d, tf), lambda i, f: (0, nf + f)),
            pl.BlockSpec((tf, d), lambda i, f: (f, 0)),
            pl.BlockSpec((1, d), lambda i, f: (0, 0)),
        ],
        out_specs=pl.BlockSpec((tm, d), lambda i, f: (i, 0)),
        out_shape=jax.ShapeDtypeStruct((m, d), _F32),
        scratch_shapes=[pltpu.VMEM((tm, d), _BF16)],
        compiler_params=_cparams(("arbitrary", "arbitrary")),
        name="ffn",
    )(x2, norm2_w, w_gate_up, w_gate_up, w_down, norm_f_w)


def kernel(x, norm1_w, w_in, conv_w, conv_b, filt_w1, filt_b1, filt_freq1, filt_w2, filt_b2, filt_freq2,
           filt_w3, hyena_bias, mlstm_gate_bias, w_branch_a, w_branch_b, w_out, norm2_w, w_gate_up, w_down,
           norm_f_w):
    b, l, d = x.shape
    assert d == D_MODEL and b % 2 == 0 and l % (_N2 * _T2_PER_PHASE) == 0
    assert norm1_w.shape[0] == 1, "single-layer block"
    hw, mw, nh = HYENA_WIDTH, MLSTM_WIDTH, MLSTM_HEADS
    m = b * l
    n = 2 * l
    n1 = n // _N2
    sc_cols = 3 * hw + 2 * mw
    g0 = sc_cols + 2 * mw

    w_t = jnp.swapaxes(w_in[0], 0, 1)
    w_all_t = w_t.astype(_BF16)
    w_merge_t = w_all_t[g0 + 4 * nh:]
    wg_t = w_t[g0:g0 + 4 * nh]
    gate_order = lambda a: jnp.concatenate(
        [a[0:nh], a[2 * nh:3 * nh], a[nh:2 * nh], a[3 * nh:4 * nh]], axis=0)
    wg_split_t = jnp.concatenate(_split_bf16(jnp.pad(gate_order(wg_t), ((0, 128 - 4 * nh), (0, 0)))), axis=0)
    gate_bias = jnp.pad(gate_order(mlstm_gate_bias[0].astype(_F32).reshape(4 * nh))[None, :],
                        ((0, 0), (0, 128 - 4 * nh)))

    x2 = x.reshape(m, d)
    proj, gates = _inproj(x2, norm1_w[0][None, :], w_merge_t, w_all_t, wg_split_t)
    proj3 = proj.reshape(b, l, _MAIN_COLS)

    cw, cb = conv_w[0], conv_b[0][None, :]
    hy5 = _shortconv(proj3, cw, cb, _COL_HV * 1024, 0, 3 * hw, True)
    qk = _shortconv(proj3, cw, cb, _COL_Q * 1024, 3 * hw, 2 * mw, False)

    fa, fa_full, fai, fb, fbi = _dft_tables(n)
    zpos = _filter_positions(l)
    w1p = jnp.pad(filt_w1[0].astype(_F32), ((0, 128 - HYENA_EMB), (0, 0)))
    max_decay = math.log(HYENA_TARGET) / HYENA_FAST_DECAY
    min_decay = math.log(HYENA_TARGET) / HYENA_SLOW_DECAY
    dabs = jnp.asarray(np.abs(np.linspace(min_decay, max_decay, hw, dtype=np.float32))[None, :])
    dup = lambda a: jnp.concatenate([a, a], axis=-1)
    fh = HYENA_FILTER_HIDDEN
    w3 = filt_w3[0].astype(_F32).reshape(fh, 2, 2, hw).transpose(2, 0, 1, 3).reshape(2 * fh, 2 * hw)
    w3_hi, w3_lo = _split_bf16(w3)
    blockdiag = lambda a: jnp.concatenate(
        [jnp.concatenate([a, jnp.zeros_like(a)], axis=1), jnp.concatenate([jnp.zeros_like(a), a], axis=1)], axis=0)
    kern = _filt_mlp(zpos, blockdiag(w1p), dup(filt_b1[0][None, :]), dup(filt_freq1[0][None, :]),
                     blockdiag(filt_w2[0].astype(_F32)), dup(filt_b2[0][None, :]),
                     dup(filt_freq2[0][None, :]), w3_hi, w3_lo, dabs)
    tiles = hw // _FFT_CT
    kspec = _filt_fft(kern.reshape(2 * tiles, _N2, n1, _FFT_CT), fa_full, fb)
    tabs = (fa, fai, fb, fbi)
    hbias = hyena_bias[0].astype(_F32)
    z5 = _hyena_conv(hy5, 0, hy5, tiles, kspec, 0, hbias[0][None, :], tabs, False)
    ya5 = _hyena_conv(z5, 0, hy5, 2 * tiles, kspec, tiles, hbias[1][None, :], tabs, True)
    ya3 = ya5.reshape(tiles, m, _FFT_CT)

    a_mat, r_rows, keep = _gate_prep(gates.reshape(b, l, 128), gate_bias)
    hf, hb = _mlstm(qk, proj3, a_mat, r_rows, keep)

    x_mid = _merge(x2, ya3, hf.reshape(m, mw), hb.reshape(m, mw), proj,
                   w_branch_a[0].astype(_BF16), w_branch_b[0].astype(_BF16), w_out[0].astype(_BF16))
    out = _ffn(x_mid, norm2_w[0][None, :], w_gate_up[0].astype(_BF16), w_down[0].astype(_BF16),
               norm_f_w[None, :])
    return out.reshape(b, l, d)
```

```python
import functools
import math

import numpy as np
import jax
import jax.numpy as jnp
from jax import lax
from jax.experimental import pallas as pl
from jax.experimental.pallas import tpu as pltpu

_F32 = jnp.float32
_BF16 = jnp.bfloat16
_HIGHEST = lax.Precision.HIGHEST

D_MODEL = 2048
HYENA_WIDTH = 1024
HYENA_EMB = 33
HYENA_FILTER_HIDDEN = 64
HYENA_FAST_DECAY = 0.3
HYENA_SLOW_DECAY = 1.5
HYENA_TARGET = 1e-2
MLSTM_WIDTH = 1024
MLSTM_HEADS = 4
MLSTM_HEAD_DIM = 256
MLSTM_CHUNK = 128
FFN_HIDDEN = 5632
RMS_EPS = 1e-6

_COL_MA, _COL_MB, _COL_HV, _COL_Q, _COL_V, _COL_O = 0, 2, 4, 7, 9, 10
_MAIN_COLS = 11 * 1024

_N2 = 32
_T2_PER_PHASE = 16
_F1_PER_PHASE = 64
_V7X_MXU_WIDTH = 256
_V7X_VMEM_BYTES = 64 * 1024 * 1024
_FFT_CT = _V7X_MXU_WIDTH

_INPROJ_TILE = (1024, 1024)
_FFN_TILE = (1024, 512)
_MERGE_ROWS = 256
_NORM_ROWS = 256

_VMEM_LIMIT = _V7X_VMEM_BYTES * 7 // 8


def _cparams(sem, vmem=_VMEM_LIMIT):
    return pltpu.CompilerParams(dimension_semantics=sem, vmem_limit_bytes=vmem)


def _split_bf16(a):
    hi = a.astype(_BF16)
    lo = (a - hi.astype(_F32)).astype(_BF16)
    return hi, lo


def _dot_nt(a, b):
    return lax.dot_general(a, b, (((1,), (1,)), ((), ())), preferred_element_type=_F32)


def _dot3(a_hi, a_lo, b_hi, b_lo):
    d = functools.partial(jnp.dot, preferred_element_type=_F32)
    return d(a_hi, b_hi) + (d(a_hi, b_lo) + d(a_lo, b_hi))


def _inproj_body(x_ref, nw_ref, wm_ref, w_ref, wg_ref, o_ref, g_ref, hn_ref, *, n_merge):
    j = pl.program_id(1)

    @pl.when(j == 0)
    def _():
        rows = _NORM_ROWS
        for r in range(0, x_ref.shape[0], rows):
            x = x_ref[r:r + rows, :]
            hn = x * lax.rsqrt(jnp.mean(x * x, axis=-1, keepdims=True) + RMS_EPS) * nw_ref[...]
            hn_hi, hn_lo = _split_bf16(hn)
            hn_ref[r:r + rows, :] = hn_hi
            hh_hl = _dot_nt(hn_hi, wg_ref[...])
            g_ref[r:r + rows, :] = hh_hl[:, :128] + (hh_hl[:, 128:] + _dot_nt(hn_lo, wg_ref[0:128, :]))
            o_ref[r:r + rows, :] = _dot_nt(hn_hi, wm_ref[...]).astype(o_ref.dtype)

    @pl.when(jnp.logical_and(j > 0, j < n_merge))
    def _():
        o_ref[...] = _dot_nt(hn_ref[...], wm_ref[...]).astype(o_ref.dtype)

    @pl.when(j >= n_merge)
    def _():
        o_ref[...] = _dot_nt(hn_ref[...], w_ref[...]).astype(o_ref.dtype)


def _inproj(x2, norm_w, w_merge_t, w_all_t, wg_t):
    m = x2.shape[0]
    tm, tn = _INPROJ_TILE
    n_merge = w_merge_t.shape[0] // tn
    body = functools.partial(_inproj_body, n_merge=n_merge)
    return pl.pallas_call(
        body,
        grid=(m // tm, _MAIN_COLS // tn),
        in_specs=[
            pl.BlockSpec((tm, D_MODEL), lambda i, j: (i, 0)),
            pl.BlockSpec((1, D_MODEL), lambda i, j: (0, 0)),
            pl.BlockSpec((tn, D_MODEL), lambda i, j: (jnp.minimum(j, n_merge - 1), 0)),
            pl.BlockSpec((tn, D_MODEL), lambda i, j: (jnp.maximum(j - n_merge, 0), 0)),
            pl.BlockSpec((256, D_MODEL), lambda i, j: (0, 0)),
        ],
        out_specs=[
            pl.BlockSpec((tm, tn), lambda i, j: (i, j)),
            pl.BlockSpec((tm, 128), lambda i, j: (i, 0)),
        ],
        out_shape=[
            jax.ShapeDtypeStruct((m, _MAIN_COLS), _BF16),
            jax.ShapeDtypeStruct((m, 128), _F32),
        ],
        scratch_shapes=[pltpu.VMEM((tm, D_MODEL), _BF16)],
        compiler_params=_cparams(("arbitrary", "arbitrary")),
        name="inproj",
    )(x2, norm_w, w_merge_t, w_all_t, wg_t)


def _shortconv_silu_body(u_ref, w_ref, b_ref, o_ref, *, k_scale):
    u = u_ref[...].astype(_F32)
    n = u.shape[0]
    row = lax.broadcasted_iota(jnp.int32, u.shape, 0)
    up = jnp.where(row == 0, 0.0, pltpu.roll(u, 1, axis=0))
    un = jnp.where(row == n - 1, 0.0, pltpu.roll(u, n - 1, axis=0))
    w = w_ref[...]
    y = up * w[0:1, :] + b_ref[...] + u * w[1:2, :] + un * w[2:3, :]
    y = y * (1.0 / (1.0 + jnp.exp(-y)))
    is_k = pl.program_id(1) >= (MLSTM_WIDTH // u.shape[1])
    o_ref[...] = (y * jnp.where(is_k, k_scale, 1.0)).astype(o_ref.dtype)


def _shortconv_t2_body(u_ref, w_ref, b_ref, o_ref, ut_ref):
    n, c = u_ref.shape
    h1 = n // _N2
    g = 16
    for k in range(h1 // g):
        blk = u_ref[k * g * _N2:(k + 1) * g * _N2, :].reshape(g, _N2, c)
        ut_ref[:, k * g:(k + 1) * g, :] = jnp.swapaxes(blk, 0, 1)
    w = w_ref[...]
    w0, w1, w2, bias = w[0:1, :], w[1:2, :], w[2:3, :], b_ref[...]
    slab = lambda t2: ut_ref[t2].astype(_F32)
    row = lax.broadcasted_iota(jnp.int32, (h1, c), 0)
    before_first = jnp.where(row == 0, 0.0, pltpu.roll(slab(_N2 - 1), 1, axis=0))
    after_last = jnp.where(row == h1 - 1, 0.0, pltpu.roll(slab(0), h1 - 1, axis=0))
    for t2 in range(_N2):
        up = before_first if t2 == 0 else slab(t2 - 1)
        un = after_last if t2 == _N2 - 1 else slab(t2 + 1)
        o_ref[t2] = (up * w0 + bias + slab(t2) * w1 + un * w2).astype(o_ref.dtype)


def _shortconv(proj3, conv_w, conv_b, proj_col0, conv_col0, ncols, t2_major):
    b, l, _ = proj3.shape
    ct = _FFT_CT
    p0, c0 = proj_col0 // ct, conv_col0 // ct
    if t2_major:
        body = _shortconv_t2_body
        out_spec = pl.BlockSpec((None, None, _N2, l // _N2, ct), lambda i, j: (i, j, 0, 0, 0))
        out_shape = jax.ShapeDtypeStruct((b, ncols // ct, _N2, l // _N2, ct), _BF16)
        scratch = [pltpu.VMEM((_N2, l // _N2, ct), _BF16)]
    else:
        body = functools.partial(_shortconv_silu_body, k_scale=MLSTM_HEAD_DIM ** -0.5)
        out_spec = pl.BlockSpec((None, l, ct), lambda i, j: (i, 0, j))
        out_shape = jax.ShapeDtypeStruct((b, l, ncols), _BF16)
        scratch = []
    return pl.pallas_call(
        body,
        grid=(b, ncols // ct),
        in_specs=[
            pl.BlockSpec((None, l, ct), lambda i, j: (i, 0, p0 + j)),
            pl.BlockSpec((3, ct), lambda i, j: (0, c0 + j)),
            pl.BlockSpec((1, ct), lambda i, j: (0, c0 + j)),
        ],
        out_specs=out_spec,
        out_shape=out_shape,
        scratch_shapes=scratch,
        compiler_params=_cparams(("arbitrary", "arbitrary")),
        name="shortconv" if t2_major else "shortconv_silu",
    )(proj3, conv_w, conv_b)


def _dft_tables(n):
    n2 = _N2
    n1 = n // n2
    h1 = n1 // 2
    f1 = np.arange(n1)[:, None]
    t1 = np.arange(n1)[None, :]
    ang = 2.0 * np.pi * ((f1 * t1) % n1) / n1
    c, s = np.cos(ang), np.sin(ang)
    ch, sh = c[:, :h1], s[:, :h1]
    fa = np.block([[ch, sh], [-sh, ch]])
    fa_full = np.concatenate([c, -s], axis=0)
    fai = np.block([[ch.T, -sh.T], [sh.T, ch.T]]) / n
    f1v = np.arange(n1)[:, None, None]
    f2 = np.arange(n2)[None, :, None]
    t2 = np.arange(n2)[None, None, :]
    ph = 2.0 * np.pi * ((f1v * t2 + f2 * t2 * n1) % n) / n
    tc, ts = np.cos(ph), np.sin(ph)
    fb = np.concatenate([np.concatenate([tc, ts], axis=2),
                         np.concatenate([-ts, tc], axis=2)], axis=1)
    tct, tst = np.swapaxes(tc, 1, 2), np.swapaxes(ts, 1, 2)
    fbi = np.concatenate([np.concatenate([tct, -tst], axis=2),
                          np.concatenate([tst, tct], axis=2)], axis=1)
    to = lambda a: jnp.asarray(a.astype(np.float32)).astype(_BF16)
    return to(fa), to(fa_full), to(fai), to(fb), to(fbi)


def _filter_positions(l):
    n = 2 * l
    n1 = n // _N2
    r = (np.arange(_N2)[:, None] + _N2 * np.arange(n1)[None, :]).reshape(-1)
    pos = np.where(r < l, r, n - r)
    pos = np.where(r == l, 0, pos)
    bands = (HYENA_EMB - 1) // 2
    tt = np.linspace(0.0, 1.0, l, dtype=np.float32).astype(np.float64)[pos]
    omega = (2.0 * math.pi * np.arange(l, dtype=np.float32) / l).astype(np.float32)
    freqs = np.linspace(1e-4, bands - 1, bands, dtype=np.float32)
    ang = (omega[:, None] * freqs[None, :]).astype(np.float64)[pos]
    z = np.zeros((n, 128), np.float64)
    z[:, 0] = tt
    z[:, 1:1 + bands] = np.cos(ang)
    z[:, 1 + bands:1 + 2 * bands] = -np.sin(ang)
    z[:, 33] = (r != l)
    z[:, 34] = (r < l)
    return jnp.asarray(z.astype(np.float32))


def _filt_mlp_body(z_ref, zp_ref, w1_ref, b1_ref, q1_ref, w2_ref, b2_ref, q2_ref, w3h_ref, w3l_ref, dabs_ref,
                   o_ref, hh_ref, hl_ref):
    fh = HYENA_FILTER_HIDDEN

    @pl.when(pl.program_id(1) == 0)
    def _():
        hp = jnp.sin(q1_ref[...] * (jnp.dot(zp_ref[...], w1_ref[...], precision=_HIGHEST,
                                            preferred_element_type=_F32) + b1_ref[...]))
        hp = jnp.sin(q2_ref[...] * (jnp.dot(hp, w2_ref[...], precision=_HIGHEST,
                                            preferred_element_type=_F32) + b2_ref[...]))
        lane_p = lax.broadcasted_iota(jnp.int32, hp.shape, 1)
        swapped = pltpu.roll(hp, fh, axis=1)
        h = jnp.concatenate([jnp.where(lane_p < fh, hp, swapped), jnp.where(lane_p < fh, swapped, hp)], axis=0)
        z = z_ref[...]
        valid = z[:, 33:34]
        fwd = z[:, 34:35]
        lane = lax.broadcasted_iota(jnp.int32, h.shape, 1)
        hi, lo = _split_bf16(h * (valid * jnp.where(lane < fh, fwd, 1.0 - fwd)))
        hh_ref[...] = hi
        hl_ref[...] = lo

    raw = _dot3(hh_ref[...], hl_ref[...], w3h_ref[...], w3l_ref[...])
    window = jnp.exp(-z_ref[:, 0:1] * dabs_ref[...])
    o_ref[...] = (raw * window).astype(o_ref.dtype)


_FILT_ROWS = 1024


def _filt_mlp(zpos, w1bd, b1d, q1d, w2bd, b2d, q2d, w3h, w3l, dabs):
    n = zpos.shape[0]
    ct = _FFT_CT
    tr = _FILT_ROWS
    per_order = HYENA_WIDTH // ct
    fh = HYENA_FILTER_HIDDEN
    zpacked = zpos.reshape(n // tr, 2, tr // 2, 128).transpose(0, 2, 1, 3).reshape(n // 2, 256)
    full = lambda shape: pl.BlockSpec(shape, lambda r, i: (0,) * len(shape))
    return pl.pallas_call(
        _filt_mlp_body,
        grid=(n // tr, 2 * per_order),
        in_specs=[
            pl.BlockSpec((tr, 128), lambda r, i: (r, 0)),
            pl.BlockSpec((tr // 2, 256), lambda r, i: (r, 0)),
            full((256, 2 * fh)), full((1, 2 * fh)), full((1, 2 * fh)),
            full((2 * fh, 2 * fh)), full((1, 2 * fh)), full((1, 2 * fh)),
            pl.BlockSpec((2 * fh, ct), lambda r, i: (0, i)),
            pl.BlockSpec((2 * fh, ct), lambda r, i: (0, i)),
            pl.BlockSpec((1, ct), lambda r, i: (0, i % per_order)),
        ],
        out_specs=pl.BlockSpec((None, tr, ct), lambda r, i: (i, r, 0)),
        out_shape=jax.ShapeDtypeStruct((2 * per_order, n, ct), _BF16),
        scratch_shapes=[pltpu.VMEM((tr, 2 * fh), _BF16), pltpu.VMEM((tr, 2 * fh), _BF16)],
        compiler_params=_cparams(("arbitrary", "arbitrary")),
        name="filt_mlp",
    )(zpos, zpacked, w1bd, b1d, q1d, w2bd, b2d, q2d, w3h, w3l, dabs)


_F1_GROUP = 16


def _f1_tiles(f_base, f1pp):
    return [(slice(g * _F1_GROUP, (g + 1) * _F1_GROUP), pl.multiple_of(f_base + g * _F1_GROUP, _F1_GROUP))
            for g in range(f1pp // _F1_GROUP)]


def _gather_f1(s_ref, vre_ref, vim_ref, f_base, f1pp, n1):
    for sl, f0 in _f1_tiles(f_base, f1pp):
        vre_ref[sl] = jnp.swapaxes(s_ref[:, pl.ds(f0, _F1_GROUP), :], 0, 1)
        vim_ref[sl] = jnp.swapaxes(s_ref[:, pl.ds(n1 + f0, _F1_GROUP), :], 0, 1)


def _filt_fft_body(kern_ref, fa_ref, fb_ref, o_ref, s_ref, vre_ref, vim_ref, *, n_in, n1, f1pp):
    s = pl.program_id(1)

    @pl.when(s < n_in)
    def _():
        for i in range(_T2_PER_PHASE):
            s_ref[s * _T2_PER_PHASE + i] = jnp.dot(fa_ref[...], kern_ref[i],
                                                   preferred_element_type=_F32).astype(_BF16)

    @pl.when(s >= n_in)
    def _():
        _gather_f1(s_ref, vre_ref, vim_ref, (s - n_in) * f1pp, f1pp, n1)

        for f in range(f1pp):
            v = jnp.concatenate([vre_ref[f], vim_ref[f]], axis=0)
            o_ref[f * 2 * _N2:(f + 1) * 2 * _N2, :] = jnp.dot(
                fb_ref[f], v, preferred_element_type=_F32).astype(o_ref.dtype)


def _filt_fft(kern4, fa_full, fb):
    tiles, n2, n1, ct = kern4.shape
    n_in = n2 // _T2_PER_PHASE
    f1pp = min(_F1_PER_PHASE, n1)
    n_mid = n1 // f1pp
    body = functools.partial(_filt_fft_body, n_in=n_in, n1=n1, f1pp=f1pp)
    return pl.pallas_call(
        body,
        grid=(tiles, n_in + n_mid),
        in_specs=[
            pl.BlockSpec((None, _T2_PER_PHASE, n1, ct), lambda j, s: (j, jnp.minimum(s, n_in - 1), 0, 0)),
            pl.BlockSpec((2 * n1, n1), lambda j, s: (0, 0)),
            pl.BlockSpec((f1pp, 2 * n2, 2 * n2), lambda j, s: (jnp.maximum(s - n_in, 0), 0, 0)),
        ],
        out_specs=pl.BlockSpec((None, f1pp * 2 * n2, ct), lambda j, s: (j, jnp.maximum(s - n_in, 0), 0)),
        out_shape=jax.ShapeDtypeStruct((tiles, n1 * 2 * n2, ct), _BF16),
        scratch_shapes=[pltpu.VMEM((n2, 2 * n1, ct), _BF16),
                        pltpu.VMEM((f1pp, n2, ct), _BF16), pltpu.VMEM((f1pp, n2, ct), _BF16)],
        compiler_params=_cparams(("arbitrary", "arbitrary")),
        name="filt_fft",
    )(kern4, fa_full, fb)


def _hyena_conv_body(u_ref, g_ref, k_ref, bias_ref, fa_ref, fai_ref, fb_ref, fbi_ref,
                     o_ref, s_ref, vre_ref, vim_ref, wre_ref, wim_ref, y_ref, ukeep_ref,
                     *, n_in, n_mid, n1, f1pp, natural_out):
    s = pl.program_id(2)
    h1 = n1 // 2

    @pl.when(s < n_in)
    def _():
        for i in range(_T2_PER_PHASE):
            u = jnp.concatenate([u_ref[0, i], u_ref[1, i]], axis=0)
            ukeep_ref[s * _T2_PER_PHASE + i] = u
            s_ref[s * _T2_PER_PHASE + i] = jnp.dot(fa_ref[...], u,
                                                   preferred_element_type=_F32).astype(_BF16)

    @pl.when(jnp.logical_and(s >= n_in, s < n_in + n_mid))
    def _():
        tiles = _f1_tiles((s - n_in) * f1pp, f1pp)

        def gather(sl, f0):
            vre_ref[sl] = jnp.swapaxes(s_ref[:, pl.ds(f0, _F1_GROUP), :], 0, 1)
            vim_ref[sl] = jnp.swapaxes(s_ref[:, pl.ds(n1 + f0, _F1_GROUP), :], 0, 1)

        def forward(f):
            v = jnp.concatenate([vre_ref[f], vim_ref[f]], axis=0)
            x = jnp.dot(fb_ref[f], v, preferred_element_type=_F32)
            xre, xim = x[:_N2], x[_N2:]
            kre = k_ref[f * 2 * _N2:f * 2 * _N2 + _N2, :].astype(_F32)
            kim = k_ref[f * 2 * _N2 + _N2:(f + 1) * 2 * _N2, :].astype(_F32)
            y_ref[f] = jnp.concatenate([xre * kre - xim * kim, xre * kim + xim * kre],
                                       axis=0).astype(_BF16)

        def inverse(f):
            w = jnp.dot(fbi_ref[f], y_ref[f], preferred_element_type=_F32)
            wre_ref[f] = w[:_N2].astype(_BF16)
            wim_ref[f] = w[_N2:].astype(_BF16)

        def scatter(sl, f0):
            s_ref[:, pl.ds(f0, _F1_GROUP), :] = jnp.swapaxes(wre_ref[sl], 0, 1)
            s_ref[:, pl.ds(n1 + f0, _F1_GROUP), :] = jnp.swapaxes(wim_ref[sl], 0, 1)

        nt = len(tiles)
        for t in tiles:
            gather(*t)
        for g in range(nt + 1):
            for k in range(_F1_GROUP):
                if g < nt:
                    forward(g * _F1_GROUP + k)
                if g >= 1:
                    inverse((g - 1) * _F1_GROUP + k)
            if g >= 1:
                scatter(*tiles[g - 1])

    @pl.when(s >= n_in + n_mid)
    def _():
        bias = bias_ref[...]
        outs = [[], []]
        for i in range(_T2_PER_PHASE):
            t2 = (s - n_in - n_mid) * _T2_PER_PHASE + i
            y = jnp.dot(fai_ref[...], s_ref[t2], preferred_element_type=_F32)
            y = y + ukeep_ref[t2].astype(_F32) * bias
            for m in range(2):
                res = g_ref[m, i].astype(_F32) * y[m * h1:(m + 1) * h1]
                if natural_out:
                    outs[m].append(res.astype(o_ref.dtype))
                else:
                    o_ref[m, i] = res.astype(o_ref.dtype)
        if natural_out:
            for m in range(2):
                o_ref[m] = jnp.swapaxes(jnp.stack(outs[m], axis=0), 0, 1)


def _hyena_conv(u5, ucol, g5, gcol, kspec, kcol, bias, tabs, natural_out):
    fa, fai, fb, fbi = tabs
    b, _, n2, h1, ct = u5.shape
    n1 = 2 * h1
    n_in = n2 // _T2_PER_PHASE
    f1pp = min(_F1_PER_PHASE, n1)
    n_mid = n1 // f1pp
    n_out = n_in
    tiles = HYENA_WIDTH // ct
    body = functools.partial(_hyena_conv_body, n_in=n_in, n_mid=n_mid, n1=n1, f1pp=f1pp,
                             natural_out=natural_out)
    mid = lambda s: jnp.clip(s - n_in, 0, n_mid - 1)
    last = lambda s: jnp.clip(s - n_in - n_mid, 0, n_out - 1)
    blk = (2, None, _T2_PER_PHASE, h1, ct)
    if natural_out:
        out_spec = pl.BlockSpec((None, 2, h1, _T2_PER_PHASE, ct), lambda j, p, s: (j, p, 0, last(s), 0))
        out_shape = jax.ShapeDtypeStruct((tiles, b, h1, n2, ct), _BF16)
    else:
        out_spec = pl.BlockSpec(blk, lambda j, p, s: (p, j, last(s), 0, 0))
        out_shape = jax.ShapeDtypeStruct((b, tiles, n2, h1, ct), _BF16)
    return pl.pallas_call(
        body,
        grid=(tiles, b // 2, n_in + n_mid + n_out),
        in_specs=[
            pl.BlockSpec(blk, lambda j, p, s: (p, ucol + j, jnp.minimum(s, n_in - 1), 0, 0)),
            pl.BlockSpec(blk, lambda j, p, s: (p, gcol + j, last(s), 0, 0)),
            pl.BlockSpec((None, f1pp * 2 * n2, ct), lambda j, p, s: (kcol + j, mid(s), 0)),
            pl.BlockSpec((1, ct), lambda j, p, s: (0, j)),
            pl.BlockSpec((2 * n1, n1), lambda j, p, s: (0, 0)),
            pl.BlockSpec((n1, 2 * n1), lambda j, p, s: (0, 0)),
            pl.BlockSpec((f1pp, 2 * n2, 2 * n2), lambda j, p, s: (mid(s), 0, 0)),
            pl.BlockSpec((f1pp, 2 * n2, 2 * n2), lambda j, p, s: (mid(s), 0, 0)),
        ],
        out_specs=out_spec,
        out_shape=out_shape,
        scratch_shapes=([pltpu.VMEM((n2, 2 * n1, ct), _BF16)] + 4 * [pltpu.VMEM((f1pp, n2, ct), _BF16)]
                        + [pltpu.VMEM((f1pp, 2 * n2, ct), _BF16), pltpu.VMEM((n2, n1, ct), _BF16)]),
        compiler_params=_cparams(("arbitrary", "arbitrary", "arbitrary")),
        name="hyena_conv",
    )(u5, g5, kspec, bias, fa, fai, fb, fbi)


_GATE_LANES = 2 * MLSTM_HEADS
_TERM_R = 12
_LOG2E = math.log2(math.e)
_MLSTM_SUB = 2


def _split3(x):
    hi = x.astype(_BF16).astype(_F32)
    mid = (x - hi).astype(_BF16).astype(_F32)
    lo = (x - hi - mid).astype(_BF16).astype(_F32)
    return hi, mid, lo


def _scan_order_max(x, is_fwd):
    n = x.shape[0]
    row = lax.broadcasted_iota(jnp.int32, x.shape, 0)
    pre, suf = x, x
    shift = 1
    while shift < n:
        pre = jnp.maximum(pre, jnp.where(row >= shift, pltpu.roll(pre, shift, axis=0), -jnp.inf))
        suf = jnp.maximum(suf, jnp.where(row < n - shift, pltpu.roll(suf, n - shift, axis=0), -jnp.inf))
        shift *= 2
    return jnp.where(is_fwd, pre, suf)


def _gate_prep_body(g_ref, bias_ref, a_ref, row_ref, keep_ref,
                    b_scr, a_scr, pm_scr, tot_scr, mloc_scr, mprev_scr, *, nc):
    ch = MLSTM_CHUNK
    lane = lax.broadcasted_iota(jnp.int32, (1, 128), 1)
    is_fwd = lane < MLSTM_HEADS
    live = lane < _GATE_LANES
    jj = lax.broadcasted_iota(jnp.int32, (ch, ch), 0)
    ss = lax.broadcasted_iota(jnp.int32, (ch, ch), 1)
    t_lo = (ss <= jj).astype(_F32)
    row_ref[...] = jnp.zeros_like(row_ref)

    def chunk_stats(c, carry):
        r0 = pl.multiple_of(c * ch, ch)
        gi = g_ref[pl.ds(r0, ch), :] + bias_ref[...]
        gf = pltpu.roll(gi, 128 - 2 * MLSTM_HEADS, axis=1)
        logf = jnp.minimum(gf, 0.0) - jnp.log1p(jnp.exp(-jnp.abs(gf)))
        cs_lo = jnp.dot(t_lo, logf, precision=_HIGHEST, preferred_element_type=_F32)
        tot = cs_lo[ch - 1:ch, :]
        cs_up = tot - cs_lo + logf
        bcs = jnp.where(is_fwd, cs_lo, cs_up)
        a = tot - bcs + gi
        b_scr[pl.ds(r0, ch), :] = bcs
        a_scr[pl.ds(r0, ch), :] = a
        tot_scr[pl.ds(c, 1), :] = tot
        mloc_scr[pl.ds(c, 1), :] = jnp.max(a, axis=0, keepdims=True)
        r = gi - bcs
        pm_scr[pl.ds(r0, ch), :] = _scan_order_max(r, is_fwd)
        hi, mid, lo = _split3(r.T[0:_GATE_LANES, :] * _LOG2E)
        base = _TERM_R * _GATE_LANES
        row_ref[base:base + 16, pl.ds(r0, ch)] = jnp.concatenate([hi, mid], axis=0).astype(_BF16)
        row_ref[base + 16:base + 32, pl.ds(r0, ch)] = jnp.concatenate(
            [lo, jnp.zeros_like(lo)], axis=0).astype(_BF16)
        return carry

    lax.fori_loop(0, nc, chunk_stats, 0, unroll=4)

    def scan_f(c, m):
        mprev_scr[pl.ds(c, 1), :] = jnp.where(is_fwd, m, mprev_scr[pl.ds(c, 1), :])
        return jnp.maximum(tot_scr[pl.ds(c, 1), :] + m, mloc_scr[pl.ds(c, 1), :])

    def scan_b(i, m):
        c = nc - 1 - i
        mprev_scr[pl.ds(c, 1), :] = jnp.where(is_fwd, mprev_scr[pl.ds(c, 1), :], m)
        return jnp.maximum(tot_scr[pl.ds(c, 1), :] + m, mloc_scr[pl.ds(c, 1), :])

    mprev_scr[...] = jnp.zeros_like(mprev_scr)
    lax.fori_loop(0, nc, scan_f, jnp.zeros((1, 128), _F32))
    lax.fori_loop(0, nc, scan_b, jnp.zeros((1, 128), _F32))

    def emit(c, carry):
        r0 = pl.multiple_of(c * ch, ch)
        mprev = mprev_scr[pl.ds(c, 1), :]
        tot = tot_scr[pl.ds(c, 1), :]
        mnew = jnp.maximum(tot + mprev, mloc_scr[pl.ds(c, 1), :])
        keep_ref[pl.ds(c, 1), :] = jnp.exp(tot + mprev - mnew)
        g = jnp.maximum(mprev, pm_scr[pl.ds(r0, ch), :])
        terms = (_split3(-g * _LOG2E) + _split3((mprev - g) * _LOG2E)
                 + _split3((-b_scr[pl.ds(r0, ch), :] - g) * _LOG2E)
                 + _split3((a_scr[pl.ds(r0, ch), :] - mnew) * _LOG2E) + 3 * (jnp.ones((ch, 128), _F32),))
        pack = jnp.where(live, terms[0], 0.0)
        for t in range(1, len(terms)):
            pack = pack + pltpu.roll(jnp.where(live, terms[t], 0.0), _GATE_LANES * t, axis=1)
        a_ref[pl.ds(r0, ch), :] = pack.astype(_BF16)
        return carry

    lax.fori_loop(0, nc, emit, 0, unroll=4)


def _gate_prep(g3, bias):
    b, l, _ = g3.shape
    nc = l // MLSTM_CHUNK
    body = functools.partial(_gate_prep_body, nc=nc)
    return pl.pallas_call(
        body,
        grid=(b,),
        in_specs=[
            pl.BlockSpec((None, l, 128), lambda i: (i, 0, 0)),
            pl.BlockSpec((1, 128), lambda i: (0, 0)),
        ],
        out_specs=[
            pl.BlockSpec((None, l, 128), lambda i: (i, 0, 0)),
            pl.BlockSpec((None, 128, l), lambda i: (i, 0, 0)),
            pl.BlockSpec((None, nc, 128), lambda i: (i, 0, 0)),
        ],
        out_shape=[
            jax.ShapeDtypeStruct((b, l, 128), _BF16),
            jax.ShapeDtypeStruct((b, 128, l), _BF16),
            jax.ShapeDtypeStruct((b, nc, 128), _F32),
        ],
        scratch_shapes=[
            pltpu.VMEM((l, 128), _F32), pltpu.VMEM((l, 128), _F32), pltpu.VMEM((l, 128), _F32),
            pltpu.VMEM((nc, 128), _F32), pltpu.VMEM((nc, 128), _F32), pltpu.VMEM((nc, 128), _F32),
        ],
        compiler_params=_cparams(("arbitrary",)),
        name="gate_prep",
    )(g3, bias)


def _gate_spread_matrix():
    bc = np.zeros((128, 4 * 128), np.float32)
    for blk in range(4):
        for t in range(3 * blk, 3 * blk + 3):
            bc[t * _GATE_LANES:(t + 1) * _GATE_LANES, blk * 128:(blk + 1) * 128] = 1.0
    return jnp.asarray(bc).astype(_BF16)


def _mlstm_body(keep_ref, bc_ref, qf_ref, kf_ref, vf_ref, af_ref, rf_ref, qb_ref, kb_ref, vb_ref, ab_ref,
                rb_ref, hf_ref, hb_ref, ct_ref, nm_ref, *, nc):
    bi, c = pl.program_id(0), pl.program_id(1)

    @pl.when(c == 0)
    def _():
        ct_ref[...] = jnp.zeros_like(ct_ref)
        nm_ref[...] = jnp.zeros_like(nm_ref)

    ch, dh = MLSTM_CHUNK, MLSTM_HEAD_DIM
    jj = lax.broadcasted_iota(jnp.int32, (ch, ch), 0)
    ss = lax.broadcasted_iota(jnp.int32, (ch, ch), 1)
    head_lane = ss % _GATE_LANES
    ones_rhs = jnp.ones((ch, 128), _BF16)
    twice = lambda a: jnp.concatenate([a, a], axis=1)
    dirs = ((qf_ref, kf_ref, vf_ref, af_ref, rf_ref, hf_ref, ss <= jj),
            (qb_ref, kb_ref, vb_ref, ab_ref, rb_ref, hb_ref, ss >= jj))
    for sub, d in [(sub, d) for sub in range(_MLSTM_SUB) for d in range(2)]:
        q_blk, k_blk, v_blk, a_blk, r_blk, o_blk, mask = dirs[d]
        pos = sub if d == 0 else _MLSTM_SUB - 1 - sub
        chunk = c * _MLSTM_SUB + pos if d == 0 else nc - (c + 1) * _MLSTM_SUB + pos
        rows = slice(pos * ch, (pos + 1) * ch)
        q_ref, k_ref, v_ref, o_ref = q_blk.at[rows], k_blk.at[rows], v_blk.at[rows], o_blk.at[rows]
        bmat = jnp.concatenate([bc_ref[:, 0:128] + r_blk[:, rows], bc_ref[:, 128:]], axis=1)
        a_all = a_blk[rows, :].astype(_F32)
        kbase = (bi * nc + chunk) * _GATE_LANES
        for h in range(MLSTM_HEADS):
            hd = d * MLSTM_HEADS + h
            a_h = jnp.where(head_lane == hd, a_all, 0.0).astype(_BF16)
            e = jnp.dot(a_h, bmat, preferred_element_type=_F32)
            decay = jnp.exp2(jnp.where(mask, e[:, 0:128], -jnp.inf))
            rest = jnp.exp2(e[:, 128:])
            iw, clamp, wrep = rest[:, 0:128], rest[:, 128:256], rest[:, 256:384]
            keep = keep_ref[kbase + hd]
            lo, hi = h * dh, (h + 1) * dh
            qh, kh, vh = q_ref[:, lo:hi], k_ref[:, lo:hi], v_ref[:, lo:hi]
            p = (_dot_nt(qh, kh) * decay).astype(_BF16)
            ct, nm = ct_ref[hd], nm_ref[hd]
            pv = jnp.dot(p, jnp.concatenate([vh, ones_rhs], axis=1), preferred_element_type=_F32)
            qc = jnp.dot(qh, jnp.concatenate([ct, nm], axis=1).astype(_BF16), preferred_element_type=_F32)
            num = pv[:, :dh] + twice(iw) * qc[:, :dh]
            den = pv[:, dh:] + iw * qc[:, dh:]
            inv = 1.0 / jnp.maximum(jnp.abs(den), clamp)
            o_ref[:, lo:hi] = (num * twice(inv)).astype(o_ref.dtype)
            vw = jnp.concatenate([(vh.astype(_F32) * twice(wrep)).astype(_BF16), wrep.astype(_BF16)], axis=1)
            upd = lax.dot_general(kh, vw, (((0,), (0,)), ((), ())), preferred_element_type=_F32)
            ct_ref[hd] = keep * ct + upd[:, :dh]
            nm_ref[hd] = keep * nm + upd[:, dh:]


def _mlstm(qk, proj3, a_mat, r_rows, keep):
    b, l, _ = qk.shape
    mw = MLSTM_WIDTH
    nc = l // MLSTM_CHUNK
    rows = _MLSTM_SUB * MLSTM_CHUNK
    nb = l // rows
    fw = lambda c: c
    bw = lambda c: nb - 1 - c

    def specs(cm):
        return [
            pl.BlockSpec((None, rows, mw), lambda i, c: (i, cm(c), 0)),
            pl.BlockSpec((None, rows, mw), lambda i, c: (i, cm(c), 1)),
            pl.BlockSpec((None, rows, mw), lambda i, c: (i, cm(c), _COL_V)),
            pl.BlockSpec((None, rows, 128), lambda i, c: (i, cm(c), 0)),
            pl.BlockSpec((None, 128, rows), lambda i, c: (i, 0, cm(c))),
        ]

    body = functools.partial(_mlstm_body, nc=nc)
    keep_flat = keep[:, :, :_GATE_LANES].reshape(b * nc * _GATE_LANES)
    return pl.pallas_call(
        body,
        grid=(b, nb),
        in_specs=[pl.BlockSpec(memory_space=pltpu.SMEM),
                  pl.BlockSpec((128, 4 * 128), lambda i, c: (0, 0))] + specs(fw) + specs(bw),
        out_specs=[
            pl.BlockSpec((None, rows, mw), lambda i, c: (i, c, 0)),
            pl.BlockSpec((None, rows, mw), lambda i, c: (i, nb - 1 - c, 0)),
        ],
        out_shape=[jax.ShapeDtypeStruct((b, l, mw), _BF16), jax.ShapeDtypeStruct((b, l, mw), _BF16)],
        scratch_shapes=[
            pltpu.VMEM((2 * MLSTM_HEADS, MLSTM_HEAD_DIM, MLSTM_HEAD_DIM), _F32),
            pltpu.VMEM((2 * MLSTM_HEADS, MLSTM_HEAD_DIM, 128), _F32),
        ],
        compiler_params=_cparams(("arbitrary", "arbitrary")),
        name="mlstm",
    )(keep_flat, _gate_spread_matrix(), qk, qk, proj3, a_mat, r_rows, qk, qk, proj3, a_mat, r_rows)


def _sigmoid(x):
    return 1.0 / (1.0 + jnp.exp(-x))


def _merge_body(x_ref, ya_ref, hf_ref, hb_ref, o_ref, ma_ref, mb_ref, wa_ref, wb_ref, wo_ref, out_ref):
    hsum = hf_ref[...].astype(_F32) + hb_ref[...].astype(_F32)
    yb = (_sigmoid(o_ref[...].astype(_F32)) * hsum).astype(_BF16)
    ya = jnp.concatenate([ya_ref[t] for t in range(ya_ref.shape[0])], axis=1)
    pa = jnp.dot(ya, wa_ref[...], preferred_element_type=_F32)
    pb = jnp.dot(yb, wb_ref[...], preferred_element_type=_F32)
    mixed = _sigmoid(ma_ref[...].astype(_F32)) * pa + _sigmoid(mb_ref[...].astype(_F32)) * pb
    out_ref[...] = x_ref[...] + jnp.dot(mixed.astype(_BF16), wo_ref[...], preferred_element_type=_F32)


def _merge(x2, ya3, hf2, hb2, proj, wa, wb, wo):
    m = x2.shape[0]
    tm = _MERGE_ROWS
    d, mw = D_MODEL, MLSTM_WIDTH
    const = lambda shape: pl.BlockSpec(shape, lambda i: (0, 0), pipeline_mode=pl.Buffered(1))
    return pl.pallas_call(
        _merge_body,
        grid=(m // tm,),
        in_specs=[
            pl.BlockSpec((tm, d), lambda i: (i, 0)),
            pl.BlockSpec((ya3.shape[0], tm, ya3.shape[2]), lambda i: (0, i, 0)),
            pl.BlockSpec((tm, mw), lambda i: (i, 0)),
            pl.BlockSpec((tm, mw), lambda i: (i, 0)),
            pl.BlockSpec((tm, mw), lambda i: (i, _COL_O)),
            pl.BlockSpec((tm, d), lambda i: (i, _COL_MA // 2)),
            pl.BlockSpec((tm, d), lambda i: (i, _COL_MB // 2)),
            const((HYENA_WIDTH, d)), const((mw, d)), const((d, d)),
        ],
        out_specs=pl.BlockSpec((tm, d), lambda i: (i, 0)),
        out_shape=jax.ShapeDtypeStruct((m, d), _F32),
        compiler_params=_cparams(("arbitrary",)),
        name="merge",
    )(x2, ya3, hf2, hb2, proj, proj, proj, wa, wb, wo)


def _ffn_body(x_ref, n2_ref, wg_ref, wu_ref, wd_ref, nf_ref, o_ref, hn_ref):
    f = pl.program_id(1)
    last = pl.num_programs(1) - 1
    rows = _NORM_ROWS

    def swiglu_down(hn):
        g = jnp.dot(hn, wg_ref[...], preferred_element_type=_F32)
        u = jnp.dot(hn, wu_ref[...], preferred_element_type=_F32)
        a = (g * _sigmoid(g) * u).astype(_BF16)
        return jnp.dot(a, wd_ref[...], preferred_element_type=_F32)

    @pl.when(f == 0)
    def _():
        for r in range(0, x_ref.shape[0], rows):
            x = x_ref[r:r + rows, :]
            hn = (x * lax.rsqrt(jnp.mean(x * x, axis=-1, keepdims=True) + RMS_EPS) * n2_ref[...]).astype(_BF16)
            hn_ref[r:r + rows, :] = hn
            o_ref[r:r + rows, :] = x + swiglu_down(hn)

    @pl.when(jnp.logical_and(f > 0, f < last))
    def _():
        o_ref[...] += swiglu_down(hn_ref[...])

    @pl.when(f == last)
    def _():
        for r in range(0, x_ref.shape[0], rows):
            y = o_ref[r:r + rows, :] + swiglu_down(hn_ref[r:r + rows, :])
            o_ref[r:r + rows, :] = y * lax.rsqrt(jnp.mean(y * y, axis=-1, keepdims=True) + RMS_EPS) * nf_ref[...]


def _ffn(x2, norm2_w, w_gate_up, w_down, norm_f_w):
    m = x2.shape[0]
    tm, tf = _FFN_TILE
    d = D_MODEL
    nf = FFN_HIDDEN // tf
    return pl.pallas_call(
        _ffn_body,
        grid=(m // tm, nf),
        in_specs=[
            pl.BlockSpec((tm, d), lambda i, f: (i, 0)),
            pl.BlockSpec((1, d), lambda i, f: (0, 0)),
            pl.BlockSpec((d, tf), lambda i, f: (0, f)),
            pl.BlockSpec((d, tf), lambda i, f: (0, nf + f)),
            pl.BlockSpec((tf, d), lambda i, f: (f, 0)),
            pl.BlockSpec((1, d), lambda i, f: (0, 0)),
        ],
        out_specs=pl.BlockSpec((tm, d), lambda i, f: (i, 0)),
        out_shape=jax.ShapeDtypeStruct((m, d), _F32),
        scratch_shapes=[pltpu.VMEM((tm, d), _BF16)],
        compiler_params=_cparams(("arbitrary", "arbitrary")),
        name="ffn",
    )(x2, norm2_w, w_gate_up, w_gate_up, w_down, norm_f_w)


def kernel(x, norm1_w, w_in, conv_w, conv_b, filt_w1, filt_b1, filt_freq1, filt_w2, filt_b2, filt_freq2,
           filt_w3, hyena_bias, mlstm_gate_bias, w_branch_a, w_branch_b, w_out, norm2_w, w_gate_up, w_down,
           norm_f_w):
    b, l, d = x.shape
    assert d == D_MODEL and b % 2 == 0 and l % (_N2 * _T2_PER_PHASE) == 0
    assert norm1_w.shape[0] == 1, "single-layer block"
    hw, mw, nh = HYENA_WIDTH, MLSTM_WIDTH, MLSTM_HEADS
    m = b * l
    n = 2 * l
    n1 = n // _N2
    sc_cols = 3 * hw + 2 * mw
    g0 = sc_cols + 2 * mw

    w_t = jnp.swapaxes(w_in[0], 0, 1)
    w_all_t = w_t.astype(_BF16)
    w_merge_t = w_all_t[g0 + 4 * nh:]
    wg_t = w_t[g0:g0 + 4 * nh]
    gate_order = lambda a: jnp.concatenate(
        [a[0:nh], a[2 * nh:3 * nh], a[nh:2 * nh], a[3 * nh:4 * nh]], axis=0)
    wg_split_t = jnp.concatenate(_split_bf16(jnp.pad(gate_order(wg_t), ((0, 128 - 4 * nh), (0, 0)))), axis=0)
    gate_bias = jnp.pad(gate_order(mlstm_gate_bias[0].astype(_F32).reshape(4 * nh))[None, :],
                        ((0, 0), (0, 128 - 4 * nh)))

    x2 = x.reshape(m, d)
    proj, gates = _inproj(x2, norm1_w[0][None, :], w_merge_t, w_all_t, wg_split_t)
    proj3 = proj.reshape(b, l, _MAIN_COLS)

    cw, cb = conv_w[0], conv_b[0][None, :]
    hy5 = _shortconv(proj3, cw, cb, _COL_HV * 1024, 0, 3 * hw, True)
    qk = _shortconv(proj3, cw, cb, _COL_Q * 1024, 3 * hw, 2 * mw, False)

    fa, fa_full, fai, fb, fbi = _dft_tables(n)
    zpos = _filter_positions(l)
    w1p = jnp.pad(filt_w1[0].astype(_F32), ((0, 128 - HYENA_EMB), (0, 0)))
    max_decay = math.log(HYENA_TARGET) / HYENA_FAST_DECAY
    min_decay = math.log(HYENA_TARGET) / HYENA_SLOW_DECAY
    dabs = jnp.asarray(np.abs(np.linspace(min_decay, max_decay, hw, dtype=np.float32))[None, :])
    dup = lambda a: jnp.concatenate([a, a], axis=-1)
    fh = HYENA_FILTER_HIDDEN
    w3 = filt_w3[0].astype(_F32).reshape(fh, 2, 2, hw).transpose(2, 0, 1, 3).reshape(2 * fh, 2 * hw)
    w3_hi, w3_lo = _split_bf16(w3)
    blockdiag = lambda a: jnp.concatenate(
        [jnp.concatenate([a, jnp.zeros_like(a)], axis=1), jnp.concatenate([jnp.zeros_like(a), a], axis=1)], axis=0)
    kern = _filt_mlp(zpos, blockdiag(w1p), dup(filt_b1[0][None, :]), dup(filt_freq1[0][None, :]),
                     blockdiag(filt_w2[0].astype(_F32)), dup(filt_b2[0][None, :]),
                     dup(filt_freq2[0][None, :]), w3_hi, w3_lo, dabs)
    tiles = hw // _FFT_CT
    kspec = _filt_fft(kern.reshape(2 * tiles, _N2, n1, _FFT_CT), fa_full, fb)
    tabs = (fa, fai, fb, fbi)
    hbias = hyena_bias[0].astype(_F32)
    z5 = _hyena_conv(hy5, 0, hy5, tiles, kspec, 0, hbias[0][None, :], tabs, False)
    ya5 = _hyena_conv(z5, 0, hy5, 2 * tiles, kspec, tiles, hbias[1][None, :], tabs, True)
    ya3 = ya5.reshape(tiles, m, _FFT_CT)

    a_mat, r_rows, keep = _gate_prep(gates.reshape(b, l, 128), gate_bias)
    hf, hb = _mlstm(qk, proj3, a_mat, r_rows, keep)

    x_mid = _merge(x2, ya3, hf.reshape(m, mw), hb.reshape(m, mw), proj,
                   w_branch_a[0].astype(_BF16), w_branch_b[0].astype(_BF16), w_out[0].astype(_BF16))
    out = _ffn(x_mid, norm2_w[0][None, :], w_gate_up[0].astype(_BF16), w_down[0].astype(_BF16),
               norm_f_w[None, :])
    return out.reshape(b, l, d)
```

```python
import functools
import math

import numpy as np
import jax
import jax.numpy as jnp
from jax import lax
from jax.experimental import pallas as pl
from jax.experimental.pallas import tpu as pltpu

_F32 = jnp.float32
_BF16 = jnp.bfloat16
_HIGHEST = lax.Precision.HIGHEST

D_MODEL = 2048
HYENA_WIDTH = 1024
HYENA_EMB = 33
HYENA_FILTER_HIDDEN = 64
HYENA_FAST_DECAY = 0.3
HYENA_SLOW_DECAY = 1.5
HYENA_TARGET = 1e-2
MLSTM_WIDTH = 1024
MLSTM_HEADS = 4
MLSTM_HEAD_DIM = 256
MLSTM_CHUNK = 128
FFN_HIDDEN = 5632
RMS_EPS = 1e-6

_COL_MA, _COL_MB, _COL_HV, _COL_Q, _COL_V, _COL_O = 0, 2, 4, 7, 9, 10
_MAIN_COLS = 11 * 1024

_N2 = 32
_T2_PER_PHASE = 16
_F1_PER_PHASE = 32
_V7X_MXU_WIDTH = 256
_V7X_VMEM_BYTES = 64 * 1024 * 1024
_FFT_CT = _V7X_MXU_WIDTH

_INPROJ_TILE = (1024, 1024)
_FFN_TILE = (1024, 512)
_MERGE_ROWS = 256
_NORM_ROWS = 256

_VMEM_LIMIT = _V7X_VMEM_BYTES * 7 // 8


def _cparams(sem, vmem=_VMEM_LIMIT):
    return pltpu.CompilerParams(dimension_semantics=sem, vmem_limit_bytes=vmem)


def _split_bf16(a):
    hi = a.astype(_BF16)
    lo = (a - hi.astype(_F32)).astype(_BF16)
    return hi, lo


def _dot_nt(a, b):
    return lax.dot_general(a, b, (((1,), (1,)), ((), ())), preferred_element_type=_F32)


def _dot3(a_hi, a_lo, b_hi, b_lo):
    d = functools.partial(jnp.dot, preferred_element_type=_F32)
    return d(a_hi, b_hi) + (d(a_hi, b_lo) + d(a_lo, b_hi))


def _inproj_body(x_ref, nw_ref, wm_ref, w_ref, wg_ref, o_ref, g_ref, hn_ref, *, n_merge):
    j = pl.program_id(1)

    @pl.when(j == 0)
    def _():
        rows = _NORM_ROWS
        for r in range(0, x_ref.shape[0], rows):
            x = x_ref[r:r + rows, :]
            hn = x * lax.rsqrt(jnp.mean(x * x, axis=-1, keepdims=True) + RMS_EPS) * nw_ref[...]
            hn_hi, hn_lo = _split_bf16(hn)
            hn_ref[r:r + rows, :] = hn_hi
            hh_hl = _dot_nt(hn_hi, wg_ref[...])
            g_ref[r:r + rows, :] = hh_hl[:, :128] + (hh_hl[:, 128:] + _dot_nt(hn_lo, wg_ref[0:128, :]))
            o_ref[r:r + rows, :] = _dot_nt(hn_hi, wm_ref[...]).astype(o_ref.dtype)

    @pl.when(jnp.logical_and(j > 0, j < n_merge))
    def _():
        o_ref[...] = _dot_nt(hn_ref[...], wm_ref[...]).astype(o_ref.dtype)

    @pl.when(j >= n_merge)
    def _():
        o_ref[...] = _dot_nt(hn_ref[...], w_ref[...]).astype(o_ref.dtype)


def _inproj(x2, norm_w, w_merge_t, w_all_t, wg_t):
    m = x2.shape[0]
    tm, tn = _INPROJ_TILE
    n_merge = w_merge_t.shape[0] // tn
    body = functools.partial(_inproj_body, n_merge=n_merge)
    return pl.pallas_call(
        body,
        grid=(m // tm, _MAIN_COLS // tn),
        in_specs=[
            pl.BlockSpec((tm, D_MODEL), lambda i, j: (i, 0)),
            pl.BlockSpec((1, D_MODEL), lambda i, j: (0, 0)),
            pl.BlockSpec((tn, D_MODEL), lambda i, j: (jnp.minimum(j, n_merge - 1), 0)),
            pl.BlockSpec((tn, D_MODEL), lambda i, j: (jnp.maximum(j - n_merge, 0), 0)),
            pl.BlockSpec((256, D_MODEL), lambda i, j: (0, 0)),
        ],
        out_specs=[
            pl.BlockSpec((tm, tn), lambda i, j: (i, j)),
            pl.BlockSpec((tm, 128), lambda i, j: (i, 0)),
        ],
        out_shape=[
            jax.ShapeDtypeStruct((m, _MAIN_COLS), _BF16),
            jax.ShapeDtypeStruct((m, 128), _F32),
        ],
        scratch_shapes=[pltpu.VMEM((tm, D_MODEL), _BF16)],
        compiler_params=_cparams(("arbitrary", "arbitrary")),
        name="inproj",
    )(x2, norm_w, w_merge_t, w_all_t, wg_t)


def _shortconv_silu_body(u_ref, w_ref, b_ref, o_ref, *, k_scale):
    u = u_ref[...].astype(_F32)
    n = u.shape[0]
    row = lax.broadcasted_iota(jnp.int32, u.shape, 0)
    up = jnp.where(row == 0, 0.0, pltpu.roll(u, 1, axis=0))
    un = jnp.where(row == n - 1, 0.0, pltpu.roll(u, n - 1, axis=0))
    w = w_ref[...]
    y = up * w[0:1, :] + b_ref[...] + u * w[1:2, :] + un * w[2:3, :]
    y = y * (1.0 / (1.0 + jnp.exp(-y)))
    is_k = pl.program_id(1) >= (MLSTM_WIDTH // u.shape[1])
    o_ref[...] = (y * jnp.where(is_k, k_scale, 1.0)).astype(o_ref.dtype)


def _shortconv_t2_body(u_ref, w_ref, b_ref, o_ref, ut_ref):
    n, c = u_ref.shape
    h1 = n // _N2
    g = 16
    for k in range(h1 // g):
        blk = u_ref[k * g * _N2:(k + 1) * g * _N2, :].reshape(g, _N2, c)
        ut_ref[:, k * g:(k + 1) * g, :] = jnp.swapaxes(blk, 0, 1)
    w = w_ref[...]
    w0, w1, w2, bias = w[0:1, :], w[1:2, :], w[2:3, :], b_ref[...]
    slab = lambda t2: ut_ref[t2].astype(_F32)
    row = lax.broadcasted_iota(jnp.int32, (h1, c), 0)
    before_first = jnp.where(row == 0, 0.0, pltpu.roll(slab(_N2 - 1), 1, axis=0))
    after_last = jnp.where(row == h1 - 1, 0.0, pltpu.roll(slab(0), h1 - 1, axis=0))
    for t2 in range(_N2):
        up = before_first if t2 == 0 else slab(t2 - 1)
        un = after_last if t2 == _N2 - 1 else slab(t2 + 1)
        o_ref[t2] = (up * w0 + bias + slab(t2) * w1 + un * w2).astype(o_ref.dtype)


def _shortconv(proj3, conv_w, conv_b, proj_col0, conv_col0, ncols, t2_major):
    b, l, _ = proj3.shape
    ct = _FFT_CT
    p0, c0 = proj_col0 // ct, conv_col0 // ct
    if t2_major:
        body = _shortconv_t2_body
        out_spec = pl.BlockSpec((None, None, _N2, l // _N2, ct), lambda i, j: (i, j, 0, 0, 0))
        out_shape = jax.ShapeDtypeStruct((b, ncols // ct, _N2, l // _N2, ct), _BF16)
        scratch = [pltpu.VMEM((_N2, l // _N2, ct), _BF16)]
    else:
        body = functools.partial(_shortconv_silu_body, k_scale=MLSTM_HEAD_DIM ** -0.5)
        out_spec = pl.BlockSpec((None, l, ct), lambda i, j: (i, 0, j))
        out_shape = jax.ShapeDtypeStruct((b, l, ncols), _BF16)
        scratch = []
    return pl.pallas_call(
        body,
        grid=(b, ncols // ct),
        in_specs=[
            pl.BlockSpec((None, l, ct), lambda i, j: (i, 0, p0 + j)),
            pl.BlockSpec((3, ct), lambda i, j: (0, c0 + j)),
            pl.BlockSpec((1, ct), lambda i, j: (0, c0 + j)),
        ],
        out_specs=out_spec,
        out_shape=out_shape,
        scratch_shapes=scratch,
        compiler_params=_cparams(("arbitrary", "arbitrary")),
        name="shortconv" if t2_major else "shortconv_silu",
    )(proj3, conv_w, conv_b)


def _dft_tables(n):
    n2 = _N2
    n1 = n // n2
    h1 = n1 // 2
    f1 = np.arange(n1)[:, None]
    t1 = np.arange(n1)[None, :]
    ang = 2.0 * np.pi * ((f1 * t1) % n1) / n1
    c, s = np.cos(ang), np.sin(ang)
    ch, sh = c[:, :h1], s[:, :h1]
    fa = np.block([[ch, sh], [-sh, ch]])
    fa_full = np.concatenate([c, -s], axis=0)
    fai = np.block([[ch.T, -sh.T], [sh.T, ch.T]]) / n
    f1v = np.arange(n1)[:, None, None]
    f2 = np.arange(n2)[None, :, None]
    t2 = np.arange(n2)[None, None, :]
    ph = 2.0 * np.pi * ((f1v * t2 + f2 * t2 * n1) % n) / n
    tc, ts = np.cos(ph), np.sin(ph)
    fb = np.concatenate([np.concatenate([tc, ts], axis=2),
                         np.concatenate([-ts, tc], axis=2)], axis=1)
    tct, tst = np.swapaxes(tc, 1, 2), np.swapaxes(ts, 1, 2)
    fbi = np.concatenate([np.concatenate([tct, -tst], axis=2),
                          np.concatenate([tst, tct], axis=2)], axis=1)
    to = lambda a: jnp.asarray(a.astype(np.float32)).astype(_BF16)
    return to(fa), to(fa_full), to(fai), to(fb), to(fbi)


def _filter_positions(l):
    n = 2 * l
    n1 = n // _N2
    r = (np.arange(_N2)[:, None] + _N2 * np.arange(n1)[None, :]).reshape(-1)
    pos = np.where(r < l, r, n - r)
    pos = np.where(r == l, 0, pos)
    bands = (HYENA_EMB - 1) // 2
    tt = np.linspace(0.0, 1.0, l, dtype=np.float32).astype(np.float64)[pos]
    omega = (2.0 * math.pi * np.arange(l, dtype=np.float32) / l).astype(np.float32)
    freqs = np.linspace(1e-4, bands - 1, bands, dtype=np.float32)
    ang = (omega[:, None] * freqs[None, :]).astype(np.float64)[pos]
    z = np.zeros((n, 128), np.float64)
    z[:, 0] = tt
    z[:, 1:1 + bands] = np.cos(ang)
    z[:, 1 + bands:1 + 2 * bands] = -np.sin(ang)
    z[:, 33] = (r != l)
    z[:, 34] = (r < l)
    return jnp.asarray(z.astype(np.float32))


def _filt_mlp_body(z_ref, zp_ref, w1_ref, b1_ref, q1_ref, w2_ref, b2_ref, q2_ref, w3h_ref, w3l_ref, dabs_ref,
                   o_ref, hh_ref, hl_ref):
    fh = HYENA_FILTER_HIDDEN

    @pl.when(pl.program_id(1) == 0)
    def _():
        hp = jnp.sin(q1_ref[...] * (jnp.dot(zp_ref[...], w1_ref[...], precision=_HIGHEST,
                                            preferred_element_type=_F32) + b1_ref[...]))
        hp = jnp.sin(q2_ref[...] * (jnp.dot(hp, w2_ref[...], precision=_HIGHEST,
                                            preferred_element_type=_F32) + b2_ref[...]))
        lane_p = lax.broadcasted_iota(jnp.int32, hp.shape, 1)
        swapped = pltpu.roll(hp, fh, axis=1)
        h = jnp.concatenate([jnp.where(lane_p < fh, hp, swapped), jnp.where(lane_p < fh, swapped, hp)], axis=0)
        z = z_ref[...]
        valid = z[:, 33:34]
        fwd = z[:, 34:35]
        lane = lax.broadcasted_iota(jnp.int32, h.shape, 1)
        hi, lo = _split_bf16(h * (valid * jnp.where(lane < fh, fwd, 1.0 - fwd)))
        hh_ref[...] = hi
        hl_ref[...] = lo

    raw = _dot3(hh_ref[...], hl_ref[...], w3h_ref[...], w3l_ref[...])
    window = jnp.exp(-z_ref[:, 0:1] * dabs_ref[...])
    o_ref[...] = (raw * window).astype(o_ref.dtype)


_FILT_ROWS = 1024


def _filt_mlp(zpos, w1bd, b1d, q1d, w2bd, b2d, q2d, w3h, w3l, dabs):
    n = zpos.shape[0]
    ct = _FFT_CT
    tr = _FILT_ROWS
    per_order = HYENA_WIDTH // ct
    fh = HYENA_FILTER_HIDDEN
    zpacked = zpos.reshape(n // tr, 2, tr // 2, 128).transpose(0, 2, 1, 3).reshape(n // 2, 256)
    full = lambda shape: pl.BlockSpec(shape, lambda r, i: (0,) * len(shape))
    return pl.pallas_call(
        _filt_mlp_body,
        grid=(n // tr, 2 * per_order),
        in_specs=[
            pl.BlockSpec((tr, 128), lambda r, i: (r, 0)),
            pl.BlockSpec((tr // 2, 256), lambda r, i: (r, 0)),
            full((256, 2 * fh)), full((1, 2 * fh)), full((1, 2 * fh)),
            full((2 * fh, 2 * fh)), full((1, 2 * fh)), full((1, 2 * fh)),
            pl.BlockSpec((2 * fh, ct), lambda r, i: (0, i)),
            pl.BlockSpec((2 * fh, ct), lambda r, i: (0, i)),
            pl.BlockSpec((1, ct), lambda r, i: (0, i % per_order)),
        ],
        out_specs=pl.BlockSpec((None, tr, ct), lambda r, i: (i, r, 0)),
        out_shape=jax.ShapeDtypeStruct((2 * per_order, n, ct), _BF16),
        scratch_shapes=[pltpu.VMEM((tr, 2 * fh), _BF16), pltpu.VMEM((tr, 2 * fh), _BF16)],
        compiler_params=_cparams(("arbitrary", "arbitrary")),
        name="filt_mlp",
    )(zpos, zpacked, w1bd, b1d, q1d, w2bd, b2d, q2d, w3h, w3l, dabs)


_F1_GROUP = 16


def _f1_tiles(f_base, f1pp):
    return [(slice(g * _F1_GROUP, (g + 1) * _F1_GROUP), pl.multiple_of(f_base + g * _F1_GROUP, _F1_GROUP))
            for g in range(f1pp // _F1_GROUP)]


def _gather_f1(s_ref, vre_ref, vim_ref, f_base, f1pp, n1):
    for sl, f0 in _f1_tiles(f_base, f1pp):
        vre_ref[sl] = jnp.swapaxes(s_ref[:, pl.ds(f0, _F1_GROUP), :], 0, 1)
        vim_ref[sl] = jnp.swapaxes(s_ref[:, pl.ds(n1 + f0, _F1_GROUP), :], 0, 1)


def _filt_fft_body(kern_ref, fa_ref, fb_ref, o_ref, s_ref, vre_ref, vim_ref, *, n_in, n1, f1pp):
    s = pl.program_id(1)

    @pl.when(s < n_in)
    def _():
        for i in range(_T2_PER_PHASE):
            s_ref[s * _T2_PER_PHASE + i] = jnp.dot(fa_ref[...], kern_ref[i],
                                                   preferred_element_type=_F32).astype(_BF16)

    @pl.when(s >= n_in)
    def _():
        _gather_f1(s_ref, vre_ref, vim_ref, (s - n_in) * f1pp, f1pp, n1)

        for f in range(f1pp):
            v = jnp.concatenate([vre_ref[f], vim_ref[f]], axis=0)
            o_ref[f * 2 * _N2:(f + 1) * 2 * _N2, :] = jnp.dot(
                fb_ref[f], v, preferred_element_type=_F32).astype(o_ref.dtype)


def _filt_fft(kern4, fa_full, fb):
    tiles, n2, n1, ct = kern4.shape
    n_in = n2 // _T2_PER_PHASE
    f1pp = min(_F1_PER_PHASE, n1)
    n_mid = n1 // f1pp
    body = functools.partial(_filt_fft_body, n_in=n_in, n1=n1, f1pp=f1pp)
    return pl.pallas_call(
        body,
        grid=(tiles, n_in + n_mid),
        in_specs=[
            pl.BlockSpec((None, _T2_PER_PHASE, n1, ct), lambda j, s: (j, jnp.minimum(s, n_in - 1), 0, 0)),
            pl.BlockSpec((2 * n1, n1), lambda j, s: (0, 0)),
            pl.BlockSpec((f1pp, 2 * n2, 2 * n2), lambda j, s: (jnp.maximum(s - n_in, 0), 0, 0)),
        ],
        out_specs=pl.BlockSpec((None, f1pp * 2 * n2, ct), lambda j, s: (j, jnp.maximum(s - n_in, 0), 0)),
        out_shape=jax.ShapeDtypeStruct((tiles, n1 * 2 * n2, ct), _BF16),
        scratch_shapes=[pltpu.VMEM((n2, 2 * n1, ct), _BF16),
                        pltpu.VMEM((f1pp, n2, ct), _BF16), pltpu.VMEM((f1pp, n2, ct), _BF16)],
        compiler_params=_cparams(("arbitrary", "arbitrary")),
        name="filt_fft",
    )(kern4, fa_full, fb)


def _hyena_conv_body(u_ref, g_ref, k_ref, bias_ref, fa_ref, fai_ref, fb_ref, fbi_ref,
                     o_ref, s_ref, vre_ref, vim_ref, wre_ref, wim_ref, y_ref, ukeep_ref,
                     *, n_in, n_mid, n1, f1pp, natural_out):
    s = pl.program_id(2)
    h1 = n1 // 2

    @pl.when(s < n_in)
    def _():
        for i in range(_T2_PER_PHASE):
            u = jnp.concatenate([u_ref[0, i], u_ref[1, i]], axis=0)
            ukeep_ref[s * _T2_PER_PHASE + i] = u
            s_ref[s * _T2_PER_PHASE + i] = jnp.dot(fa_ref[...], u,
                                                   preferred_element_type=_F32).astype(_BF16)

    @pl.when(jnp.logical_and(s >= n_in, s < n_in + n_mid))
    def _():
        tiles = _f1_tiles((s - n_in) * f1pp, f1pp)

        def gather(sl, f0):
            vre_ref[sl] = jnp.swapaxes(s_ref[:, pl.ds(f0, _F1_GROUP), :], 0, 1)
            vim_ref[sl] = jnp.swapaxes(s_ref[:, pl.ds(n1 + f0, _F1_GROUP), :], 0, 1)

        def forward(f):
            v = jnp.concatenate([vre_ref[f], vim_ref[f]], axis=0)
            x = jnp.dot(fb_ref[f], v, preferred_element_type=_F32)
            xre, xim = x[:_N2], x[_N2:]
            kre = k_ref[f * 2 * _N2:f * 2 * _N2 + _N2, :].astype(_F32)
            kim = k_ref[f * 2 * _N2 + _N2:(f + 1) * 2 * _N2, :].astype(_F32)
            y_ref[f] = jnp.concatenate([xre * kre - xim * kim, xre * kim + xim * kre],
                                       axis=0).astype(_BF16)

        def inverse(f):
            w = jnp.dot(fbi_ref[f], y_ref[f], preferred_element_type=_F32)
            wre_ref[f] = w[:_N2].astype(_BF16)
            wim_ref[f] = w[_N2:].astype(_BF16)

        def scatter(sl, f0):
            s_ref[:, pl.ds(f0, _F1_GROUP), :] = jnp.swapaxes(wre_ref[sl], 0, 1)
            s_ref[:, pl.ds(n1 + f0, _F1_GROUP), :] = jnp.swapaxes(wim_ref[sl], 0, 1)

        nt = len(tiles)
        for t in tiles:
            gather(*t)
        for g in range(nt + 1):
            for k in range(_F1_GROUP):
                if g < nt:
                    forward(g * _F1_GROUP + k)
                if g >= 1:
                    inverse((g - 1) * _F1_GROUP + k)
            if g >= 1:
                scatter(*tiles[g - 1])

    @pl.when(s >= n_in + n_mid)
    def _():
        bias = bias_ref[...]
        outs = [[], []]
        for i in range(_T2_PER_PHASE):
            t2 = (s - n_in - n_mid) * _T2_PER_PHASE + i
            y = jnp.dot(fai_ref[...], s_ref[t2], preferred_element_type=_F32)
            y = y + ukeep_ref[t2].astype(_F32) * bias
            for m in range(2):
                res = g_ref[m, i].astype(_F32) * y[m * h1:(m + 1) * h1]
                if natural_out:
                    outs[m].append(res.astype(o_ref.dtype))
                else:
                    o_ref[m, i] = res.astype(o_ref.dtype)
        if natural_out:
            for m in range(2):
                o_ref[m] = jnp.swapaxes(jnp.stack(outs[m], axis=0), 0, 1)


def _hyena_conv(u5, ucol, g5, gcol, kspec, kcol, bias, tabs, natural_out):
    fa, fai, fb, fbi = tabs
    b, _, n2, h1, ct = u5.shape
    n1 = 2 * h1
    n_in = n2 // _T2_PER_PHASE
    f1pp = min(_F1_PER_PHASE, n1)
    n_mid = n1 // f1pp
    n_out = n_in
    tiles = HYENA_WIDTH // ct
    body = functools.partial(_hyena_conv_body, n_in=n_in, n_mid=n_mid, n1=n1, f1pp=f1pp,
                             natural_out=natural_out)
    mid = lambda s: jnp.clip(s - n_in, 0, n_mid - 1)
    last = lambda s: jnp.clip(s - n_in - n_mid, 0, n_out - 1)
    blk = (2, None, _T2_PER_PHASE, h1, ct)
    if natural_out:
        out_spec = pl.BlockSpec((None, 2, h1, _T2_PER_PHASE, ct), lambda j, p, s: (j, p, 0, last(s), 0))
        out_shape = jax.ShapeDtypeStruct((tiles, b, h1, n2, ct), _BF16)
    else:
        out_spec = pl.BlockSpec(blk, lambda j, p, s: (p, j, last(s), 0, 0))
        out_shape = jax.ShapeDtypeStruct((b, tiles, n2, h1, ct), _BF16)
    return pl.pallas_call(
        body,
        grid=(tiles, b // 2, n_in + n_mid + n_out),
        in_specs=[
            pl.BlockSpec(blk, lambda j, p, s: (p, ucol + j, jnp.minimum(s, n_in - 1), 0, 0)),
            pl.BlockSpec(blk, lambda j, p, s: (p, gcol + j, last(s), 0, 0)),
            pl.BlockSpec((None, f1pp * 2 * n2, ct), lambda j, p, s: (kcol + j, mid(s), 0)),
            pl.BlockSpec((1, ct), lambda j, p, s: (0, j)),
            pl.BlockSpec((2 * n1, n1), lambda j, p, s: (0, 0)),
            pl.BlockSpec((n1, 2 * n1), lambda j, p, s: (0, 0)),
            pl.BlockSpec((f1pp, 2 * n2, 2 * n2), lambda j, p, s: (mid(s), 0, 0)),
            pl.BlockSpec((f1pp, 2 * n2, 2 * n2), lambda j, p, s: (mid(s), 0, 0)),
        ],
        out_specs=out_spec,
        out_shape=out_shape,
        scratch_shapes=([pltpu.VMEM((n2, 2 * n1, ct), _BF16)] + 4 * [pltpu.VMEM((f1pp, n2, ct), _BF16)]
                        + [pltpu.VMEM((f1pp, 2 * n2, ct), _BF16), pltpu.VMEM((n2, n1, ct), _BF16)]),
        compiler_params=_cparams(("arbitrary", "arbitrary", "arbitrary")),
        name="hyena_conv",
    )(u5, g5, kspec, bias, fa, fai, fb, fbi)


_GATE_LANES = 2 * MLSTM_HEADS
_TERM_R = 12
_LOG2E = math.log2(math.e)
_MLSTM_SUB = 2


def _split3(x):
    hi = x.astype(_BF16).astype(_F32)
    mid = (x - hi).astype(_BF16).astype(_F32)
    lo = (x - hi - mid).astype(_BF16).astype(_F32)
    return hi, mid, lo


def _scan_order_max(x, is_fwd):
    n = x.shape[0]
    row = lax.broadcasted_iota(jnp.int32, x.shape, 0)
    pre, suf = x, x
    shift = 1
    while shift < n:
        pre = jnp.maximum(pre, jnp.where(row >= shift, pltpu.roll(pre, shift, axis=0), -jnp.inf))
        suf = jnp.maximum(suf, jnp.where(row < n - shift, pltpu.roll(suf, n - shift, axis=0), -jnp.inf))
        shift *= 2
    return jnp.where(is_fwd, pre, suf)


def _gate_prep_body(g_ref, bias_ref, a_ref, row_ref, keep_ref,
                    b_scr, a_scr, pm_scr, tot_scr, mloc_scr, mprev_scr, *, nc):
    ch = MLSTM_CHUNK
    lane = lax.broadcasted_iota(jnp.int32, (1, 128), 1)
    is_fwd = lane < MLSTM_HEADS
    live = lane < _GATE_LANES
    jj = lax.broadcasted_iota(jnp.int32, (ch, ch), 0)
    ss = lax.broadcasted_iota(jnp.int32, (ch, ch), 1)
    t_lo = (ss <= jj).astype(_F32)
    row_ref[...] = jnp.zeros_like(row_ref)

    def chunk_stats(c, carry):
        r0 = pl.multiple_of(c * ch, ch)
        gi = g_ref[pl.ds(r0, ch), :] + bias_ref[...]
        gf = pltpu.roll(gi, 128 - 2 * MLSTM_HEADS, axis=1)
        logf = jnp.minimum(gf, 0.0) - jnp.log1p(jnp.exp(-jnp.abs(gf)))
        cs_lo = jnp.dot(t_lo, logf, precision=_HIGHEST, preferred_element_type=_F32)
        tot = cs_lo[ch - 1:ch, :]
        cs_up = tot - cs_lo + logf
        bcs = jnp.where(is_fwd, cs_lo, cs_up)
        a = tot - bcs + gi
        b_scr[pl.ds(r0, ch), :] = bcs
        a_scr[pl.ds(r0, ch), :] = a
        tot_scr[pl.ds(c, 1), :] = tot
        mloc_scr[pl.ds(c, 1), :] = jnp.max(a, axis=0, keepdims=True)
        r = gi - bcs
        pm_scr[pl.ds(r0, ch), :] = _scan_order_max(r, is_fwd)
        hi, mid, lo = _split3(r.T[0:_GATE_LANES, :] * _LOG2E)
        base = _TERM_R * _GATE_LANES
        row_ref[base:base + 16, pl.ds(r0, ch)] = jnp.concatenate([hi, mid], axis=0).astype(_BF16)
        row_ref[base + 16:base + 32, pl.ds(r0, ch)] = jnp.concatenate(
            [lo, jnp.zeros_like(lo)], axis=0).astype(_BF16)
        return carry

    lax.fori_loop(0, nc, chunk_stats, 0, unroll=4)

    def scan_f(c, m):
        mprev_scr[pl.ds(c, 1), :] = jnp.where(is_fwd, m, mprev_scr[pl.ds(c, 1), :])
        return jnp.maximum(tot_scr[pl.ds(c, 1), :] + m, mloc_scr[pl.ds(c, 1), :])

    def scan_b(i, m):
        c = nc - 1 - i
        mprev_scr[pl.ds(c, 1), :] = jnp.where(is_fwd, mprev_scr[pl.ds(c, 1), :], m)
        return jnp.maximum(tot_scr[pl.ds(c, 1), :] + m, mloc_scr[pl.ds(c, 1), :])

    mprev_scr[...] = jnp.zeros_like(mprev_scr)
    lax.fori_loop(0, nc, scan_f, jnp.zeros((1, 128), _F32))
    lax.fori_loop(0, nc, scan_b, jnp.zeros((1, 128), _F32))

    def emit(c, carry):
        r0 = pl.multiple_of(c * ch, ch)
        mprev = mprev_scr[pl.ds(c, 1), :]
        tot = tot_scr[pl.ds(c, 1), :]
        mnew = jnp.maximum(tot + mprev, mloc_scr[pl.ds(c, 1), :])
        keep_ref[pl.ds(c, 1), :] = jnp.exp(tot + mprev - mnew)
        g = jnp.maximum(mprev, pm_scr[pl.ds(r0, ch), :])
        terms = (_split3(-g * _LOG2E) + _split3((mprev - g) * _LOG2E)
                 + _split3((-b_scr[pl.ds(r0, ch), :] - g) * _LOG2E)
                 + _split3((a_scr[pl.ds(r0, ch), :] - mnew) * _LOG2E) + 3 * (jnp.ones((ch, 128), _F32),))
        pack = jnp.where(live, terms[0], 0.0)
        for t in range(1, len(terms)):
            pack = pack + pltpu.roll(jnp.where(live, terms[t], 0.0), _GATE_LANES * t, axis=1)
        a_ref[pl.ds(r0, ch), :] = pack.astype(_BF16)
        return carry

    lax.fori_loop(0, nc, emit, 0, unroll=4)


def _gate_prep(g3, bias):
    b, l, _ = g3.shape
    nc = l // MLSTM_CHUNK
    body = functools.partial(_gate_prep_body, nc=nc)
    return pl.pallas_call(
        body,
        grid=(b,),
        in_specs=[
            pl.BlockSpec((None, l, 128), lambda i: (i, 0, 0)),
            pl.BlockSpec((1, 128), lambda i: (0, 0)),
        ],
        out_specs=[
            pl.BlockSpec((None, l, 128), lambda i: (i, 0, 0)),
            pl.BlockSpec((None, 128, l), lambda i: (i, 0, 0)),
            pl.BlockSpec((None, nc, 128), lambda i: (i, 0, 0)),
        ],
        out_shape=[
            jax.ShapeDtypeStruct((b, l, 128), _BF16),
            jax.ShapeDtypeStruct((b, 128, l), _BF16),
            jax.ShapeDtypeStruct((b, nc, 128), _F32),
        ],
        scratch_shapes=[
            pltpu.VMEM((l, 128), _F32), pltpu.VMEM((l, 128), _F32), pltpu.VMEM((l, 128), _F32),
            pltpu.VMEM((nc, 128), _F32), pltpu.VMEM((nc, 128), _F32), pltpu.VMEM((nc, 128), _F32),
        ],
        compiler_params=_cparams(("arbitrary",)),
        name="gate_prep",
    )(g3, bias)


def _gate_spread_matrix():
    bc = np.zeros((128, 4 * 128), np.float32)
    for blk in range(4):
        for t in range(3 * blk, 3 * blk + 3):
            bc[t * _GATE_LANES:(t + 1) * _GATE_LANES, blk * 128:(blk + 1) * 128] = 1.0
    return jnp.asarray(bc).astype(_BF16)


def _mlstm_body(keep_ref, bc_ref, qf_ref, kf_ref, vf_ref, af_ref, rf_ref, qb_ref, kb_ref, vb_ref, ab_ref,
                rb_ref, hf_ref, hb_ref, ct_ref, nm_ref, *, nc):
    bi, c = pl.program_id(0), pl.program_id(1)

    @pl.when(c == 0)
    def _():
        ct_ref[...] = jnp.zeros_like(ct_ref)
        nm_ref[...] = jnp.zeros_like(nm_ref)

    ch, dh = MLSTM_CHUNK, MLSTM_HEAD_DIM
    jj = lax.broadcasted_iota(jnp.int32, (ch, ch), 0)
    ss = lax.broadcasted_iota(jnp.int32, (ch, ch), 1)
    head_lane = ss % _GATE_LANES
    ones_rhs = jnp.ones((ch, 128), _BF16)
    twice = lambda a: jnp.concatenate([a, a], axis=1)
    dirs = ((qf_ref, kf_ref, vf_ref, af_ref, rf_ref, hf_ref, ss <= jj),
            (qb_ref, kb_ref, vb_ref, ab_ref, rb_ref, hb_ref, ss >= jj))
    for sub, d in [(sub, d) for sub in range(_MLSTM_SUB) for d in range(2)]:
        q_blk, k_blk, v_blk, a_blk, r_blk, o_blk, mask = dirs[d]
        pos = sub if d == 0 else _MLSTM_SUB - 1 - sub
        chunk = c * _MLSTM_SUB + pos if d == 0 else nc - (c + 1) * _MLSTM_SUB + pos
        rows = slice(pos * ch, (pos + 1) * ch)
        q_ref, k_ref, v_ref, o_ref = q_blk.at[rows], k_blk.at[rows], v_blk.at[rows], o_blk.at[rows]
        bmat = jnp.concatenate([bc_ref[:, 0:128] + r_blk[:, rows], bc_ref[:, 128:]], axis=1)
        a_all = a_blk[rows, :].astype(_F32)
        kbase = (bi * nc + chunk) * _GATE_LANES
        for h in range(MLSTM_HEADS):
            hd = d * MLSTM_HEADS + h
            a_h = jnp.where(head_lane == hd, a_all, 0.0).astype(_BF16)
            e = jnp.dot(a_h, bmat, preferred_element_type=_F32)
            decay = jnp.exp2(jnp.where(mask, e[:, 0:128], -jnp.inf))
            rest = jnp.exp2(e[:, 128:])
            iw, clamp, wrep = rest[:, 0:128], rest[:, 128:256], rest[:, 256:384]
            keep = keep_ref[kbase + hd]
            lo, hi = h * dh, (h + 1) * dh
            qh, kh, vh = q_ref[:, lo:hi], k_ref[:, lo:hi], v_ref[:, lo:hi]
            p = (_dot_nt(qh, kh) * decay).astype(_BF16)
            ct, nm = ct_ref[hd], nm_ref[hd]
            pv = jnp.dot(p, jnp.concatenate([vh, ones_rhs], axis=1), preferred_element_type=_F32)
            qc = jnp.dot(qh, jnp.concatenate([ct, nm], axis=1).astype(_BF16), preferred_element_type=_F32)
            num = pv[:, :dh] + twice(iw) * qc[:, :dh]
            den = pv[:, dh:] + iw * qc[:, dh:]
            inv = 1.0 / jnp.maximum(jnp.abs(den), clamp)
            o_ref[:, lo:hi] = (num * twice(inv)).astype(o_ref.dtype)
            vw = jnp.concatenate([(vh.astype(_F32) * twice(wrep)).astype(_BF16), wrep.astype(_BF16)], axis=1)
            upd = lax.dot_general(kh, vw, (((0,), (0,)), ((), ())), preferred_element_type=_F32)
            ct_ref[hd] = keep * ct + upd[:, :dh]
            nm_ref[hd] = keep * nm + upd[:, dh:]


def _mlstm(qk, proj3, a_mat, r_rows, keep):
    b, l, _ = qk.shape
    mw = MLSTM_WIDTH
    nc = l // MLSTM_CHUNK
    rows = _MLSTM_SUB * MLSTM_CHUNK
    nb = l // rows
    fw = lambda c: c
    bw = lambda c: nb - 1 - c

    def specs(cm):
        return [
            pl.BlockSpec((None, rows, mw), lambda i, c: (i, cm(c), 0)),
            pl.BlockSpec((None, rows, mw), lambda i, c: (i, cm(c), 1)),
            pl.BlockSpec((None, rows, mw), lambda i, c: (i, cm(c), _COL_V)),
            pl.BlockSpec((None, rows, 128), lambda i, c: (i, cm(c), 0)),
            pl.BlockSpec((None, 128, rows), lambda i, c: (i, 0, cm(c))),
        ]

    body = functools.partial(_mlstm_body, nc=nc)
    keep_flat = keep[:, :, :_GATE_LANES].reshape(b * nc * _GATE_LANES)
    return pl.pallas_call(
        body,
        grid=(b, nb),
        in_specs=[pl.BlockSpec(memory_space=pltpu.SMEM),
                  pl.BlockSpec((128, 4 * 128), lambda i, c: (0, 0))] + specs(fw) + specs(bw),
        out_specs=[
            pl.BlockSpec((None, rows, mw), lambda i, c: (i, c, 0)),
            pl.BlockSpec((None, rows, mw), lambda i, c: (i, nb - 1 - c, 0)),
        ],
        out_shape=[jax.ShapeDtypeStruct((b, l, mw), _BF16), jax.ShapeDtypeStruct((b, l, mw), _BF16)],
        scratch_shapes=[
            pltpu.VMEM((2 * MLSTM_HEADS, MLSTM_HEAD_DIM, MLSTM_HEAD_DIM), _F32),
            pltpu.VMEM((2 * MLSTM_HEADS, MLSTM_HEAD_DIM, 128), _F32),
        ],
        compiler_params=_cparams(("arbitrary", "arbitrary")),
        name="mlstm",
    )(keep_flat, _gate_spread_matrix(), qk, qk, proj3, a_mat, r_rows, qk, qk, proj3, a_mat, r_rows)


def _sigmoid(x):
    return 1.0 / (1.0 + jnp.exp(-x))


def _merge_body(x_ref, ya_ref, hf_ref, hb_ref, o_ref, ma_ref, mb_ref, wa_ref, wb_ref, wo_ref, out_ref):
    hsum = hf_ref[...].astype(_F32) + hb_ref[...].astype(_F32)
    yb = (_sigmoid(o_ref[...].astype(_F32)) * hsum).astype(_BF16)
    ya = jnp.concatenate([ya_ref[t] for t in range(ya_ref.shape[0])], axis=1)
    pa = jnp.dot(ya, wa_ref[...], preferred_element_type=_F32)
    pb = jnp.dot(yb, wb_ref[...], preferred_element_type=_F32)
    mixed = _sigmoid(ma_ref[...].astype(_F32)) * pa + _sigmoid(mb_ref[...].astype(_F32)) * pb
    out_ref[...] = x_ref[...] + jnp.dot(mixed.astype(_BF16), wo_ref[...], preferred_element_type=_F32)


def _merge(x2, ya3, hf2, hb2, proj, wa, wb, wo):
    m = x2.shape[0]
    tm = _MERGE_ROWS
    d, mw = D_MODEL, MLSTM_WIDTH
    const = lambda shape: pl.BlockSpec(shape, lambda i: (0, 0), pipeline_mode=pl.Buffered(1))
    return pl.pallas_call(
        _merge_body,
        grid=(m // tm,),
        in_specs=[
            pl.BlockSpec((tm, d), lambda i: (i, 0)),
            pl.BlockSpec((ya3.shape[0], tm, ya3.shape[2]), lambda i: (0, i, 0)),
            pl.BlockSpec((tm, mw), lambda i: (i, 0)),
            pl.BlockSpec((tm, mw), lambda i: (i, 0)),
            pl.BlockSpec((tm, mw), lambda i: (i, _COL_O)),
            pl.BlockSpec((tm, d), lambda i: (i, _COL_MA // 2)),
            pl.BlockSpec((tm, d), lambda i: (i, _COL_MB // 2)),
            const((HYENA_WIDTH, d)), const((mw, d)), const((d, d)),
        ],
        out_specs=pl.BlockSpec((tm, d), lambda i: (i, 0)),
        out_shape=jax.ShapeDtypeStruct((m, d), _F32),
        compiler_params=_cparams(("arbitrary",)),
        name="merge",
    )(x2, ya3, hf2, hb2, proj, proj, proj, wa, wb, wo)


def _ffn_body(x_ref, n2_ref, wg_ref, wu_ref, wd_ref, nf_ref, o_ref, hn_ref):
    f = pl.program_id(1)
    last = pl.num_programs(1) - 1
    rows = _NORM_ROWS

    def swiglu_down(hn):
        g = jnp.dot(hn, wg_ref[...], preferred_element_type=_F32)
        u = jnp.dot(hn, wu_ref[...], preferred_element_type=_F32)
        a = (g * _sigmoid(g) * u).astype(_BF16)
        return jnp.dot(a, wd_ref[...], preferred_element_type=_F32)

    @pl.when(f == 0)
    def _():
        for r in range(0, x_ref.shape[0], rows):
            x = x_ref[r:r + rows, :]
            hn = (x * lax.rsqrt(jnp.mean(x * x, axis=-1, keepdims=True) + RMS_EPS) * n2_ref[...]).astype(_BF16)
            hn_ref[r:r + rows, :] = hn
            o_ref[r:r + rows, :] = x + swiglu_down(hn)

    @pl.when(jnp.logical_and(f > 0, f < last))
    def _():
        o_ref[...] += swiglu_down(hn_ref[...])

    @pl.when(f == last)
    def _():
        for r in range(0, x_ref.shape[0], rows):
            y = o_ref[r:r + rows, :] + swiglu_down(hn_ref[r:r + rows, :])
            o_ref[r:r + rows, :] = y * lax.rsqrt(jnp.mean(y * y, axis=-1, keepdims=True) + RMS_EPS) * nf_ref[...]


def _ffn(x2, norm2_w, w_gate_up, w_down, norm_f_w):
    m = x2.shape[0]
    tm, tf = _FFN_TILE
    d = D_MODEL
    nf = FFN_HIDDEN // tf
    return pl.pallas_call(
        _ffn_body,
        grid=(m // tm, nf),
        in_specs=[
            pl.BlockSpec((tm, d), lambda i, f: (i, 0)),
            pl.BlockSpec((1, d), lambda i, f: (0, 0)),
            pl.BlockSpec((d, tf), lambda i, f: (0, f)),
            pl.BlockSpec((d, tf), lambda i, f: (0, nf + f)),
            pl.BlockSpec((tf, d), lambda i, f: (f, 0)),
            pl.BlockSpec((1, d), lambda i, f: (0, 0)),
        ],
        out_specs=pl.BlockSpec((tm, d), lambda i, f: (i, 0)),
        out_shape=jax.ShapeDtypeStruct((m, d), _F32),
        scratch_shapes=[pltpu.VMEM((tm, d), _BF16)],
        compiler_params=_cparams(("arbitrary", "arbitrary")),
        name="ffn",
    )(x2, norm2_w, w_gate_up, w_gate_up, w_down, norm_f_w)


def kernel(x, norm1_w, w_in, conv_w, conv_b, filt_w1, filt_b1, filt_freq1, filt_w2, filt_b2, filt_freq2,
           filt_w3, hyena_bias, mlstm_gate_bias, w_branch_a, w_branch_b, w_out, norm2_w, w_gate_up, w_down,
           norm_f_w):
    b, l, d = x.shape
    assert d == D_MODEL and b % 2 == 0 and l % (_N2 * _T2_PER_PHASE) == 0
    assert norm1_w.shape[0] == 1, "single-layer block"
    hw, mw, nh = HYENA_WIDTH, MLSTM_WIDTH, MLSTM_HEADS
    m = b * l
    n = 2 * l
    n1 = n // _N2
    sc_cols = 3 * hw + 2 * mw
    g0 = sc_cols + 2 * mw

    w_t = jnp.swapaxes(w_in[0], 0, 1)
    w_all_t = w_t.astype(_BF16)
    w_merge_t = w_all_t[g0 + 4 * nh:]
    wg_t = w_t[g0:g0 + 4 * nh]
    gate_order = lambda a: jnp.concatenate(
        [a[0:nh], a[2 * nh:3 * nh], a[nh:2 * nh], a[3 * nh:4 * nh]], axis=0)
    wg_split_t = jnp.concatenate(_split_bf16(jnp.pad(gate_order(wg_t), ((0, 128 - 4 * nh), (0, 0)))), axis=0)
    gate_bias = jnp.pad(gate_order(mlstm_gate_bias[0].astype(_F32).reshape(4 * nh))[None, :],
                        ((0, 0), (0, 128 - 4 * nh)))

    x2 = x.reshape(m, d)
    proj, gates = _inproj(x2, norm1_w[0][None, :], w_merge_t, w_all_t, wg_split_t)
    proj3 = proj.reshape(b, l, _MAIN_COLS)

    cw, cb = conv_w[0], conv_b[0][None, :]
    hy5 = _shortconv(proj3, cw, cb, _COL_HV * 1024, 0, 3 * hw, True)
    qk = _shortconv(proj3, cw, cb, _COL_Q * 1024, 3 * hw, 2 * mw, False)

    fa, fa_full, fai, fb, fbi = _dft_tables(n)
    zpos = _filter_positions(l)
    w1p = jnp.pad(filt_w1[0].astype(_F32), ((0, 128 - HYENA_EMB), (0, 0)))
    max_decay = math.log(HYENA_TARGET) / HYENA_FAST_DECAY
    min_decay = math.log(HYENA_TARGET) / HYENA_SLOW_DECAY
    dabs = jnp.asarray(np.abs(np.linspace(min_decay, max_decay, hw, dtype=np.float32))[None, :])
    dup = lambda a: jnp.concatenate([a, a], axis=-1)
    fh = HYENA_FILTER_HIDDEN
    w3 = filt_w3[0].astype(_F32).reshape(fh, 2, 2, hw).transpose(2, 0, 1, 3).reshape(2 * fh, 2 * hw)
    w3_hi, w3_lo = _split_bf16(w3)
    blockdiag = lambda a: jnp.concatenate(
        [jnp.concatenate([a, jnp.zeros_like(a)], axis=1), jnp.concatenate([jnp.zeros_like(a), a], axis=1)], axis=0)
    kern = _filt_mlp(zpos, blockdiag(w1p), dup(filt_b1[0][None, :]), dup(filt_freq1[0][None, :]),
                     blockdiag(filt_w2[0].astype(_F32)), dup(filt_b2[0][None, :]),
                     dup(filt_freq2[0][None, :]), w3_hi, w3_lo, dabs)
    tiles = hw // _FFT_CT
    kspec = _filt_fft(kern.reshape(2 * tiles, _N2, n1, _FFT_CT), fa_full, fb)
    tabs = (fa, fai, fb, fbi)
    hbias = hyena_bias[0].astype(_F32)
    z5 = _hyena_conv(hy5, 0, hy5, tiles, kspec, 0, hbias[0][None, :], tabs, False)
    ya5 = _hyena_conv(z5, 0, hy5, 2 * tiles, kspec, tiles, hbias[1][None, :], tabs, True)
    ya3 = ya5.reshape(tiles, m, _FFT_CT)

    a_mat, r_rows, keep = _gate_prep(gates.reshape(b, l, 128), gate_bias)
    hf, hb = _mlstm(qk, proj3, a_mat, r_rows, keep)

    x_mid = _merge(x2, ya3, hf.reshape(m, mw), hb.reshape(m, mw), proj,
                   w_branch_a[0].astype(_BF16), w_branch_b[0].astype(_BF16), w_out[0].astype(_BF16))
    out = _ffn(x_mid, norm2_w[0][None, :], w_gate_up[0].astype(_BF16), w_down[0].astype(_BF16),
               norm_f_w[None, :])
    return out.reshape(b, l, d)
```

```python
import functools
import math

import numpy as np
import jax
import jax.numpy as jnp
from jax import lax
from jax.experimental import pallas as pl
from jax.experimental.pallas import tpu as pltpu

_F32 = jnp.float32
_BF16 = jnp.bfloat16
_HIGHEST = lax.Precision.HIGHEST

D_MODEL = 2048
HYENA_WIDTH = 1024
HYENA_EMB = 33
HYENA_FILTER_HIDDEN = 64
HYENA_FAST_DECAY = 0.3
HYENA_SLOW_DECAY = 1.5
HYENA_TARGET = 1e-2
MLSTM_WIDTH = 1024
MLSTM_HEADS = 4
MLSTM_HEAD_DIM = 256
MLSTM_CHUNK = 128
FFN_HIDDEN = 5632
RMS_EPS = 1e-6

_COL_MA, _COL_MB, _COL_HV, _COL_Q, _COL_V, _COL_O = 0, 2, 4, 7, 9, 10
_MAIN_COLS = 11 * 1024

_N2 = 32
_T2_PER_PHASE = 16
_F1_PER_PHASE = 64
_V7X_MXU_WIDTH = 256
_V7X_VMEM_BYTES = 64 * 1024 * 1024
_FFT_CT = _V7X_MXU_WIDTH

_INPROJ_TILE = (1024, 1024)
_FFN_TILE = (1024, 512)
_MERGE_ROWS = 256
_NORM_ROWS = 256

_VMEM_LIMIT = _V7X_VMEM_BYTES * 7 // 8


def _cparams(sem, vmem=_VMEM_LIMIT):
    return pltpu.CompilerParams(dimension_semantics=sem, vmem_limit_bytes=vmem)


def _split_bf16(a):
    hi = a.astype(_BF16)
    lo = (a - hi.astype(_F32)).astype(_BF16)
    return hi, lo


def _dot_nt(a, b):
    return lax.dot_general(a, b, (((1,), (1,)), ((), ())), preferred_element_type=_F32)


def _dot3(a_hi, a_lo, b_hi, b_lo):
    d = functools.partial(jnp.dot, preferred_element_type=_F32)
    return d(a_hi, b_hi) + (d(a_hi, b_lo) + d(a_lo, b_hi))


def _inproj_body(x_ref, nw_ref, wm_ref, w_ref, wg_ref, o_ref, g_ref, hn_ref, *, n_merge):
    j = pl.program_id(1)

    @pl.when(j == 0)
    def _():
        rows = _NORM_ROWS
        for r in range(0, x_ref.shape[0], rows):
            x = x_ref[r:r + rows, :]
            hn = x * lax.rsqrt(jnp.mean(x * x, axis=-1, keepdims=True) + RMS_EPS) * nw_ref[...]
            hn_hi, hn_lo = _split_bf16(hn)
            hn_ref[r:r + rows, :] = hn_hi
            hh_hl = _dot_nt(hn_hi, wg_ref[...])
            g_ref[r:r + rows, :] = hh_hl[:, :128] + (hh_hl[:, 128:] + _dot_nt(hn_lo, wg_ref[0:128, :]))
            o_ref[r:r + rows, :] = _dot_nt(hn_hi, wm_ref[...]).astype(o_ref.dtype)

    @pl.when(jnp.logical_and(j > 0, j < n_merge))
    def _():
        o_ref[...] = _dot_nt(hn_ref[...], wm_ref[...]).astype(o_ref.dtype)

    @pl.when(j >= n_merge)
    def _():
        o_ref[...] = _dot_nt(hn_ref[...], w_ref[...]).astype(o_ref.dtype)


def _inproj(x2, norm_w, w_merge_t, w_all_t, wg_t):
    m = x2.shape[0]
    tm, tn = _INPROJ_TILE
    n_merge = w_merge_t.shape[0] // tn
    body = functools.partial(_inproj_body, n_merge=n_merge)
    return pl.pallas_call(
        body,
        grid=(m // tm, _MAIN_COLS // tn),
        in_specs=[
            pl.BlockSpec((tm, D_MODEL), lambda i, j: (i, 0)),
            pl.BlockSpec((1, D_MODEL), lambda i, j: (0, 0)),
            pl.BlockSpec((tn, D_MODEL), lambda i, j: (jnp.minimum(j, n_merge - 1), 0)),
            pl.BlockSpec((tn, D_MODEL), lambda i, j: (jnp.maximum(j - n_merge, 0), 0)),
            pl.BlockSpec((256, D_MODEL), lambda i, j: (0, 0)),
        ],
        out_specs=[
            pl.BlockSpec((tm, tn), lambda i, j: (i, j)),
            pl.BlockSpec((tm, 128), lambda i, j: (i, 0)),
        ],
        out_shape=[
            jax.ShapeDtypeStruct((m, _MAIN_COLS), _BF16),
            jax.ShapeDtypeStruct((m, 128), _F32),
        ],
        scratch_shapes=[pltpu.VMEM((tm, D_MODEL), _BF16)],
        compiler_params=_cparams(("arbitrary", "arbitrary")),
        name="inproj",
    )(x2, norm_w, w_merge_t, w_all_t, wg_t)


def _shortconv_silu_body(u_ref, w_ref, b_ref, o_ref, *, k_scale):
    u = u_ref[...].astype(_F32)
    n = u.shape[0]
    row = lax.broadcasted_iota(jnp.int32, u.shape, 0)
    up = jnp.where(row == 0, 0.0, pltpu.roll(u, 1, axis=0))
    un = jnp.where(row == n - 1, 0.0, pltpu.roll(u, n - 1, axis=0))
    w = w_ref[...]
    y = up * w[0:1, :] + b_ref[...] + u * w[1:2, :] + un * w[2:3, :]
    y = y * (1.0 / (1.0 + jnp.exp(-y)))
    is_k = pl.program_id(1) >= (MLSTM_WIDTH // u.shape[1])
    o_ref[...] = (y * jnp.where(is_k, k_scale, 1.0)).astype(o_ref.dtype)


def _shortconv_t2_body(u_ref, w_ref, b_ref, o_ref, ut_ref):
    n, c = u_ref.shape
    h1 = n // _N2
    g = 16
    for k in range(h1 // g):
        blk = u_ref[k * g * _N2:(k + 1) * g * _N2, :].reshape(g, _N2, c)
        ut_ref[:, k * g:(k + 1) * g, :] = jnp.swapaxes(blk, 0, 1)
    w = w_ref[...]
    w0, w1, w2, bias = w[0:1, :], w[1:2, :], w[2:3, :], b_ref[...]
    slab = lambda t2: ut_ref[t2].astype(_F32)
    row = lax.broadcasted_iota(jnp.int32, (h1, c), 0)
    before_first = jnp.where(row == 0, 0.0, pltpu.roll(slab(_N2 - 1), 1, axis=0))
    after_last = jnp.where(row == h1 - 1, 0.0, pltpu.roll(slab(0), h1 - 1, axis=0))
    for t2 in range(_N2):
        up = before_first if t2 == 0 else slab(t2 - 1)
        un = after_last if t2 == _N2 - 1 else slab(t2 + 1)
        o_ref[t2] = (up * w0 + bias + slab(t2) * w1 + un * w2).astype(o_ref.dtype)


def _shortconv(proj3, conv_w, conv_b, proj_col0, conv_col0, ncols, t2_major):
    b, l, _ = proj3.shape
    ct = _FFT_CT
    p0, c0 = proj_col0 // ct, conv_col0 // ct
    if t2_major:
        body = _shortconv_t2_body
        out_spec = pl.BlockSpec((None, None, _N2, l // _N2, ct), lambda i, j: (i, j, 0, 0, 0))
        out_shape = jax.ShapeDtypeStruct((b, ncols // ct, _N2, l // _N2, ct), _BF16)
        scratch = [pltpu.VMEM((_N2, l // _N2, ct), _BF16)]
    else:
        body = functools.partial(_shortconv_silu_body, k_scale=MLSTM_HEAD_DIM ** -0.5)
        out_spec = pl.BlockSpec((None, l, ct), lambda i, j: (i, 0, j))
        out_shape = jax.ShapeDtypeStruct((b, l, ncols), _BF16)
        scratch = []
    return pl.pallas_call(
        body,
        grid=(b, ncols // ct),
        in_specs=[
            pl.BlockSpec((None, l, ct), lambda i, j: (i, 0, p0 + j)),
            pl.BlockSpec((3, ct), lambda i, j: (0, c0 + j)),
            pl.BlockSpec((1, ct), lambda i, j: (0, c0 + j)),
        ],
        out_specs=out_spec,
        out_shape=out_shape,
        scratch_shapes=scratch,
        compiler_params=_cparams(("arbitrary", "arbitrary")),
        name="shortconv" if t2_major else "shortconv_silu",
    )(proj3, conv_w, conv_b)


def _dft_tables(n):
    n2 = _N2
    n1 = n // n2
    h1 = n1 // 2
    f1 = np.arange(n1)[:, None]
    t1 = np.arange(n1)[None, :]
    ang = 2.0 * np.pi * ((f1 * t1) % n1) / n1
    c, s = np.cos(ang), np.sin(ang)
    ch, sh = c[:, :h1], s[:, :h1]
    fa = np.block([[ch, sh], [-sh, ch]])
    fa_full = np.concatenate([c, -s], axis=0)
    fai = np.block([[ch.T, -sh.T], [sh.T, ch.T]]) / n
    f1v = np.arange(n1)[:, None, None]
    f2 = np.arange(n2)[None, :, None]
    t2 = np.arange(n2)[None, None, :]
    ph = 2.0 * np.pi * ((f1v * t2 + f2 * t2 * n1) % n) / n
    tc, ts = np.cos(ph), np.sin(ph)
    fb = np.concatenate([np.concatenate([tc, ts], axis=2),
                         np.concatenate([-ts, tc], axis=2)], axis=1)
    tct, tst = np.swapaxes(tc, 1, 2), np.swapaxes(ts, 1, 2)
    fbi = np.concatenate([np.concatenate([tct, -tst], axis=2),
                          np.concatenate([tst, tct], axis=2)], axis=1)
    to = lambda a: jnp.asarray(a.astype(np.float32)).astype(_BF16)
    return to(fa), to(fa_full), to(fai), to(fb), to(fbi)


def _filter_positions(l):
    n = 2 * l
    n1 = n // _N2
    r = (np.arange(_N2)[:, None] + _N2 * np.arange(n1)[None, :]).reshape(-1)
    pos = np.where(r < l, r, n - r)
    pos = np.where(r == l, 0, pos)
    bands = (HYENA_EMB - 1) // 2
    tt = np.linspace(0.0, 1.0, l, dtype=np.float32).astype(np.float64)[pos]
    omega = (2.0 * math.pi * np.arange(l, dtype=np.float32) / l).astype(np.float32)
    freqs = np.linspace(1e-4, bands - 1, bands, dtype=np.float32)
    ang = (omega[:, None] * freqs[None, :]).astype(np.float64)[pos]
    z = np.zeros((n, 128), np.float64)
    z[:, 0] = tt
    z[:, 1:1 + bands] = np.cos(ang)
    z[:, 1 + bands:1 + 2 * bands] = -np.sin(ang)
    z[:, 33] = (r != l)
    z[:, 34] = (r < l)
    return jnp.asarray(z.astype(np.float32))


def _filt_mlp_body(z_ref, zp_ref, w1_ref, b1_ref, q1_ref, w2_ref, b2_ref, q2_ref, w3h_ref, w3l_ref, dabs_ref,
                   o_ref, hh_ref, hl_ref):
    fh = HYENA_FILTER_HIDDEN

    @pl.when(pl.program_id(1) == 0)
    def _():
        hp = jnp.sin(q1_ref[...] * (jnp.dot(zp_ref[...], w1_ref[...], precision=_HIGHEST,
                                            preferred_element_type=_F32) + b1_ref[...]))
        hp = jnp.sin(q2_ref[...] * (jnp.dot(hp, w2_ref[...], precision=_HIGHEST,
                                            preferred_element_type=_F32) + b2_ref[...]))
        lane_p = lax.broadcasted_iota(jnp.int32, hp.shape, 1)
        swapped = pltpu.roll(hp, fh, axis=1)
        h = jnp.concatenate([jnp.where(lane_p < fh, hp, swapped), jnp.where(lane_p < fh, swapped, hp)], axis=0)
        z = z_ref[...]
        valid = z[:, 33:34]
        fwd = z[:, 34:35]
        lane = lax.broadcasted_iota(jnp.int32, h.shape, 1)
        hi, lo = _split_bf16(h * (valid * jnp.where(lane < fh, fwd, 1.0 - fwd)))
        hh_ref[...] = hi
        hl_ref[...] = lo

    raw = _dot3(hh_ref[...], hl_ref[...], w3h_ref[...], w3l_ref[...])
    window = jnp.exp(-z_ref[:, 0:1] * dabs_ref[...])
    o_ref[...] = (raw * window).astype(o_ref.dtype)


_FILT_ROWS = 1024


def _filt_mlp(zpos, w1bd, b1d, q1d, w2bd, b2d, q2d, w3h, w3l, dabs):
    n = zpos.shape[0]
    ct = _FFT_CT
    tr = _FILT_ROWS
    per_order = HYENA_WIDTH // ct
    fh = HYENA_FILTER_HIDDEN
    zpacked = zpos.reshape(n // tr, 2, tr // 2, 128).transpose(0, 2, 1, 3).reshape(n // 2, 256)
    full = lambda shape: pl.BlockSpec(shape, lambda r, i: (0,) * len(shape))
    return pl.pallas_call(
        _filt_mlp_body,
        grid=(n // tr, 2 * per_order),
        in_specs=[
            pl.BlockSpec((tr, 128), lambda r, i: (r, 0)),
            pl.BlockSpec((tr // 2, 256), lambda r, i: (r, 0)),
            full((256, 2 * fh)), full((1, 2 * fh)), full((1, 2 * fh)),
            full((2 * fh, 2 * fh)), full((1, 2 * fh)), full((1, 2 * fh)),
            pl.BlockSpec((2 * fh, ct), lambda r, i: (0, i)),
            pl.BlockSpec((2 * fh, ct), lambda r, i: (0, i)),
            pl.BlockSpec((1, ct), lambda r, i: (0, i % per_order)),
        ],
        out_specs=pl.BlockSpec((None, tr, ct), lambda r, i: (i, r, 0)),
        out_shape=jax.ShapeDtypeStruct((2 * per_order, n, ct), _BF16),
        scratch_shapes=[pltpu.VMEM((tr, 2 * fh), _BF16), pltpu.VMEM((tr, 2 * fh), _BF16)],
        compiler_params=_cparams(("arbitrary", "arbitrary")),
        name="filt_mlp",
    )(zpos, zpacked, w1bd, b1d, q1d, w2bd, b2d, q2d, w3h, w3l, dabs)


_F1_GROUP = 16


def _f1_tiles(f_base, f1pp):
    return [(slice(g * _F1_GROUP, (g + 1) * _F1_GROUP), pl.multiple_of(f_base + g * _F1_GROUP, _F1_GROUP))
            for g in range(f1pp // _F1_GROUP)]


def _gather_f1(s_ref, vre_ref, vim_ref, f_base, f1pp, n1):
    for sl, f0 in _f1_tiles(f_base, f1pp):
        vre_ref[sl] = jnp.swapaxes(s_ref[:, pl.ds(f0, _F1_GROUP), :], 0, 1)
        vim_ref[sl] = jnp.swapaxes(s_ref[:, pl.ds(n1 + f0, _F1_GROUP), :], 0, 1)


def _filt_fft_body(kern_ref, fa_ref, fb_ref, o_ref, s_ref, vre_ref, vim_ref, *, n_in, n1, f1pp):
    s = pl.program_id(1)

    @pl.when(s < n_in)
    def _():
        for i in range(_T2_PER_PHASE):
            s_ref[s * _T2_PER_PHASE + i] = jnp.dot(fa_ref[...], kern_ref[i],
                                                   preferred_element_type=_F32).astype(_BF16)

    @pl.when(s >= n_in)
    def _():
        _gather_f1(s_ref, vre_ref, vim_ref, (s - n_in) * f1pp, f1pp, n1)

        for f in range(f1pp):
            v = jnp.concatenate([vre_ref[f], vim_ref[f]], axis=0)
            o_ref[f * 2 * _N2:(f + 1) * 2 * _N2, :] = jnp.dot(
                fb_ref[f], v, preferred_element_type=_F32).astype(o_ref.dtype)


def _filt_fft(kern4, fa_full, fb):
    tiles, n2, n1, ct = kern4.shape
    n_in = n2 // _T2_PER_PHASE
    f1pp = min(_F1_PER_PHASE, n1)
    n_mid = n1 // f1pp
    body = functools.partial(_filt_fft_body, n_in=n_in, n1=n1, f1pp=f1pp)
    return pl.pallas_call(
        body,
        grid=(tiles, n_in + n_mid),
        in_specs=[
            pl.BlockSpec((None, _T2_PER_PHASE, n1, ct), lambda j, s: (j, jnp.minimum(s, n_in - 1), 0, 0)),
            pl.BlockSpec((2 * n1, n1), lambda j, s: (0, 0)),
            pl.BlockSpec((f1pp, 2 * n2, 2 * n2), lambda j, s: (jnp.maximum(s - n_in, 0), 0, 0)),
        ],
        out_specs=pl.BlockSpec((None, f1pp * 2 * n2, ct), lambda j, s: (j, jnp.maximum(s - n_in, 0), 0)),
        out_shape=jax.ShapeDtypeStruct((tiles, n1 * 2 * n2, ct), _BF16),
        scratch_shapes=[pltpu.VMEM((n2, 2 * n1, ct), _BF16),
                        pltpu.VMEM((f1pp, n2, ct), _BF16), pltpu.VMEM((f1pp, n2, ct), _BF16)],
        compiler_params=_cparams(("arbitrary", "arbitrary")),
        name="filt_fft",
    )(kern4, fa_full, fb)


def _hyena_conv_body(u_ref, g_ref, k_ref, bias_ref, fa_ref, fai_ref, fb_ref, fbi_ref,
                     o_ref, s_ref, vre_ref, vim_ref, wre_ref, wim_ref, y_ref, ukeep_ref,
                     *, n_in, n_mid, n1, f1pp, natural_out):
    s = pl.program_id(2)
    h1 = n1 // 2

    @pl.when(s < n_in)
    def _():
        for i in range(_T2_PER_PHASE):
            u = jnp.concatenate([u_ref[0, i], u_ref[1, i]], axis=0)
            ukeep_ref[s * _T2_PER_PHASE + i] = u
            s_ref[s * _T2_PER_PHASE + i] = jnp.dot(fa_ref[...], u,
                                                   preferred_element_type=_F32).astype(_BF16)

    @pl.when(jnp.logical_and(s >= n_in, s < n_in + n_mid))
    def _():
        tiles = _f1_tiles((s - n_in) * f1pp, f1pp)

        def gather(sl, f0):
            vre_ref[sl] = jnp.swapaxes(s_ref[:, pl.ds(f0, _F1_GROUP), :], 0, 1)
            vim_ref[sl] = jnp.swapaxes(s_ref[:, pl.ds(n1 + f0, _F1_GROUP), :], 0, 1)

        def forward(f):
            v = jnp.concatenate([vre_ref[f], vim_ref[f]], axis=0)
            x = jnp.dot(fb_ref[f], v, preferred_element_type=_F32)
            xre, xim = x[:_N2], x[_N2:]
            kre = k_ref[f * 2 * _N2:f * 2 * _N2 + _N2, :].astype(_F32)
            kim = k_ref[f * 2 * _N2 + _N2:(f + 1) * 2 * _N2, :].astype(_F32)
            y_ref[f] = jnp.concatenate([xre * kre - xim * kim, xre * kim + xim * kre],
                                       axis=0).astype(_BF16)

        def inverse(f):
            w = jnp.dot(fbi_ref[f], y_ref[f], preferred_element_type=_F32)
            wre_ref[f] = w[:_N2].astype(_BF16)
            wim_ref[f] = w[_N2:].astype(_BF16)

        def scatter(sl, f0):
            s_ref[:, pl.ds(f0, _F1_GROUP), :] = jnp.swapaxes(wre_ref[sl], 0, 1)
            s_ref[:, pl.ds(n1 + f0, _F1_GROUP), :] = jnp.swapaxes(wim_ref[sl], 0, 1)

        nt = len(tiles)
        for t in tiles:
            gather(*t)
        for g in range(nt + 1):
            for k in range(_F1_GROUP):
                if g < nt:
                    forward(g * _F1_GROUP + k)
                if g >= 1:
                    inverse((g - 1) * _F1_GROUP + k)
            if g >= 1:
                scatter(*tiles[g - 1])

    @pl.when(s >= n_in + n_mid)
    def _():
        bias = bias_ref[...]
        outs = [[], []]
        for i in range(_T2_PER_PHASE):
            t2 = (s - n_in - n_mid) * _T2_PER_PHASE + i
            y = jnp.dot(fai_ref[...], s_ref[t2], preferred_element_type=_F32)
            y = y + ukeep_ref[t2].astype(_F32) * bias
            for m in range(2):
                res = g_ref[m, i].astype(_F32) * y[m * h1:(m + 1) * h1]
                if natural_out:
                    outs[m].append(res.astype(o_ref.dtype))
                else:
                    o_ref[m, i] = res.astype(o_ref.dtype)
        if natural_out:
            for m in range(2):
                o_ref[m] = jnp.swapaxes(jnp.stack(outs[m], axis=0), 0, 1)


def _hyena_conv(u5, ucol, g5, gcol, kspec, kcol, bias, tabs, natural_out):
    fa, fai, fb, fbi = tabs
    b, _, n2, h1, ct = u5.shape
    n1 = 2 * h1
    n_in = n2 // _T2_PER_PHASE
    f1pp = min(_F1_PER_PHASE, n1)
    n_mid = n1 // f1pp
    n_out = n_in
    tiles = HYENA_WIDTH // ct
    body = functools.partial(_hyena_conv_body, n_in=n_in, n_mid=n_mid, n1=n1, f1pp=f1pp,
                             natural_out=natural_out)
    mid = lambda s: jnp.clip(s - n_in, 0, n_mid - 1)
    last = lambda s: jnp.clip(s - n_in - n_mid, 0, n_out - 1)
    blk = (2, None, _T2_PER_PHASE, h1, ct)
    if natural_out:
        out_spec = pl.BlockSpec((None, 2, h1, _T2_PER_PHASE, ct), lambda j, p, s: (j, p, 0, last(s), 0))
        out_shape = jax.ShapeDtypeStruct((tiles, b, h1, n2, ct), _BF16)
    else:
        out_spec = pl.BlockSpec(blk, lambda j, p, s: (p, j, last(s), 0, 0))
        out_shape = jax.ShapeDtypeStruct((b, tiles, n2, h1, ct), _BF16)
    return pl.pallas_call(
        body,
        grid=(tiles, b // 2, n_in + n_mid + n_out),
        in_specs=[
            pl.BlockSpec(blk, lambda j, p, s: (p, ucol + j, jnp.minimum(s, n_in - 1), 0, 0)),
            pl.BlockSpec(blk, lambda j, p, s: (p, gcol + j, last(s), 0, 0)),
            pl.BlockSpec((None, f1pp * 2 * n2, ct), lambda j, p, s: (kcol + j, mid(s), 0)),
            pl.BlockSpec((1, ct), lambda j, p, s: (0, j)),
            pl.BlockSpec((2 * n1, n1), lambda j, p, s: (0, 0)),
            pl.BlockSpec((n1, 2 * n1), lambda j, p, s: (0, 0)),
            pl.BlockSpec((f1pp, 2 * n2, 2 * n2), lambda j, p, s: (mid(s), 0, 0)),
            pl.BlockSpec((f1pp, 2 * n2, 2 * n2), lambda j, p, s: (mid(s), 0, 0)),
        ],
        out_specs=out_spec,
        out_shape=out_shape,
        scratch_shapes=([pltpu.VMEM((n2, 2 * n1, ct), _BF16)] + 4 * [pltpu.VMEM((f1pp, n2, ct), _BF16)]
                        + [pltpu.VMEM((f1pp, 2 * n2, ct), _BF16), pltpu.VMEM((n2, n1, ct), _BF16)]),
        compiler_params=_cparams(("arbitrary", "arbitrary", "arbitrary")),
        name="hyena_conv",
    )(u5, g5, kspec, bias, fa, fai, fb, fbi)


_GATE_LANES = 2 * MLSTM_HEADS
_TERM_R = 12
_LOG2E = math.log2(math.e)
_MLSTM_SUB = 4


def _split3(x):
    hi = x.astype(_BF16).astype(_F32)
    mid = (x - hi).astype(_BF16).astype(_F32)
    lo = (x - hi - mid).astype(_BF16).astype(_F32)
    return hi, mid, lo


def _scan_order_max(x, is_fwd):
    n = x.shape[0]
    row = lax.broadcasted_iota(jnp.int32, x.shape, 0)
    pre, suf = x, x
    shift = 1
    while shift < n:
        pre = jnp.maximum(pre, jnp.where(row >= shift, pltpu.roll(pre, shift, axis=0), -jnp.inf))
        suf = jnp.maximum(suf, jnp.where(row < n - shift, pltpu.roll(suf, n - shift, axis=0), -jnp.inf))
        shift *= 2
    return jnp.where(is_fwd, pre, suf)


def _gate_prep_body(g_ref, bias_ref, a_ref, row_ref, keep_ref,
                    b_scr, a_scr, pm_scr, tot_scr, mloc_scr, mprev_scr, *, nc):
    ch = MLSTM_CHUNK
    lane = lax.broadcasted_iota(jnp.int32, (1, 128), 1)
    is_fwd = lane < MLSTM_HEADS
    live = lane < _GATE_LANES
    jj = lax.broadcasted_iota(jnp.int32, (ch, ch), 0)
    ss = lax.broadcasted_iota(jnp.int32, (ch, ch), 1)
    t_lo = (ss <= jj).astype(_F32)
    row_ref[...] = jnp.zeros_like(row_ref)

    def chunk_stats(c, carry):
        r0 = pl.multiple_of(c * ch, ch)
        gi = g_ref[pl.ds(r0, ch), :] + bias_ref[...]
        gf = pltpu.roll(gi, 128 - 2 * MLSTM_HEADS, axis=1)
        logf = jnp.minimum(gf, 0.0) - jnp.log1p(jnp.exp(-jnp.abs(gf)))
        cs_lo = jnp.dot(t_lo, logf, precision=_HIGHEST, preferred_element_type=_F32)
        tot = cs_lo[ch - 1:ch, :]
        cs_up = tot - cs_lo + logf
        bcs = jnp.where(is_fwd, cs_lo, cs_up)
        a = tot - bcs + gi
        b_scr[pl.ds(r0, ch), :] = bcs
        a_scr[pl.ds(r0, ch), :] = a
        tot_scr[pl.ds(c, 1), :] = tot
        mloc_scr[pl.ds(c, 1), :] = jnp.max(a, axis=0, keepdims=True)
        r = gi - bcs
        pm_scr[pl.ds(r0, ch), :] = _scan_order_max(r, is_fwd)
        hi, mid, lo = _split3(r.T[0:_GATE_LANES, :] * _LOG2E)
        base = _TERM_R * _GATE_LANES
        row_ref[base:base + 16, pl.ds(r0, ch)] = jnp.concatenate([hi, mid], axis=0).astype(_BF16)
        row_ref[base + 16:base + 32, pl.ds(r0, ch)] = jnp.concatenate(
            [lo, jnp.zeros_like(lo)], axis=0).astype(_BF16)
        return carry

    lax.fori_loop(0, nc, chunk_stats, 0, unroll=4)

    def scan_f(c, m):
        mprev_scr[pl.ds(c, 1), :] = jnp.where(is_fwd, m, mprev_scr[pl.ds(c, 1), :])
        return jnp.maximum(tot_scr[pl.ds(c, 1), :] + m, mloc_scr[pl.ds(c, 1), :])

    def scan_b(i, m):
        c = nc - 1 - i
        mprev_scr[pl.ds(c, 1), :] = jnp.where(is_fwd, mprev_scr[pl.ds(c, 1), :], m)
        return jnp.maximum(tot_scr[pl.ds(c, 1), :] + m, mloc_scr[pl.ds(c, 1), :])

    mprev_scr[...] = jnp.zeros_like(mprev_scr)
    lax.fori_loop(0, nc, scan_f, jnp.zeros((1, 128), _F32))
    lax.fori_loop(0, nc, scan_b, jnp.zeros((1, 128), _F32))

    def emit(c, carry):
        r0 = pl.multiple_of(c * ch, ch)
        mprev = mprev_scr[pl.ds(c, 1), :]
        tot = tot_scr[pl.ds(c, 1), :]
        mnew = jnp.maximum(tot + mprev, mloc_scr[pl.ds(c, 1), :])
        keep_ref[pl.ds(c, 1), :] = jnp.exp(tot + mprev - mnew)
        g = jnp.maximum(mprev, pm_scr[pl.ds(r0, ch), :])
        terms = (_split3(-g * _LOG2E) + _split3((mprev - g) * _LOG2E)
                 + _split3((-b_scr[pl.ds(r0, ch), :] - g) * _LOG2E)
                 + _split3((a_scr[pl.ds(r0, ch), :] - mnew) * _LOG2E) + 3 * (jnp.ones((ch, 128), _F32),))
        pack = jnp.where(live, terms[0], 0.0)
        for t in range(1, len(terms)):
            pack = pack + pltpu.roll(jnp.where(live, terms[t], 0.0), _GATE_LANES * t, axis=1)
        a_ref[pl.ds(r0, ch), :] = pack.astype(_BF16)
        return carry

    lax.fori_loop(0, nc, emit, 0, unroll=4)


def _gate_prep(g3, bias):
    b, l, _ = g3.shape
    nc = l // MLSTM_CHUNK
    body = functools.partial(_gate_prep_body, nc=nc)
    return pl.pallas_call(
        body,
        grid=(b,),
        in_specs=[
            pl.BlockSpec((None, l, 128), lambda i: (i, 0, 0)),
            pl.BlockSpec((1, 128), lambda i: (0, 0)),
        ],
        out_specs=[
            pl.BlockSpec((None, l, 128), lambda i: (i, 0, 0)),
            pl.BlockSpec((None, 128, l), lambda i: (i, 0, 0)),
            pl.BlockSpec((None, nc, 128), lambda i: (i, 0, 0)),
        ],
        out_shape=[
            jax.ShapeDtypeStruct((b, l, 128), _BF16),
            jax.ShapeDtypeStruct((b, 128, l), _BF16),
            jax.ShapeDtypeStruct((b, nc, 128), _F32),
        ],
        scratch_shapes=[
            pltpu.VMEM((l, 128), _F32), pltpu.VMEM((l, 128), _F32), pltpu.VMEM((l, 128), _F32),
            pltpu.VMEM((nc, 128), _F32), pltpu.VMEM((nc, 128), _F32), pltpu.VMEM((nc, 128), _F32),
        ],
        compiler_params=_cparams(("arbitrary",)),
        name="gate_prep",
    )(g3, bias)


def _gate_spread_matrix():
    bc = np.zeros((128, 4 * 128), np.float32)
    for blk in range(4):
        for t in range(3 * blk, 3 * blk + 3):
            bc[t * _GATE_LANES:(t + 1) * _GATE_LANES, blk * 128:(blk + 1) * 128] = 1.0
    return jnp.asarray(bc).astype(_BF16)


def _mlstm_body(keep_ref, bc_ref, qf_ref, kf_ref, vf_ref, af_ref, rf_ref, qb_ref, kb_ref, vb_ref, ab_ref,
                rb_ref, hf_ref, hb_ref, ct_ref, nm_ref, *, nc):
    bi, c = pl.program_id(0), pl.program_id(1)

    @pl.when(c == 0)
    def _():
        ct_ref[...] = jnp.zeros_like(ct_ref)
        nm_ref[...] = jnp.zeros_like(nm_ref)

    ch, dh = MLSTM_CHUNK, MLSTM_HEAD_DIM
    jj = lax.broadcasted_iota(jnp.int32, (ch, ch), 0)
    ss = lax.broadcasted_iota(jnp.int32, (ch, ch), 1)
    head_lane = ss % _GATE_LANES
    ones_rhs = jnp.ones((ch, 128), _BF16)
    twice = lambda a: jnp.concatenate([a, a], axis=1)
    dirs = ((qf_ref, kf_ref, vf_ref, af_ref, rf_ref, hf_ref, ss <= jj),
            (qb_ref, kb_ref, vb_ref, ab_ref, rb_ref, hb_ref, ss >= jj))
    for sub, d in [(sub, d) for sub in range(_MLSTM_SUB) for d in range(2)]:
        q_blk, k_blk, v_blk, a_blk, r_blk, o_blk, mask = dirs[d]
        pos = sub if d == 0 else _MLSTM_SUB - 1 - sub
        chunk = c * _MLSTM_SUB + pos if d == 0 else nc - (c + 1) * _MLSTM_SUB + pos
        rows = slice(pos * ch, (pos + 1) * ch)
        q_ref, k_ref, v_ref, o_ref = q_blk.at[rows], k_blk.at[rows], v_blk.at[rows], o_blk.at[rows]
        bmat = jnp.concatenate([bc_ref[:, 0:128] + r_blk[:, rows], bc_ref[:, 128:]], axis=1)
        a_all = a_blk[rows, :].astype(_F32)
        kbase = (bi * nc + chunk) * _GATE_LANES
        for h in range(MLSTM_HEADS):
            hd = d * MLSTM_HEADS + h
            a_h = jnp.where(head_lane == hd, a_all, 0.0).astype(_BF16)
            e = jnp.dot(a_h, bmat, preferred_element_type=_F32)
            decay = jnp.exp2(jnp.where(mask, e[:, 0:128], -jnp.inf))
            rest = jnp.exp2(e[:, 128:])
            iw, clamp, wrep = rest[:, 0:128], rest[:, 128:256], rest[:, 256:384]
            keep = keep_ref[kbase + hd]
            lo, hi = h * dh, (h + 1) * dh
            qh, kh, vh = q_ref[:, lo:hi], k_ref[:, lo:hi], v_ref[:, lo:hi]
            p = (_dot_nt(qh, kh) * decay).astype(_BF16)
            ct, nm = ct_ref[hd], nm_ref[hd]
            pv = jnp.dot(p, jnp.concatenate([vh, ones_rhs], axis=1), preferred_element_type=_F32)
            qc = jnp.dot(qh, jnp.concatenate([ct, nm], axis=1).astype(_BF16), preferred_element_type=_F32)
            num = pv[:, :dh] + twice(iw) * qc[:, :dh]
            den = pv[:, dh:] + iw * qc[:, dh:]
            inv = 1.0 / jnp.maximum(jnp.abs(den), clamp)
            o_ref[:, lo:hi] = (num * twice(inv)).astype(o_ref.dtype)
            vw = jnp.concatenate([(vh.astype(_F32) * twice(wrep)).astype(_BF16), wrep.astype(_BF16)], axis=1)
            upd = lax.dot_general(kh, vw, (((0,), (0,)), ((), ())), preferred_element_type=_F32)
            ct_ref[hd] = keep * ct + upd[:, :dh]
            nm_ref[hd] = keep * nm + upd[:, dh:]


def _mlstm(qk, proj3, a_mat, r_rows, keep):
    b, l, _ = qk.shape
    mw = MLSTM_WIDTH
    nc = l // MLSTM_CHUNK
    rows = _MLSTM_SUB * MLSTM_CHUNK
    nb = l // rows
    fw = lambda c: c
    bw = lambda c: nb - 1 - c

    def specs(cm):
        return [
            pl.BlockSpec((None, rows, mw), lambda i, c: (i, cm(c), 0)),
            pl.BlockSpec((None, rows, mw), lambda i, c: (i, cm(c), 1)),
            pl.BlockSpec((None, rows, mw), lambda i, c: (i, cm(c), _COL_V)),
            pl.BlockSpec((None, rows, 128), lambda i, c: (i, cm(c), 0)),
            pl.BlockSpec((None, 128, rows), lambda i, c: (i, 0, cm(c))),
        ]

    body = functools.partial(_mlstm_body, nc=nc)
    keep_flat = keep[:, :, :_GATE_LANES].reshape(b * nc * _GATE_LANES)
    return pl.pallas_call(
        body,
        grid=(b, nb),
        in_specs=[pl.BlockSpec(memory_space=pltpu.SMEM),
                  pl.BlockSpec((128, 4 * 128), lambda i, c: (0, 0))] + specs(fw) + specs(bw),
        out_specs=[
            pl.BlockSpec((None, rows, mw), lambda i, c: (i, c, 0)),
            pl.BlockSpec((None, rows, mw), lambda i, c: (i, nb - 1 - c, 0)),
        ],
        out_shape=[jax.ShapeDtypeStruct((b, l, mw), _BF16), jax.ShapeDtypeStruct((b, l, mw), _BF16)],
        scratch_shapes=[
            pltpu.VMEM((2 * MLSTM_HEADS, MLSTM_HEAD_DIM, MLSTM_HEAD_DIM), _F32),
            pltpu.VMEM((2 * MLSTM_HEADS, MLSTM_HEAD_DIM, 128), _F32),
        ],
        compiler_params=_cparams(("arbitrary", "arbitrary")),
        name="mlstm",
    )(keep_flat, _gate_spread_matrix(), qk, qk, proj3, a_mat, r_rows, qk, qk, proj3, a_mat, r_rows)


def _sigmoid(x):
    return 1.0 / (1.0 + jnp.exp(-x))


def _merge_body(x_ref, ya_ref, hf_ref, hb_ref, o_ref, ma_ref, mb_ref, wa_ref, wb_ref, wo_ref, out_ref):
    hsum = hf_ref[...].astype(_F32) + hb_ref[...].astype(_F32)
    yb = (_sigmoid(o_ref[...].astype(_F32)) * hsum).astype(_BF16)
    ya = jnp.concatenate([ya_ref[t] for t in range(ya_ref.shape[0])], axis=1)
    pa = jnp.dot(ya, wa_ref[...], preferred_element_type=_F32)
    pb = jnp.dot(yb, wb_ref[...], preferred_element_type=_F32)
    mixed = _sigmoid(ma_ref[...].astype(_F32)) * pa + _sigmoid(mb_ref[...].astype(_F32)) * pb
    out_ref[...] = x_ref[...] + jnp.dot(mixed.astype(_BF16), wo_ref[...], preferred_element_type=_F32)


def _merge(x2, ya3, hf2, hb2, proj, wa, wb, wo):
    m = x2.shape[0]
    tm = _MERGE_ROWS
    d, mw = D_MODEL, MLSTM_WIDTH
    const = lambda shape: pl.BlockSpec(shape, lambda i: (0, 0), pipeline_mode=pl.Buffered(1))
    return pl.pallas_call(
        _merge_body,
        grid=(m // tm,),
        in_specs=[
            pl.BlockSpec((tm, d), lambda i: (i, 0)),
            pl.BlockSpec((ya3.shape[0], tm, ya3.shape[2]), lambda i: (0, i, 0)),
            pl.BlockSpec((tm, mw), lambda i: (i, 0)),
            pl.BlockSpec((tm, mw), lambda i: (i, 0)),
            pl.BlockSpec((tm, mw), lambda i: (i, _COL_O)),
            pl.BlockSpec((tm, d), lambda i: (i, _COL_MA // 2)),
            pl.BlockSpec((tm, d), lambda i: (i, _COL_MB // 2)),
            const((HYENA_WIDTH, d)), const((mw, d)), const((d, d)),
        ],
        out_specs=pl.BlockSpec((tm, d), lambda i: (i, 0)),
        out_shape=jax.ShapeDtypeStruct((m, d), _F32),
        compiler_params=_cparams(("arbitrary",)),
        name="merge",
    )(x2, ya3, hf2, hb2, proj, proj, proj, wa, wb, wo)


def _ffn_body(x_ref, n2_ref, wg_ref, wu_ref, wd_ref, nf_ref, o_ref, hn_ref):
    f = pl.program_id(1)
    last = pl.num_programs(1) - 1
    rows = _NORM_ROWS

    def swiglu_down(hn):
        g = jnp.dot(hn, wg_ref[...], preferred_element_type=_F32)
        u = jnp.dot(hn, wu_ref[...], preferred_element_type=_F32)
        a = (g * _sigmoid(g) * u).astype(_BF16)
        return jnp.dot(a, wd_ref[...], preferred_element_type=_F32)

    @pl.when(f == 0)
    def _():
        for r in range(0, x_ref.shape[0], rows):
            x = x_ref[r:r + rows, :]
            hn = (x * lax.rsqrt(jnp.mean(x * x, axis=-1, keepdims=True) + RMS_EPS) * n2_ref[...]).astype(_BF16)
            hn_ref[r:r + rows, :] = hn
            o_ref[r:r + rows, :] = x + swiglu_down(hn)

    @pl.when(jnp.logical_and(f > 0, f < last))
    def _():
        o_ref[...] += swiglu_down(hn_ref[...])

    @pl.when(f == last)
    def _():
        for r in range(0, x_ref.shape[0], rows):
            y = o_ref[r:r + rows, :] + swiglu_down(hn_ref[r:r + rows, :])
            o_ref[r:r + rows, :] = y * lax.rsqrt(jnp.mean(y * y, axis=-1, keepdims=True) + RMS_EPS) * nf_ref[...]


def _ffn(x2, norm2_w, w_gate_up, w_down, norm_f_w):
    m = x2.shape[0]
    tm, tf = _FFN_TILE
    d = D_MODEL
    nf = FFN_HIDDEN // tf
    return pl.pallas_call(
        _ffn_body,
        grid=(m // tm, nf),
        in_specs=[
            pl.BlockSpec((tm, d), lambda i, f: (i, 0)),
            pl.BlockSpec((1, d), lambda i, f: (0, 0)),
            pl.BlockSpec((d, tf), lambda i, f: (0, f)),
            pl.BlockSpec((d, tf), lambda i, f: (0, nf + f)),
            pl.BlockSpec((tf, d), lambda i, f: (f, 0)),
            pl.BlockSpec((1, d), lambda i, f: (0, 0)),
        ],
        out_specs=pl.BlockSpec((tm, d), lambda i, f: (i, 0)),
        out_shape=jax.ShapeDtypeStruct((m, d), _F32),
        scratch_shapes=[pltpu.VMEM((tm, d), _BF16)],
        compiler_params=_cparams(("arbitrary", "arbitrary")),
        name="ffn",
    )(x2, norm2_w, w_gate_up, w_gate_up, w_down, norm_f_w)


def kernel(x, norm1_w, w_in, conv_w, conv_b, filt_w1, filt_b1, filt_freq1, filt_w2, filt_b2, filt_freq2,
           filt_w3, hyena_bias, mlstm_gate_bias, w_branch_a, w_branch_b, w_out, norm2_w, w_gate_up, w_down,
           norm_f_w):
    b, l, d = x.shape
    assert d == D_MODEL and b % 2 == 0 and l % (_N2 * _T2_PER_PHASE) == 0
    assert norm1_w.shape[0] == 1, "single-layer block"
    hw, mw, nh = HYENA_WIDTH, MLSTM_WIDTH, MLSTM_HEADS
    m = b * l
    n = 2 * l
    n1 = n // _N2
    sc_cols = 3 * hw + 2 * mw
    g0 = sc_cols + 2 * mw

    w_t = jnp.swapaxes(w_in[0], 0, 1)
    w_all_t = w_t.astype(_BF16)
    w_merge_t = w_all_t[g0 + 4 * nh:]
    wg_t = w_t[g0:g0 + 4 * nh]
    gate_order = lambda a: jnp.concatenate(
        [a[0:nh], a[2 * nh:3 * nh], a[nh:2 * nh], a[3 * nh:4 * nh]], axis=0)
    wg_split_t = jnp.concatenate(_split_bf16(jnp.pad(gate_order(wg_t), ((0, 128 - 4 * nh), (0, 0)))), axis=0)
    gate_bias = jnp.pad(gate_order(mlstm_gate_bias[0].astype(_F32).reshape(4 * nh))[None, :],
                        ((0, 0), (0, 128 - 4 * nh)))

    x2 = x.reshape(m, d)
    proj, gates = _inproj(x2, norm1_w[0][None, :], w_merge_t, w_all_t, wg_split_t)
    proj3 = proj.reshape(b, l, _MAIN_COLS)

    cw, cb = conv_w[0], conv_b[0][None, :]
    hy5 = _shortconv(proj3, cw, cb, _COL_HV * 1024, 0, 3 * hw, True)
    qk = _shortconv(proj3, cw, cb, _COL_Q * 1024, 3 * hw, 2 * mw, False)

    fa, fa_full, fai, fb, fbi = _dft_tables(n)
    zpos = _filter_positions(l)
    w1p = jnp.pad(filt_w1[0].astype(_F32), ((0, 128 - HYENA_EMB), (0, 0)))
    max_decay = math.log(HYENA_TARGET) / HYENA_FAST_DECAY
    min_decay = math.log(HYENA_TARGET) / HYENA_SLOW_DECAY
    dabs = jnp.asarray(np.abs(np.linspace(min_decay, max_decay, hw, dtype=np.float32))[None, :])
    dup = lambda a: jnp.concatenate([a, a], axis=-1)
    fh = HYENA_FILTER_HIDDEN
    w3 = filt_w3[0].astype(_F32).reshape(fh, 2, 2, hw).transpose(2, 0, 1, 3).reshape(2 * fh, 2 * hw)
    w3_hi, w3_lo = _split_bf16(w3)
    blockdiag = lambda a: jnp.concatenate(
        [jnp.concatenate([a, jnp.zeros_like(a)], axis=1), jnp.concatenate([jnp.zeros_like(a), a], axis=1)], axis=0)
    kern = _filt_mlp(zpos, blockdiag(w1p), dup(filt_b1[0][None, :]), dup(filt_freq1[0][None, :]),
                     blockdiag(filt_w2[0].astype(_F32)), dup(filt_b2[0][None, :]),
                     dup(filt_freq2[0][None, :]), w3_hi, w3_lo, dabs)
    tiles = hw // _FFT_CT
    kspec = _filt_fft(kern.reshape(2 * tiles, _N2, n1, _FFT_CT), fa_full, fb)
    tabs = (fa, fai, fb, fbi)
    hbias = hyena_bias[0].astype(_F32)
    z5 = _hyena_conv(hy5, 0, hy5, tiles, kspec, 0, hbias[0][None, :], tabs, False)
    ya5 = _hyena_conv(z5, 0, hy5, 2 * tiles, kspec, tiles, hbias[1][None, :], tabs, True)
    ya3 = ya5.reshape(tiles, m, _FFT_CT)

    a_mat, r_rows, keep = _gate_prep(gates.reshape(b, l, 128), gate_bias)
    hf, hb = _mlstm(qk, proj3, a_mat, r_rows, keep)

    x_mid = _merge(x2, ya3, hf.reshape(m, mw), hb.reshape(m, mw), proj,
                   w_branch_a[0].astype(_BF16), w_branch_b[0].astype(_BF16), w_out[0].astype(_BF16))
    out = _ffn(x_mid, norm2_w[0][None, :], w_gate_up[0].astype(_BF16), w_down[0].astype(_BF16),
               norm_f_w[None, :])
    return out.reshape(b, l, d)
```

```python
import functools
import math

import numpy as np
import jax
import jax.numpy as jnp
from jax import lax
from jax.experimental import pallas as pl
from jax.experimental.pallas import tpu as pltpu

_F32 = jnp.float32
_BF16 = jnp.bfloat16
_HIGHEST = lax.Precision.HIGHEST

D_MODEL = 2048
HYENA_WIDTH = 1024
HYENA_EMB = 33
HYENA_FILTER_HIDDEN = 64
HYENA_FAST_DECAY = 0.3
HYENA_SLOW_DECAY = 1.5
HYENA_TARGET = 1e-2
MLSTM_WIDTH = 1024
MLSTM_HEADS = 4
MLSTM_HEAD_DIM = 256
MLSTM_CHUNK = 128
FFN_HIDDEN = 5632
RMS_EPS = 1e-6

_COL_MA, _COL_MB, _COL_HV, _COL_Q, _COL_V, _COL_O = 0, 2, 4, 7, 9, 10
_MAIN_COLS = 11 * 1024

_N2 = 32
_T2_PER_PHASE = 16
_F1_PER_PHASE = 64
_V7X_MXU_WIDTH = 256
_V7X_VMEM_BYTES = 64 * 1024 * 1024
_FFT_CT = _V7X_MXU_WIDTH

_INPROJ_TILE = (1024, 1024)
_FFN_TILE = (1024, 512)
_MERGE_ROWS = 256
_NORM_ROWS = 256

_VMEM_LIMIT = _V7X_VMEM_BYTES * 7 // 8


def _cparams(sem, vmem=_VMEM_LIMIT):
    return pltpu.CompilerParams(dimension_semantics=sem, vmem_limit_bytes=vmem)


def _split_bf16(a):
    hi = a.astype(_BF16)
    lo = (a - hi.astype(_F32)).astype(_BF16)
    return hi, lo


def _dot_nt(a, b):
    return lax.dot_general(a, b, (((1,), (1,)), ((), ())), preferred_element_type=_F32)


def _dot3(a_hi, a_lo, b_hi, b_lo):
    d = functools.partial(jnp.dot, preferred_element_type=_F32)
    return d(a_hi, b_hi) + (d(a_hi, b_lo) + d(a_lo, b_hi))


def _inproj_body(x_ref, nw_ref, wm_ref, w_ref, wg_ref, o_ref, g_ref, hn_ref, *, n_merge):
    j = pl.program_id(1)

    @pl.when(j == 0)
    def _():
        rows = _NORM_ROWS
        for r in range(0, x_ref.shape[0], rows):
            x = x_ref[r:r + rows, :]
            hn = x * lax.rsqrt(jnp.mean(x * x, axis=-1, keepdims=True) + RMS_EPS) * nw_ref[...]
            hn_hi, hn_lo = _split_bf16(hn)
            hn_ref[r:r + rows, :] = hn_hi
            hh_hl = _dot_nt(hn_hi, wg_ref[...])
            g_ref[r:r + rows, :] = hh_hl[:, :128] + (hh_hl[:, 128:] + _dot_nt(hn_lo, wg_ref[0:128, :]))
            o_ref[r:r + rows, :] = _dot_nt(hn_hi, wm_ref[...]).astype(o_ref.dtype)

    @pl.when(jnp.logical_and(j > 0, j < n_merge))
    def _():
        o_ref[...] = _dot_nt(hn_ref[...], wm_ref[...]).astype(o_ref.dtype)

    @pl.when(j >= n_merge)
    def _():
        o_ref[...] = _dot_nt(hn_ref[...], w_ref[...]).astype(o_ref.dtype)


def _inproj(x2, norm_w, w_merge_t, w_all_t, wg_t):
    m = x2.shape[0]
    tm, tn = _INPROJ_TILE
    n_merge = w_merge_t.shape[0] // tn
    body = functools.partial(_inproj_body, n_merge=n_merge)
    return pl.pallas_call(
        body,
        grid=(m // tm, _MAIN_COLS // tn),
        in_specs=[
            pl.BlockSpec((tm, D_MODEL), lambda i, j: (i, 0)),
            pl.BlockSpec((1, D_MODEL), lambda i, j: (0, 0)),
            pl.BlockSpec((tn, D_MODEL), lambda i, j: (jnp.minimum(j, n_merge - 1), 0)),
            pl.BlockSpec((tn, D_MODEL), lambda i, j: (jnp.maximum(j - n_merge, 0), 0)),
            pl.BlockSpec((256, D_MODEL), lambda i, j: (0, 0)),
        ],
        out_specs=[
            pl.BlockSpec((tm, tn), lambda i, j: (i, j)),
            pl.BlockSpec((tm, 128), lambda i, j: (i, 0)),
        ],
        out_shape=[
            jax.ShapeDtypeStruct((m, _MAIN_COLS), _BF16),
            jax.ShapeDtypeStruct((m, 128), _F32),
        ],
        scratch_shapes=[pltpu.VMEM((tm, D_MODEL), _BF16)],
        compiler_params=_cparams(("arbitrary", "arbitrary")),
        name="inproj",
    )(x2, norm_w, w_merge_t, w_all_t, wg_t)


def _shortconv_silu_body(u_ref, w_ref, b_ref, o_ref):
    u = u_ref[...].astype(_F32)
    n = u.shape[0]
    row = lax.broadcasted_iota(jnp.int32, u.shape, 0)
    up = jnp.where(row == 0, 0.0, pltpu.roll(u, 1, axis=0))
    un = jnp.where(row == n - 1, 0.0, pltpu.roll(u, n - 1, axis=0))
    w = w_ref[...]
    y = up * w[0:1, :] + b_ref[...] + u * w[1:2, :] + un * w[2:3, :]
    o_ref[...] = (y * (1.0 / (1.0 + jnp.exp(-y)))).astype(o_ref.dtype)


def _shortconv_t2_body(u_ref, w_ref, b_ref, o_ref, ut_ref):
    n, c = u_ref.shape
    h1 = n // _N2
    g = 16
    for k in range(h1 // g):
        blk = u_ref[k * g * _N2:(k + 1) * g * _N2, :].reshape(g, _N2, c)
        ut_ref[:, k * g:(k + 1) * g, :] = jnp.swapaxes(blk, 0, 1)
    w = w_ref[...]
    w0, w1, w2, bias = w[0:1, :], w[1:2, :], w[2:3, :], b_ref[...]
    slab = lambda t2: ut_ref[t2].astype(_F32)
    row = lax.broadcasted_iota(jnp.int32, (h1, c), 0)
    before_first = jnp.where(row == 0, 0.0, pltpu.roll(slab(_N2 - 1), 1, axis=0))
    after_last = jnp.where(row == h1 - 1, 0.0, pltpu.roll(slab(0), h1 - 1, axis=0))
    for t2 in range(_N2):
        up = before_first if t2 == 0 else slab(t2 - 1)
        un = after_last if t2 == _N2 - 1 else slab(t2 + 1)
        o_ref[t2] = (up * w0 + bias + slab(t2) * w1 + un * w2).astype(o_ref.dtype)


def _shortconv(proj3, conv_w, conv_b, proj_col0, conv_col0, ncols, t2_major):
    b, l, _ = proj3.shape
    ct = _FFT_CT
    p0, c0 = proj_col0 // ct, conv_col0 // ct
    if t2_major:
        body = _shortconv_t2_body
        out_spec = pl.BlockSpec((None, None, _N2, l // _N2, ct), lambda i, j: (i, j, 0, 0, 0))
        out_shape = jax.ShapeDtypeStruct((b, ncols // ct, _N2, l // _N2, ct), _BF16)
        scratch = [pltpu.VMEM((_N2, l // _N2, ct), _BF16)]
    else:
        body = _shortconv_silu_body
        out_spec = pl.BlockSpec((None, l, ct), lambda i, j: (i, 0, j))
        out_shape = jax.ShapeDtypeStruct((b, l, ncols), _BF16)
        scratch = []
    return pl.pallas_call(
        body,
        grid=(b, ncols // ct),
        in_specs=[
            pl.BlockSpec((None, l, ct), lambda i, j: (i, 0, p0 + j)),
            pl.BlockSpec((3, ct), lambda i, j: (0, c0 + j)),
            pl.BlockSpec((1, ct), lambda i, j: (0, c0 + j)),
        ],
        out_specs=out_spec,
        out_shape=out_shape,
        scratch_shapes=scratch,
        compiler_params=_cparams(("arbitrary", "arbitrary")),
        name="shortconv" if t2_major else "shortconv_silu",
    )(proj3, conv_w, conv_b)


def _dft_tables(n):
    n2 = _N2
    n1 = n // n2
    h1 = n1 // 2
    f1 = np.arange(n1)[:, None]
    t1 = np.arange(n1)[None, :]
    ang = 2.0 * np.pi * ((f1 * t1) % n1) / n1
    c, s = np.cos(ang), np.sin(ang)
    ch, sh = c[:, :h1], s[:, :h1]
    fa = np.block([[ch, sh], [-sh, ch]])
    fa_full = np.concatenate([c, -s], axis=0)
    fai = np.block([[ch.T, -sh.T], [sh.T, ch.T]]) / n
    f1v = np.arange(n1)[:, None, None]
    f2 = np.arange(n2)[None, :, None]
    t2 = np.arange(n2)[None, None, :]
    ph = 2.0 * np.pi * ((f1v * t2 + f2 * t2 * n1) % n) / n
    tc, ts = np.cos(ph), np.sin(ph)
    fb = np.concatenate([np.concatenate([tc, ts], axis=2),
                         np.concatenate([-ts, tc], axis=2)], axis=1)
    tct, tst = np.swapaxes(tc, 1, 2), np.swapaxes(ts, 1, 2)
    fbi = np.concatenate([np.concatenate([tct, -tst], axis=2),
                          np.concatenate([tst, tct], axis=2)], axis=1)
    to = lambda a: jnp.asarray(a.astype(np.float32)).astype(_BF16)
    return to(fa), to(fa_full), to(fai), to(fb), to(fbi)


def _filter_positions(l):
    n = 2 * l
    n1 = n // _N2
    r = (np.arange(_N2)[:, None] + _N2 * np.arange(n1)[None, :]).reshape(-1)
    pos = np.where(r < l, r, n - r)
    pos = np.where(r == l, 0, pos)
    bands = (HYENA_EMB - 1) // 2
    tt = np.linspace(0.0, 1.0, l, dtype=np.float32).astype(np.float64)[pos]
    omega = (2.0 * math.pi * np.arange(l, dtype=np.float32) / l).astype(np.float32)
    freqs = np.linspace(1e-4, bands - 1, bands, dtype=np.float32)
    ang = (omega[:, None] * freqs[None, :]).astype(np.float64)[pos]
    z = np.zeros((n, 128), np.float64)
    z[:, 0] = tt
    z[:, 1:1 + bands] = np.cos(ang)
    z[:, 1 + bands:1 + 2 * bands] = -np.sin(ang)
    z[:, 33] = (r != l)
    z[:, 34] = (r < l)
    return jnp.asarray(z.astype(np.float32))


def _filt_mlp_body(z_ref, zp_ref, w1_ref, b1_ref, q1_ref, w2_ref, b2_ref, q2_ref, w3h_ref, w3l_ref, dabs_ref,
                   o_ref, hh_ref, hl_ref):
    fh = HYENA_FILTER_HIDDEN

    @pl.when(pl.program_id(1) == 0)
    def _():
        hp = jnp.sin(q1_ref[...] * (jnp.dot(zp_ref[...], w1_ref[...], precision=_HIGHEST,
                                            preferred_element_type=_F32) + b1_ref[...]))
        hp = jnp.sin(q2_ref[...] * (jnp.dot(hp, w2_ref[...], precision=_HIGHEST,
                                            preferred_element_type=_F32) + b2_ref[...]))
        lane_p = lax.broadcasted_iota(jnp.int32, hp.shape, 1)
        swapped = pltpu.roll(hp, fh, axis=1)
        h = jnp.concatenate([jnp.where(lane_p < fh, hp, swapped), jnp.where(lane_p < fh, swapped, hp)], axis=0)
        z = z_ref[...]
        valid = z[:, 33:34]
        fwd = z[:, 34:35]
        lane = lax.broadcasted_iota(jnp.int32, h.shape, 1)
        hi, lo = _split_bf16(h * (valid * jnp.where(lane < fh, fwd, 1.0 - fwd)))
        hh_ref[...] = hi
        hl_ref[...] = lo

    raw = _dot3(hh_ref[...], hl_ref[...], w3h_ref[...], w3l_ref[...])
    window = jnp.exp(-z_ref[:, 0:1] * dabs_ref[...])
    o_ref[...] = (raw * window).astype(o_ref.dtype)


_FILT_ROWS = 1024


def _filt_mlp(zpos, w1bd, b1d, q1d, w2bd, b2d, q2d, w3h, w3l, dabs):
    n = zpos.shape[0]
    ct = _FFT_CT
    tr = _FILT_ROWS
    per_order = HYENA_WIDTH // ct
    fh = HYENA_FILTER_HIDDEN
    zpacked = zpos.reshape(n // tr, 2, tr // 2, 128).transpose(0, 2, 1, 3).reshape(n // 2, 256)
    full = lambda shape: pl.BlockSpec(shape, lambda r, i: (0,) * len(shape))
    return pl.pallas_call(
        _filt_mlp_body,
        grid=(n // tr, 2 * per_order),
        in_specs=[
            pl.BlockSpec((tr, 128), lambda r, i: (r, 0)),
            pl.BlockSpec((tr // 2, 256), lambda r, i: (r, 0)),
            full((256, 2 * fh)), full((1, 2 * fh)), full((1, 2 * fh)),
            full((2 * fh, 2 * fh)), full((1, 2 * fh)), full((1, 2 * fh)),
            pl.BlockSpec((2 * fh, ct), lambda r, i: (0, i)),
            pl.BlockSpec((2 * fh, ct), lambda r, i: (0, i)),
            pl.BlockSpec((1, ct), lambda r, i: (0, i % per_order)),
        ],
        out_specs=pl.BlockSpec((None, tr, ct), lambda r, i: (i, r, 0)),
        out_shape=jax.ShapeDtypeStruct((2 * per_order, n, ct), _BF16),
        scratch_shapes=[pltpu.VMEM((tr, 2 * fh), _BF16), pltpu.VMEM((tr, 2 * fh), _BF16)],
        compiler_params=_cparams(("arbitrary", "arbitrary")),
        name="filt_mlp",
    )(zpos, zpacked, w1bd, b1d, q1d, w2bd, b2d, q2d, w3h, w3l, dabs)


_F1_GROUP = 16


def _f1_tiles(f_base, f1pp):
    return [(slice(g * _F1_GROUP, (g + 1) * _F1_GROUP), pl.multiple_of(f_base + g * _F1_GROUP, _F1_GROUP))
            for g in range(f1pp // _F1_GROUP)]


def _gather_f1(s_ref, vre_ref, vim_ref, f_base, f1pp, n1):
    for sl, f0 in _f1_tiles(f_base, f1pp):
        vre_ref[sl] = jnp.swapaxes(s_ref[:, pl.ds(f0, _F1_GROUP), :], 0, 1)
        vim_ref[sl] = jnp.swapaxes(s_ref[:, pl.ds(n1 + f0, _F1_GROUP), :], 0, 1)


def _filt_fft_body(kern_ref, fa_ref, fb_ref, o_ref, s_ref, vre_ref, vim_ref, *, n_in, n1, f1pp):
    s = pl.program_id(1)

    @pl.when(s < n_in)
    def _():
        for i in range(_T2_PER_PHASE):
            s_ref[s * _T2_PER_PHASE + i] = jnp.dot(fa_ref[...], kern_ref[i],
                                                   preferred_element_type=_F32).astype(_BF16)

    @pl.when(s >= n_in)
    def _():
        _gather_f1(s_ref, vre_ref, vim_ref, (s - n_in) * f1pp, f1pp, n1)

        for f in range(f1pp):
            v = jnp.concatenate([vre_ref[f], vim_ref[f]], axis=0)
            o_ref[f * 2 * _N2:(f + 1) * 2 * _N2, :] = jnp.dot(
                fb_ref[f], v, preferred_element_type=_F32).astype(o_ref.dtype)


def _filt_fft(kern4, fa_full, fb):
    tiles, n2, n1, ct = kern4.shape
    n_in = n2 // _T2_PER_PHASE
    f1pp = min(_F1_PER_PHASE, n1)
    n_mid = n1 // f1pp
    body = functools.partial(_filt_fft_body, n_in=n_in, n1=n1, f1pp=f1pp)
    return pl.pallas_call(
        body,
        grid=(tiles, n_in + n_mid),
        in_specs=[
            pl.BlockSpec((None, _T2_PER_PHASE, n1, ct), lambda j, s: (j, jnp.minimum(s, n_in - 1), 0, 0)),
            pl.BlockSpec((2 * n1, n1), lambda j, s: (0, 0)),
            pl.BlockSpec((f1pp, 2 * n2, 2 * n2), lambda j, s: (jnp.maximum(s - n_in, 0), 0, 0)),
        ],
        out_specs=pl.BlockSpec((None, f1pp * 2 * n2, ct), lambda j, s: (j, jnp.maximum(s - n_in, 0), 0)),
        out_shape=jax.ShapeDtypeStruct((tiles, n1 * 2 * n2, ct), _BF16),
        scratch_shapes=[pltpu.VMEM((n2, 2 * n1, ct), _BF16),
                        pltpu.VMEM((f1pp, n2, ct), _BF16), pltpu.VMEM((f1pp, n2, ct), _BF16)],
        compiler_params=_cparams(("arbitrary", "arbitrary")),
        name="filt_fft",
    )(kern4, fa_full, fb)


def _hyena_conv_body(u_ref, g_ref, k_ref, bias_ref, fa_ref, fai_ref, fb_ref, fbi_ref,
                     o_ref, s_ref, vre_ref, vim_ref, wre_ref, wim_ref, y_ref, ukeep_ref,
                     *, n_in, n_mid, n1, f1pp, natural_out):
    s = pl.program_id(2)
    h1 = n1 // 2

    @pl.when(s < n_in)
    def _():
        for i in range(_T2_PER_PHASE):
            u = jnp.concatenate([u_ref[0, i], u_ref[1, i]], axis=0)
            ukeep_ref[s * _T2_PER_PHASE + i] = u
            s_ref[s * _T2_PER_PHASE + i] = jnp.dot(fa_ref[...], u,
                                                   preferred_element_type=_F32).astype(_BF16)

    @pl.when(jnp.logical_and(s >= n_in, s < n_in + n_mid))
    def _():
        tiles = _f1_tiles((s - n_in) * f1pp, f1pp)

        def gather(sl, f0):
            vre_ref[sl] = jnp.swapaxes(s_ref[:, pl.ds(f0, _F1_GROUP), :], 0, 1)
            vim_ref[sl] = jnp.swapaxes(s_ref[:, pl.ds(n1 + f0, _F1_GROUP), :], 0, 1)

        def forward(f):
            v = jnp.concatenate([vre_ref[f], vim_ref[f]], axis=0)
            x = jnp.dot(fb_ref[f], v, preferred_element_type=_F32)
            xre, xim = x[:_N2], x[_N2:]
            kre = k_ref[f * 2 * _N2:f * 2 * _N2 + _N2, :].astype(_F32)
            kim = k_ref[f * 2 * _N2 + _N2:(f + 1) * 2 * _N2, :].astype(_F32)
            y_ref[f] = jnp.concatenate([xre * kre - xim * kim, xre * kim + xim * kre],
                                       axis=0).astype(_BF16)

        def inverse(f):
            w = jnp.dot(fbi_ref[f], y_ref[f], preferred_element_type=_F32)
            wre_ref[f] = w[:_N2].astype(_BF16)
            wim_ref[f] = w[_N2:].astype(_BF16)

        def scatter(sl, f0):
            s_ref[:, pl.ds(f0, _F1_GROUP), :] = jnp.swapaxes(wre_ref[sl], 0, 1)
            s_ref[:, pl.ds(n1 + f0, _F1_GROUP), :] = jnp.swapaxes(wim_ref[sl], 0, 1)

        nt = len(tiles)
        for t in tiles:
            gather(*t)
        for g in range(nt + 1):
            for k in range(_F1_GROUP):
                if g < nt:
                    forward(g * _F1_GROUP + k)
                if g >= 1:
                    inverse((g - 1) * _F1_GROUP + k)
            if g >= 1:
                scatter(*tiles[g - 1])

    @pl.when(s >= n_in + n_mid)
    def _():
        bias = bias_ref[...]
        outs = [[], []]
        for i in range(_T2_PER_PHASE):
            t2 = (s - n_in - n_mid) * _T2_PER_PHASE + i
            y = jnp.dot(fai_ref[...], s_ref[t2], preferred_element_type=_F32)
            y = y + ukeep_ref[t2].astype(_F32) * bias
            for m in range(2):
                res = g_ref[m, i].astype(_F32) * y[m * h1:(m + 1) * h1]
                if natural_out:
                    outs[m].append(res.astype(o_ref.dtype))
                else:
                    o_ref[m, i] = res.astype(o_ref.dtype)
        if natural_out:
            for m in range(2):
                o_ref[m] = jnp.swapaxes(jnp.stack(outs[m], axis=0), 0, 1)


def _hyena_conv(u5, ucol, g5, gcol, kspec, kcol, bias, tabs, natural_out):
    fa, fai, fb, fbi = tabs
    b, _, n2, h1, ct = u5.shape
    n1 = 2 * h1
    n_in = n2 // _T2_PER_PHASE
    f1pp = min(_F1_PER_PHASE, n1)
    n_mid = n1 // f1pp
    n_out = n_in
    tiles = HYENA_WIDTH // ct
    body = functools.partial(_hyena_conv_body, n_in=n_in, n_mid=n_mid, n1=n1, f1pp=f1pp,
                             natural_out=natural_out)
    mid = lambda s: jnp.clip(s - n_in, 0, n_mid - 1)
    last = lambda s: jnp.clip(s - n_in - n_mid, 0, n_out - 1)
    blk = (2, None, _T2_PER_PHASE, h1, ct)
    if natural_out:
        out_spec = pl.BlockSpec((None, 2, h1, _T2_PER_PHASE, ct), lambda j, p, s: (j, p, 0, last(s), 0))
        out_shape = jax.ShapeDtypeStruct((tiles, b, h1, n2, ct), _BF16)
    else:
        out_spec = pl.BlockSpec(blk, lambda j, p, s: (p, j, last(s), 0, 0))
        out_shape = jax.ShapeDtypeStruct((b, tiles, n2, h1, ct), _BF16)
    return pl.pallas_call(
        body,
        grid=(tiles, b // 2, n_in + n_mid + n_out),
        in_specs=[
            pl.BlockSpec(blk, lambda j, p, s: (p, ucol + j, jnp.minimum(s, n_in - 1), 0, 0)),
            pl.BlockSpec(blk, lambda j, p, s: (p, gcol + j, last(s), 0, 0)),
            pl.BlockSpec((None, f1pp * 2 * n2, ct), lambda j, p, s: (kcol + j, mid(s), 0)),
            pl.BlockSpec((1, ct), lambda j, p, s: (0, j)),
            pl.BlockSpec((2 * n1, n1), lambda j, p, s: (0, 0)),
            pl.BlockSpec((n1, 2 * n1), lambda j, p, s: (0, 0)),
            pl.BlockSpec((f1pp, 2 * n2, 2 * n2), lambda j, p, s: (mid(s), 0, 0)),
            pl.BlockSpec((f1pp, 2 * n2, 2 * n2), lambda j, p, s: (mid(s), 0, 0)),
        ],
        out_specs=out_spec,
        out_shape=out_shape,
        scratch_shapes=([pltpu.VMEM((n2, 2 * n1, ct), _BF16)] + 4 * [pltpu.VMEM((f1pp, n2, ct), _BF16)]
                        + [pltpu.VMEM((f1pp, 2 * n2, ct), _BF16), pltpu.VMEM((n2, n1, ct), _BF16)]),
        compiler_params=_cparams(("arbitrary", "arbitrary", "arbitrary")),
        name="hyena_conv",
    )(u5, g5, kspec, bias, fa, fai, fb, fbi)


_GATE_LANES = 2 * MLSTM_HEADS
_TERM_R = 12
_LOG2E = math.log2(math.e)
_LOG2_K_SCALE = -0.5 * math.log2(MLSTM_HEAD_DIM)
assert _LOG2_K_SCALE == int(_LOG2_K_SCALE), "head_dim must be a power of 4 for the exact fold"
_MLSTM_SUB = 4


def _split3(x):
    hi = x.astype(_BF16).astype(_F32)
    mid = (x - hi).astype(_BF16).astype(_F32)
    lo = (x - hi - mid).astype(_BF16).astype(_F32)
    return hi, mid, lo


def _scan_order_max(x, is_fwd):
    n = x.shape[0]
    row = lax.broadcasted_iota(jnp.int32, x.shape, 0)
    pre, suf = x, x
    shift = 1
    while shift < n:
        pre = jnp.maximum(pre, jnp.where(row >= shift, pltpu.roll(pre, shift, axis=0), -jnp.inf))
        suf = jnp.maximum(suf, jnp.where(row < n - shift, pltpu.roll(suf, n - shift, axis=0), -jnp.inf))
        shift *= 2
    return jnp.where(is_fwd, pre, suf)


def _gate_prep_body(g_ref, bias_ref, a_ref, row_ref, keep_ref,
                    b_scr, a_scr, pm_scr, tot_scr, mloc_scr, mprev_scr, *, nc):
    ch = MLSTM_CHUNK
    lane = lax.broadcasted_iota(jnp.int32, (1, 128), 1)
    is_fwd = lane < MLSTM_HEADS
    live = lane < _GATE_LANES
    jj = lax.broadcasted_iota(jnp.int32, (ch, ch), 0)
    ss = lax.broadcasted_iota(jnp.int32, (ch, ch), 1)
    t_lo = (ss <= jj).astype(_F32)
    row_ref[...] = jnp.zeros_like(row_ref)

    def chunk_stats(c, carry):
        r0 = pl.multiple_of(c * ch, ch)
        gi = g_ref[pl.ds(r0, ch), :] + bias_ref[...]
        gf = pltpu.roll(gi, 128 - 2 * MLSTM_HEADS, axis=1)
        logf = jnp.minimum(gf, 0.0) - jnp.log1p(jnp.exp(-jnp.abs(gf)))
        cs_lo = jnp.dot(t_lo, logf, precision=_HIGHEST, preferred_element_type=_F32)
        tot = cs_lo[ch - 1:ch, :]
        cs_up = tot - cs_lo + logf
        bcs = jnp.where(is_fwd, cs_lo, cs_up)
        a = tot - bcs + gi
        b_scr[pl.ds(r0, ch), :] = bcs
        a_scr[pl.ds(r0, ch), :] = a
        tot_scr[pl.ds(c, 1), :] = tot
        mloc_scr[pl.ds(c, 1), :] = jnp.max(a, axis=0, keepdims=True)
        r = gi - bcs
        pm_scr[pl.ds(r0, ch), :] = _scan_order_max(r, is_fwd)
        hi, mid, lo = _split3(r.T[0:_GATE_LANES, :] * _LOG2E)
        base = _TERM_R * _GATE_LANES
        row_ref[base:base + 16, pl.ds(r0, ch)] = jnp.concatenate([hi, mid], axis=0).astype(_BF16)
        row_ref[base + 16:base + 32, pl.ds(r0, ch)] = jnp.concatenate(
            [lo, jnp.zeros_like(lo)], axis=0).astype(_BF16)
        return carry

    lax.fori_loop(0, nc, chunk_stats, 0, unroll=4)

    def scan_f(c, m):
        mprev_scr[pl.ds(c, 1), :] = jnp.where(is_fwd, m, mprev_scr[pl.ds(c, 1), :])
        return jnp.maximum(tot_scr[pl.ds(c, 1), :] + m, mloc_scr[pl.ds(c, 1), :])

    def scan_b(i, m):
        c = nc - 1 - i
        mprev_scr[pl.ds(c, 1), :] = jnp.where(is_fwd, mprev_scr[pl.ds(c, 1), :], m)
        return jnp.maximum(tot_scr[pl.ds(c, 1), :] + m, mloc_scr[pl.ds(c, 1), :])

    mprev_scr[...] = jnp.zeros_like(mprev_scr)
    lax.fori_loop(0, nc, scan_f, jnp.zeros((1, 128), _F32))
    lax.fori_loop(0, nc, scan_b, jnp.zeros((1, 128), _F32))

    def emit(c, carry):
        r0 = pl.multiple_of(c * ch, ch)
        mprev = mprev_scr[pl.ds(c, 1), :]
        tot = tot_scr[pl.ds(c, 1), :]
        mnew = jnp.maximum(tot + mprev, mloc_scr[pl.ds(c, 1), :])
        keep_ref[pl.ds(c, 1), :] = jnp.exp(tot + mprev - mnew)
        g = jnp.maximum(mprev, pm_scr[pl.ds(r0, ch), :])
        terms = (_split3(-g * _LOG2E + _LOG2_K_SCALE) + _split3((mprev - g) * _LOG2E + _LOG2_K_SCALE)
                 + _split3((-b_scr[pl.ds(r0, ch), :] - g) * _LOG2E)
                 + _split3((a_scr[pl.ds(r0, ch), :] - mnew) * _LOG2E) + 3 * (jnp.ones((ch, 128), _F32),))
        pack = jnp.where(live, terms[0], 0.0)
        for t in range(1, len(terms)):
            pack = pack + pltpu.roll(jnp.where(live, terms[t], 0.0), _GATE_LANES * t, axis=1)
        a_ref[pl.ds(r0, ch), :] = pack.astype(_BF16)
        return carry

    lax.fori_loop(0, nc, emit, 0, unroll=4)


def _gate_prep(g3, bias):
    b, l, _ = g3.shape
    nc = l // MLSTM_CHUNK
    body = functools.partial(_gate_prep_body, nc=nc)
    return pl.pallas_call(
        body,
        grid=(b,),
        in_specs=[
            pl.BlockSpec((None, l, 128), lambda i: (i, 0, 0)),
            pl.BlockSpec((1, 128), lambda i: (0, 0)),
        ],
        out_specs=[
            pl.BlockSpec((None, l, 128), lambda i: (i, 0, 0)),
            pl.BlockSpec((None, 128, l), lambda i: (i, 0, 0)),
            pl.BlockSpec((None, nc, 128), lambda i: (i, 0, 0)),
        ],
        out_shape=[
            jax.ShapeDtypeStruct((b, l, 128), _BF16),
            jax.ShapeDtypeStruct((b, 128, l), _BF16),
            jax.ShapeDtypeStruct((b, nc, 128), _F32),
        ],
        scratch_shapes=[
            pltpu.VMEM((l, 128), _F32), pltpu.VMEM((l, 128), _F32), pltpu.VMEM((l, 128), _F32),
            pltpu.VMEM((nc, 128), _F32), pltpu.VMEM((nc, 128), _F32), pltpu.VMEM((nc, 128), _F32),
        ],
        compiler_params=_cparams(("arbitrary",)),
        name="gate_prep",
    )(g3, bias)


def _gate_spread_matrix():
    bc = np.zeros((128, 4 * 128), np.float32)
    for blk in range(4):
        for t in range(3 * blk, 3 * blk + 3):
            bc[t * _GATE_LANES:(t + 1) * _GATE_LANES, blk * 128:(blk + 1) * 128] = 1.0
    return jnp.asarray(bc).astype(_BF16)


def _mlstm_body(keep_ref, bc_ref, qf_ref, kf_ref, vf_ref, af_ref, rf_ref, qb_ref, kb_ref, vb_ref, ab_ref,
                rb_ref, hf_ref, hb_ref, ct_ref, nm_ref, *, nc):
    bi, c = pl.program_id(0), pl.program_id(1)

    @pl.when(c == 0)
    def _():
        ct_ref[...] = jnp.zeros_like(ct_ref)
        nm_ref[...] = jnp.zeros_like(nm_ref)

    ch, dh = MLSTM_CHUNK, MLSTM_HEAD_DIM
    jj = lax.broadcasted_iota(jnp.int32, (ch, ch), 0)
    ss = lax.broadcasted_iota(jnp.int32, (ch, ch), 1)
    head_lane = ss % _GATE_LANES
    ones_rhs = jnp.ones((ch, 128), _BF16)
    twice = lambda a: jnp.concatenate([a, a], axis=1)
    dirs = ((qf_ref, kf_ref, vf_ref, af_ref, rf_ref, hf_ref, ss <= jj),
            (qb_ref, kb_ref, vb_ref, ab_ref, rb_ref, hb_ref, ss >= jj))
    for sub, d in [(sub, d) for sub in range(_MLSTM_SUB) for d in range(2)]:
        q_blk, k_blk, v_blk, a_blk, r_blk, o_blk, mask = dirs[d]
        pos = sub if d == 0 else _MLSTM_SUB - 1 - sub
        chunk = c * _MLSTM_SUB + pos if d == 0 else nc - (c + 1) * _MLSTM_SUB + pos
        rows = slice(pos * ch, (pos + 1) * ch)
        q_ref, k_ref, v_ref, o_ref = q_blk.at[rows], k_blk.at[rows], v_blk.at[rows], o_blk.at[rows]
        bmat = jnp.concatenate([bc_ref[:, 0:128] + r_blk[:, rows], bc_ref[:, 128:]], axis=1)
        a_all = a_blk[rows, :].astype(_F32)
        kbase = (bi * nc + chunk) * _GATE_LANES
        for h in range(MLSTM_HEADS):
            hd = d * MLSTM_HEADS + h
            a_h = jnp.where(head_lane == hd, a_all, 0.0).astype(_BF16)
            e = jnp.dot(a_h, bmat, preferred_element_type=_F32)
            decay = jnp.exp2(jnp.where(mask, e[:, 0:128], -jnp.inf))
            rest = jnp.exp2(e[:, 128:])
            iw, clamp, wrep = rest[:, 0:128], rest[:, 128:256], rest[:, 256:384]
            keep = keep_ref[kbase + hd]
            lo, hi = h * dh, (h + 1) * dh
            qh, kh, vh = q_ref[:, lo:hi], k_ref[:, lo:hi], v_ref[:, lo:hi]
            p = (_dot_nt(qh, kh) * decay).astype(_BF16)
            ct, nm = ct_ref[hd], nm_ref[hd]
            pv = jnp.dot(p, jnp.concatenate([vh, ones_rhs], axis=1), preferred_element_type=_F32)
            qc = jnp.dot(qh, jnp.concatenate([ct, nm], axis=1).astype(_BF16), preferred_element_type=_F32)
            num = pv[:, :dh] + twice(iw) * qc[:, :dh]
            den = pv[:, dh:] + iw * qc[:, dh:]
            inv = 1.0 / jnp.maximum(jnp.abs(den), clamp)
            o_ref[:, lo:hi] = (num * twice(inv)).astype(o_ref.dtype)
            vw = jnp.concatenate([(vh.astype(_F32) * twice(wrep)).astype(_BF16), wrep.astype(_BF16)], axis=1)
            upd = lax.dot_general(kh, vw, (((0,), (0,)), ((), ())), preferred_element_type=_F32)
            ct_ref[hd] = keep * ct + upd[:, :dh]
            nm_ref[hd] = keep * nm + upd[:, dh:]


def _mlstm(qk, proj3, a_mat, r_rows, keep):
    b, l, _ = qk.shape
    mw = MLSTM_WIDTH
    nc = l // MLSTM_CHUNK
    rows = _MLSTM_SUB * MLSTM_CHUNK
    nb = l // rows
    fw = lambda c: c
    bw = lambda c: nb - 1 - c

    def specs(cm):
        return [
            pl.BlockSpec((None, rows, mw), lambda i, c: (i, cm(c), 0)),
            pl.BlockSpec((None, rows, mw), lambda i, c: (i, cm(c), 1)),
            pl.BlockSpec((None, rows, mw), lambda i, c: (i, cm(c), _COL_V)),
            pl.BlockSpec((None, rows, 128), lambda i, c: (i, cm(c), 0)),
            pl.BlockSpec((None, 128, rows), lambda i, c: (i, 0, cm(c))),
        ]

    body = functools.partial(_mlstm_body, nc=nc)
    keep_flat = keep[:, :, :_GATE_LANES].reshape(b * nc * _GATE_LANES)
    return pl.pallas_call(
        body,
        grid=(b, nb),
        in_specs=[pl.BlockSpec(memory_space=pltpu.SMEM),
                  pl.BlockSpec((128, 4 * 128), lambda i, c: (0, 0))] + specs(fw) + specs(bw),
        out_specs=[
            pl.BlockSpec((None, rows, mw), lambda i, c: (i, c, 0)),
            pl.BlockSpec((None, rows, mw), lambda i, c: (i, nb - 1 - c, 0)),
        ],
        out_shape=[jax.ShapeDtypeStruct((b, l, mw), _BF16), jax.ShapeDtypeStruct((b, l, mw), _BF16)],
        scratch_shapes=[
            pltpu.VMEM((2 * MLSTM_HEADS, MLSTM_HEAD_DIM, MLSTM_HEAD_DIM), _F32),
            pltpu.VMEM((2 * MLSTM_HEADS, MLSTM_HEAD_DIM, 128), _F32),
        ],
        compiler_params=_cparams(("arbitrary", "arbitrary")),
        name="mlstm",
    )(keep_flat, _gate_spread_matrix(), qk, qk, proj3, a_mat, r_rows, qk, qk, proj3, a_mat, r_rows)


def _sigmoid(x):
    return 1.0 / (1.0 + jnp.exp(-x))


def _merge_body(x_ref, ya_ref, hf_ref, hb_ref, o_ref, ma_ref, mb_ref, wa_ref, wb_ref, wo_ref, out_ref):
    hsum = hf_ref[...].astype(_F32) + hb_ref[...].astype(_F32)
    yb = (_sigmoid(o_ref[...].astype(_F32)) * hsum).astype(_BF16)
    ya = jnp.concatenate([ya_ref[t] for t in range(ya_ref.shape[0])], axis=1)
    pa = jnp.dot(ya, wa_ref[...], preferred_element_type=_F32)
    pb = jnp.dot(yb, wb_ref[...], preferred_element_type=_F32)
    mixed = _sigmoid(ma_ref[...].astype(_F32)) * pa + _sigmoid(mb_ref[...].astype(_F32)) * pb
    out_ref[...] = x_ref[...] + jnp.dot(mixed.astype(_BF16), wo_ref[...], preferred_element_type=_F32)


def _merge(x2, ya3, hf2, hb2, proj, wa, wb, wo):
    m = x2.shape[0]
    tm = _MERGE_ROWS
    d, mw = D_MODEL, MLSTM_WIDTH
    const = lambda shape: pl.BlockSpec(shape, lambda i: (0, 0), pipeline_mode=pl.Buffered(1))
    return pl.pallas_call(
        _merge_body,
        grid=(m // tm,),
        in_specs=[
            pl.BlockSpec((tm, d), lambda i: (i, 0)),
            pl.BlockSpec((ya3.shape[0], tm, ya3.shape[2]), lambda i: (0, i, 0)),
            pl.BlockSpec((tm, mw), lambda i: (i, 0)),
            pl.BlockSpec((tm, mw), lambda i: (i, 0)),
            pl.BlockSpec((tm, mw), lambda i: (i, _COL_O)),
            pl.BlockSpec((tm, d), lambda i: (i, _COL_MA // 2)),
            pl.BlockSpec((tm, d), lambda i: (i, _COL_MB // 2)),
            const((HYENA_WIDTH, d)), const((mw, d)), const((d, d)),
        ],
        out_specs=pl.BlockSpec((tm, d), lambda i: (i, 0)),
        out_shape=jax.ShapeDtypeStruct((m, d), _F32),
        compiler_params=_cparams(("arbitrary",)),
        name="merge",
    )(x2, ya3, hf2, hb2, proj, proj, proj, wa, wb, wo)


def _ffn_body(x_ref, n2_ref, wg_ref, wu_ref, wd_ref, nf_ref, o_ref, hn_ref):
    f = pl.program_id(1)
    last = pl.num_programs(1) - 1
    rows = _NORM_ROWS

    def swiglu_down(hn):
        g = jnp.dot(hn, wg_ref[...], preferred_element_type=_F32)
        u = jnp.dot(hn, wu_ref[...], preferred_element_type=_F32)
        a = (g * _sigmoid(g) * u).astype(_BF16)
        return jnp.dot(a, wd_ref[...], preferred_element_type=_F32)

    @pl.when(f == 0)
    def _():
        for r in range(0, x_ref.shape[0], rows):
            x = x_ref[r:r + rows, :]
            hn = (x * lax.rsqrt(jnp.mean(x * x, axis=-1, keepdims=True) + RMS_EPS) * n2_ref[...]).astype(_BF16)
            hn_ref[r:r + rows, :] = hn
            o_ref[r:r + rows, :] = x + swiglu_down(hn)

    @pl.when(jnp.logical_and(f > 0, f < last))
    def _():
        o_ref[...] += swiglu_down(hn_ref[...])

    @pl.when(f == last)
    def _():
        for r in range(0, x_ref.shape[0], rows):
            y = o_ref[r:r + rows, :] + swiglu_down(hn_ref[r:r + rows, :])
            o_ref[r:r + rows, :] = y * lax.rsqrt(jnp.mean(y * y, axis=-1, keepdims=True) + RMS_EPS) * nf_ref[...]


def _ffn(x2, norm2_w, w_gate_up, w_down, norm_f_w):
    m = x2.shape[0]
    tm, tf = _FFN_TILE
    d = D_MODEL
    nf = FFN_HIDDEN // tf
    return pl.pallas_call(
        _ffn_body,
        grid=(m // tm, nf),
        in_specs=[
            pl.BlockSpec((tm, d), lambda i, f: (i, 0)),
            pl.BlockSpec((1, d), lambda i, f: (0, 0)),
            pl.BlockSpec((d, tf), lambda i, f: (0, f)),
            pl.BlockSpec((d, tf), lambda i, f: (0, nf + f)),
            pl.BlockSpec((tf, d), lambda i, f: (f, 0)),
            pl.BlockSpec((1, d), lambda i, f: (0, 0)),
        ],
        out_specs=pl.BlockSpec((tm, d), lambda i, f: (i, 0)),
        out_shape=jax.ShapeDtypeStruct((m, d), _F32),
        scratch_shapes=[pltpu.VMEM((tm, d), _BF16)],
        compiler_params=_cparams(("arbitrary", "arbitrary")),
        name="ffn",
    )(x2, norm2_w, w_gate_up, w_gate_up, w_down, norm_f_w)


def kernel(x, norm1_w, w_in, conv_w, conv_b, filt_w1, filt_b1, filt_freq1, filt_w2, filt_b2, filt_freq2,
           filt_w3, hyena_bias, mlstm_gate_bias, w_branch_a, w_branch_b, w_out, norm2_w, w_gate_up, w_down,
           norm_f_w):
    b, l, d = x.shape
    assert d == D_MODEL and b % 2 == 0 and l % (_N2 * _T2_PER_PHASE) == 0
    assert norm1_w.shape[0] == 1, "single-layer block"
    hw, mw, nh = HYENA_WIDTH, MLSTM_WIDTH, MLSTM_HEADS
    m = b * l
    n = 2 * l
    n1 = n // _N2
    sc_cols = 3 * hw + 2 * mw
    g0 = sc_cols + 2 * mw

    w_t = jnp.swapaxes(w_in[0], 0, 1)
    w_all_t = w_t.astype(_BF16)
    w_merge_t = w_all_t[g0 + 4 * nh:]
    wg_t = w_t[g0:g0 + 4 * nh]
    gate_order = lambda a: jnp.concatenate(
        [a[0:nh], a[2 * nh:3 * nh], a[nh:2 * nh], a[3 * nh:4 * nh]], axis=0)
    wg_split_t = jnp.concatenate(_split_bf16(jnp.pad(gate_order(wg_t), ((0, 128 - 4 * nh), (0, 0)))), axis=0)
    gate_bias = jnp.pad(gate_order(mlstm_gate_bias[0].astype(_F32).reshape(4 * nh))[None, :],
                        ((0, 0), (0, 128 - 4 * nh)))

    x2 = x.reshape(m, d)
    proj, gates = _inproj(x2, norm1_w[0][None, :], w_merge_t, w_all_t, wg_split_t)
    proj3 = proj.reshape(b, l, _MAIN_COLS)

    cw, cb = conv_w[0], conv_b[0][None, :]
    hy5 = _shortconv(proj3, cw, cb, _COL_HV * 1024, 0, 3 * hw, True)
    qk = _shortconv(proj3, cw, cb, _COL_Q * 1024, 3 * hw, 2 * mw, False)

    fa, fa_full, fai, fb, fbi = _dft_tables(n)
    zpos = _filter_positions(l)
    w1p = jnp.pad(filt_w1[0].astype(_F32), ((0, 128 - HYENA_EMB), (0, 0)))
    max_decay = math.log(HYENA_TARGET) / HYENA_FAST_DECAY
    min_decay = math.log(HYENA_TARGET) / HYENA_SLOW_DECAY
    dabs = jnp.asarray(np.abs(np.linspace(min_decay, max_decay, hw, dtype=np.float32))[None, :])
    dup = lambda a: jnp.concatenate([a, a], axis=-1)
    fh = HYENA_FILTER_HIDDEN
    w3 = filt_w3[0].astype(_F32).reshape(fh, 2, 2, hw).transpose(2, 0, 1, 3).reshape(2 * fh, 2 * hw)
    w3_hi, w3_lo = _split_bf16(w3)
    blockdiag = lambda a: jnp.concatenate(
        [jnp.concatenate([a, jnp.zeros_like(a)], axis=1), jnp.concatenate([jnp.zeros_like(a), a], axis=1)], axis=0)
    kern = _filt_mlp(zpos, blockdiag(w1p), dup(filt_b1[0][None, :]), dup(filt_freq1[0][None, :]),
                     blockdiag(filt_w2[0].astype(_F32)), dup(filt_b2[0][None, :]),
                     dup(filt_freq2[0][None, :]), w3_hi, w3_lo, dabs)
    tiles = hw // _FFT_CT
    kspec = _filt_fft(kern.reshape(2 * tiles, _N2, n1, _FFT_CT), fa_full, fb)
    tabs = (fa, fai, fb, fbi)
    hbias = hyena_bias[0].astype(_F32)
    z5 = _hyena_conv(hy5, 0, hy5, tiles, kspec, 0, hbias[0][None, :], tabs, False)
    ya5 = _hyena_conv(z5, 0, hy5, 2 * tiles, kspec, tiles, hbias[1][None, :], tabs, True)
    ya3 = ya5.reshape(tiles, m, _FFT_CT)

    a_mat, r_rows, keep = _gate_prep(gates.reshape(b, l, 128), gate_bias)
    hf, hb = _mlstm(qk, proj3, a_mat, r_rows, keep)

    x_mid = _merge(x2, ya3, hf.reshape(m, mw), hb.reshape(m, mw), proj,
                   w_branch_a[0].astype(_BF16), w_branch_b[0].astype(_BF16), w_out[0].astype(_BF16))
    out = _ffn(x_mid, norm2_w[0][None, :], w_gate_up[0].astype(_BF16), w_down[0].astype(_BF16),
               norm_f_w[None, :])
    return out.reshape(b, l, d)
```
